```python
import math
import jax, jax.numpy as jnp
from jax import lax
import numpy as np

D_MODEL = 1024
BATCH = 32
SEQ = 256
DEPTH = 4
DEC_BATCH = 4
DEC_SEQ = 1024
PAST_LEN = 256

GRID_W = 64
N_EVEN = (DEPTH + 1) // 2
N_ODD = DEPTH // 2
MIX_HALF = D_MODEL // 2
HY_D = MIX_HALF
HY_ORDER = 2
HY_BANDS = 16
HY_EMB = 2 * HY_BANDS + 1
HY_FILTER_HIDDEN = 64
SHORT_CONV = 3
MLA_HEADS = 8
MLA_NOPE = 64
MLA_ROPE = 32
MLA_QK = MLA_NOPE + MLA_ROPE
MLA_V = MIX_HALF // MLA_HEADS
MLA_Q_LORA = 3 * D_MODEL // 8
MLA_KV_LORA = D_MODEL // 4
GQA_HEADS = 16
GQA_KV_HEADS = 4
GQA_HEAD_DIM = D_MODEL // GQA_HEADS
N_EXPERTS = 32
TOP_K = 4
D_EXPERT = D_MODEL
SWIGLU_LIMIT = 7.0
SWIGLU_ALPHA = 1.702
ROPE_THETA = 10000.0
Q_BLOCK = 128
MOE_BLOCK = 128
RMS_EPS = 1e-6
IN_AB = (HY_ORDER + 1) * HY_D + MLA_Q_LORA + MLA_KV_LORA + MLA_ROPE
OUT_AB = HY_D + MLA_HEADS * MLA_V
QKV_C = (GQA_HEADS + 2 * GQA_KV_HEADS) * GQA_HEAD_DIM

kernel_name = 'hybrid_hyena_mla_gqa_moe_diffusion_step'


def rms_norm(x, gain):
    xf = x.astype(jnp.float32)
    y = xf * lax.rsqrt(jnp.mean(xf * xf, axis=-1, keepdims=True) + RMS_EPS)
    return (y * gain.astype(jnp.float32)).astype(x.dtype)


def _rope_1d(x, pos):
    d = x.shape[-1]
    inv = ROPE_THETA ** (-jnp.arange(0, d, 2, dtype=jnp.float32) / d)
    ang = pos.astype(jnp.float32)[:, None] * inv[None, :]
    ang = jnp.concatenate([ang, ang], axis=-1)[None, :, None, :]
    x1, x2 = jnp.split(x, 2, axis=-1)
    return x * jnp.cos(ang) + jnp.concatenate([-x2, x1], axis=-1) * jnp.sin(ang)


def axial_rope(x, rows, cols):
    half = x.shape[-1] // 2
    xf = x.astype(jnp.float32)
    out = jnp.concatenate([_rope_1d(xf[..., :half], rows), _rope_1d(xf[..., half:], cols)], axis=-1)
    return out.astype(x.dtype)


def attention(q, k, v):
    b, sq, h, dk = q.shape
    g = k.shape[2]
    r = h // g
    dv = v.shape[-1]
    blk = math.gcd(sq, Q_BLOCK)
    nb = sq // blk
    scale = dk ** -0.5
    qb = q.reshape(b, nb, blk, g, r, dk).transpose(1, 0, 2, 3, 4, 5)

    def one_block(qi):
        s = jnp.einsum('bqgrd,bkgd->bgrqk', qi, k, preferred_element_type=jnp.float32) * scale
        p = jax.nn.softmax(s, axis=-1)
        return jnp.einsum('bgrqk,bkgd->bqgrd', p.astype(v.dtype), v)

    o = lax.map(one_block, qb)
    return o.transpose(1, 0, 2, 3, 4, 5).reshape(b, sq, h, dv)


def short_conv(u, w, b):
    L = u.shape[1]
    pad = SHORT_CONV // 2
    up = jnp.pad(u, ((0, 0), (pad, SHORT_CONV - 1 - pad), (0, 0)))
    return sum(up[:, j:j + L] * w[j] for j in range(SHORT_CONV)) + b


def hyena_filter_spectrum(L, w1, b1, freq, w2, b2, w3, b3, log_decay):
    f32 = jnp.float32
    p = jnp.arange(L, dtype=f32)
    t = p / max(L - 1, 1)
    bands = jnp.linspace(1e-4, HY_BANDS - 1, HY_BANDS, dtype=f32)
    ang = (2.0 * math.pi / L) * p[:, None] * bands[None, :]
    z = jnp.concatenate([t[:, None], jnp.cos(ang), -jnp.sin(ang)], axis=-1)
    fr = freq.astype(f32)
    hdn = jnp.sin(fr * (z @ w1.astype(f32) + b1.astype(f32)))
    hdn = jnp.sin(fr * (hdn @ w2.astype(f32) + b2.astype(f32)))
    filt = (hdn @ w3.astype(f32) + b3.astype(f32)).reshape(L, HY_ORDER, 2, HY_D)
    decay = jnp.exp(-t[:, None, None, None] * jnp.exp(log_decay.astype(f32))[None])
    filt = filt * decay
    filt = filt / (jnp.sum(jnp.abs(filt), axis=(0, 2), keepdims=True) + 1e-6)
    fwd, bwd = filt[:, :, 0], filt[:, :, 1]
    kcirc = jnp.concatenate([fwd, jnp.zeros((1, HY_ORDER, HY_D), f32), bwd[1:][::-1]], axis=0)
    return jnp.fft.rfft(kcirc, axis=0)


def hyena_mix(u, conv_w, conv_b, kf, hy_bias):
    L = u.shape[1]
    u = short_conv(u, conv_w, conv_b)
    parts = jnp.split(u, HY_ORDER + 1, axis=-1)
    z = parts[0].astype(jnp.float32)
    for o in range(HY_ORDER):
        zf = jnp.fft.rfft(z, n=2 * L, axis=1)
        conv = jnp.fft.irfft(zf * kf[None, :, o], n=2 * L, axis=1)[:, :L]
        z = parts[o + 1].astype(jnp.float32) * (conv + z * hy_bias[o].astype(jnp.float32))
    return z.astype(u.dtype)


def mla_expand(latent, wkv_b, k_gain):
    b, L, _ = latent.shape
    kv = (latent[..., :MLA_KV_LORA] @ wkv_b).reshape(b, L, MLA_HEADS, MLA_NOPE + MLA_V)
    k_nope, v = kv[..., :MLA_NOPE], kv[..., MLA_NOPE:]
    k_rope = jnp.broadcast_to(latent[:, :, None, MLA_KV_LORA:], (b, L, MLA_HEADS, MLA_ROPE))
    k = rms_norm(jnp.concatenate([k_nope, k_rope], axis=-1), k_gain)
    return k, v


def even_mixer(h, pos, ctx_latent, w_in, conv_w, conv_b, kf, hy_bias, q_lora_gain, wq_b,
               kv_lora_gain, wkv_b, q_gain, k_gain, w_out):
    b, L, _ = h.shape
    proj = h @ w_in
    s1 = (HY_ORDER + 1) * HY_D
    s2 = s1 + MLA_Q_LORA
    u_hy, q_a, kv_a = proj[..., :s1], proj[..., s1:s2], proj[..., s2:]
    y_hy = hyena_mix(u_hy, conv_w, conv_b, kf, hy_bias)
    q = (rms_norm(q_a, q_lora_gain) @ wq_b).reshape(b, L, MLA_HEADS, MLA_QK)
    q = rms_norm(q, q_gain)
    latent = jnp.concatenate([rms_norm(kv_a[..., :MLA_KV_LORA], kv_lora_gain), kv_a[..., MLA_KV_LORA:]], axis=-1)
    k, v = mla_expand(latent, wkv_b, k_gain)
    if pos is not None:
        q = jnp.concatenate([q[..., :MLA_NOPE], axial_rope(q[..., MLA_NOPE:], pos[0], pos[1])], axis=-1)
        k = jnp.concatenate([k[..., :MLA_NOPE], axial_rope(k[..., MLA_NOPE:], pos[0], pos[1])], axis=-1)
    if ctx_latent is not None:
        kc, vc = mla_expand(ctx_latent, wkv_b, k_gain)
        k = jnp.concatenate([kc, k], axis=1)
        v = jnp.concatenate([vc, v], axis=1)
    o = attention(q, k, v).reshape(b, L, MLA_HEADS * MLA_V)
    y = jnp.concatenate([y_hy, o], axis=-1) @ w_out
    return y, latent


def odd_mixer(h, pos, ctx_k, ctx_v, w_qkv, q_gain, k_gain, w_out):
    b, L, _ = h.shape
    qkv = h @ w_qkv
    nq = GQA_HEADS * GQA_HEAD_DIM
    nk = GQA_KV_HEADS * GQA_HEAD_DIM
    q = rms_norm(qkv[..., :nq].reshape(b, L, GQA_HEADS, GQA_HEAD_DIM), q_gain)
    k = rms_norm(qkv[..., nq:nq + nk].reshape(b, L, GQA_KV_HEADS, GQA_HEAD_DIM), k_gain)
    v = qkv[..., nq + nk:].reshape(b, L, GQA_KV_HEADS, GQA_HEAD_DIM)
    k_plain, v_plain = k, v
    if pos is not None:
        q = axial_rope(q, pos[0], pos[1])
        k = axial_rope(k, pos[0], pos[1])
    if ctx_k is not None:
        k = jnp.concatenate([ctx_k, k], axis=1)
        v = jnp.concatenate([ctx_v, v], axis=1)
    o = attention(q, k, v).reshape(b, L, nq)
    return o @ w_out, k_plain, v_plain


def moe_ffn(x, w_router, b_router, w_gu, b_gu, w_down, b_down):
    n, d = x.shape
    logits = (x @ w_router + b_router).astype(jnp.float32)
    top_val, top_idx = lax.top_k(logits, TOP_K)
    gates = jax.nn.softmax(top_val, axis=-1).astype(x.dtype)
    nk = n * TOP_K
    flat_e = top_idx.reshape(-1)
    flat_tok = jnp.arange(nk, dtype=jnp.int32) // TOP_K
    flat_g = gates.reshape(-1)
    order = jnp.argsort(flat_e)
    sorted_e = flat_e[order]
    counts = jnp.bincount(flat_e, length=N_EXPERTS)
    padded = (counts + MOE_BLOCK - 1) // MOE_BLOCK * MOE_BLOCK
    start_sorted = jnp.cumsum(counts) - counts
    pad_end = jnp.cumsum(padded)
    start_padded = pad_end - padded
    dest = start_padded[sorted_e] + (jnp.arange(nk, dtype=jnp.int32) - start_sorted[sorted_e])
    n_blocks = -(-nk // MOE_BLOCK) + N_EXPERTS
    cap = n_blocks * MOE_BLOCK
    tok_buf = jnp.full((cap,), n, jnp.int32).at[dest].set(flat_tok[order])
    gate_buf = jnp.zeros((cap,), x.dtype).at[dest].set(flat_g[order])
    block_e = jnp.minimum(jnp.searchsorted(pad_end, jnp.arange(n_blocks) * MOE_BLOCK, side='right'), N_EXPERTS - 1)
    x_pad = jnp.concatenate([x, jnp.zeros((1, d), x.dtype)], axis=0)

    def expert_block(args):
        e, toks = args
        gu = x_pad[toks] @ w_gu[e] + b_gu[e]
        g, u = gu[:, :D_EXPERT], gu[:, D_EXPERT:]
        g = jnp.minimum(g, SWIGLU_LIMIT)
        u = jnp.clip(u, -SWIGLU_LIMIT, SWIGLU_LIMIT)
        act = (u + 1) * (g * jax.nn.sigmoid(SWIGLU_ALPHA * g))
        return act @ w_down[e] + b_down[e]

    y = lax.map(expert_block, (block_e, tok_buf.reshape(n_blocks, MOE_BLOCK)))
    out = jnp.zeros((n + 1, d), x.dtype).at[tok_buf].add(y.reshape(cap, d) * gate_buf[:, None])
    return out[:n]


def setup_inputs(seed: int = 0) -> dict:
    key = jax.random.key(seed)
    ks = iter(jax.random.split(key, 48))
    f32 = jnp.float32

    def nrm(shape, scale):
        return jax.random.normal(next(ks), shape, f32) * scale

    def gain(shape):
        return 1.0 + 0.01 * jax.random.normal(next(ks), shape, f32)

    D = D_MODEL
    return {
        'x_prompt': nrm((BATCH, SEQ, D), 1.0),
        'x_sample': nrm((DEC_BATCH, DEC_SEQ, D), 1.0),
        'cache_mla_latent': nrm((DEC_BATCH, N_EVEN, PAST_LEN, MLA_KV_LORA + MLA_ROPE), 1.0),
        'cache_gqa_k': nrm((DEC_BATCH, N_ODD, PAST_LEN, GQA_KV_HEADS, GQA_HEAD_DIM), 1.0),
        'cache_gqa_v': nrm((DEC_BATCH, N_ODD, PAST_LEN, GQA_KV_HEADS, GQA_HEAD_DIM), 1.0),
        'c': nrm((DEC_BATCH, D), 1.0),
        'c_ctx': nrm((D,), 1.0),
        'w_ada': nrm((DEPTH, D, 6 * D), 0.5 * D ** -0.5),
        'b_ada': nrm((DEPTH, 6 * D), 0.01),
        'norm_mix': gain((DEPTH, D)),
        'norm_ffn': gain((DEPTH, D)),
        'w_in_ab': nrm((N_EVEN, D, IN_AB), D ** -0.5),
        'hy_conv_w': nrm((N_EVEN, SHORT_CONV, (HY_ORDER + 1) * HY_D), SHORT_CONV ** -0.5),
        'hy_conv_b': nrm((N_EVEN, (HY_ORDER + 1) * HY_D), 0.01),
        'hy_filter_w1': nrm((N_EVEN, HY_EMB, HY_FILTER_HIDDEN), HY_EMB ** -0.5),
        'hy_filter_b1': nrm((N_EVEN, HY_FILTER_HIDDEN), 0.01),
        'hy_filter_freq': 1.0 + 0.1 * jax.random.normal(next(ks), (N_EVEN, HY_FILTER_HIDDEN), f32),
        'hy_filter_w2': nrm((N_EVEN, HY_FILTER_HIDDEN, HY_FILTER_HIDDEN), HY_FILTER_HIDDEN ** -0.5),
        'hy_filter_b2': nrm((N_EVEN, HY_FILTER_HIDDEN), 0.01),
        'hy_filter_w3': nrm((N_EVEN, HY_FILTER_HIDDEN, HY_ORDER * 2 * HY_D), HY_FILTER_HIDDEN ** -0.5),
        'hy_filter_b3': nrm((N_EVEN, HY_ORDER * 2 * HY_D), 0.01),
        'hy_log_decay': jax.random.uniform(next(ks), (N_EVEN, HY_ORDER, 2, HY_D), f32, math.log(3.0), math.log(15.0)),
        'hy_bias': nrm((N_EVEN, HY_ORDER, HY_D), 0.5),
        'mla_q_lora_norm': gain((N_EVEN, MLA_Q_LORA)),
        'mla_wq_b': nrm((N_EVEN, MLA_Q_LORA, MLA_HEADS * MLA_QK), MLA_Q_LORA ** -0.5),
        'mla_kv_lora_norm': gain((N_EVEN, MLA_KV_LORA)),
        'mla_wkv_b': nrm((N_EVEN, MLA_KV_LORA, MLA_HEADS * (MLA_NOPE + MLA_V)), MLA_KV_LORA ** -0.5),
        'mla_q_norm': gain((N_EVEN, MLA_QK)),
        'mla_k_norm': gain((N_EVEN, MLA_QK)),
        'w_out_ab': nrm((N_EVEN, OUT_AB, D), OUT_AB ** -0.5),
        'w_qkv_c': nrm((N_ODD, D, QKV_C), D ** -0.5),
        'gqa_q_norm': gain((N_ODD, GQA_HEAD_DIM)),
        'gqa_k_norm': gain((N_ODD, GQA_HEAD_DIM)),
        'w_out_c': nrm((N_ODD, GQA_HEADS * GQA_HEAD_DIM, D), (GQA_HEADS * GQA_HEAD_DIM) ** -0.5),
        'moe_router_w': nrm((DEPTH, D, N_EXPERTS), D ** -0.5),
        'moe_router_b': nrm((DEPTH, N_EXPERTS), 0.01),
        'moe_w_gate_up': nrm((DEPTH, N_EXPERTS, D, 2 * D_EXPERT), D ** -0.5),
        'moe_b_gate_up': nrm((DEPTH, N_EXPERTS, 2 * D_EXPERT), 0.01),
        'moe_w_down': nrm((DEPTH, N_EXPERTS, D_EXPERT, D), D_EXPERT ** -0.5),
        'moe_b_down': nrm((DEPTH, N_EXPERTS, D), 0.01),
    }


def reference(x_prompt, x_sample, cache_mla_latent, cache_gqa_k, cache_gqa_v, c, c_ctx,
              w_ada, b_ada, norm_mix, norm_ffn, w_in_ab, hy_conv_w, hy_conv_b,
              hy_filter_w1, hy_filter_b1, hy_filter_freq, hy_filter_w2, hy_filter_b2,
              hy_filter_w3, hy_filter_b3, hy_log_decay, hy_bias, mla_q_lora_norm, mla_wq_b,
              mla_kv_lora_norm, mla_wkv_b, mla_q_norm, mla_k_norm, w_out_ab, w_qkv_c,
              gqa_q_norm, gqa_k_norm, w_out_c, moe_router_w, moe_router_b, moe_w_gate_up,
              moe_b_gate_up, moe_w_down, moe_b_down):

    def run_trunk(x, cond, pos, ctx_mla, ctx_k, ctx_v):
        seq_len = x.shape[1]
        lat_list, k_list, v_list = [], [], []
        for l in range(DEPTH):
            mod = jax.nn.silu(cond) @ w_ada[l] + b_ada[l]
            sh1, sc1, g1, sh2, sc2, g2 = jnp.split(mod[:, None, :], 6, axis=-1)
            h = rms_norm(x, norm_mix[l]) * (1 + sc1) + sh1
            i = l // 2
            if l % 2 == 0:
                kf = hyena_filter_spectrum(seq_len, hy_filter_w1[i], hy_filter_b1[i], hy_filter_freq[i],
                                           hy_filter_w2[i], hy_filter_b2[i], hy_filter_w3[i],
                                           hy_filter_b3[i], hy_log_decay[i])
                y, lat = even_mixer(h, pos, None if ctx_mla is None else ctx_mla[:, i], w_in_ab[i],
                                    hy_conv_w[i], hy_conv_b[i], kf, hy_bias[i], mla_q_lora_norm[i],
                                    mla_wq_b[i], mla_kv_lora_norm[i], mla_wkv_b[i], mla_q_norm[i],
                                    mla_k_norm[i], w_out_ab[i])
                lat_list.append(lat)
            else:
                y, k_new, v_new = odd_mixer(h, pos, None if ctx_k is None else ctx_k[:, i],
                                            None if ctx_v is None else ctx_v[:, i], w_qkv_c[i],
                                            gqa_q_norm[i], gqa_k_norm[i], w_out_c[i])
                k_list.append(k_new)
                v_list.append(v_new)
            x = x + g1 * y
            h = rms_norm(x, norm_ffn[l]) * (1 + sc2) + sh2
            ff = moe_ffn(h.reshape(-1, D_MODEL), moe_router_w[l], moe_router_b[l], moe_w_gate_up[l],
                         moe_b_gate_up[l], moe_w_down[l], moe_b_down[l]).reshape(x.shape)
            x = x + g2 * ff
        return x, lat_list, k_list, v_list

    y_prompt, lat_p, k_p, v_p = run_trunk(x_prompt, c_ctx[None, :], None, None, None, None)
    new_mla_latent = jnp.stack(lat_p, axis=1)
    new_gqa_k = jnp.stack(k_p, axis=1)
    new_gqa_v = jnp.stack(v_p, axis=1)

    n_rows = x_sample.shape[1] // GRID_W
    rows = jnp.repeat(jnp.arange(n_rows, dtype=jnp.int32), GRID_W)
    cols = jnp.tile(jnp.arange(GRID_W, dtype=jnp.int32), n_rows)
    y_sample, _, _, _ = run_trunk(x_sample, c, (rows, cols), cache_mla_latent, cache_gqa_k, cache_gqa_v)

    return (y_prompt, y_sample, new_mla_latent, new_gqa_k, new_gqa_v)
```

```python
import functools
import math

import jax
import jax.numpy as jnp
from jax import lax
from jax.experimental import pallas as pl
from jax.experimental.pallas import tpu as pltpu

F32 = jnp.float32
BF16 = jnp.bfloat16

D_MODEL = 1024
BATCH = 32
SEQ = 256
DEPTH = 4
DEC_BATCH = 4
DEC_SEQ = 1024
PAST_LEN = 256
GRID_W = 64
N_EVEN = (DEPTH + 1) // 2
N_ODD = DEPTH // 2
HY_D = D_MODEL // 2
HY_ORDER = 2
HY_BANDS = 16
HY_EMB = 2 * HY_BANDS + 1
HY_FILTER_HIDDEN = 64
MLA_HEADS = 8
MLA_NOPE = 64
MLA_ROPE = 32
MLA_QK = MLA_NOPE + MLA_ROPE
MLA_V = HY_D // MLA_HEADS
MLA_Q_LORA = 3 * D_MODEL // 8
MLA_KV_LORA = D_MODEL // 4
GQA_HEADS = 16
GQA_KV_HEADS = 4
GQA_HEAD_DIM = D_MODEL // GQA_HEADS
N_EXPERTS = 32
TOP_K = 4
D_EXPERT = D_MODEL
SWIGLU_LIMIT = 7.0
SWIGLU_ALPHA = 1.702
ROPE_THETA = 10000.0
RMS_EPS = 1e-6
IN_AB = (HY_ORDER + 1) * HY_D + MLA_Q_LORA + MLA_KV_LORA + MLA_ROPE

N_PROMPT = BATCH * SEQ
N_SAMPLE = DEC_BATCH * DEC_SEQ
N_TOK = N_PROMPT + N_SAMPLE

LANES = 128
SUBLANES = 8
VMEM_LIMIT_BYTES = 56 * 1024 * 1024

TM = 256
NT = N_TOK // TM
NT_PROMPT = N_PROMPT // TM
TILES_PER_SAMPLE = DEC_SEQ // TM
N_COND = 1 + DEC_BATCH
COND_ROWS = 8
HEAD_PAD = LANES
IN_AB_PAD = 2304
KV_PAD = 384
MOE_BLK = 256
N_SLOTS = N_TOK * TOP_K
N_BLOCKS = N_SLOTS // MOE_BLK + N_EXPERTS
CAP = N_BLOCKS * MOE_BLK
HY_CH = 256


def _dot(a, b):
    return jnp.dot(a, b, preferred_element_type=F32)


def _dot_nt(a, b):
    return lax.dot_general(a, b, (((1,), (1,)), ((), ())), preferred_element_type=F32)


def _split(x):
    hi = x.astype(BF16)
    lo = (x - hi.astype(F32)).astype(BF16)
    return hi, lo


def _dot3(a, b):
    ah, al = _split(a)
    bh, bl = _split(b)
    return _dot(ah, bh) + (_dot(ah, bl) + _dot(al, bh))


def _lane_iota(shape):
    return lax.broadcasted_iota(jnp.int32, shape, len(shape) - 1)


def _params(*sem, vmem=None):
    return pltpu.CompilerParams(dimension_semantics=sem, vmem_limit_bytes=vmem)


def _mod_kernel(c_ref, w_ref, b_ref, o_ref):
    c = c_ref[...]
    s = c * jax.nn.sigmoid(c)
    o_ref[0] = _dot(s.astype(BF16), w_ref[0].astype(BF16)) + b_ref[0]


def _modulation(cond, w_ada, b_ada):
    nblk = 6
    return pl.pallas_call(
        _mod_kernel,
        grid=(DEPTH, nblk),
        in_specs=[
            pl.BlockSpec((COND_ROWS, D_MODEL), lambda l, j: (0, 0)),
            pl.BlockSpec((1, D_MODEL, D_MODEL), lambda l, j: (l, 0, j)),
            pl.BlockSpec((1, 1, D_MODEL), lambda l, j: (l, 0, j)),
        ],
        out_specs=pl.BlockSpec((1, COND_ROWS, D_MODEL), lambda l, j: (l, 0, j)),
        out_shape=jax.ShapeDtypeStruct((DEPTH, COND_ROWS, nblk * D_MODEL), F32),
        compiler_params=_params("arbitrary", "arbitrary"),
        name="modulation",
    )(cond, w_ada, b_ada.reshape(DEPTH, 1, nblk * D_MODEL))


def _norm_mod(x, g, sc, sh):
    ms = jnp.mean(x * x, axis=-1, keepdims=True)
    y = x * lax.rsqrt(ms + RMS_EPS)
    return (y * g) * (1.0 + sc) + sh


def _row_spec(width):
    return pl.BlockSpec((TM, width), lambda i: (i, 0))


def _tile_vec_spec():
    return pl.BlockSpec((1, 1, D_MODEL), lambda i: (i, 0, 0))


def _full_spec(shape):
    nd = len(shape)
    return pl.BlockSpec(shape, lambda i: (0,) * nd)


def _normlin_kernel(x_ref, g_ref, sc_ref, sh_ref, w_ref, o_ref):
    h = _norm_mod(x_ref[...], g_ref[...], sc_ref[0], sh_ref[0])
    o_ref[...] = _dot(h.astype(BF16), w_ref[...])


def _normlin(x, g, sc, sh, w):
    nout = w.shape[1]
    return pl.pallas_call(
        _normlin_kernel,
        grid=(NT,),
        in_specs=[_row_spec(D_MODEL), _full_spec((1, D_MODEL)), _tile_vec_spec(), _tile_vec_spec(),
                  _full_spec((D_MODEL, nout))],
        out_specs=_row_spec(nout),
        out_shape=jax.ShapeDtypeStruct((N_TOK, nout), F32),
        compiler_params=_params("arbitrary", vmem=VMEM_LIMIT_BYTES),
        name="normlin",
    )(x, g, sc, sh, w)


def _hy_filter_kernel(z_ref, w1_ref, b1_ref, fr_ref, w2_ref, b2_ref, w3_ref, b3_ref, ed_ref,
                      c_ref, s_ref, kc_ref, ks_ref):
    L = z_ref.shape[0]
    z = z_ref[...]
    fr = fr_ref[...]
    hdn = jnp.sin(fr * (_dot3(z, w1_ref[...]) + b1_ref[...]))
    hdn = jnp.sin(fr * (_dot3(hdn, w2_ref[...]) + b2_ref[...]))
    filt = _dot3(hdn, w3_ref[...]) + b3_ref[...]
    t = z[:, 0:1]
    filt = filt * jnp.exp(-t * ed_ref[...])
    row = lax.broadcasted_iota(jnp.int32, (L, HY_D), 0)
    cm = c_ref[...]
    sm = s_ref[...]
    for o in range(HY_ORDER):
        fw = filt[:, (2 * o) * HY_D:(2 * o + 1) * HY_D]
        bw = filt[:, (2 * o + 1) * HY_D:(2 * o + 2) * HY_D]
        den = (jnp.sum(jnp.abs(fw), axis=0, keepdims=True)
               + jnp.sum(jnp.abs(bw), axis=0, keepdims=True)) + 1e-6
        fw = fw / den
        bw = jnp.where(row == 0, 0.0, bw / den)
        kc_ref[o] = _dot3(cm, fw + bw) * (1.0 / L)
        ks_ref[o] = _dot3(sm, fw - bw) * (1.0 / L)


def _dft_tables(L):
    m = jnp.arange(L, dtype=jnp.int32)
    phase = ((2 * m[:, None] + 1) * m[None, :]) % (4 * L)
    ang = phase.astype(F32) * (2.0 * math.pi / (4 * L))
    return jnp.cos(ang), jnp.sin(ang)


def _filter_features(L):
    p = jnp.arange(L, dtype=F32)
    t = p / max(L - 1, 1)
    bands = jnp.linspace(1e-4, HY_BANDS - 1, HY_BANDS, dtype=F32)
    ang = (2.0 * math.pi / L) * p[:, None] * bands[None, :]
    z = jnp.concatenate([t[:, None], jnp.cos(ang), -jnp.sin(ang)], axis=-1)
    return jnp.pad(z, ((0, 0), (0, LANES - HY_EMB)))


def _hy_filter(L, cmat, smat, w1, b1, fr, w2, b2, w3, b3, log_decay):
    nf = HY_ORDER * 2 * HY_D
    args = (
        _filter_features(L),
        jnp.pad(w1, ((0, LANES - HY_EMB), (0, 0))),
        b1.reshape(1, -1), fr.reshape(1, -1), w2, b2.reshape(1, -1), w3, b3.reshape(1, -1),
        jnp.exp(log_decay.astype(F32)).reshape(1, nf),
        cmat, smat,
    )
    out_sds = jax.ShapeDtypeStruct((HY_ORDER, L, HY_D), F32)
    return pl.pallas_call(
        _hy_filter_kernel,
        grid=(1,),
        in_specs=[_full_spec(a.shape) for a in args],
        out_specs=[_full_spec(out_sds.shape)] * 2,
        out_shape=[out_sds, out_sds],
        compiler_params=_params("arbitrary", vmem=VMEM_LIMIT_BYTES),
        name=f"hy_filter_{L}",
    )(*args)


def _hy_mix_kernel(u0_ref, u1_ref, u2_ref, cw0_ref, cw1_ref, cw2_ref, cb0_ref, cb1_ref, cb2_ref,
                   kc_ref, ks_ref, hb_ref, f_ref, ft_ref, o_ref):
    L = u0_ref.shape[0]
    row = lax.broadcasted_iota(jnp.int32, (L, HY_CH), 0)

    def short_conv(u_ref, cw_ref, cb_ref):
        u = u_ref[...]
        w = cw_ref[0]
        prev = jnp.where(row == 0, 0.0, pltpu.roll(u, 1, 0))
        nxt = jnp.where(row == L - 1, 0.0, pltpu.roll(u, L - 1, 0))
        return (prev * w[0:1] + u * w[1:2]) + nxt * w[2:3] + cb_ref[0]

    z = short_conv(u0_ref, cw0_ref, cb0_ref)
    gates = (short_conv(u1_ref, cw1_ref, cb1_ref), short_conv(u2_ref, cw2_ref, cb2_ref))
    for o in range(HY_ORDER):
        zz = _dot(f_ref[...], z.astype(BF16))
        cz, sz = zz[:L], zz[L:]
        kc, ks = kc_ref[o], ks_ref[o]
        w1 = cz * kc - sz * ks
        w2 = cz * ks + sz * kc
        ww = jnp.concatenate([w1, w2], axis=0).astype(BF16)
        conv = _dot(ft_ref[...], ww)
        z = gates[o] * (conv + z * hb_ref[0, o])
    o_ref[...] = z.astype(o_ref.dtype)


def _hy_mix(proj, row_block0, nseq, L, cw, cb, kc, ks, hb, fmat, ftmat):
    nch = HY_D // HY_CH
    nparts = HY_ORDER + 1
    cw3 = cw.reshape(3, nparts * nch, HY_CH).transpose(1, 0, 2)
    cb3 = cb.reshape(nparts * nch, 1, HY_CH)
    hb3 = hb.reshape(HY_ORDER, nch, 1, HY_CH).transpose(1, 0, 2, 3)

    def u_spec(part):
        return pl.BlockSpec((L, HY_CH), lambda s, c: (row_block0 + s, part * nch + c))

    def cw_spec(part):
        return pl.BlockSpec((1, 3, HY_CH), lambda s, c: (part * nch + c, 0, 0))

    def cb_spec(part):
        return pl.BlockSpec((1, 1, HY_CH), lambda s, c: (part * nch + c, 0, 0))

    return pl.pallas_call(
        _hy_mix_kernel,
        grid=(nseq, nch),
        in_specs=[
            u_spec(0), u_spec(1), u_spec(2),
            cw_spec(0), cw_spec(1), cw_spec(2),
            cb_spec(0), cb_spec(1), cb_spec(2),
            pl.BlockSpec((HY_ORDER, L, HY_CH), lambda s, c: (0, 0, c)),
            pl.BlockSpec((HY_ORDER, L, HY_CH), lambda s, c: (0, 0, c)),
            pl.BlockSpec((1, HY_ORDER, 1, HY_CH), lambda s, c: (c, 0, 0, 0)),
            pl.BlockSpec((2 * L, L), lambda s, c: (0, 0)),
            pl.BlockSpec((L, 2 * L), lambda s, c: (0, 0)),
        ],
        out_specs=pl.BlockSpec((L, HY_CH), lambda s, c: (s, c)),
        out_shape=jax.ShapeDtypeStruct((nseq * L, HY_D), BF16),
        compiler_params=_params("arbitrary", "arbitrary", vmem=VMEM_LIMIT_BYTES),
        name=f"hy_mix_{L}",
    )(proj, proj, proj, cw3, cw3, cw3, cb3, cb3, cb3, kc, ks, hb3, fmat, ftmat)


def _head_rms(xh, gain, dim):
    ms = jnp.sum(xh * xh, axis=-1, keepdims=True) * (1.0 / dim)
    return (xh * lax.rsqrt(ms + RMS_EPS)) * gain


def _rope(xh, cos, sin, half):
    lane = _lane_iota(xh.shape)
    first = (lane % (2 * half)) < half
    rot = jnp.where(first, pltpu.roll(xh, LANES - half, 1), pltpu.roll(xh, half, 1))
    return xh * cos + rot * sin


def _mla_keys_values(lat_n, kr_blk, wk_ref, wv_ref, kg_ref, rope):
    lb = lat_n.astype(BF16)
    kn = _dot(lb, wk_ref[...])
    v = _dot(lb, wv_ref[...])
    kr = pltpu.roll(kr_blk, MLA_NOPE, 1)
    ks = []
    for h in range(MLA_HEADS):
        kh = kn[:, h * HEAD_PAD:(h + 1) * HEAD_PAD] + kr
        kh = _head_rms(kh, kg_ref[...], MLA_QK)
        if rope is not None:
            kh = _rope(kh, rope[0], rope[1], MLA_ROPE // 4)
        ks.append(kh)
    return ks, v


def _mla_prep_kernel(qa_ref, kva_ref, gq_ref, wq_ref, qg_ref, gkv_ref, wk_ref, wv_ref, kg_ref,
                     cos_ref, sin_ref, q_ref, k_ref, v_ref, lat_ref):
    rope = (cos_ref[...], sin_ref[...])
    qa = qa_ref[...]
    ms = jnp.mean(qa * qa, axis=-1, keepdims=True)
    qn = (qa * lax.rsqrt(ms + RMS_EPS)) * gq_ref[...]
    q = _dot(qn.astype(BF16), wq_ref[...])
    for h in range(MLA_HEADS):
        qh = _head_rms(q[:, h * HEAD_PAD:(h + 1) * HEAD_PAD], qg_ref[...], MLA_QK)
        q_ref[:, h * HEAD_PAD:(h + 1) * HEAD_PAD] = _rope(qh, rope[0], rope[1], MLA_ROPE // 4).astype(BF16)

    kva = kva_ref[...]
    lat = kva[:, :MLA_KV_LORA]
    ms = jnp.mean(lat * lat, axis=-1, keepdims=True)
    lat_n = (lat * lax.rsqrt(ms + RMS_EPS)) * gkv_ref[...]
    kr_blk = kva[:, MLA_KV_LORA:]
    lat_ref[:, :MLA_KV_LORA] = lat_n
    lat_ref[:, MLA_KV_LORA:] = kr_blk[:, :MLA_ROPE]
    ks, v = _mla_keys_values(lat_n, kr_blk, wk_ref, wv_ref, kg_ref, rope)
    for h in range(MLA_HEADS):
        k_ref[:, h * HEAD_PAD:(h + 1) * HEAD_PAD] = ks[h].astype(BF16)
    v_ref[...] = v.astype(BF16)


def _mla_prep(proj, gq, wq, qg, gkv, wk, wv, kg, cos_t, sin_t):
    nq = MLA_HEADS * HEAD_PAD
    nv = MLA_HEADS * MLA_V
    qa_blk = (HY_ORDER + 1) * HY_D // KV_PAD
    return pl.pallas_call(
        _mla_prep_kernel,
        grid=(NT,),
        in_specs=[
            pl.BlockSpec((TM, KV_PAD), lambda i: (i, qa_blk)),
            pl.BlockSpec((TM, KV_PAD), lambda i: (i, qa_blk + 1)),
            _full_spec(gq.shape), _full_spec(wq.shape), _full_spec(qg.shape), _full_spec(gkv.shape),
            _full_spec(wk.shape), _full_spec(wv.shape), _full_spec(kg.shape),
            _row_spec(LANES), _row_spec(LANES),
        ],
        out_specs=[_row_spec(nq), _row_spec(nq), _row_spec(nv), _row_spec(MLA_KV_LORA + MLA_ROPE)],
        out_shape=[
            jax.ShapeDtypeStruct((N_TOK, nq), BF16),
            jax.ShapeDtypeStruct((N_TOK, nq), BF16),
            jax.ShapeDtypeStruct((N_TOK, nv), BF16),
            jax.ShapeDtypeStruct((N_TOK, MLA_KV_LORA + MLA_ROPE), F32),
        ],
        compiler_params=_params("arbitrary", vmem=VMEM_LIMIT_BYTES),
        name="mla_prep",
    )(proj, proj, gq, wq, qg, gkv, wk, wv, kg, cos_t, sin_t)


def _mla_ctx_kernel(lat_ref, wk_ref, wv_ref, kg_ref, k_ref, v_ref):
    lat = lat_ref[...]
    ks, v = _mla_keys_values(lat[:, :MLA_KV_LORA], lat[:, MLA_KV_LORA:], wk_ref, wv_ref, kg_ref, None)
    for h in range(MLA_HEADS):
        k_ref[:, h * HEAD_PAD:(h + 1) * HEAD_PAD] = ks[h].astype(BF16)
    v_ref[...] = v.astype(BF16)


def _mla_ctx(lat_pad, wk, wv, kg):
    n = lat_pad.shape[0]
    nq = MLA_HEADS * HEAD_PAD
    nv = MLA_HEADS * MLA_V
    return pl.pallas_call(
        _mla_ctx_kernel,
        grid=(n // TM,),
        in_specs=[_row_spec(KV_PAD), _full_spec(wk.shape), _full_spec(wv.shape), _full_spec(kg.shape)],
        out_specs=[_row_spec(nq), _row_spec(nv)],
        out_shape=[jax.ShapeDtypeStruct((n, nq), BF16), jax.ShapeDtypeStruct((n, nv), BF16)],
        compiler_params=_params("arbitrary"),
        name="mla_ctx",
    )(lat_pad, wk, wv, kg)


def _softmax_pv(q, keys, vals, scale):
    ss = [_dot_nt(q, k) * scale for k in keys]
    m = ss[0].max(axis=-1, keepdims=True)
    for s in ss[1:]:
        m = jnp.maximum(m, s.max(axis=-1, keepdims=True))
    ps = [jnp.exp(s - m) for s in ss]
    l = ps[0].sum(axis=-1, keepdims=True)
    for p in ps[1:]:
        l = l + p.sum(axis=-1, keepdims=True)
    o = _dot(ps[0].astype(BF16), vals[0])
    for p, v in zip(ps[1:], vals[1:]):
        o = o + _dot(p.astype(BF16), v)
    return o / l


def _mla_attn_kernel(*refs, has_ctx):
    if has_ctx:
        q_ref, k_ref, v_ref, kc_ref, vc_ref, o_ref = refs
    else:
        q_ref, k_ref, v_ref, o_ref = refs
    scale = MLA_QK ** -0.5
    lo = _lane_iota((q_ref.shape[0], LANES)) < MLA_V
    for j in range(MLA_HEADS // 2):
        vs = [v_ref[:, j * LANES:(j + 1) * LANES]]
        if has_ctx:
            vs.append(vc_ref[:, j * LANES:(j + 1) * LANES])
        outs = []
        for h in (2 * j, 2 * j + 1):
            sl = slice(h * HEAD_PAD, (h + 1) * HEAD_PAD)
            ks = [k_ref[:, sl]]
            if has_ctx:
                ks.append(kc_ref[:, sl])
            outs.append(_softmax_pv(q_ref[:, sl], ks, vs, scale))
        o_ref[:, j * LANES:(j + 1) * LANES] = jnp.where(lo, outs[0], outs[1]).astype(o_ref.dtype)


def _gqa_attn_kernel(*refs, has_ctx):
    if has_ctx:
        q_ref, k_ref, v_ref, kc_ref, vc_ref, o_ref = refs
    else:
        q_ref, k_ref, v_ref, o_ref = refs
    scale = GQA_HEAD_DIM ** -0.5
    lo = _lane_iota((q_ref.shape[0], LANES)) < GQA_HEAD_DIM
    pairs_per_kv = (GQA_HEADS // 2) // (GQA_KV_HEADS // 2)
    for p in range(GQA_HEADS // 2):
        kv = slice((p // pairs_per_kv) * LANES, (p // pairs_per_kv + 1) * LANES)
        ks, vs = [k_ref[:, kv]], [v_ref[:, kv]]
        if has_ctx:
            ks.append(kc_ref[:, kv])
            vs.append(vc_ref[:, kv])
        qp = q_ref[:, p * LANES:(p + 1) * LANES]
        zero = jnp.zeros_like(qp)
        o_lo = _softmax_pv(jnp.where(lo, qp, zero), ks, vs, scale)
        o_hi = _softmax_pv(jnp.where(lo, zero, qp), ks, vs, scale)
        o_ref[:, p * LANES:(p + 1) * LANES] = jnp.where(lo, o_lo, o_hi).astype(o_ref.dtype)


def _attention(body, q, k, v, kc, vc, wq, wk, wv, wo):
    outs = []
    outs.append(pl.pallas_call(
        functools.partial(body, has_ctx=False),
        grid=(BATCH,),
        in_specs=[pl.BlockSpec((SEQ, wq), lambda b: (b, 0)),
                  pl.BlockSpec((SEQ, wk), lambda b: (b, 0)),
                  pl.BlockSpec((SEQ, wv), lambda b: (b, 0))],
        out_specs=pl.BlockSpec((SEQ, wo), lambda b: (b, 0)),
        out_shape=jax.ShapeDtypeStruct((N_PROMPT, wo), BF16),
        compiler_params=_params("arbitrary", vmem=VMEM_LIMIT_BYTES),
        name="attn_prompt",
    )(q, k, v))
    qt = DEC_SEQ // TM
    q0 = N_PROMPT // TM
    s0 = N_PROMPT // DEC_SEQ
    outs.append(pl.pallas_call(
        functools.partial(body, has_ctx=True),
        grid=(DEC_BATCH, qt),
        in_specs=[pl.BlockSpec((TM, wq), lambda b, t: (q0 + b * qt + t, 0)),
                  pl.BlockSpec((DEC_SEQ, wk), lambda b, t: (s0 + b, 0)),
                  pl.BlockSpec((DEC_SEQ, wv), lambda b, t: (s0 + b, 0)),
                  pl.BlockSpec((PAST_LEN, wk), lambda b, t: (b, 0)),
                  pl.BlockSpec((PAST_LEN, wv), lambda b, t: (b, 0))],
        out_specs=pl.BlockSpec((TM, wo), lambda b, t: (b * qt + t, 0)),
        out_shape=jax.ShapeDtypeStruct((N_SAMPLE, wo), BF16),
        compiler_params=_params("arbitrary", "arbitrary", vmem=VMEM_LIMIT_BYTES),
        name="attn_sample",
    )(q, k, v, kc, vc))
    return jnp.concatenate(outs, axis=0)


def _gqa_prep_kernel(x_ref, g_ref, sc_ref, sh_ref, w_ref, qg_ref, kg_ref, cos_ref, sin_ref,
                     q_ref, k_ref, v_ref, kp_ref, vp_ref):
    h = _norm_mod(x_ref[...], g_ref[...], sc_ref[0], sh_ref[0])
    qkv = _dot(h.astype(BF16), w_ref[...])
    cos, sin = cos_ref[...], sin_ref[...]
    lo = _lane_iota((TM, LANES)) < GQA_HEAD_DIM
    nq = GQA_HEADS * GQA_HEAD_DIM
    nk = GQA_KV_HEADS * GQA_HEAD_DIM

    def pair_norm(xp, gain):
        sq = xp * xp
        ms_lo = jnp.sum(jnp.where(lo, sq, 0.0), axis=-1, keepdims=True)
        ms_hi = jnp.sum(jnp.where(lo, 0.0, sq), axis=-1, keepdims=True)
        ms = jnp.where(lo, ms_lo, ms_hi) * (1.0 / GQA_HEAD_DIM)
        return (xp * lax.rsqrt(ms + RMS_EPS)) * gain

    for p in range(nq // LANES):
        sl = slice(p * LANES, (p + 1) * LANES)
        qn = pair_norm(qkv[:, sl], qg_ref[...])
        q_ref[:, sl] = _rope(qn, cos, sin, GQA_HEAD_DIM // 4).astype(BF16)
    for p in range(nk // LANES):
        sl = slice(p * LANES, (p + 1) * LANES)
        kn = pair_norm(qkv[:, nq + p * LANES:nq + (p + 1) * LANES], kg_ref[...])
        kp_ref[:, sl] = kn
        k_ref[:, sl] = _rope(kn, cos, sin, GQA_HEAD_DIM // 4).astype(BF16)
    v = qkv[:, nq + nk:]
    vp_ref[...] = v
    v_ref[...] = v.astype(BF16)


def _gqa_prep(x, g, sc, sh, w, qg, kg, cos_t, sin_t):
    nq = GQA_HEADS * GQA_HEAD_DIM
    nk = GQA_KV_HEADS * GQA_HEAD_DIM
    return pl.pallas_call(
        _gqa_prep_kernel,
        grid=(NT,),
        in_specs=[_row_spec(D_MODEL), _full_spec((1, D_MODEL)), _tile_vec_spec(), _tile_vec_spec(),
                  _full_spec(w.shape), _full_spec(qg.shape), _full_spec(kg.shape),
                  _row_spec(LANES), _row_spec(LANES)],
        out_specs=[_row_spec(nq), _row_spec(nk), _row_spec(nk), _row_spec(nk), _row_spec(nk)],
        out_shape=[
            jax.ShapeDtypeStruct((N_TOK, nq), BF16),
            jax.ShapeDtypeStruct((N_TOK, nk), BF16),
            jax.ShapeDtypeStruct((N_TOK, nk), BF16),
            jax.ShapeDtypeStruct((N_TOK, nk), F32),
            jax.ShapeDtypeStruct((N_TOK, nk), F32),
        ],
        compiler_params=_params("arbitrary", vmem=VMEM_LIMIT_BYTES),
        name="gqa_prep",
    )(x, g, sc, sh, w, qg, kg, cos_t, sin_t)


def _outproj_kernel(*refs, n_in):
    a_refs = refs[:n_in]
    w_refs = refs[n_in:2 * n_in]
    x_ref, g_ref, o_ref = refs[2 * n_in:]
    y = _dot(a_refs[0][...], w_refs[0][...])
    for a, w in zip(a_refs[1:], w_refs[1:]):
        y = y + _dot(a[...], w[...])
    o_ref[...] = x_ref[...] + g_ref[0] * y


def _outproj(acts, ws, x, gate):
    n_in = len(acts)
    return pl.pallas_call(
        functools.partial(_outproj_kernel, n_in=n_in),
        grid=(NT,),
        in_specs=([_row_spec(a.shape[1]) for a in acts] + [_full_spec(w.shape) for w in ws]
                  + [_row_spec(D_MODEL), _tile_vec_spec()]),
        out_specs=_row_spec(D_MODEL),
        out_shape=jax.ShapeDtypeStruct((N_TOK, D_MODEL), F32),
        compiler_params=_params("arbitrary", vmem=VMEM_LIMIT_BYTES),
        name="outproj",
    )(*acts, *ws, x, gate)


def _router_kernel(x_ref, g_ref, sc_ref, sh_ref, wh_ref, wl_ref, br_ref, tri_ref,
                   h_ref, idx_ref, gate_ref, pos_ref, cnt_ref, run_ref):
    @pl.when(pl.program_id(0) == 0)
    def _():
        run_ref[...] = jnp.zeros_like(run_ref)

    h = _norm_mod(x_ref[...], g_ref[...], sc_ref[0], sh_ref[0])
    h_ref[...] = h
    hh, hl = _split(h)
    logits = _dot(hh, wh_ref[...]) + (_dot(hh, wl_ref[...]) + _dot(hl, wh_ref[...])) + br_ref[...]
    lane = _lane_iota((TM, LANES)).astype(F32)
    neg = jnp.float32(-jnp.inf)
    lg = jnp.where(lane < N_EXPERTS, logits, neg)
    tops, sels, hots = [], [], []
    for _ in range(TOP_K):
        m = lg.max(axis=-1, keepdims=True)
        sel = jnp.where(lg == m, lane, float(LANES)).min(axis=-1, keepdims=True)
        hot = lane == sel
        lg = jnp.where(hot, neg, lg)
        tops.append(m)
        sels.append(sel)
        hots.append(hot)
    es = [jnp.exp(t - tops[0]) for t in tops]
    den = es[0] + es[1] + es[2] + es[3]
    member = jnp.zeros((TM, LANES), F32)
    for hot in hots:
        member = member + hot.astype(F32)
    ranks = _dot(tri_ref[...], member.astype(BF16)) + run_ref[...]
    lane4 = _lane_iota((TM, TOP_K))
    idx4 = jnp.zeros((TM, TOP_K), F32)
    gate4 = jnp.zeros((TM, TOP_K), F32)
    pos4 = jnp.zeros((TM, TOP_K), F32)
    for k in range(TOP_K):
        pk = jnp.sum(jnp.where(hots[k], ranks, 0.0), axis=-1, keepdims=True)
        idx4 = jnp.where(lane4 == k, sels[k], idx4)
        gate4 = jnp.where(lane4 == k, es[k] / den, gate4)
        pos4 = jnp.where(lane4 == k, pk, pos4)
    idx_ref[...] = idx4.astype(jnp.int32)
    gate_ref[...] = gate4
    pos_ref[...] = pos4.astype(jnp.int32)
    run_ref[...] = run_ref[...] + jnp.sum(member, axis=0, keepdims=True)
    cnt_ref[...] = run_ref[...]


def _router(x, g, sc, sh, w_router, b_router):
    wpad = jnp.pad(w_router, ((0, 0), (0, LANES - N_EXPERTS)))
    wh = wpad.astype(BF16)
    wl = (wpad - wh.astype(F32)).astype(BF16)
    bpad = jnp.pad(b_router, (0, LANES - N_EXPERTS)).reshape(1, LANES)
    r = jnp.arange(TM)
    tri = (r[None, :] < r[:, None]).astype(BF16)
    narrow = pl.BlockSpec((TM, TOP_K), lambda i: (i, 0))
    return pl.pallas_call(
        _router_kernel,
        grid=(NT,),
        in_specs=[_row_spec(D_MODEL), _full_spec((1, D_MODEL)), _tile_vec_spec(), _tile_vec_spec(),
                  _full_spec(wh.shape), _full_spec(wl.shape), _full_spec(bpad.shape), _full_spec(tri.shape)],
        out_specs=[_row_spec(D_MODEL), narrow, narrow, narrow, _full_spec((1, LANES))],
        out_shape=[
            jax.ShapeDtypeStruct((N_TOK, D_MODEL), F32),
            jax.ShapeDtypeStruct((N_TOK, TOP_K), jnp.int32),
            jax.ShapeDtypeStruct((N_TOK, TOP_K), F32),
            jax.ShapeDtypeStruct((N_TOK, TOP_K), jnp.int32),
            jax.ShapeDtypeStruct((1, LANES), F32),
        ],
        scratch_shapes=[pltpu.VMEM((1, LANES), F32)],
        compiler_params=_params("arbitrary", vmem=VMEM_LIMIT_BYTES),
        name="router",
    )(x, g, sc, sh, wh, wl, bpad, tri)


def _dispatch_kernel(dest_ref, last_ref, h_ref, xs_ref, zero_ref, sem_ref):
    def row_copy(t, k):
        d = dest_ref[(pl.program_id(0) * TM + t) * TOP_K + k]
        return pltpu.make_async_copy(h_ref.at[pl.ds(t, 1), :], xs_ref.at[pl.ds(d, 1), :], sem_ref.at[0])

    def zero_block(blk):
        row0 = pl.multiple_of(blk * MOE_BLK, MOE_BLK)
        return pltpu.make_async_copy(zero_ref, xs_ref.at[pl.ds(row0, MOE_BLK), :], sem_ref.at[1])

    @pl.when(pl.program_id(0) == 0)
    def _():
        zero_ref[...] = jnp.zeros_like(zero_ref)
        n_used = last_ref[N_EXPERTS]

        def start(e, carry):
            @pl.when(last_ref[e] >= 0)
            def _():
                zero_block(last_ref[e]).start()
            return carry

        def wait(e, carry):
            @pl.when(last_ref[e] >= 0)
            def _():
                zero_block(last_ref[e]).wait()
            return carry

        def start_tail(b, carry):
            zero_block(b).start()
            return carry

        def wait_tail(b, carry):
            zero_block(b).wait()
            return carry

        lax.fori_loop(0, N_EXPERTS, start, 0)
        lax.fori_loop(n_used, N_BLOCKS, start_tail, 0)
        lax.fori_loop(0, N_EXPERTS, wait, 0)
        lax.fori_loop(n_used, N_BLOCKS, wait_tail, 0)

    def start_rows(t, carry):
        for k in range(TOP_K):
            row_copy(t, k).start()
        return carry

    def wait_rows(t, carry):
        for k in range(TOP_K):
            row_copy(t, k).wait()
        return carry

    lax.fori_loop(0, TM, start_rows, 0)
    lax.fori_loop(0, TM, wait_rows, 0)


def _dispatch(dest_flat, last_blk, h):
    return pl.pallas_call(
        _dispatch_kernel,
        grid_spec=pltpu.PrefetchScalarGridSpec(
            num_scalar_prefetch=2,
            grid=(NT,),
            in_specs=[pl.BlockSpec((TM, D_MODEL), lambda i, d, l: (i, 0))],
            out_specs=pl.BlockSpec(memory_space=pl.ANY),
            scratch_shapes=[pltpu.VMEM((MOE_BLK, D_MODEL), F32), pltpu.SemaphoreType.DMA((2,))],
        ),
        out_shape=jax.ShapeDtypeStruct((CAP, D_MODEL), F32),
        compiler_params=_params("arbitrary", vmem=VMEM_LIMIT_BYTES),
        name="moe_dispatch",
    )(dest_flat, last_blk, h)


def _expert_kernel(be_ref, nused_ref, xs_ref, wgu_ref, bgu_ref, wd_ref, bd_ref, ys_ref, wgu_s, wd_s):
    b = pl.program_id(0)
    used = b < nused_ref[0]

    @pl.when(used & ((b == 0) | (be_ref[b] != be_ref[jnp.maximum(b - 1, 0)])))
    def _():
        wgu_s[...] = wgu_ref[0].astype(BF16)
        wd_s[...] = wd_ref[0].astype(BF16)

    @pl.when(used)
    def _():
        gu = _dot(xs_ref[...].astype(BF16), wgu_s[...]) + bgu_ref[0]
        g = jnp.minimum(gu[:, :D_EXPERT], SWIGLU_LIMIT)
        u = jnp.clip(gu[:, D_EXPERT:], -SWIGLU_LIMIT, SWIGLU_LIMIT)
        act = (u + 1.0) * (g * jax.nn.sigmoid(SWIGLU_ALPHA * g))
        ys_ref[...] = _dot(act.astype(BF16), wd_s[...]) + bd_ref[0]

    @pl.when(jnp.logical_not(used))
    def _():
        ys_ref[...] = jnp.zeros_like(ys_ref)


def _experts(block_e, n_used, xs, w_gu, b_gu, w_down, b_down):
    def xs_map(b, be, nu):
        return (jnp.minimum(b, nu[0] - 1), 0)

    def e_map(b, be, nu):
        return (be[b], 0, 0)

    return pl.pallas_call(
        _expert_kernel,
        grid_spec=pltpu.PrefetchScalarGridSpec(
            num_scalar_prefetch=2,
            grid=(N_BLOCKS,),
            in_specs=[
                pl.BlockSpec((MOE_BLK, D_MODEL), xs_map),
                pl.BlockSpec((1, D_MODEL, 2 * D_EXPERT), e_map),
                pl.BlockSpec((1, 1, 2 * D_EXPERT), e_map),
                pl.BlockSpec((1, D_EXPERT, D_MODEL), e_map),
                pl.BlockSpec((1, 1, D_MODEL), e_map),
            ],
            out_specs=pl.BlockSpec((MOE_BLK, D_MODEL), lambda b, be, nu: (b, 0)),
            scratch_shapes=[pltpu.VMEM((D_MODEL, 2 * D_EXPERT), BF16), pltpu.VMEM((D_EXPERT, D_MODEL), BF16)],
        ),
        out_shape=jax.ShapeDtypeStruct((CAP, D_MODEL), F32),
        compiler_params=_params("arbitrary", vmem=VMEM_LIMIT_BYTES),
        name="moe_experts",
    )(block_e, n_used, xs, w_gu, b_gu.reshape(N_EXPERTS, 1, -1), w_down, b_down.reshape(N_EXPERTS, 1, -1))


def _combine_kernel(dest_ref, x_ref, g_ref, gate_ref, ys_ref, o_ref, buf_ref, sem_ref):
    def row_copy(t, k):
        d = dest_ref[(pl.program_id(0) * TM + t) * TOP_K + k]
        return pltpu.make_async_copy(ys_ref.at[pl.ds(d, 1), :], buf_ref.at[k, pl.ds(t, 1), :], sem_ref.at[0])

    def start_rows(t, carry):
        for k in range(TOP_K):
            row_copy(t, k).start()
        return carry

    def wait_rows(t, carry):
        for k in range(TOP_K):
            row_copy(t, k).wait()
        return carry

    lax.fori_loop(0, TM, start_rows, 0)
    lax.fori_loop(0, TM, wait_rows, 0)
    gates = gate_ref[...]
    ff = gates[:, 0:1] * buf_ref[0]
    for k in range(1, TOP_K):
        ff = ff + gates[:, k:k + 1] * buf_ref[k]
    o_ref[...] = x_ref[...] + g_ref[0] * ff


def _combine(dest_flat, x, gate_vec, gates, ys):
    return pl.pallas_call(
        _combine_kernel,
        grid_spec=pltpu.PrefetchScalarGridSpec(
            num_scalar_prefetch=1,
            grid=(NT,),
            in_specs=[
                pl.BlockSpec((TM, D_MODEL), lambda i, d: (i, 0)),
                pl.BlockSpec((1, 1, D_MODEL), lambda i, d: (i, 0, 0)),
                pl.BlockSpec((TM, TOP_K), lambda i, d: (i, 0)),
                pl.BlockSpec(memory_space=pl.ANY),
            ],
            out_specs=pl.BlockSpec((TM, D_MODEL), lambda i, d: (i, 0)),
            scratch_shapes=[pltpu.VMEM((TOP_K, TM, D_MODEL), F32), pltpu.SemaphoreType.DMA((1,))],
        ),
        out_shape=jax.ShapeDtypeStruct((N_TOK, D_MODEL), F32),
        compiler_params=_params("arbitrary", vmem=VMEM_LIMIT_BYTES),
        name="moe_combine",
    )(dest_flat, x, gate_vec, gates, ys)


def _moe(x, g, sc, sh, gate_vec, w_router, b_router, w_gu, b_gu, w_down, b_down):
    h, idx, gates, pos, counts = _router(x, g, sc, sh, w_router, b_router)
    cnt = counts[0, :N_EXPERTS].astype(jnp.int32)
    nblk = (cnt + MOE_BLK - 1) // MOE_BLK
    blk_end = jnp.cumsum(nblk)
    blk_start = blk_end - nblk
    dest = (blk_start * MOE_BLK)[idx] + pos
    n_used = blk_end[-1]
    last_blk = jnp.concatenate([jnp.where(nblk > 0, blk_end - 1, -1), n_used[None]]).astype(jnp.int32)
    b_ids = jnp.minimum(jnp.arange(N_BLOCKS, dtype=jnp.int32), n_used - 1)
    block_e = jnp.sum((blk_end[None, :] <= b_ids[:, None]).astype(jnp.int32), axis=1).astype(jnp.int32)
    dest_flat = dest.reshape(-1).astype(jnp.int32)
    xs = _dispatch(dest_flat, last_blk, h)
    ys = _experts(block_e, n_used.reshape(1).astype(jnp.int32), xs, w_gu, b_gu, w_down, b_down)
    return _combine(dest_flat, x, gate_vec, gates, ys)


def _rope_tables(d_rot, lane0, period):
    n_rows = DEC_SEQ // GRID_W
    rows = jnp.repeat(jnp.arange(n_rows, dtype=jnp.int32), GRID_W).astype(F32)
    cols = jnp.tile(jnp.arange(GRID_W, dtype=jnp.int32), n_rows).astype(F32)
    half = d_rot // 2
    inv = ROPE_THETA ** (-jnp.arange(0, half, 2, dtype=F32) / half)
    lane = jnp.arange(LANES)
    i = (lane - lane0) % period
    active = (lane >= lane0) & (i < d_rot)
    w = i % half
    f = w % (half // 2)
    pos = jnp.where((i // half)[None, :] == 0, rows[:, None], cols[:, None])
    ang = pos * inv[f][None, :]
    cos = jnp.where(active[None, :], jnp.cos(ang), 1.0)
    sin = jnp.where(active[None, :], jnp.sin(ang) * jnp.where(w < half // 2, -1.0, 1.0)[None, :], 0.0)
    cos = jnp.tile(cos, (DEC_BATCH, 1))
    sin = jnp.tile(sin, (DEC_BATCH, 1))
    cos = jnp.concatenate([jnp.ones((N_PROMPT, LANES), F32), cos], axis=0)
    sin = jnp.concatenate([jnp.zeros((N_PROMPT, LANES), F32), sin], axis=0)
    return cos, sin


def _pad_heads(w, n_heads, width):
    lead = w.shape[:-1]
    w = w.reshape(lead + (n_heads, width))
    w = jnp.pad(w, [(0, 0)] * len(lead) + [(0, 0), (0, HEAD_PAD - width)])
    return w.reshape(lead + (n_heads * HEAD_PAD,))


_Q_ORDER = (0, 4, 1, 5, 2, 6, 3, 7, 8, 12, 9, 13, 10, 14, 11, 15)


def _perm_q_heads(w, axis):
    shape = w.shape
    n = shape[axis]
    w = jnp.moveaxis(w, axis, 0).reshape((GQA_HEADS, n // GQA_HEADS) + tuple(s for a, s in enumerate(shape) if a != axis))
    w = w[jnp.array(_Q_ORDER)]
    w = w.reshape((n,) + w.shape[2:])
    return jnp.moveaxis(w, 0, axis)


def _constants():
    mla_cos, mla_sin = _rope_tables(MLA_ROPE, MLA_NOPE, LANES)
    gqa_cos, gqa_sin = _rope_tables(GQA_HEAD_DIM, 0, GQA_HEAD_DIM)
    dft = {}
    for L in (SEQ, DEC_SEQ):
        cm, sm = _dft_tables(L)
        fmat = jnp.concatenate([cm, sm], axis=0).astype(BF16)
        dft[L] = (cm, sm, fmat, fmat.T)
    return dict(mla=(mla_cos, mla_sin), gqa=(gqa_cos, gqa_sin), dft=dft)


def _even_mixer(x, p, i, gmix, sc1, sh1, g1, consts):
    w_in = jnp.pad(p["w_in_ab"][i], ((0, 0), (0, IN_AB_PAD - IN_AB))).astype(BF16)
    proj = _normlin(x, gmix, sc1, sh1, w_in)
    y_hy = []
    for L, blk0, nseq in ((SEQ, 0, BATCH), (DEC_SEQ, N_PROMPT // DEC_SEQ, DEC_BATCH)):
        cm, sm, fmat, ftmat = consts["dft"][L]
        kc, ks = _hy_filter(L, cm, sm, p["hy_filter_w1"][i], p["hy_filter_b1"][i], p["hy_filter_freq"][i],
                            p["hy_filter_w2"][i], p["hy_filter_b2"][i], p["hy_filter_w3"][i],
                            p["hy_filter_b3"][i], p["hy_log_decay"][i])
        y_hy.append(_hy_mix(proj, blk0, nseq, L, p["hy_conv_w"][i], p["hy_conv_b"][i], kc, ks,
                            p["hy_bias"][i], fmat, ftmat))
    y_hy = jnp.concatenate(y_hy, axis=0)

    wq = _pad_heads(p["mla_wq_b"][i], MLA_HEADS, MLA_QK).astype(BF16)
    wkv = p["mla_wkv_b"][i].reshape(MLA_KV_LORA, MLA_HEADS, MLA_NOPE + MLA_V)
    wk = _pad_heads(wkv[:, :, :MLA_NOPE].reshape(MLA_KV_LORA, -1), MLA_HEADS, MLA_NOPE).astype(BF16)
    wv = wkv[:, :, MLA_NOPE:].reshape(MLA_KV_LORA, -1).astype(BF16)
    qg = jnp.pad(p["mla_q_norm"][i], (0, HEAD_PAD - MLA_QK)).reshape(1, HEAD_PAD)
    kg = jnp.pad(p["mla_k_norm"][i], (0, HEAD_PAD - MLA_QK)).reshape(1, HEAD_PAD)
    mla_cos, mla_sin = consts["mla"]
    q, k, v, lat = _mla_prep(proj, p["mla_q_lora_norm"][i].reshape(1, -1), wq, qg,
                             p["mla_kv_lora_norm"][i].reshape(1, -1), wk, wv, kg, mla_cos, mla_sin)
    ctx = jnp.pad(p["cache_mla_latent"][:, i].reshape(DEC_BATCH * PAST_LEN, -1),
                  ((0, 0), (0, KV_PAD - MLA_KV_LORA - MLA_ROPE)))
    kc_ctx, vc_ctx = _mla_ctx(ctx, wk, wv, kg)
    o = _attention(_mla_attn_kernel, q, k, v, kc_ctx, vc_ctx,
                   MLA_HEADS * HEAD_PAD, MLA_HEADS * HEAD_PAD, MLA_HEADS * MLA_V, MLA_HEADS * MLA_V)
    w_out = p["w_out_ab"][i].astype(BF16)
    x = _outproj([y_hy, o], [w_out[:HY_D], w_out[HY_D:]], x, g1)
    return x, lat, dict(y_hy=y_hy, o=o)


def _odd_mixer(x, p, i, gmix, sc1, sh1, g1, consts):
    nq = GQA_HEADS * GQA_HEAD_DIM
    nk = GQA_KV_HEADS * GQA_HEAD_DIM
    w = p["w_qkv_c"][i]
    w_qkv = jnp.concatenate([_perm_q_heads(w[:, :nq], 1), w[:, nq:]], axis=1).astype(BF16)
    qg = jnp.tile(p["gqa_q_norm"][i], LANES // GQA_HEAD_DIM).reshape(1, LANES)
    kg = jnp.tile(p["gqa_k_norm"][i], LANES // GQA_HEAD_DIM).reshape(1, LANES)
    gqa_cos, gqa_sin = consts["gqa"]
    q, k, v, k_plain, v_plain = _gqa_prep(x, gmix, sc1, sh1, w_qkv, qg, kg, gqa_cos, gqa_sin)
    kc_ctx = p["cache_gqa_k"][:, i].reshape(DEC_BATCH * PAST_LEN, -1).astype(BF16)
    vc_ctx = p["cache_gqa_v"][:, i].reshape(DEC_BATCH * PAST_LEN, -1).astype(BF16)
    o = _attention(_gqa_attn_kernel, q, k, v, kc_ctx, vc_ctx, nq, nk, nk, nq)
    w_out = _perm_q_heads(p["w_out_c"][i], 0).astype(BF16)
    x = _outproj([o], [w_out], x, g1)
    return x, k_plain, v_plain


def kernel(x_prompt, x_sample, cache_mla_latent, cache_gqa_k, cache_gqa_v, c, c_ctx, w_ada, b_ada, norm_mix, norm_ffn, w_in_ab, hy_conv_w, hy_conv_b, hy_filter_w1, hy_filter_b1, hy_filter_freq, hy_filter_w2, hy_filter_b2, hy_filter_w3, hy_filter_b3, hy_log_decay, hy_bias, mla_q_lora_norm, mla_wq_b, mla_kv_lora_norm, mla_wkv_b, mla_q_norm, mla_k_norm, w_out_ab, w_qkv_c, gqa_q_norm, gqa_k_norm, w_out_c, moe_router_w, moe_router_b, moe_w_gate_up, moe_b_gate_up, moe_w_down, moe_b_down):
    p = dict(locals())
    x = jnp.concatenate([x_prompt.reshape(N_PROMPT, D_MODEL), x_sample.reshape(N_SAMPLE, D_MODEL)], axis=0)

    cond = jnp.concatenate([c_ctx[None, :], c, jnp.zeros((COND_ROWS - N_COND, D_MODEL), F32)], axis=0)
    mods = _modulation(cond, w_ada, b_ada)
    tile_cond = jnp.concatenate([jnp.zeros((NT_PROMPT,), jnp.int32),
                                 1 + jnp.arange(NT - NT_PROMPT, dtype=jnp.int32) // TILES_PER_SAMPLE])
    mods = mods[:, tile_cond].reshape(DEPTH, NT, 6, 1, D_MODEL).transpose(0, 2, 1, 3, 4)

    consts = _constants()

    lat_out, k_out, v_out = [], [], []
    for l in range(DEPTH):
        sh1, sc1, g1, sh2, sc2, g2 = (mods[l, j] for j in range(6))
        i = l // 2
        gmix = norm_mix[l].reshape(1, D_MODEL)
        if l % 2 == 0:
            x, lat, _ = _even_mixer(x, p, i, gmix, sc1, sh1, g1, consts)
            lat_out.append(lat[:N_PROMPT].reshape(BATCH, SEQ, -1))
        else:
            x, k_plain, v_plain = _odd_mixer(x, p, i, gmix, sc1, sh1, g1, consts)
            k_out.append(k_plain[:N_PROMPT].reshape(BATCH, SEQ, GQA_KV_HEADS, GQA_HEAD_DIM))
            v_out.append(v_plain[:N_PROMPT].reshape(BATCH, SEQ, GQA_KV_HEADS, GQA_HEAD_DIM))
        x = _moe(x, norm_ffn[l].reshape(1, D_MODEL), sc2, sh2, g2, moe_router_w[l], moe_router_b[l],
                 moe_w_gate_up[l], moe_b_gate_up[l], moe_w_down[l], moe_b_down[l])

    y_prompt = x[:N_PROMPT].reshape(BATCH, SEQ, D_MODEL)
    y_sample = x[N_PROMPT:].reshape(DEC_BATCH, DEC_SEQ, D_MODEL)
    return (y_prompt, y_sample, jnp.stack(lat_out, axis=1), jnp.stack(k_out, axis=1), jnp.stack(v_out, axis=1))
```

```python
import functools
import math

import jax
import jax.numpy as jnp
import numpy as np
from jax import lax
from jax.experimental import pallas as pl
from jax.experimental.pallas import tpu as pltpu

F32 = jnp.float32
BF16 = jnp.bfloat16

D_MODEL = 1024
BATCH = 32
SEQ = 256
DEPTH = 4
DEC_BATCH = 4
DEC_SEQ = 1024
PAST_LEN = 256
GRID_W = 64
N_EVEN = (DEPTH + 1) // 2
N_ODD = DEPTH // 2
HY_D = D_MODEL // 2
HY_ORDER = 2
HY_BANDS = 16
HY_EMB = 2 * HY_BANDS + 1
HY_FILTER_HIDDEN = 64
MLA_HEADS = 8
MLA_NOPE = 64
MLA_ROPE = 32
MLA_QK = MLA_NOPE + MLA_ROPE
MLA_V = HY_D // MLA_HEADS
MLA_Q_LORA = 3 * D_MODEL // 8
MLA_KV_LORA = D_MODEL // 4
GQA_HEADS = 16
GQA_KV_HEADS = 4
GQA_HEAD_DIM = D_MODEL // GQA_HEADS
N_EXPERTS = 32
TOP_K = 4
D_EXPERT = D_MODEL
SWIGLU_LIMIT = 7.0
SWIGLU_ALPHA = 1.702
ROPE_THETA = 10000.0
RMS_EPS = 1e-6
IN_AB = (HY_ORDER + 1) * HY_D + MLA_Q_LORA + MLA_KV_LORA + MLA_ROPE

N_PROMPT = BATCH * SEQ
N_SAMPLE = DEC_BATCH * DEC_SEQ
N_TOK = N_PROMPT + N_SAMPLE

LANES = 128
SUBLANES = 8
VMEM_LIMIT_BYTES = 56 * 1024 * 1024

TM = 256
NT = N_TOK // TM
NT_PROMPT = N_PROMPT // TM
TILES_PER_SAMPLE = DEC_SEQ // TM
N_COND = 1 + DEC_BATCH
COND_ROWS = 8
HEAD_PAD = LANES
IN_AB_PAD = 2304
KV_PAD = 384
MOE_HALF = 256
MOE_BLK = 2 * MOE_HALF
N_SLOTS = N_TOK * TOP_K
N_BLOCKS = N_SLOTS // MOE_BLK + N_EXPERTS
CAP = N_BLOCKS * MOE_BLK
ROW_TILE = (SUBLANES, LANES)
assert SUBLANES * LANES == D_MODEL
HY_CH = 256


def _dot(a, b):
    return jnp.dot(a, b, preferred_element_type=F32)


def _dot_nt(a, b):
    return lax.dot_general(a, b, (((1,), (1,)), ((), ())), preferred_element_type=F32)


def _split(x):
    hi = x.astype(BF16)
    lo = (x - hi.astype(F32)).astype(BF16)
    return hi, lo


def _dot3(a, b):
    ah, al = _split(a)
    bh, bl = _split(b)
    return _dot(ah, bh) + (_dot(ah, bl) + _dot(al, bh))


def _lane_iota(shape):
    return lax.broadcasted_iota(jnp.int32, shape, len(shape) - 1)


def _params(*sem, vmem=None):
    return pltpu.CompilerParams(dimension_semantics=sem, vmem_limit_bytes=vmem)


def _mod_kernel(c_ref, w_ref, b_ref, o_ref):
    c = c_ref[...]
    s = c * jax.nn.sigmoid(c)
    o_ref[0] = _dot(s.astype(BF16), w_ref[0].astype(BF16)) + b_ref[0]


def _modulation(cond, w_ada, b_ada):
    nblk = 6
    return pl.pallas_call(
        _mod_kernel,
        grid=(DEPTH, nblk),
        in_specs=[
            pl.BlockSpec((COND_ROWS, D_MODEL), lambda l, j: (0, 0)),
            pl.BlockSpec((1, D_MODEL, D_MODEL), lambda l, j: (l, 0, j)),
            pl.BlockSpec((1, 1, D_MODEL), lambda l, j: (l, 0, j)),
        ],
        out_specs=pl.BlockSpec((1, COND_ROWS, D_MODEL), lambda l, j: (l, 0, j)),
        out_shape=jax.ShapeDtypeStruct((DEPTH, COND_ROWS, nblk * D_MODEL), F32),
        compiler_params=_params("arbitrary", "arbitrary"),
        name="modulation",
    )(cond, w_ada, b_ada.reshape(DEPTH, 1, nblk * D_MODEL))


def _norm_mod(x, g, sc, sh):
    ms = jnp.mean(x * x, axis=-1, keepdims=True)
    y = x * lax.rsqrt(ms + RMS_EPS)
    return (y * g) * (1.0 + sc) + sh


def _row_spec(width):
    return pl.BlockSpec((TM, width), lambda i: (i, 0))


def _tile_vec_spec():
    return pl.BlockSpec((1, 1, D_MODEL), lambda i: (i, 0, 0))


def _full_spec(shape):
    nd = len(shape)
    return pl.BlockSpec(shape, lambda i: (0,) * nd)


def _normlin_kernel(x_ref, g_ref, sc_ref, sh_ref, w_ref, o_ref):
    h = _norm_mod(x_ref[...], g_ref[...], sc_ref[0], sh_ref[0])
    o_ref[...] = _dot(h.astype(BF16), w_ref[...])


def _normlin(x, g, sc, sh, w):
    nout = w.shape[1]
    return pl.pallas_call(
        _normlin_kernel,
        grid=(NT,),
        in_specs=[_row_spec(D_MODEL), _full_spec((1, D_MODEL)), _tile_vec_spec(), _tile_vec_spec(),
                  _full_spec((D_MODEL, nout))],
        out_specs=_row_spec(nout),
        out_shape=jax.ShapeDtypeStruct((N_TOK, nout), F32),
        compiler_params=_params("arbitrary", vmem=VMEM_LIMIT_BYTES),
        name="normlin",
    )(x, g, sc, sh, w)


def _hy_filter_kernel(z_ref, w1_ref, b1_ref, fr_ref, w2_ref, b2_ref, w3_ref, b3_ref, ed_ref,
                      c_ref, s_ref, kc_ref, ks_ref):
    L = z_ref.shape[0]
    z = z_ref[...]
    fr = fr_ref[...]
    hdn = jnp.sin(fr * (_dot3(z, w1_ref[...]) + b1_ref[...]))
    hdn = jnp.sin(fr * (_dot3(hdn, w2_ref[...]) + b2_ref[...]))
    filt = _dot3(hdn, w3_ref[...]) + b3_ref[...]
    t = z[:, 0:1]
    filt = filt * jnp.exp(-t * ed_ref[...])
    row = lax.broadcasted_iota(jnp.int32, (L, HY_D), 0)
    cm = c_ref[...]
    sm = s_ref[...]
    for o in range(HY_ORDER):
        fw = filt[:, (2 * o) * HY_D:(2 * o + 1) * HY_D]
        bw = filt[:, (2 * o + 1) * HY_D:(2 * o + 2) * HY_D]
        den = (jnp.sum(jnp.abs(fw), axis=0, keepdims=True)
               + jnp.sum(jnp.abs(bw), axis=0, keepdims=True)) + 1e-6
        fw = fw / den
        bw = jnp.where(row == 0, 0.0, bw / den)
        kc_ref[o] = _dot3(cm, fw + bw) * (1.0 / L)
        ks_ref[o] = _dot3(sm, fw - bw) * (1.0 / L)


def _dft_tables(L):
    m = np.arange(L, dtype=np.int64)
    phase = ((2 * m[:, None] + 1) * m[None, :]) % (4 * L)
    ang = phase.astype(np.float64) * (2.0 * math.pi / (4 * L))
    return np.cos(ang).astype(np.float32), np.sin(ang).astype(np.float32)


def _filter_features(L):
    p = np.arange(L, dtype=np.float32)
    t = p / np.float32(max(L - 1, 1))
    bands = np.linspace(1e-4, HY_BANDS - 1, HY_BANDS, dtype=np.float32)
    ang = (np.float32(2.0 * math.pi / L) * p[:, None] * bands[None, :]).astype(np.float64)
    z = np.concatenate([t[:, None], np.cos(ang), -np.sin(ang)], axis=-1).astype(np.float32)
    return jnp.asarray(np.pad(z, ((0, 0), (0, LANES - HY_EMB))))


def _hy_filter(L, cmat, smat, w1, b1, fr, w2, b2, w3, b3, log_decay):
    nf = HY_ORDER * 2 * HY_D
    args = (
        _filter_features(L),
        jnp.pad(w1, ((0, LANES - HY_EMB), (0, 0))),
        b1.reshape(1, -1), fr.reshape(1, -1), w2, b2.reshape(1, -1), w3, b3.reshape(1, -1),
        jnp.exp(log_decay.astype(F32)).reshape(1, nf),
        cmat, smat,
    )
    out_sds = jax.ShapeDtypeStruct((HY_ORDER, L, HY_D), F32)
    return pl.pallas_call(
        _hy_filter_kernel,
        grid=(1,),
        in_specs=[_full_spec(a.shape) for a in args],
        out_specs=[_full_spec(out_sds.shape)] * 2,
        out_shape=[out_sds, out_sds],
        compiler_params=_params("arbitrary", vmem=VMEM_LIMIT_BYTES),
        name=f"hy_filter_{L}",
    )(*args)


def _hy_mix_kernel(u0_ref, u1_ref, u2_ref, cw0_ref, cw1_ref, cw2_ref, cb0_ref, cb1_ref, cb2_ref,
                   kc_ref, ks_ref, hb_ref, f_ref, ft_ref, o_ref):
    L = u0_ref.shape[0]
    row = lax.broadcasted_iota(jnp.int32, (L, HY_CH), 0)

    def short_conv(u_ref, cw_ref, cb_ref):
        u = u_ref[...]
        w = cw_ref[0]
        prev = jnp.where(row == 0, 0.0, pltpu.roll(u, 1, 0))
        nxt = jnp.where(row == L - 1, 0.0, pltpu.roll(u, L - 1, 0))
        return (prev * w[0:1] + u * w[1:2]) + nxt * w[2:3] + cb_ref[0]

    z = short_conv(u0_ref, cw0_ref, cb0_ref)
    gates = (short_conv(u1_ref, cw1_ref, cb1_ref), short_conv(u2_ref, cw2_ref, cb2_ref))
    for o in range(HY_ORDER):
        zz = _dot(f_ref[...], z.astype(BF16))
        cz, sz = zz[:L], zz[L:]
        kc, ks = kc_ref[o], ks_ref[o]
        w1 = cz * kc - sz * ks
        w2 = cz * ks + sz * kc
        ww = jnp.concatenate([w1, w2], axis=0).astype(BF16)
        conv = _dot(ft_ref[...], ww)
        z = gates[o] * (conv + z * hb_ref[0, o])
    o_ref[...] = z.astype(o_ref.dtype)


def _hy_mix(proj, row_block0, nseq, L, cw, cb, kc, ks, hb, fmat, ftmat):
    nch = HY_D // HY_CH
    nparts = HY_ORDER + 1
    cw3 = cw.reshape(3, nparts * nch, HY_CH).transpose(1, 0, 2)
    cb3 = cb.reshape(nparts * nch, 1, HY_CH)
    hb3 = hb.reshape(HY_ORDER, nch, 1, HY_CH).transpose(1, 0, 2, 3)

    def u_spec(part):
        return pl.BlockSpec((L, HY_CH), lambda s, c: (row_block0 + s, part * nch + c))

    def cw_spec(part):
        return pl.BlockSpec((1, 3, HY_CH), lambda s, c: (part * nch + c, 0, 0))

    def cb_spec(part):
        return pl.BlockSpec((1, 1, HY_CH), lambda s, c: (part * nch + c, 0, 0))

    return pl.pallas_call(
        _hy_mix_kernel,
        grid=(nseq, nch),
        in_specs=[
            u_spec(0), u_spec(1), u_spec(2),
            cw_spec(0), cw_spec(1), cw_spec(2),
            cb_spec(0), cb_spec(1), cb_spec(2),
            pl.BlockSpec((HY_ORDER, L, HY_CH), lambda s, c: (0, 0, c)),
            pl.BlockSpec((HY_ORDER, L, HY_CH), lambda s, c: (0, 0, c)),
            pl.BlockSpec((1, HY_ORDER, 1, HY_CH), lambda s, c: (c, 0, 0, 0)),
            pl.BlockSpec((2 * L, L), lambda s, c: (0, 0)),
            pl.BlockSpec((L, 2 * L), lambda s, c: (0, 0)),
        ],
        out_specs=pl.BlockSpec((L, HY_CH), lambda s, c: (s, c)),
        out_shape=jax.ShapeDtypeStruct((nseq * L, HY_D), BF16),
        compiler_params=_params("arbitrary", "arbitrary", vmem=VMEM_LIMIT_BYTES),
        name=f"hy_mix_{L}",
    )(proj, proj, proj, cw3, cw3, cw3, cb3, cb3, cb3, kc, ks, hb3, fmat, ftmat)


def _head_rms(xh, gain, dim):
    ms = jnp.sum(xh * xh, axis=-1, keepdims=True) * (1.0 / dim)
    return (xh * lax.rsqrt(ms + RMS_EPS)) * gain


def _rope(xh, cos, sin, half):
    lane = _lane_iota(xh.shape)
    first = (lane % (2 * half)) < half
    rot = jnp.where(first, pltpu.roll(xh, LANES - half, 1), pltpu.roll(xh, half, 1))
    return xh * cos + rot * sin


def _mla_keys_values(lat_n, kr_blk, wk_ref, wv_ref, kg_ref, rope):
    lb = lat_n.astype(BF16)
    kn = _dot(lb, wk_ref[...])
    v = _dot(lb, wv_ref[...])
    kr = pltpu.roll(kr_blk, MLA_NOPE, 1)
    ks = []
    for h in range(MLA_HEADS):
        kh = kn[:, h * HEAD_PAD:(h + 1) * HEAD_PAD] + kr
        kh = _head_rms(kh, kg_ref[...], MLA_QK)
        if rope is not None:
            kh = _rope(kh, rope[0], rope[1], MLA_ROPE // 4)
        ks.append(kh)
    return ks, v


def _mla_prep_kernel(qa_ref, kva_ref, gq_ref, wq_ref, qg_ref, gkv_ref, wk_ref, wv_ref, kg_ref,
                     cos_ref, sin_ref, q_ref, k_ref, v_ref, lat_ref):
    rope = (cos_ref[...], sin_ref[...])
    qa = qa_ref[...]
    ms = jnp.mean(qa * qa, axis=-1, keepdims=True)
    qn = (qa * lax.rsqrt(ms + RMS_EPS)) * gq_ref[...]
    q = _dot(qn.astype(BF16), wq_ref[...])
    for h in range(MLA_HEADS):
        qh = _head_rms(q[:, h * HEAD_PAD:(h + 1) * HEAD_PAD], qg_ref[...], MLA_QK)
        q_ref[:, h * HEAD_PAD:(h + 1) * HEAD_PAD] = _rope(qh, rope[0], rope[1], MLA_ROPE // 4).astype(BF16)

    kva = kva_ref[...]
    lat = kva[:, :MLA_KV_LORA]
    ms = jnp.mean(lat * lat, axis=-1, keepdims=True)
    lat_n = (lat * lax.rsqrt(ms + RMS_EPS)) * gkv_ref[...]
    kr_blk = kva[:, MLA_KV_LORA:]
    lat_ref[:, :MLA_KV_LORA] = lat_n
    lat_ref[:, MLA_KV_LORA:] = kr_blk[:, :MLA_ROPE]
    ks, v = _mla_keys_values(lat_n, kr_blk, wk_ref, wv_ref, kg_ref, rope)
    for h in range(MLA_HEADS):
        k_ref[:, h * HEAD_PAD:(h + 1) * HEAD_PAD] = ks[h].astype(BF16)
    v_ref[...] = v.astype(BF16)


def _mla_prep(proj, gq, wq, qg, gkv, wk, wv, kg, cos_t, sin_t):
    nq = MLA_HEADS * HEAD_PAD
    nv = MLA_HEADS * MLA_V
    qa_blk = (HY_ORDER + 1) * HY_D // KV_PAD
    return pl.pallas_call(
        _mla_prep_kernel,
        grid=(NT,),
        in_specs=[
            pl.BlockSpec((TM, KV_PAD), lambda i: (i, qa_blk)),
            pl.BlockSpec((TM, KV_PAD), lambda i: (i, qa_blk + 1)),
            _full_spec(gq.shape), _full_spec(wq.shape), _full_spec(qg.shape), _full_spec(gkv.shape),
            _full_spec(wk.shape), _full_spec(wv.shape), _full_spec(kg.shape),
            _rope_spec(), _rope_spec(),
        ],
        out_specs=[_row_spec(nq), _row_spec(nq), _row_spec(nv), _row_spec(MLA_KV_LORA + MLA_ROPE)],
        out_shape=[
            jax.ShapeDtypeStruct((N_TOK, nq), BF16),
            jax.ShapeDtypeStruct((N_TOK, nq), BF16),
            jax.ShapeDtypeStruct((N_TOK, nv), BF16),
            jax.ShapeDtypeStruct((N_TOK, MLA_KV_LORA + MLA_ROPE), F32),
        ],
        compiler_params=_params("arbitrary", vmem=VMEM_LIMIT_BYTES),
        name="mla_prep",
    )(proj, proj, gq, wq, qg, gkv, wk, wv, kg, cos_t, sin_t)


def _mla_ctx_kernel(lat_ref, wk_ref, wv_ref, kg_ref, k_ref, v_ref):
    lat = lat_ref[...]
    ks, v = _mla_keys_values(lat[:, :MLA_KV_LORA], lat[:, MLA_KV_LORA:], wk_ref, wv_ref, kg_ref, None)
    for h in range(MLA_HEADS):
        k_ref[:, h * HEAD_PAD:(h + 1) * HEAD_PAD] = ks[h].astype(BF16)
    v_ref[...] = v.astype(BF16)


def _mla_ctx(lat_pad, wk, wv, kg):
    n = lat_pad.shape[0]
    nq = MLA_HEADS * HEAD_PAD
    nv = MLA_HEADS * MLA_V
    return pl.pallas_call(
        _mla_ctx_kernel,
        grid=(n // TM,),
        in_specs=[_row_spec(KV_PAD), _full_spec(wk.shape), _full_spec(wv.shape), _full_spec(kg.shape)],
        out_specs=[_row_spec(nq), _row_spec(nv)],
        out_shape=[jax.ShapeDtypeStruct((n, nq), BF16), jax.ShapeDtypeStruct((n, nv), BF16)],
        compiler_params=_params("arbitrary"),
        name="mla_ctx",
    )(lat_pad, wk, wv, kg)


def _softmax_pv(q, keys, vals, scale):
    ss = [_dot_nt(q, k) * scale for k in keys]
    m = ss[0].max(axis=-1, keepdims=True)
    for s in ss[1:]:
        m = jnp.maximum(m, s.max(axis=-1, keepdims=True))
    ps = [jnp.exp(s - m) for s in ss]
    l = ps[0].sum(axis=-1, keepdims=True)
    for p in ps[1:]:
        l = l + p.sum(axis=-1, keepdims=True)
    o = _dot(ps[0].astype(BF16), vals[0])
    for p, v in zip(ps[1:], vals[1:]):
        o = o + _dot(p.astype(BF16), v)
    return o / l


def _mla_attn_kernel(*refs, has_ctx):
    if has_ctx:
        q_ref, k_ref, v_ref, kc_ref, vc_ref, o_ref = refs
    else:
        q_ref, k_ref, v_ref, o_ref = refs
    scale = MLA_QK ** -0.5
    lo = _lane_iota((q_ref.shape[0], LANES)) < MLA_V
    for j in range(MLA_HEADS // 2):
        vs = [v_ref[:, j * LANES:(j + 1) * LANES]]
        if has_ctx:
            vs.append(vc_ref[:, j * LANES:(j + 1) * LANES])
        outs = []
        for h in (2 * j, 2 * j + 1):
            sl = slice(h * HEAD_PAD, (h + 1) * HEAD_PAD)
            ks = [k_ref[:, sl]]
            if has_ctx:
                ks.append(kc_ref[:, sl])
            outs.append(_softmax_pv(q_ref[:, sl], ks, vs, scale))
        o_ref[:, j * LANES:(j + 1) * LANES] = jnp.where(lo, outs[0], outs[1]).astype(o_ref.dtype)


def _gqa_attn_kernel(*refs, has_ctx):
    if has_ctx:
        q_ref, k_ref, v_ref, kc_ref, vc_ref, o_ref = refs
    else:
        q_ref, k_ref, v_ref, o_ref = refs
    scale = GQA_HEAD_DIM ** -0.5
    lo = _lane_iota((q_ref.shape[0], LANES)) < GQA_HEAD_DIM
    pairs_per_kv = (GQA_HEADS // 2) // (GQA_KV_HEADS // 2)
    for p in range(GQA_HEADS // 2):
        kv = slice((p // pairs_per_kv) * LANES, (p // pairs_per_kv + 1) * LANES)
        ks, vs = [k_ref[:, kv]], [v_ref[:, kv]]
        if has_ctx:
            ks.append(kc_ref[:, kv])
            vs.append(vc_ref[:, kv])
        qp = q_ref[:, p * LANES:(p + 1) * LANES]
        zero = jnp.zeros_like(qp)
        o_lo = _softmax_pv(jnp.where(lo, qp, zero), ks, vs, scale)
        o_hi = _softmax_pv(jnp.where(lo, zero, qp), ks, vs, scale)
        o_ref[:, p * LANES:(p + 1) * LANES] = jnp.where(lo, o_lo, o_hi).astype(o_ref.dtype)


def _attention(body, q, k, v, kc, vc, wq, wk, wv, wo):
    outs = []
    outs.append(pl.pallas_call(
        functools.partial(body, has_ctx=False),
        grid=(BATCH,),
        in_specs=[pl.BlockSpec((SEQ, wq), lambda b: (b, 0)),
                  pl.BlockSpec((SEQ, wk), lambda b: (b, 0)),
                  pl.BlockSpec((SEQ, wv), lambda b: (b, 0))],
        out_specs=pl.BlockSpec((SEQ, wo), lambda b: (b, 0)),
        out_shape=jax.ShapeDtypeStruct((N_PROMPT, wo), BF16),
        compiler_params=_params("arbitrary", vmem=VMEM_LIMIT_BYTES),
        name="attn_prompt",
    )(q, k, v))
    qt = DEC_SEQ // TM
    q0 = N_PROMPT // TM
    s0 = N_PROMPT // DEC_SEQ
    outs.append(pl.pallas_call(
        functools.partial(body, has_ctx=True),
        grid=(DEC_BATCH, qt),
        in_specs=[pl.BlockSpec((TM, wq), lambda b, t: (q0 + b * qt + t, 0)),
                  pl.BlockSpec((DEC_SEQ, wk), lambda b, t: (s0 + b, 0)),
                  pl.BlockSpec((DEC_SEQ, wv), lambda b, t: (s0 + b, 0)),
                  pl.BlockSpec((PAST_LEN, wk), lambda b, t: (b, 0)),
                  pl.BlockSpec((PAST_LEN, wv), lambda b, t: (b, 0))],
        out_specs=pl.BlockSpec((TM, wo), lambda b, t: (b * qt + t, 0)),
        out_shape=jax.ShapeDtypeStruct((N_SAMPLE, wo), BF16),
        compiler_params=_params("arbitrary", "arbitrary", vmem=VMEM_LIMIT_BYTES),
        name="attn_sample",
    )(q, k, v, kc, vc))
    return jnp.concatenate(outs, axis=0)


def _gqa_prep_kernel(x_ref, g_ref, sc_ref, sh_ref, w_ref, qg_ref, kg_ref, cos_ref, sin_ref,
                     q_ref, k_ref, v_ref, kp_ref, vp_ref):
    h = _norm_mod(x_ref[...], g_ref[...], sc_ref[0], sh_ref[0])
    qkv = _dot(h.astype(BF16), w_ref[...])
    cos, sin = cos_ref[...], sin_ref[...]
    lo = _lane_iota((TM, LANES)) < GQA_HEAD_DIM
    nq = GQA_HEADS * GQA_HEAD_DIM
    nk = GQA_KV_HEADS * GQA_HEAD_DIM

    def pair_norm(xp, gain):
        sq = xp * xp
        ms_lo = jnp.sum(jnp.where(lo, sq, 0.0), axis=-1, keepdims=True)
        ms_hi = jnp.sum(jnp.where(lo, 0.0, sq), axis=-1, keepdims=True)
        ms = jnp.where(lo, ms_lo, ms_hi) * (1.0 / GQA_HEAD_DIM)
        return (xp * lax.rsqrt(ms + RMS_EPS)) * gain

    for p in range(nq // LANES):
        sl = slice(p * LANES, (p + 1) * LANES)
        qn = pair_norm(qkv[:, sl], qg_ref[...])
        q_ref[:, sl] = _rope(qn, cos, sin, GQA_HEAD_DIM // 4).astype(BF16)
    for p in range(nk // LANES):
        sl = slice(p * LANES, (p + 1) * LANES)
        kn = pair_norm(qkv[:, nq + p * LANES:nq + (p + 1) * LANES], kg_ref[...])
        kp_ref[:, sl] = kn
        k_ref[:, sl] = _rope(kn, cos, sin, GQA_HEAD_DIM // 4).astype(BF16)
    v = qkv[:, nq + nk:]
    vp_ref[...] = v
    v_ref[...] = v.astype(BF16)


def _gqa_prep(x, g, sc, sh, w, qg, kg, cos_t, sin_t):
    nq = GQA_HEADS * GQA_HEAD_DIM
    nk = GQA_KV_HEADS * GQA_HEAD_DIM
    return pl.pallas_call(
        _gqa_prep_kernel,
        grid=(NT,),
        in_specs=[_row_spec(D_MODEL), _full_spec((1, D_MODEL)), _tile_vec_spec(), _tile_vec_spec(),
                  _full_spec(w.shape), _full_spec(qg.shape), _full_spec(kg.shape),
                  _rope_spec(), _rope_spec()],
        out_specs=[_row_spec(nq), _row_spec(nk), _row_spec(nk), _row_spec(nk), _row_spec(nk)],
        out_shape=[
            jax.ShapeDtypeStruct((N_TOK, nq), BF16),
            jax.ShapeDtypeStruct((N_TOK, nk), BF16),
            jax.ShapeDtypeStruct((N_TOK, nk), BF16),
            jax.ShapeDtypeStruct((N_TOK, nk), F32),
            jax.ShapeDtypeStruct((N_TOK, nk), F32),
        ],
        compiler_params=_params("arbitrary", vmem=VMEM_LIMIT_BYTES),
        name="gqa_prep",
    )(x, g, sc, sh, w, qg, kg, cos_t, sin_t)


def _outproj_kernel(*refs, n_in):
    a_refs = refs[:n_in]
    w_refs = refs[n_in:2 * n_in]
    x_ref, g_ref, o_ref = refs[2 * n_in:]
    y = _dot(a_refs[0][...], w_refs[0][...])
    for a, w in zip(a_refs[1:], w_refs[1:]):
        y = y + _dot(a[...], w[...])
    o_ref[...] = x_ref[...] + g_ref[0] * y


def _outproj(acts, ws, x, gate):
    n_in = len(acts)
    return pl.pallas_call(
        functools.partial(_outproj_kernel, n_in=n_in),
        grid=(NT,),
        in_specs=([_row_spec(a.shape[1]) for a in acts] + [_full_spec(w.shape) for w in ws]
                  + [_row_spec(D_MODEL), _tile_vec_spec()]),
        out_specs=_row_spec(D_MODEL),
        out_shape=jax.ShapeDtypeStruct((N_TOK, D_MODEL), F32),
        compiler_params=_params("arbitrary", vmem=VMEM_LIMIT_BYTES),
        name="outproj",
    )(*acts, *ws, x, gate)


def _router_kernel(x_ref, g_ref, sc_ref, sh_ref, wh_ref, wl_ref, br_ref, tri_ref,
                   h_ref, idx_ref, gate_ref, pos_ref, cnt_ref, run_ref):
    @pl.when(pl.program_id(0) == 0)
    def _():
        run_ref[...] = jnp.zeros_like(run_ref)

    h = _norm_mod(x_ref[...], g_ref[...], sc_ref[0], sh_ref[0])
    h_ref[...] = h.reshape((TM,) + ROW_TILE)
    hh, hl = _split(h)
    logits = _dot(hh, wh_ref[...]) + (_dot(hh, wl_ref[...]) + _dot(hl, wh_ref[...])) + br_ref[...]
    lane = _lane_iota((TM, LANES)).astype(F32)
    neg = jnp.float32(-jnp.inf)
    lg = jnp.where(lane < N_EXPERTS, logits, neg)
    tops, sels, hots = [], [], []
    for _ in range(TOP_K):
        m = lg.max(axis=-1, keepdims=True)
        sel = jnp.where(lg == m, lane, float(LANES)).min(axis=-1, keepdims=True)
        hot = lane == sel
        lg = jnp.where(hot, neg, lg)
        tops.append(m)
        sels.append(sel)
        hots.append(hot)
    es = [jnp.exp(t - tops[0]) for t in tops]
    den = es[0] + es[1] + es[2] + es[3]
    member = jnp.zeros((TM, LANES), F32)
    for hot in hots:
        member = member + hot.astype(F32)
    ranks = _dot(tri_ref[...], member.astype(BF16)) + run_ref[...]
    lane4 = _lane_iota((TM, TOP_K))
    idx4 = jnp.zeros((TM, TOP_K), F32)
    gate4 = jnp.zeros((TM, TOP_K), F32)
    pos4 = jnp.zeros((TM, TOP_K), F32)
    for k in range(TOP_K):
        pk = jnp.sum(jnp.where(hots[k], ranks, 0.0), axis=-1, keepdims=True)
        idx4 = jnp.where(lane4 == k, sels[k], idx4)
        gate4 = jnp.where(lane4 == k, es[k] / den, gate4)
        pos4 = jnp.where(lane4 == k, pk, pos4)
    idx_ref[...] = idx4.astype(jnp.int32)
    gate_ref[...] = gate4
    pos_ref[...] = pos4.astype(jnp.int32)
    run_ref[...] = run_ref[...] + jnp.sum(member, axis=0, keepdims=True)
    cnt_ref[...] = run_ref[...]


def _router(x, g, sc, sh, w_router, b_router):
    wpad = jnp.pad(w_router, ((0, 0), (0, LANES - N_EXPERTS)))
    wh = wpad.astype(BF16)
    wl = (wpad - wh.astype(F32)).astype(BF16)
    bpad = jnp.pad(b_router, (0, LANES - N_EXPERTS)).reshape(1, LANES)
    r = np.arange(TM)
    tri = jnp.asarray(r[None, :] < r[:, None], dtype=BF16)
    narrow = pl.BlockSpec((TM, TOP_K), lambda i: (i, 0))
    return pl.pallas_call(
        _router_kernel,
        grid=(NT,),
        in_specs=[_row_spec(D_MODEL), _full_spec((1, D_MODEL)), _tile_vec_spec(), _tile_vec_spec(),
                  _full_spec(wh.shape), _full_spec(wl.shape), _full_spec(bpad.shape), _full_spec(tri.shape)],
        out_specs=[pl.BlockSpec((TM,) + ROW_TILE, lambda i: (i, 0, 0)), narrow, narrow, narrow,
                   _full_spec((1, LANES))],
        out_shape=[
            jax.ShapeDtypeStruct((N_TOK,) + ROW_TILE, F32),
            jax.ShapeDtypeStruct((N_TOK, TOP_K), jnp.int32),
            jax.ShapeDtypeStruct((N_TOK, TOP_K), F32),
            jax.ShapeDtypeStruct((N_TOK, TOP_K), jnp.int32),
            jax.ShapeDtypeStruct((1, LANES), F32),
        ],
        scratch_shapes=[pltpu.VMEM((1, LANES), F32)],
        compiler_params=_params("arbitrary", vmem=VMEM_LIMIT_BYTES),
        name="router",
    )(x, g, sc, sh, wh, wl, bpad, tri)


def _dispatch_kernel(dest_ref, clear_ref, h_ref, xs_ref, zero_ref, sem_ref):
    def row_copy(t, k):
        d = dest_ref[(pl.program_id(0) * TM + t) * TOP_K + k]
        return pltpu.make_async_copy(h_ref.at[pl.ds(t, 1)], xs_ref.at[pl.ds(d, 1)], sem_ref.at[0])

    def zero_half(j):
        row0 = pl.multiple_of(j * MOE_HALF, MOE_HALF)
        return pltpu.make_async_copy(zero_ref, xs_ref.at[pl.ds(row0, MOE_HALF)], sem_ref.at[1])

    @pl.when(pl.program_id(0) == 0)
    def _():
        zero_ref[...] = jnp.zeros_like(zero_ref)

        def start(j, carry):
            @pl.when(clear_ref[j] > 0)
            def _():
                zero_half(j).start()
            return carry

        def wait(j, carry):
            @pl.when(clear_ref[j] > 0)
            def _():
                zero_half(j).wait()
            return carry

        lax.fori_loop(0, 2 * N_BLOCKS, start, 0)
        lax.fori_loop(0, 2 * N_BLOCKS, wait, 0)

    def start_rows(t, carry):
        for k in range(TOP_K):
            row_copy(t, k).start()
        return carry

    def wait_rows(t, carry):
        for k in range(TOP_K):
            row_copy(t, k).wait()
        return carry

    lax.fori_loop(0, TM, start_rows, 0)
    lax.fori_loop(0, TM, wait_rows, 0)


def _dispatch(dest_flat, clear, h):
    return pl.pallas_call(
        _dispatch_kernel,
        grid_spec=pltpu.PrefetchScalarGridSpec(
            num_scalar_prefetch=2,
            grid=(NT,),
            in_specs=[pl.BlockSpec((TM,) + ROW_TILE, lambda i, d, c: (i, 0, 0))],
            out_specs=pl.BlockSpec(memory_space=pl.ANY),
            scratch_shapes=[pltpu.VMEM((MOE_HALF,) + ROW_TILE, F32), pltpu.SemaphoreType.DMA((2,))],
        ),
        out_shape=jax.ShapeDtypeStruct((CAP,) + ROW_TILE, F32),
        compiler_params=_params("arbitrary", vmem=VMEM_LIMIT_BYTES),
        name="moe_dispatch",
    )(dest_flat, clear, h)


def _expert_kernel(be_ref, nh_ref, last_ref, xs_ref, wgu_ref, bgu_ref, wd_ref, bd_ref, ys_ref, wgu_s, wd_s):
    del last_ref
    b = pl.program_id(0)
    nh = nh_ref[b]

    @pl.when((nh > 0) & ((b == 0) | (be_ref[b] != be_ref[jnp.maximum(b - 1, 0)])))
    def _():
        wgu_s[...] = wgu_ref[...].astype(BF16)
        wd_s[...] = wd_ref[...].astype(BF16)

    def ffn(rows):
        x = xs_ref[0:rows].reshape(rows, D_MODEL)
        gu = _dot(x.astype(BF16), wgu_s[...]) + bgu_ref[...]
        g = jnp.minimum(gu[:, :D_EXPERT], SWIGLU_LIMIT)
        u = jnp.clip(gu[:, D_EXPERT:], -SWIGLU_LIMIT, SWIGLU_LIMIT)
        act = (u + 1.0) * (g * jax.nn.sigmoid(SWIGLU_ALPHA * g))
        y = _dot(act.astype(BF16), wd_s[...]) + bd_ref[...]
        ys_ref[0:rows] = y.reshape((rows,) + ROW_TILE)

    @pl.when(nh == 2)
    def _():
        ffn(MOE_BLK)

    @pl.when(nh == 1)
    def _():
        ffn(MOE_HALF)
        ys_ref[MOE_HALF:] = jnp.zeros((MOE_BLK - MOE_HALF,) + ROW_TILE, F32)

    @pl.when(nh == 0)
    def _():
        ys_ref[...] = jnp.zeros_like(ys_ref)


def _experts(layer, block_e, block_nh, last_used, xs, w_gu, b_gu, w_down, b_down):
    def xs_map(b, be, nh, lu):
        return (jnp.minimum(b, lu[0]), 0, 0)

    def e_map(b, be, nh, lu):
        return (layer, be[b], 0, 0)

    return pl.pallas_call(
        _expert_kernel,
        grid_spec=pltpu.PrefetchScalarGridSpec(
            num_scalar_prefetch=3,
            grid=(N_BLOCKS,),
            in_specs=[
                pl.BlockSpec((MOE_BLK,) + ROW_TILE, xs_map),
                pl.BlockSpec((None, None, D_MODEL, 2 * D_EXPERT), e_map),
                pl.BlockSpec((None, None, 1, 2 * D_EXPERT), e_map),
                pl.BlockSpec((None, None, D_EXPERT, D_MODEL), e_map),
                pl.BlockSpec((None, None, 1, D_MODEL), e_map),
            ],
            out_specs=pl.BlockSpec((MOE_BLK,) + ROW_TILE, lambda b, be, nh, lu: (b, 0, 0)),
            scratch_shapes=[pltpu.VMEM((D_MODEL, 2 * D_EXPERT), BF16), pltpu.VMEM((D_EXPERT, D_MODEL), BF16)],
        ),
        out_shape=jax.ShapeDtypeStruct((CAP,) + ROW_TILE, F32),
        compiler_params=_params("arbitrary", vmem=VMEM_LIMIT_BYTES),
        name="moe_experts",
    )(block_e, block_nh, last_used, xs, w_gu, b_gu.reshape(DEPTH, N_EXPERTS, 1, -1), w_down,
      b_down.reshape(DEPTH, N_EXPERTS, 1, -1))


def _combine_kernel(dest_ref, x_ref, g_ref, gate_ref, ys_ref, o_ref, buf_ref, sem_ref):
    def row_copy(t, k):
        d = dest_ref[(pl.program_id(0) * TM + t) * TOP_K + k]
        return pltpu.make_async_copy(ys_ref.at[pl.ds(d, 1)], buf_ref.at[k, pl.ds(t, 1)], sem_ref.at[0])

    def start_rows(t, carry):
        for k in range(TOP_K):
            row_copy(t, k).start()
        return carry

    def wait_rows(t, carry):
        for k in range(TOP_K):
            row_copy(t, k).wait()
        return carry

    lax.fori_loop(0, TM, start_rows, 0)
    lax.fori_loop(0, TM, wait_rows, 0)
    gates = gate_ref[...]
    ff = gates[:, 0:1] * buf_ref[0].reshape(TM, D_MODEL)
    for k in range(1, TOP_K):
        ff = ff + gates[:, k:k + 1] * buf_ref[k].reshape(TM, D_MODEL)
    o_ref[...] = x_ref[...] + g_ref[0] * ff


def _combine(dest_flat, x, gate_vec, gates, ys):
    return pl.pallas_call(
        _combine_kernel,
        grid_spec=pltpu.PrefetchScalarGridSpec(
            num_scalar_prefetch=1,
            grid=(NT,),
            in_specs=[
                pl.BlockSpec((TM, D_MODEL), lambda i, d: (i, 0)),
                pl.BlockSpec((1, 1, D_MODEL), lambda i, d: (i, 0, 0)),
                pl.BlockSpec((TM, TOP_K), lambda i, d: (i, 0)),
                pl.BlockSpec(memory_space=pl.ANY),
            ],
            out_specs=pl.BlockSpec((TM, D_MODEL), lambda i, d: (i, 0)),
            scratch_shapes=[pltpu.VMEM((TOP_K, TM) + ROW_TILE, F32), pltpu.SemaphoreType.DMA((1,))],
        ),
        out_shape=jax.ShapeDtypeStruct((N_TOK, D_MODEL), F32),
        compiler_params=_params("arbitrary", vmem=VMEM_LIMIT_BYTES),
        name="moe_combine",
    )(dest_flat, x, gate_vec, gates, ys)


def _moe(layer, x, g, sc, sh, gate_vec, w_router, b_router, w_gu, b_gu, w_down, b_down):
    h, idx, gates, pos, counts = _router(x, g, sc, sh, w_router, b_router)
    cnt = counts[0, :N_EXPERTS].astype(jnp.int32)
    nhalf = (cnt + MOE_HALF - 1) // MOE_HALF
    nblk = (nhalf + 1) // 2
    blk_end = jnp.cumsum(nblk)
    blk_start = blk_end - nblk
    dest = (blk_start * MOE_BLK)[idx] + pos
    last_used = blk_end[-1] - 1
    b_ids = jnp.arange(N_BLOCKS, dtype=jnp.int32)
    block_e = jnp.sum((blk_end[None, :] <= jnp.minimum(b_ids, last_used)[:, None]).astype(jnp.int32), axis=1)
    block_nh = jnp.where(b_ids <= last_used, jnp.clip(nhalf[block_e] - 2 * (b_ids - blk_start[block_e]), 0, 2), 0)
    h_ids = jnp.arange(2 * N_BLOCKS, dtype=jnp.int32)
    h_e = block_e[h_ids // 2]
    h_local = h_ids - 2 * blk_start[h_e]
    holds_rows = (h_ids // 2 <= last_used) & (h_local < nhalf[h_e])
    clear = jnp.logical_not(holds_rows) | (h_local == nhalf[h_e] - 1)
    dest_flat = dest.reshape(-1).astype(jnp.int32)
    xs = _dispatch(dest_flat, clear.astype(jnp.int32), h)
    ys = _experts(layer, block_e.astype(jnp.int32), block_nh.astype(jnp.int32),
                  last_used.reshape(1).astype(jnp.int32), xs, w_gu, b_gu, w_down, b_down)
    return _combine(dest_flat, x, gate_vec, gates, ys)


def _rope_tables(d_rot, lane0, period):
    n_rows = DEC_SEQ // GRID_W
    rows = np.repeat(np.arange(n_rows), GRID_W).astype(np.float32)
    cols = np.tile(np.arange(GRID_W), n_rows).astype(np.float32)
    half = d_rot // 2
    inv = np.float32(ROPE_THETA) ** (-np.arange(0, half, 2, dtype=np.float32) / np.float32(half))
    lane = np.arange(LANES)
    i = (lane - lane0) % period
    active = (lane >= lane0) & (i < d_rot)
    w = i % half
    f = w % (half // 2)
    pos = np.where((i // half)[None, :] == 0, rows[:, None], cols[:, None])
    ang = (pos * inv[f][None, :]).astype(np.float32).astype(np.float64)
    cos = np.where(active[None, :], np.cos(ang), 1.0)
    sin = np.where(active[None, :], np.sin(ang) * np.where(w < half // 2, -1.0, 1.0)[None, :], 0.0)
    cos = np.concatenate([np.ones((TM, LANES)), cos], axis=0).astype(np.float32)
    sin = np.concatenate([np.zeros((TM, LANES)), sin], axis=0).astype(np.float32)
    return jnp.asarray(cos), jnp.asarray(sin)


def _rope_spec():
    def index(i):
        return (jnp.where(i < NT_PROMPT, 0, 1 + (i - NT_PROMPT) % TILES_PER_SAMPLE), 0)
    return pl.BlockSpec((TM, LANES), index)


def _pad_heads(w, n_heads, width):
    lead = w.shape[:-1]
    w = w.reshape(lead + (n_heads, width))
    w = jnp.pad(w, [(0, 0)] * len(lead) + [(0, 0), (0, HEAD_PAD - width)])
    return w.reshape(lead + (n_heads * HEAD_PAD,))


_Q_ORDER = (0, 4, 1, 5, 2, 6, 3, 7, 8, 12, 9, 13, 10, 14, 11, 15)


def _perm_q_heads(w, axis):
    shape = w.shape
    n = shape[axis]
    w = jnp.moveaxis(w, axis, 0).reshape((GQA_HEADS, n // GQA_HEADS) + tuple(s for a, s in enumerate(shape) if a != axis))
    w = w[jnp.array(_Q_ORDER)]
    w = w.reshape((n,) + w.shape[2:])
    return jnp.moveaxis(w, 0, axis)


def _constants():
    mla_cos, mla_sin = _rope_tables(MLA_ROPE, MLA_NOPE, LANES)
    gqa_cos, gqa_sin = _rope_tables(GQA_HEAD_DIM, 0, GQA_HEAD_DIM)
    dft = {}
    for L in (SEQ, DEC_SEQ):
        cm, sm = _dft_tables(L)
        fmat = jnp.asarray(np.concatenate([cm, sm], axis=0), dtype=BF16)
        ftmat = jnp.asarray(np.concatenate([cm, sm], axis=0).T, dtype=BF16)
        dft[L] = (jnp.asarray(cm), jnp.asarray(sm), fmat, ftmat)
    return dict(mla=(mla_cos, mla_sin), gqa=(gqa_cos, gqa_sin), dft=dft)


def _even_mixer(x, p, i, gmix, sc1, sh1, g1, consts):
    w_in = jnp.pad(p["w_in_ab"][i], ((0, 0), (0, IN_AB_PAD - IN_AB))).astype(BF16)
    proj = _normlin(x, gmix, sc1, sh1, w_in)
    y_hy = []
    for L, blk0, nseq in ((SEQ, 0, BATCH), (DEC_SEQ, N_PROMPT // DEC_SEQ, DEC_BATCH)):
        cm, sm, fmat, ftmat = consts["dft"][L]
        kc, ks = _hy_filter(L, cm, sm, p["hy_filter_w1"][i], p["hy_filter_b1"][i], p["hy_filter_freq"][i],
                            p["hy_filter_w2"][i], p["hy_filter_b2"][i], p["hy_filter_w3"][i],
                            p["hy_filter_b3"][i], p["hy_log_decay"][i])
        y_hy.append(_hy_mix(proj, blk0, nseq, L, p["hy_conv_w"][i], p["hy_conv_b"][i], kc, ks,
                            p["hy_bias"][i], fmat, ftmat))
    y_hy = jnp.concatenate(y_hy, axis=0)

    wq = _pad_heads(p["mla_wq_b"][i], MLA_HEADS, MLA_QK).astype(BF16)
    wkv = p["mla_wkv_b"][i].reshape(MLA_KV_LORA, MLA_HEADS, MLA_NOPE + MLA_V)
    wk = _pad_heads(wkv[:, :, :MLA_NOPE].reshape(MLA_KV_LORA, -1), MLA_HEADS, MLA_NOPE).astype(BF16)
    wv = wkv[:, :, MLA_NOPE:].reshape(MLA_KV_LORA, -1).astype(BF16)
    qg = jnp.pad(p["mla_q_norm"][i], (0, HEAD_PAD - MLA_QK)).reshape(1, HEAD_PAD)
    kg = jnp.pad(p["mla_k_norm"][i], (0, HEAD_PAD - MLA_QK)).reshape(1, HEAD_PAD)
    mla_cos, mla_sin = consts["mla"]
    q, k, v, lat = _mla_prep(proj, p["mla_q_lora_norm"][i].reshape(1, -1), wq, qg,
                             p["mla_kv_lora_norm"][i].reshape(1, -1), wk, wv, kg, mla_cos, mla_sin)
    ctx = jnp.pad(p["cache_mla_latent"][:, i].reshape(DEC_BATCH * PAST_LEN, -1),
                  ((0, 0), (0, KV_PAD - MLA_KV_LORA - MLA_ROPE)))
    kc_ctx, vc_ctx = _mla_ctx(ctx, wk, wv, kg)
    o = _attention(_mla_attn_kernel, q, k, v, kc_ctx, vc_ctx,
                   MLA_HEADS * HEAD_PAD, MLA_HEADS * HEAD_PAD, MLA_HEADS * MLA_V, MLA_HEADS * MLA_V)
    w_out = p["w_out_ab"][i].astype(BF16)
    x = _outproj([y_hy, o], [w_out[:HY_D], w_out[HY_D:]], x, g1)
    return x, lat, dict(y_hy=y_hy, o=o)


def _odd_mixer(x, p, i, gmix, sc1, sh1, g1, consts):
    nq = GQA_HEADS * GQA_HEAD_DIM
    nk = GQA_KV_HEADS * GQA_HEAD_DIM
    w = p["w_qkv_c"][i]
    w_qkv = jnp.concatenate([_perm_q_heads(w[:, :nq], 1), w[:, nq:]], axis=1).astype(BF16)
    qg = jnp.tile(p["gqa_q_norm"][i], LANES // GQA_HEAD_DIM).reshape(1, LANES)
    kg = jnp.tile(p["gqa_k_norm"][i], LANES // GQA_HEAD_DIM).reshape(1, LANES)
    gqa_cos, gqa_sin = consts["gqa"]
    q, k, v, k_plain, v_plain = _gqa_prep(x, gmix, sc1, sh1, w_qkv, qg, kg, gqa_cos, gqa_sin)
    kc_ctx = p["cache_gqa_k"][:, i].reshape(DEC_BATCH * PAST_LEN, -1).astype(BF16)
    vc_ctx = p["cache_gqa_v"][:, i].reshape(DEC_BATCH * PAST_LEN, -1).astype(BF16)
    o = _attention(_gqa_attn_kernel, q, k, v, kc_ctx, vc_ctx, nq, nk, nk, nq)
    w_out = _perm_q_heads(p["w_out_c"][i], 0).astype(BF16)
    x = _outproj([o], [w_out], x, g1)
    return x, k_plain, v_plain


def kernel(x_prompt, x_sample, cache_mla_latent, cache_gqa_k, cache_gqa_v, c, c_ctx, w_ada, b_ada, norm_mix, norm_ffn, w_in_ab, hy_conv_w, hy_conv_b, hy_filter_w1, hy_filter_b1, hy_filter_freq, hy_filter_w2, hy_filter_b2, hy_filter_w3, hy_filter_b3, hy_log_decay, hy_bias, mla_q_lora_norm, mla_wq_b, mla_kv_lora_norm, mla_wkv_b, mla_q_norm, mla_k_norm, w_out_ab, w_qkv_c, gqa_q_norm, gqa_k_norm, w_out_c, moe_router_w, moe_router_b, moe_w_gate_up, moe_b_gate_up, moe_w_down, moe_b_down):
    p = dict(locals())
    x = jnp.concatenate([x_prompt.reshape(N_PROMPT, D_MODEL), x_sample.reshape(N_SAMPLE, D_MODEL)], axis=0)

    cond = jnp.concatenate([c_ctx[None, :], c, jnp.zeros((COND_ROWS - N_COND, D_MODEL), F32)], axis=0)
    mods = _modulation(cond, w_ada, b_ada)
    tile_cond = jnp.concatenate([jnp.zeros((NT_PROMPT,), jnp.int32),
                                 1 + jnp.arange(NT - NT_PROMPT, dtype=jnp.int32) // TILES_PER_SAMPLE])
    mods = mods[:, tile_cond].reshape(DEPTH, NT, 6, 1, D_MODEL).transpose(0, 2, 1, 3, 4)

    consts = _constants()

    lat_out, k_out, v_out = [], [], []
    for l in range(DEPTH):
        sh1, sc1, g1, sh2, sc2, g2 = (mods[l, j] for j in range(6))
        i = l // 2
        gmix = norm_mix[l].reshape(1, D_MODEL)
        if l % 2 == 0:
            x, lat, _ = _even_mixer(x, p, i, gmix, sc1, sh1, g1, consts)
            lat_out.append(lat[:N_PROMPT].reshape(BATCH, SEQ, -1))
        else:
            x, k_plain, v_plain = _odd_mixer(x, p, i, gmix, sc1, sh1, g1, consts)
            k_out.append(k_plain[:N_PROMPT].reshape(BATCH, SEQ, GQA_KV_HEADS, GQA_HEAD_DIM))
            v_out.append(v_plain[:N_PROMPT].reshape(BATCH, SEQ, GQA_KV_HEADS, GQA_HEAD_DIM))
        x = _moe(l, x, norm_ffn[l].reshape(1, D_MODEL), sc2, sh2, g2, moe_router_w[l], moe_router_b[l],
                 moe_w_gate_up, moe_b_gate_up, moe_w_down, moe_b_down)

    y_prompt = x[:N_PROMPT].reshape(BATCH, SEQ, D_MODEL)
    y_sample = x[N_PROMPT:].reshape(DEC_BATCH, DEC_SEQ, D_MODEL)
    return (y_prompt, y_sample, jnp.stack(lat_out, axis=1), jnp.stack(k_out, axis=1), jnp.stack(v_out, axis=1))
```

```python
import functools
import math

import jax
import jax.numpy as jnp
import numpy as np
from jax import lax
from jax.experimental import pallas as pl
from jax.experimental.pallas import tpu as pltpu

F32 = jnp.float32
BF16 = jnp.bfloat16

D_MODEL = 1024
BATCH = 32
SEQ = 256
DEPTH = 4
DEC_BATCH = 4
DEC_SEQ = 1024
PAST_LEN = 256
GRID_W = 64
N_EVEN = (DEPTH + 1) // 2
N_ODD = DEPTH // 2
HY_D = D_MODEL // 2
HY_ORDER = 2
HY_BANDS = 16
HY_EMB = 2 * HY_BANDS + 1
HY_FILTER_HIDDEN = 64
MLA_HEADS = 8
MLA_NOPE = 64
MLA_ROPE = 32
MLA_QK = MLA_NOPE + MLA_ROPE
MLA_V = HY_D // MLA_HEADS
MLA_Q_LORA = 3 * D_MODEL // 8
MLA_KV_LORA = D_MODEL // 4
GQA_HEADS = 16
GQA_KV_HEADS = 4
GQA_HEAD_DIM = D_MODEL // GQA_HEADS
N_EXPERTS = 32
TOP_K = 4
D_EXPERT = D_MODEL
SWIGLU_LIMIT = 7.0
SWIGLU_ALPHA = 1.702
ROPE_THETA = 10000.0
RMS_EPS = 1e-6
GQA_SCALE = GQA_HEAD_DIM ** -0.5
assert math.frexp(GQA_SCALE)[0] == 0.5
IN_AB = (HY_ORDER + 1) * HY_D + MLA_Q_LORA + MLA_KV_LORA + MLA_ROPE

N_PROMPT = BATCH * SEQ
N_SAMPLE = DEC_BATCH * DEC_SEQ
N_TOK = N_PROMPT + N_SAMPLE

LANES = 128
SUBLANES = 8
VMEM_LIMIT_BYTES = 56 * 1024 * 1024

TM = 256
NT = N_TOK // TM
NT_PROMPT = N_PROMPT // TM
TILES_PER_SAMPLE = DEC_SEQ // TM
N_COND = 1 + DEC_BATCH
COND_ROWS = 8
HEAD_PAD = LANES
IN_AB_PAD = 2304
KV_PAD = 384
MOE_HALF = 256
MOE_BLK = 2 * MOE_HALF
N_SLOTS = N_TOK * TOP_K
N_BLOCKS = N_SLOTS // MOE_BLK + N_EXPERTS
CAP = N_BLOCKS * MOE_BLK
ROW_TILE = (SUBLANES, LANES)
assert SUBLANES * LANES == D_MODEL
HY_CH = 256
HY_PROMPT_SEQS = 8


def _dot(a, b):
    return jnp.dot(a, b, preferred_element_type=F32)


def _dot_nt(a, b):
    return lax.dot_general(a, b, (((1,), (1,)), ((), ())), preferred_element_type=F32)


def _split(x):
    hi = x.astype(BF16)
    lo = (x - hi.astype(F32)).astype(BF16)
    return hi, lo


def _dot3(a, b):
    ah, al = _split(a)
    bh, bl = _split(b)
    return _dot(ah, bh) + (_dot(ah, bl) + _dot(al, bh))


def _lane_iota(shape):
    return lax.broadcasted_iota(jnp.int32, shape, len(shape) - 1)


def _params(*sem, vmem=None):
    return pltpu.CompilerParams(dimension_semantics=sem, vmem_limit_bytes=vmem)


def _mod_kernel(c_ref, w_ref, b_ref, o_ref):
    c = c_ref[...]
    s = c * jax.nn.sigmoid(c)
    o_ref[0] = _dot(s.astype(BF16), w_ref[0].astype(BF16)) + b_ref[0]


def _modulation(cond, w_ada, b_ada):
    nblk = 6
    return pl.pallas_call(
        _mod_kernel,
        grid=(DEPTH, nblk),
        in_specs=[
            pl.BlockSpec((COND_ROWS, D_MODEL), lambda l, j: (0, 0)),
            pl.BlockSpec((1, D_MODEL, D_MODEL), lambda l, j: (l, 0, j)),
            pl.BlockSpec((1, 1, D_MODEL), lambda l, j: (l, 0, j)),
        ],
        out_specs=pl.BlockSpec((1, COND_ROWS, D_MODEL), lambda l, j: (l, 0, j)),
        out_shape=jax.ShapeDtypeStruct((DEPTH, COND_ROWS, nblk * D_MODEL), F32),
        compiler_params=_params("arbitrary", "arbitrary"),
        name="modulation",
    )(cond, w_ada, b_ada.reshape(DEPTH, 1, nblk * D_MODEL))


def _norm_mod(x, g, sc, sh):
    ms = jnp.mean(x * x, axis=-1, keepdims=True)
    y = x * lax.rsqrt(ms + RMS_EPS)
    return (y * g) * (1.0 + sc) + sh


def _row_spec(width):
    return pl.BlockSpec((TM, width), lambda i: (i, 0))


def _tile_vec_spec():
    return pl.BlockSpec((1, 1, D_MODEL), lambda i: (i, 0, 0))


def _full_spec(shape):
    nd = len(shape)
    return pl.BlockSpec(shape, lambda i: (0,) * nd)


def _normlin_kernel(x_ref, g_ref, sc_ref, sh_ref, w_ref, o_ref, w_scr):
    nin = w_ref.shape[1]

    @pl.when(pl.program_id(0) == 0)
    def _():
        w_scr[:, :nin] = w_ref[...].astype(BF16)
        w_scr[:, nin:] = jnp.zeros((D_MODEL, w_scr.shape[1] - nin), BF16)

    h = _norm_mod(x_ref[...], g_ref[...], sc_ref[0], sh_ref[0])
    o_ref[...] = _dot(h.astype(BF16), w_scr[...])


def _normlin(x, g, sc, sh, w_all, layer, nout):
    nin = w_all.shape[2]
    return pl.pallas_call(
        _normlin_kernel,
        grid=(NT,),
        in_specs=[_row_spec(D_MODEL), _full_spec((1, D_MODEL)), _tile_vec_spec(), _tile_vec_spec(),
                  pl.BlockSpec((None, D_MODEL, nin), lambda i: (layer, 0, 0))],
        out_specs=_row_spec(nout),
        out_shape=jax.ShapeDtypeStruct((N_TOK, nout), F32),
        scratch_shapes=[pltpu.VMEM((D_MODEL, nout), BF16)],
        compiler_params=_params("arbitrary", vmem=VMEM_LIMIT_BYTES),
        name="normlin",
    )(x, g, sc, sh, w_all)


def _hy_filter_kernel(z_ref, w1_ref, b1_ref, fr_ref, w2_ref, b2_ref, w3_ref, b3_ref, ed_ref,
                      c_ref, s_ref, kc_ref, ks_ref):
    L = z_ref.shape[0]
    z = z_ref[...]
    fr = fr_ref[...]
    hdn = jnp.sin(fr * (_dot3(z, w1_ref[...]) + b1_ref[...]))
    hdn = jnp.sin(fr * (_dot3(hdn, w2_ref[...]) + b2_ref[...]))
    filt = _dot3(hdn, w3_ref[...]) + b3_ref[...]
    t = z[:, 0:1]
    filt = filt * jnp.exp(-t * ed_ref[...])
    row = lax.broadcasted_iota(jnp.int32, (L, HY_D), 0)
    cm = c_ref[...]
    sm = s_ref[...]
    for o in range(HY_ORDER):
        fw = filt[:, (2 * o) * HY_D:(2 * o + 1) * HY_D]
        bw = filt[:, (2 * o + 1) * HY_D:(2 * o + 2) * HY_D]
        den = (jnp.sum(jnp.abs(fw), axis=0, keepdims=True)
               + jnp.sum(jnp.abs(bw), axis=0, keepdims=True)) + 1e-6
        fw = fw / den
        bw = jnp.where(row == 0, 0.0, bw / den)
        kc_ref[o] = _dot3(cm, fw + bw) * (1.0 / L)
        ks_ref[o] = _dot3(sm, fw - bw) * (1.0 / L)


def _dft_tables(L, transpose=False):
    m = jnp.arange(L, dtype=jnp.int32)
    odd = 2 * m + 1
    phase = (m[:, None] * odd[None, :] if transpose else odd[:, None] * m[None, :]) % (4 * L)
    ang = phase.astype(F32) * (2.0 * math.pi / (4 * L))
    return jnp.cos(ang), jnp.sin(ang)


def _filter_features(L):
    p = jnp.arange(L, dtype=F32)
    t = p / max(L - 1, 1)
    bands = jnp.linspace(1e-4, HY_BANDS - 1, HY_BANDS, dtype=F32)
    ang = (2.0 * math.pi / L) * p[:, None] * bands[None, :]
    z = jnp.concatenate([t[:, None], jnp.cos(ang), -jnp.sin(ang)], axis=-1)
    return jnp.pad(z, ((0, 0), (0, LANES - HY_EMB)))


def _hy_filter(L, cmat, smat, w1, b1, fr, w2, b2, w3, b3, log_decay):
    nf = HY_ORDER * 2 * HY_D
    args = (
        _filter_features(L),
        jnp.pad(w1, ((0, LANES - HY_EMB), (0, 0))),
        b1.reshape(1, -1), fr.reshape(1, -1), w2, b2.reshape(1, -1), w3, b3.reshape(1, -1),
        jnp.exp(log_decay.astype(F32)).reshape(1, nf),
        cmat, smat,
    )
    out_sds = jax.ShapeDtypeStruct((HY_ORDER, L, HY_D), F32)
    return pl.pallas_call(
        _hy_filter_kernel,
        grid=(1,),
        in_specs=[_full_spec(a.shape) for a in args],
        out_specs=[_full_spec(out_sds.shape)] * 2,
        out_shape=[out_sds, out_sds],
        compiler_params=_params("arbitrary", vmem=VMEM_LIMIT_BYTES),
        name=f"hy_filter_{L}",
    )(*args)


def _hy_mix_kernel(u0_ref, u1_ref, u2_ref, cw0_ref, cw1_ref, cw2_ref, cb0_ref, cb1_ref, cb2_ref,
                   kc_ref, ks_ref, hb_ref, f_ref, ft_ref, o_ref):
    S, L, _ = u0_ref.shape
    row = lax.broadcasted_iota(jnp.int32, (L, HY_CH), 0)

    def lanes(per_seq):
        return per_seq[0] if S == 1 else jnp.concatenate(per_seq, axis=1)

    def short_conv(u_ref, cw_ref, cb_ref):
        w = cw_ref[0]
        out = []
        for s in range(S):
            u = u_ref[s]
            prev = jnp.where(row == 0, 0.0, pltpu.roll(u, 1, 0))
            nxt = jnp.where(row == L - 1, 0.0, pltpu.roll(u, L - 1, 0))
            out.append((prev * w[0:1] + u * w[1:2]) + nxt * w[2:3] + cb_ref[0])
        return lanes(out)

    z = short_conv(u0_ref, cw0_ref, cb0_ref)
    gates = (short_conv(u1_ref, cw1_ref, cb1_ref), short_conv(u2_ref, cw2_ref, cb2_ref))
    for o in range(HY_ORDER):
        zz = _dot(f_ref[...], z.astype(BF16))
        cz, sz = zz[:L], zz[L:]
        kc, ks = lanes([kc_ref[o]] * S), lanes([ks_ref[o]] * S)
        w1 = cz * kc - sz * ks
        w2 = cz * ks + sz * kc
        ww = jnp.concatenate([w1, w2], axis=0).astype(BF16)
        conv = _dot(ft_ref[...], ww)
        z = gates[o] * (conv + z * lanes([hb_ref[0, o]] * S))
    for s in range(S):
        o_ref[s] = z[:, s * HY_CH:(s + 1) * HY_CH].astype(o_ref.dtype)


def _hy_mix(proj, row_block0, nseq, seqs_per_step, L, cw, cb, kc, ks, hb, fmat, ftmat):
    nch = HY_D // HY_CH
    nparts = HY_ORDER + 1
    S = seqs_per_step
    assert nseq % S == 0 and row_block0 % S == 0
    cw3 = cw.reshape(3, nparts * nch, HY_CH).transpose(1, 0, 2)
    cb3 = cb.reshape(nparts * nch, 1, HY_CH)
    hb3 = hb.reshape(HY_ORDER, nch, 1, HY_CH).transpose(1, 0, 2, 3)
    proj = proj.reshape(N_TOK // L, L, proj.shape[1])

    def u_spec(part):
        return pl.BlockSpec((S, L, HY_CH), lambda s, c: (row_block0 // S + s, 0, part * nch + c))

    def cw_spec(part):
        return pl.BlockSpec((1, 3, HY_CH), lambda s, c: (part * nch + c, 0, 0))

    def cb_spec(part):
        return pl.BlockSpec((1, 1, HY_CH), lambda s, c: (part * nch + c, 0, 0))

    return pl.pallas_call(
        _hy_mix_kernel,
        grid=(nseq // S, nch),
        in_specs=[
            u_spec(0), u_spec(1), u_spec(2),
            cw_spec(0), cw_spec(1), cw_spec(2),
            cb_spec(0), cb_spec(1), cb_spec(2),
            pl.BlockSpec((HY_ORDER, L, HY_CH), lambda s, c: (0, 0, c)),
            pl.BlockSpec((HY_ORDER, L, HY_CH), lambda s, c: (0, 0, c)),
            pl.BlockSpec((1, HY_ORDER, 1, HY_CH), lambda s, c: (c, 0, 0, 0)),
            pl.BlockSpec((2 * L, L), lambda s, c: (0, 0)),
            pl.BlockSpec((L, 2 * L), lambda s, c: (0, 0)),
        ],
        out_specs=pl.BlockSpec((S, L, HY_CH), lambda s, c: (s, 0, c)),
        out_shape=jax.ShapeDtypeStruct((nseq, L, HY_D), BF16),
        compiler_params=_params("arbitrary", "arbitrary", vmem=VMEM_LIMIT_BYTES),
        name=f"hy_mix_{L}",
    )(proj, proj, proj, cw3, cw3, cw3, cb3, cb3, cb3, kc, ks, hb3, fmat, ftmat).reshape(nseq * L, HY_D)


def _head_rms(xh, gain, dim):
    ms = jnp.sum(xh * xh, axis=-1, keepdims=True) * (1.0 / dim)
    return (xh * lax.rsqrt(ms + RMS_EPS)) * gain


def _per_group(context_fn, latent_fn):
    is_context = pl.program_id(0) < NT_PROMPT
    pl.when(is_context)(context_fn)
    pl.when(jnp.logical_not(is_context))(latent_fn)


def _rope(xh, cos, sin, half):
    lane = _lane_iota(xh.shape)
    first = (lane % (2 * half)) < half
    rot = jnp.where(first, pltpu.roll(xh, LANES - half, 1), pltpu.roll(xh, half, 1))
    return xh * cos + rot * sin


def _mla_keys_values(lat_n, kr_blk, wkv_ref, kg_ref, rope):
    kv = _dot(lat_n.astype(BF16), wkv_ref[...])
    kr = pltpu.roll(kr_blk, MLA_NOPE, 1)
    nope = _lane_iota(kr.shape) < MLA_NOPE
    ks = []
    for h in range(MLA_HEADS):
        kh = jnp.where(nope, kv[:, h * HEAD_PAD:(h + 1) * HEAD_PAD], 0.0) + kr
        kh = _head_rms(kh, kg_ref[...], MLA_QK)
        if rope is not None:
            kh = _rope(kh, rope[0], rope[1], MLA_ROPE // 4)
        ks.append(kh)
    return ks, kv


def _mla_prep_kernel(qa_ref, kva_ref, gq_ref, wq_ref, qg_ref, gkv_ref, wkv_ref, kg_ref,
                     cos_ref, sin_ref, q_ref, k_ref, v_ref, lat_ref):
    def body(rope):
        qa = qa_ref[...]
        ms = jnp.mean(qa * qa, axis=-1, keepdims=True)
        qn = (qa * lax.rsqrt(ms + RMS_EPS)) * gq_ref[...]
        q = _dot(qn.astype(BF16), wq_ref[...])
        for h in range(MLA_HEADS):
            qh = _head_rms(q[:, h * HEAD_PAD:(h + 1) * HEAD_PAD], qg_ref[...], MLA_QK)
            if rope is not None:
                qh = _rope(qh, rope[0], rope[1], MLA_ROPE // 4)
            q_ref[:, h * HEAD_PAD:(h + 1) * HEAD_PAD] = qh.astype(BF16)

        kva = kva_ref[...]
        lat = kva[:, :MLA_KV_LORA]
        ms = jnp.mean(lat * lat, axis=-1, keepdims=True)
        lat_n = (lat * lax.rsqrt(ms + RMS_EPS)) * gkv_ref[...]
        kr_blk = kva[:, MLA_KV_LORA:]
        lat_ref[:, :MLA_KV_LORA] = lat_n
        lat_ref[:, MLA_KV_LORA:] = kr_blk[:, :MLA_ROPE]
        ks, v = _mla_keys_values(lat_n, kr_blk, wkv_ref, kg_ref, rope)
        for h in range(MLA_HEADS):
            k_ref[:, h * HEAD_PAD:(h + 1) * HEAD_PAD] = ks[h].astype(BF16)
        v_ref[...] = v.astype(BF16)

    _per_group(lambda: body(None), lambda: body((cos_ref[...], sin_ref[...])))


def _mla_prep(proj, gq, wq, qg, gkv, wkv, kg, cos_t, sin_t):
    nq = MLA_HEADS * HEAD_PAD
    nv = MLA_HEADS * (MLA_NOPE + MLA_V)
    qa_blk = (HY_ORDER + 1) * HY_D // KV_PAD
    return pl.pallas_call(
        _mla_prep_kernel,
        grid=(NT,),
        in_specs=[
            pl.BlockSpec((TM, KV_PAD), lambda i: (i, qa_blk)),
            pl.BlockSpec((TM, KV_PAD), lambda i: (i, qa_blk + 1)),
            _full_spec(gq.shape), _full_spec(wq.shape), _full_spec(qg.shape), _full_spec(gkv.shape),
            _full_spec(wkv.shape), _full_spec(kg.shape),
            _rope_spec(), _rope_spec(),
        ],
        out_specs=[_row_spec(nq), _row_spec(nq), _row_spec(nv), _row_spec(MLA_KV_LORA + MLA_ROPE)],
        out_shape=[
            jax.ShapeDtypeStruct((N_TOK, nq), BF16),
            jax.ShapeDtypeStruct((N_TOK, nq), BF16),
            jax.ShapeDtypeStruct((N_TOK, nv), BF16),
            jax.ShapeDtypeStruct((N_TOK, MLA_KV_LORA + MLA_ROPE), F32),
        ],
        compiler_params=_params("arbitrary", vmem=VMEM_LIMIT_BYTES),
        name="mla_prep",
    )(proj, proj, gq, wq, qg, gkv, wkv, kg, cos_t, sin_t)


def _mla_ctx_kernel(lat_ref, wkv_ref, kg_ref, k_ref, v_ref):
    lat = lat_ref[...]
    ks, v = _mla_keys_values(lat[:, :MLA_KV_LORA], lat[:, MLA_KV_LORA:], wkv_ref, kg_ref, None)
    for h in range(MLA_HEADS):
        k_ref[:, h * HEAD_PAD:(h + 1) * HEAD_PAD] = ks[h].astype(BF16)
    v_ref[...] = v.astype(BF16)


def _mla_ctx(lat_pad, wkv, kg):
    n = lat_pad.shape[0]
    nq = MLA_HEADS * HEAD_PAD
    nv = MLA_HEADS * (MLA_NOPE + MLA_V)
    return pl.pallas_call(
        _mla_ctx_kernel,
        grid=(n // TM,),
        in_specs=[_row_spec(KV_PAD), _full_spec(wkv.shape), _full_spec(kg.shape)],
        out_specs=[_row_spec(nq), _row_spec(nv)],
        out_shape=[jax.ShapeDtypeStruct((n, nq), BF16), jax.ShapeDtypeStruct((n, nv), BF16)],
        compiler_params=_params("arbitrary"),
        name="mla_ctx",
    )(lat_pad, wkv, kg)


def _softmax_pv(q, keys, vals, scale):
    ss = [_dot_nt(q, k) for k in keys]
    if scale is not None:
        ss = [s * scale for s in ss]
    m = ss[0].max(axis=-1, keepdims=True)
    for s in ss[1:]:
        m = jnp.maximum(m, s.max(axis=-1, keepdims=True))
    ps = [jnp.exp(s - m) for s in ss]
    l = ps[0].sum(axis=-1, keepdims=True)
    for p in ps[1:]:
        l = l + p.sum(axis=-1, keepdims=True)
    o = _dot(ps[0].astype(BF16), vals[0])
    for p, v in zip(ps[1:], vals[1:]):
        o = o + _dot(p.astype(BF16), v)
    return o / l


def _mla_attn_kernel(*refs, has_ctx):
    if has_ctx:
        q_ref, k_ref, v_ref, kc_ref, vc_ref, o_ref = refs
    else:
        q_ref, k_ref, v_ref, o_ref = refs
    scale = MLA_QK ** -0.5
    lo = _lane_iota((q_ref.shape[0], LANES)) < MLA_V
    for j in range(MLA_HEADS // 2):
        outs = []
        for h in (2 * j, 2 * j + 1):
            sl = slice(h * HEAD_PAD, (h + 1) * HEAD_PAD)
            ks, vs = [k_ref[:, sl]], [v_ref[:, sl]]
            if has_ctx:
                ks.append(kc_ref[:, sl])
                vs.append(vc_ref[:, sl])
            outs.append(_softmax_pv(q_ref[:, sl], ks, vs, scale))
        pair = jnp.where(lo, pltpu.roll(outs[0], MLA_V, 1), outs[1])
        o_ref[:, j * LANES:(j + 1) * LANES] = pair.astype(o_ref.dtype)


def _gqa_attn_kernel(*refs, has_ctx):
    if has_ctx:
        q_ref, k_ref, v_ref, kc_ref, vc_ref, o_ref = refs
    else:
        q_ref, k_ref, v_ref, o_ref = refs
    scale = None
    lo = _lane_iota((q_ref.shape[0], LANES)) < GQA_HEAD_DIM
    pairs_per_kv = (GQA_HEADS // 2) // (GQA_KV_HEADS // 2)
    for p in range(GQA_HEADS // 2):
        kv = slice((p // pairs_per_kv) * LANES, (p // pairs_per_kv + 1) * LANES)
        ks, vs = [k_ref[:, kv]], [v_ref[:, kv]]
        if has_ctx:
            ks.append(kc_ref[:, kv])
            vs.append(vc_ref[:, kv])
        qp = q_ref[:, p * LANES:(p + 1) * LANES]
        zero = jnp.zeros_like(qp)
        o_lo = _softmax_pv(jnp.where(lo, qp, zero), ks, vs, scale)
        o_hi = _softmax_pv(jnp.where(lo, zero, qp), ks, vs, scale)
        o_ref[:, p * LANES:(p + 1) * LANES] = jnp.where(lo, o_lo, o_hi).astype(o_ref.dtype)


def _attention(body, q, k, v, kc, vc, wq, wk, wv, wo):
    outs = []
    outs.append(pl.pallas_call(
        functools.partial(body, has_ctx=False),
        grid=(BATCH,),
        in_specs=[pl.BlockSpec((SEQ, wq), lambda b: (b, 0)),
                  pl.BlockSpec((SEQ, wk), lambda b: (b, 0)),
                  pl.BlockSpec((SEQ, wv), lambda b: (b, 0))],
        out_specs=pl.BlockSpec((SEQ, wo), lambda b: (b, 0)),
        out_shape=jax.ShapeDtypeStruct((N_PROMPT, wo), BF16),
        compiler_params=_params("arbitrary", vmem=VMEM_LIMIT_BYTES),
        name="attn_prompt",
    )(q, k, v))
    qt = DEC_SEQ // TM
    q0 = N_PROMPT // TM
    s0 = N_PROMPT // DEC_SEQ
    outs.append(pl.pallas_call(
        functools.partial(body, has_ctx=True),
        grid=(DEC_BATCH, qt),
        in_specs=[pl.BlockSpec((TM, wq), lambda b, t: (q0 + b * qt + t, 0)),
                  pl.BlockSpec((DEC_SEQ, wk), lambda b, t: (s0 + b, 0)),
                  pl.BlockSpec((DEC_SEQ, wv), lambda b, t: (s0 + b, 0)),
                  pl.BlockSpec((PAST_LEN, wk), lambda b, t: (b, 0)),
                  pl.BlockSpec((PAST_LEN, wv), lambda b, t: (b, 0))],
        out_specs=pl.BlockSpec((TM, wo), lambda b, t: (b * qt + t, 0)),
        out_shape=jax.ShapeDtypeStruct((N_SAMPLE, wo), BF16),
        compiler_params=_params("arbitrary", "arbitrary", vmem=VMEM_LIMIT_BYTES),
        name="attn_sample",
    )(q, k, v, kc, vc))
    return outs


def _gqa_prep_kernel(x_ref, g_ref, sc_ref, sh_ref, w_ref, qg_ref, kg_ref, cos_ref, sin_ref,
                     q_ref, k_ref, v_ref, kp_ref, vp_ref):
    lo = _lane_iota((TM, LANES)) < GQA_HEAD_DIM
    nq = GQA_HEADS * GQA_HEAD_DIM
    nk = GQA_KV_HEADS * GQA_HEAD_DIM

    def pair_norm(xp, gain):
        sq = xp * xp
        ms_lo = jnp.sum(jnp.where(lo, sq, 0.0), axis=-1, keepdims=True)
        ms_hi = jnp.sum(jnp.where(lo, 0.0, sq), axis=-1, keepdims=True)
        ms = jnp.where(lo, ms_lo, ms_hi) * (1.0 / GQA_HEAD_DIM)
        return (xp * lax.rsqrt(ms + RMS_EPS)) * gain

    def body(rope):
        def rotate(xp):
            return xp if rope is None else _rope(xp, rope[0], rope[1], GQA_HEAD_DIM // 4)

        h = _norm_mod(x_ref[...], g_ref[...], sc_ref[0], sh_ref[0])
        qkv = _dot(h.astype(BF16), w_ref[...])
        for p in range(nq // LANES):
            sl = slice(p * LANES, (p + 1) * LANES)
            qn = pair_norm(qkv[:, sl], qg_ref[...])
            q_ref[:, sl] = (rotate(qn) * GQA_SCALE).astype(BF16)
        for p in range(nk // LANES):
            sl = slice(p * LANES, (p + 1) * LANES)
            kn = pair_norm(qkv[:, nq + p * LANES:nq + (p + 1) * LANES], kg_ref[...])
            kp_ref[:, sl] = kn
            k_ref[:, sl] = rotate(kn).astype(BF16)
        v = qkv[:, nq + nk:]
        vp_ref[...] = v
        v_ref[...] = v.astype(BF16)

    _per_group(lambda: body(None), lambda: body((cos_ref[...], sin_ref[...])))


def _gqa_prep(x, g, sc, sh, w, qg, kg, cos_t, sin_t):
    nq = GQA_HEADS * GQA_HEAD_DIM
    nk = GQA_KV_HEADS * GQA_HEAD_DIM
    return pl.pallas_call(
        _gqa_prep_kernel,
        grid=(NT,),
        in_specs=[_row_spec(D_MODEL), _full_spec((1, D_MODEL)), _tile_vec_spec(), _tile_vec_spec(),
                  _full_spec(w.shape), _full_spec(qg.shape), _full_spec(kg.shape),
                  _rope_spec(), _rope_spec()],
        out_specs=[_row_spec(nq), _row_spec(nk), _row_spec(nk), _row_spec(nk), _row_spec(nk)],
        out_shape=[
            jax.ShapeDtypeStruct((N_TOK, nq), BF16),
            jax.ShapeDtypeStruct((N_TOK, nk), BF16),
            jax.ShapeDtypeStruct((N_TOK, nk), BF16),
            jax.ShapeDtypeStruct((N_TOK, nk), F32),
            jax.ShapeDtypeStruct((N_TOK, nk), F32),
        ],
        compiler_params=_params("arbitrary", vmem=VMEM_LIMIT_BYTES),
        name="gqa_prep",
    )(x, g, sc, sh, w, qg, kg, cos_t, sin_t)


def _outproj_kernel(*refs, n_in):
    ap_refs = refs[:n_in]
    as_refs = refs[n_in:2 * n_in]
    w_refs = refs[2 * n_in:3 * n_in]
    x_ref, g_ref, o_ref = refs[3 * n_in:3 * n_in + 3]
    w_scr = refs[3 * n_in + 3:]
    is_prompt = pl.program_id(0) < NT_PROMPT

    @pl.when(pl.program_id(0) == 0)
    def _():
        for w, s in zip(w_refs, w_scr):
            s[...] = w[...].astype(BF16)

    y = None
    for ap, asm, s in zip(ap_refs, as_refs, w_scr):
        a = jnp.where(is_prompt, ap[...], asm[...])
        d = _dot(a, s[...])
        y = d if y is None else y + d
    o_ref[...] = x_ref[...] + g_ref[0] * y


def _outproj(acts, ws, x, gate):
    n_in = len(acts)

    def w_spec(index, rows):
        return pl.BlockSpec((None, rows, D_MODEL), lambda i: index)

    def prompt_spec(width):
        return pl.BlockSpec((TM, width), lambda i: (jnp.minimum(i, NT_PROMPT - 1), 0))

    def sample_spec(width):
        return pl.BlockSpec((TM, width), lambda i: (jnp.maximum(i - NT_PROMPT, 0), 0))

    return pl.pallas_call(
        functools.partial(_outproj_kernel, n_in=n_in),
        grid=(NT,),
        in_specs=([prompt_spec(ap.shape[1]) for ap, _ in acts] + [sample_spec(asm.shape[1]) for _, asm in acts]
                  + [w_spec(index, rows) for _, index, rows in ws] + [_row_spec(D_MODEL), _tile_vec_spec()]),
        out_specs=_row_spec(D_MODEL),
        out_shape=jax.ShapeDtypeStruct((N_TOK, D_MODEL), F32),
        scratch_shapes=[pltpu.VMEM((rows, D_MODEL), BF16) for _, _, rows in ws],
        compiler_params=_params("arbitrary", vmem=VMEM_LIMIT_BYTES),
        name="outproj",
    )(*(ap for ap, _ in acts), *(asm for _, asm in acts), *(w for w, _, _ in ws), x, gate)


def _router_kernel(x_ref, g_ref, sc_ref, sh_ref, wh_ref, wl_ref, br_ref, tri_ref,
                   h_ref, idx_ref, gate_ref, pos_ref, cnt_ref, run_ref):
    @pl.when(pl.program_id(0) == 0)
    def _():
        run_ref[...] = jnp.zeros_like(run_ref)

    h = _norm_mod(x_ref[...], g_ref[...], sc_ref[0], sh_ref[0])
    h_ref[...] = h.reshape((TM,) + ROW_TILE)
    hh, hl = _split(h)
    logits = _dot(hh, wh_ref[...]) + (_dot(hh, wl_ref[...]) + _dot(hl, wh_ref[...])) + br_ref[...]
    lane = _lane_iota((TM, LANES)).astype(F32)
    neg = jnp.float32(-jnp.inf)
    lg = jnp.where(lane < N_EXPERTS, logits, neg)
    tops, sels, hots = [], [], []
    for _ in range(TOP_K):
        m = lg.max(axis=-1, keepdims=True)
        sel = jnp.where(lg == m, lane, float(LANES)).min(axis=-1, keepdims=True)
        hot = lane == sel
        lg = jnp.where(hot, neg, lg)
        tops.append(m)
        sels.append(sel)
        hots.append(hot)
    es = [jnp.exp(t - tops[0]) for t in tops]
    den = es[0] + es[1] + es[2] + es[3]
    member = jnp.zeros((TM, LANES), F32)
    for hot in hots:
        member = member + hot.astype(F32)
    ranks = _dot(tri_ref[...], member.astype(BF16)) + run_ref[...]
    lane4 = _lane_iota((TM, TOP_K))
    idx4 = jnp.zeros((TM, TOP_K), F32)
    gate4 = jnp.zeros((TM, TOP_K), F32)
    pos4 = jnp.zeros((TM, TOP_K), F32)
    for k in range(TOP_K):
        pk = jnp.sum(jnp.where(hots[k], ranks, 0.0), axis=-1, keepdims=True)
        idx4 = jnp.where(lane4 == k, sels[k], idx4)
        gate4 = jnp.where(lane4 == k, es[k] / den, gate4)
        pos4 = jnp.where(lane4 == k, pk, pos4)
    idx_ref[...] = idx4.astype(jnp.int32)
    gate_ref[...] = gate4
    pos_ref[...] = pos4.astype(jnp.int32)
    run_ref[...] = run_ref[...] + jnp.sum(member, axis=0, keepdims=True)
    cnt_ref[...] = run_ref[...]


def _router(x, g, sc, sh, w_router, b_router):
    wpad = jnp.pad(w_router, ((0, 0), (0, LANES - N_EXPERTS)))
    wh = wpad.astype(BF16)
    wl = (wpad - wh.astype(F32)).astype(BF16)
    bpad = jnp.pad(b_router, (0, LANES - N_EXPERTS)).reshape(1, LANES)
    r = np.arange(TM)
    tri = jnp.asarray(r[None, :] < r[:, None], dtype=BF16)
    narrow = pl.BlockSpec((TM, TOP_K), lambda i: (i, 0))
    return pl.pallas_call(
        _router_kernel,
        grid=(NT,),
        in_specs=[_row_spec(D_MODEL), _full_spec((1, D_MODEL)), _tile_vec_spec(), _tile_vec_spec(),
                  _full_spec(wh.shape), _full_spec(wl.shape), _full_spec(bpad.shape), _full_spec(tri.shape)],
        out_specs=[pl.BlockSpec((TM,) + ROW_TILE, lambda i: (i, 0, 0)), narrow, narrow, narrow,
                   _full_spec((1, LANES))],
        out_shape=[
            jax.ShapeDtypeStruct((N_TOK,) + ROW_TILE, F32),
            jax.ShapeDtypeStruct((N_TOK, TOP_K), jnp.int32),
            jax.ShapeDtypeStruct((N_TOK, TOP_K), F32),
            jax.ShapeDtypeStruct((N_TOK, TOP_K), jnp.int32),
            jax.ShapeDtypeStruct((1, LANES), F32),
        ],
        scratch_shapes=[pltpu.VMEM((1, LANES), F32)],
        compiler_params=_params("arbitrary", vmem=VMEM_LIMIT_BYTES),
        name="router",
    )(x, g, sc, sh, wh, wl, bpad, tri)


ROW_UNROLL = 2
DMA_QUEUES = 2


def _copy_all_rows(row_copy):
    def start_rows(j, carry):
        for u in range(ROW_UNROLL):
            for k in range(TOP_K):
                row_copy(j * ROW_UNROLL + u, k).start(priority=k % DMA_QUEUES)
        return carry

    def wait_rows(j, carry):
        for u in range(ROW_UNROLL):
            for k in range(TOP_K):
                row_copy(j * ROW_UNROLL + u, k).wait()
        return carry

    lax.fori_loop(0, TM // ROW_UNROLL, start_rows, 0)
    lax.fori_loop(0, TM // ROW_UNROLL, wait_rows, 0)


def _dispatch_kernel(dest_ref, clear_ref, h_ref, xs_ref, zero_ref, sem_ref):
    def row_copy(t, k):
        d = dest_ref[(pl.program_id(0) * TM + t) * TOP_K + k]
        return pltpu.make_async_copy(h_ref.at[pl.ds(t, 1)], xs_ref.at[pl.ds(d, 1)], sem_ref.at[0])

    def zero_half(j):
        row0 = pl.multiple_of(j * MOE_HALF, MOE_HALF)
        return pltpu.make_async_copy(zero_ref, xs_ref.at[pl.ds(row0, MOE_HALF)], sem_ref.at[1])

    @pl.when(pl.program_id(0) == 0)
    def _():
        zero_ref[...] = jnp.zeros_like(zero_ref)

        def start(j, carry):
            @pl.when(clear_ref[j] > 0)
            def _():
                zero_half(j).start()
            return carry

        def wait(j, carry):
            @pl.when(clear_ref[j] > 0)
            def _():
                zero_half(j).wait()
            return carry

        lax.fori_loop(0, 2 * N_BLOCKS, start, 0)
        lax.fori_loop(0, 2 * N_BLOCKS, wait, 0)

    _copy_all_rows(row_copy)


def _dispatch(dest_flat, clear, h):
    return pl.pallas_call(
        _dispatch_kernel,
        grid_spec=pltpu.PrefetchScalarGridSpec(
            num_scalar_prefetch=2,
            grid=(NT,),
            in_specs=[pl.BlockSpec((TM,) + ROW_TILE, lambda i, d, c: (i, 0, 0))],
            out_specs=pl.BlockSpec(memory_space=pl.ANY),
            scratch_shapes=[pltpu.VMEM((MOE_HALF,) + ROW_TILE, F32), pltpu.SemaphoreType.DMA((2,))],
        ),
        out_shape=jax.ShapeDtypeStruct((CAP,) + ROW_TILE, F32),
        compiler_params=_params("arbitrary", vmem=VMEM_LIMIT_BYTES),
        name="moe_dispatch",
    )(dest_flat, clear, h)


def _expert_kernel(be_ref, nh_ref, last_ref, xs_ref, wgu_ref, bgu_ref, wd_ref, bd_ref, ys_ref, wgu_s, wd_s):
    del last_ref
    b = pl.program_id(0)
    nh = nh_ref[b]

    @pl.when((nh > 0) & ((b == 0) | (be_ref[b] != be_ref[jnp.maximum(b - 1, 0)])))
    def _():
        wgu_s[...] = wgu_ref[...].astype(BF16)
        wd_s[...] = wd_ref[...].astype(BF16)

    def ffn(rows):
        x = xs_ref[0:rows].reshape(rows, D_MODEL)
        gu = _dot(x.astype(BF16), wgu_s[...]) + bgu_ref[...]
        g = jnp.minimum(gu[:, :D_EXPERT], SWIGLU_LIMIT)
        u = jnp.clip(gu[:, D_EXPERT:], -SWIGLU_LIMIT, SWIGLU_LIMIT)
        act = (u + 1.0) * (g * jax.nn.sigmoid(SWIGLU_ALPHA * g))
        y = _dot(act.astype(BF16), wd_s[...]) + bd_ref[...]
        ys_ref[0:rows] = y.reshape((rows,) + ROW_TILE)

    @pl.when(nh == 2)
    def _():
        ffn(MOE_BLK)

    @pl.when(nh == 1)
    def _():
        ffn(MOE_HALF)
        ys_ref[MOE_HALF:] = jnp.zeros((MOE_BLK - MOE_HALF,) + ROW_TILE, F32)

    @pl.when(nh == 0)
    def _():
        ys_ref[...] = jnp.zeros_like(ys_ref)


def _experts(layer, block_e, block_nh, last_used, xs, w_gu, b_gu, w_down, b_down):
    def xs_map(b, be, nh, lu):
        return (jnp.minimum(b, lu[0]), 0, 0)

    def e_map(b, be, nh, lu):
        return (layer, be[b], 0, 0)

    return pl.pallas_call(
        _expert_kernel,
        grid_spec=pltpu.PrefetchScalarGridSpec(
            num_scalar_prefetch=3,
            grid=(N_BLOCKS,),
            in_specs=[
                pl.BlockSpec((MOE_BLK,) + ROW_TILE, xs_map),
                pl.BlockSpec((None, None, D_MODEL, 2 * D_EXPERT), e_map),
                pl.BlockSpec((None, None, 1, 2 * D_EXPERT), e_map),
                pl.BlockSpec((None, None, D_EXPERT, D_MODEL), e_map),
                pl.BlockSpec((None, None, 1, D_MODEL), e_map),
            ],
            out_specs=pl.BlockSpec((MOE_BLK,) + ROW_TILE, lambda b, be, nh, lu: (b, 0, 0)),
            scratch_shapes=[pltpu.VMEM((D_MODEL, 2 * D_EXPERT), BF16), pltpu.VMEM((D_EXPERT, D_MODEL), BF16)],
        ),
        out_shape=jax.ShapeDtypeStruct((CAP,) + ROW_TILE, F32),
        compiler_params=_params("arbitrary", vmem=VMEM_LIMIT_BYTES),
        name="moe_experts",
    )(block_e, block_nh, last_used, xs, w_gu, b_gu.reshape(DEPTH, N_EXPERTS, 1, -1), w_down,
      b_down.reshape(DEPTH, N_EXPERTS, 1, -1))


def _combine_kernel(dest_ref, x_ref, g_ref, gate_ref, ys_ref, o_ref, buf_ref, sem_ref):
    def row_copy(t, k):
        d = dest_ref[(pl.program_id(0) * TM + t) * TOP_K + k]
        return pltpu.make_async_copy(ys_ref.at[pl.ds(d, 1)], buf_ref.at[k, pl.ds(t, 1)], sem_ref.at[0])

    _copy_all_rows(row_copy)
    gates = gate_ref[...]
    ff = gates[:, 0:1] * buf_ref[0].reshape(TM, D_MODEL)
    for k in range(1, TOP_K):
        ff = ff + gates[:, k:k + 1] * buf_ref[k].reshape(TM, D_MODEL)
    o_ref[...] = x_ref[...] + g_ref[0] * ff


def _combine(dest_flat, x, gate_vec, gates, ys):
    return pl.pallas_call(
        _combine_kernel,
        grid_spec=pltpu.PrefetchScalarGridSpec(
            num_scalar_prefetch=1,
            grid=(NT,),
            in_specs=[
                pl.BlockSpec((TM, D_MODEL), lambda i, d: (i, 0)),
                pl.BlockSpec((1, 1, D_MODEL), lambda i, d: (i, 0, 0)),
                pl.BlockSpec((TM, TOP_K), lambda i, d: (i, 0)),
                pl.BlockSpec(memory_space=pl.ANY),
            ],
            out_specs=pl.BlockSpec((TM, D_MODEL), lambda i, d: (i, 0)),
            scratch_shapes=[pltpu.VMEM((TOP_K, TM) + ROW_TILE, F32), pltpu.SemaphoreType.DMA((1,))],
        ),
        out_shape=jax.ShapeDtypeStruct((N_TOK, D_MODEL), F32),
        compiler_params=_params("arbitrary", vmem=VMEM_LIMIT_BYTES),
        name="moe_combine",
    )(dest_flat, x, gate_vec, gates, ys)


def _moe(layer, x, g, sc, sh, gate_vec, w_router, b_router, w_gu, b_gu, w_down, b_down):
    h, idx, gates, pos, counts = _router(x, g, sc, sh, w_router, b_router)
    cnt = counts[0, :N_EXPERTS].astype(jnp.int32)
    nhalf = (cnt + MOE_HALF - 1) // MOE_HALF
    nblk = (nhalf + 1) // 2
    e_ids = jnp.arange(N_EXPERTS, dtype=jnp.int32)
    blk_end = jnp.sum(jnp.where(e_ids[None, :] <= e_ids[:, None], nblk[None, :], 0), axis=1)
    blk_start = blk_end - nblk
    last_used = blk_end[-1] - 1

    def per_expert(table, e):
        return jnp.sum(jnp.where(e[..., None] == e_ids, table, 0), axis=-1)

    def expert_of_block(blk):
        return jnp.sum((blk_end <= jnp.minimum(blk, last_used)[:, None]).astype(jnp.int32), axis=1)

    dest = per_expert(blk_start * MOE_BLK, idx) + pos
    b_ids = jnp.arange(N_BLOCKS, dtype=jnp.int32)
    block_e = expert_of_block(b_ids)
    block_nh = jnp.where(b_ids <= last_used,
                         jnp.clip(per_expert(nhalf, block_e) - 2 * (b_ids - per_expert(blk_start, block_e)), 0, 2), 0)
    h_ids = jnp.arange(2 * N_BLOCKS, dtype=jnp.int32)
    h_e = expert_of_block(h_ids // 2)
    h_nhalf = per_expert(nhalf, h_e)
    h_local = h_ids - 2 * per_expert(blk_start, h_e)
    holds_rows = (h_ids // 2 <= last_used) & (h_local < h_nhalf)
    clear = jnp.logical_not(holds_rows) | (h_local == h_nhalf - 1)
    dest_flat = dest.reshape(-1).astype(jnp.int32)
    xs = _dispatch(dest_flat, clear.astype(jnp.int32), h)
    ys = _experts(layer, block_e.astype(jnp.int32), block_nh.astype(jnp.int32),
                  last_used.reshape(1).astype(jnp.int32), xs, w_gu, b_gu, w_down, b_down)
    return _combine(dest_flat, x, gate_vec, gates, ys)


def _rope_tables(d_rot, lane0, period):
    n_rows = DEC_SEQ // GRID_W
    rows = np.repeat(np.arange(n_rows), GRID_W).astype(np.float32)
    cols = np.tile(np.arange(GRID_W), n_rows).astype(np.float32)
    half = d_rot // 2
    lane = np.arange(LANES)
    i = (lane - lane0) % period
    active = (lane >= lane0) & (i < d_rot)
    w = i % half
    f = w % (half // 2)
    pos = np.where((i // half)[None, :] == 0, rows[:, None], cols[:, None])
    sign = np.where(w < half // 2, -1.0, 1.0).astype(np.float32)
    inv = ROPE_THETA ** (-jnp.arange(0, half, 2, dtype=F32) / half)
    ang = jnp.asarray(pos) * inv[f][None, :]
    cos = jnp.where(active[None, :], jnp.cos(ang), 1.0)
    sin = jnp.where(active[None, :], jnp.sin(ang) * sign[None, :], 0.0)
    return cos, sin


def _rope_spec():
    def index(i):
        return (jnp.maximum(i - NT_PROMPT, 0) % TILES_PER_SAMPLE, 0)
    return pl.BlockSpec((TM, LANES), index)


def _pad_heads(w, n_heads, width):
    lead = w.shape[:-1]
    w = w.reshape(lead + (n_heads, width))
    w = jnp.pad(w, [(0, 0)] * len(lead) + [(0, 0), (0, HEAD_PAD - width)])
    return w.reshape(lead + (n_heads * HEAD_PAD,))


_Q_ORDER = (0, 4, 1, 5, 2, 6, 3, 7, 8, 12, 9, 13, 10, 14, 11, 15)


def _perm_q_heads(w, axis):
    shape = w.shape
    n = shape[axis]
    w = jnp.moveaxis(w, axis, 0).reshape((GQA_HEADS, n // GQA_HEADS) + tuple(s for a, s in enumerate(shape) if a != axis))
    w = w[jnp.array(_Q_ORDER)]
    w = w.reshape((n,) + w.shape[2:])
    return jnp.moveaxis(w, 0, axis)


def _constants():
    mla_cos, mla_sin = _rope_tables(MLA_ROPE, MLA_NOPE, LANES)
    gqa_cos, gqa_sin = _rope_tables(GQA_HEAD_DIM, 0, GQA_HEAD_DIM)
    dft = {}
    for L in (SEQ, DEC_SEQ):
        cm, sm = _dft_tables(L)
        fmat = jnp.concatenate([cm, sm], axis=0).astype(BF16)
        ftmat = jnp.concatenate(_dft_tables(L, transpose=True), axis=1).astype(BF16)
        dft[L] = (cm, sm, fmat, ftmat)
    return dict(mla=(mla_cos, mla_sin), gqa=(gqa_cos, gqa_sin), dft=dft)


def _even_mixer(x, p, i, gmix, sc1, sh1, g1, consts):
    proj = _normlin(x, gmix, sc1, sh1, p["w_in_ab"], i, IN_AB_PAD)
    y_hy = []
    for L, blk0, nseq, per_step in ((SEQ, 0, BATCH, HY_PROMPT_SEQS), (DEC_SEQ, N_PROMPT // DEC_SEQ, DEC_BATCH, 1)):
        cm, sm, fmat, ftmat = consts["dft"][L]
        kc, ks = _hy_filter(L, cm, sm, p["hy_filter_w1"][i], p["hy_filter_b1"][i], p["hy_filter_freq"][i],
                            p["hy_filter_w2"][i], p["hy_filter_b2"][i], p["hy_filter_w3"][i],
                            p["hy_filter_b3"][i], p["hy_log_decay"][i])
        y_hy.append(_hy_mix(proj, blk0, nseq, per_step, L, p["hy_conv_w"][i], p["hy_conv_b"][i], kc, ks,
                            p["hy_bias"][i], fmat, ftmat))

    wq = _pad_heads(p["mla_wq_b"][i], MLA_HEADS, MLA_QK).astype(BF16)
    wkv = p["mla_wkv_b"][i].astype(BF16)
    qg = jnp.pad(p["mla_q_norm"][i], (0, HEAD_PAD - MLA_QK)).reshape(1, HEAD_PAD)
    kg = jnp.pad(p["mla_k_norm"][i], (0, HEAD_PAD - MLA_QK)).reshape(1, HEAD_PAD)
    mla_cos, mla_sin = consts["mla"]
    q, k, v, lat = _mla_prep(proj, p["mla_q_lora_norm"][i].reshape(1, -1), wq, qg,
                             p["mla_kv_lora_norm"][i].reshape(1, -1), wkv, kg, mla_cos, mla_sin)
    ctx = jnp.pad(p["cache_mla_latent"][:, i].reshape(DEC_BATCH * PAST_LEN, -1),
                  ((0, 0), (0, KV_PAD - MLA_KV_LORA - MLA_ROPE)))
    kc_ctx, vc_ctx = _mla_ctx(ctx, wkv, kg)
    nqk = MLA_HEADS * HEAD_PAD
    o = _attention(_mla_attn_kernel, q, k, v, kc_ctx, vc_ctx, nqk, nqk, nqk, MLA_HEADS * MLA_V)
    w_out = p["w_out_ab"]
    x = _outproj([y_hy, o], [(w_out, (i, 0, 0), HY_D), (w_out, (i, 1, 0), HY_D)], x, g1)
    return x, lat, dict(y_hy=jnp.concatenate(y_hy, axis=0), o=jnp.concatenate(o, axis=0))


def _odd_mixer(x, p, i, gmix, sc1, sh1, g1, consts):
    nq = GQA_HEADS * GQA_HEAD_DIM
    nk = GQA_KV_HEADS * GQA_HEAD_DIM
    w = p["w_qkv_c"][i]
    w_qkv = jnp.concatenate([_perm_q_heads(w[:, :nq], 1), w[:, nq:]], axis=1).astype(BF16)
    qg = jnp.tile(p["gqa_q_norm"][i], LANES // GQA_HEAD_DIM).reshape(1, LANES)
    kg = jnp.tile(p["gqa_k_norm"][i], LANES // GQA_HEAD_DIM).reshape(1, LANES)
    gqa_cos, gqa_sin = consts["gqa"]
    q, k, v, k_plain, v_plain = _gqa_prep(x, gmix, sc1, sh1, w_qkv, qg, kg, gqa_cos, gqa_sin)
    kc_ctx = p["cache_gqa_k"][:, i].reshape(DEC_BATCH * PAST_LEN, -1).astype(BF16)
    vc_ctx = p["cache_gqa_v"][:, i].reshape(DEC_BATCH * PAST_LEN, -1).astype(BF16)
    o = _attention(_gqa_attn_kernel, q, k, v, kc_ctx, vc_ctx, nq, nk, nk, nq)
    w_out = _perm_q_heads(p["w_out_c"][i], 0)[None]
    x = _outproj([o], [(w_out, (0, 0, 0), nq)], x, g1)
    return x, k_plain, v_plain


def kernel(x_prompt, x_sample, cache_mla_latent, cache_gqa_k, cache_gqa_v, c, c_ctx, w_ada, b_ada, norm_mix, norm_ffn, w_in_ab, hy_conv_w, hy_conv_b, hy_filter_w1, hy_filter_b1, hy_filter_freq, hy_filter_w2, hy_filter_b2, hy_filter_w3, hy_filter_b3, hy_log_decay, hy_bias, mla_q_lora_norm, mla_wq_b, mla_kv_lora_norm, mla_wkv_b, mla_q_norm, mla_k_norm, w_out_ab, w_qkv_c, gqa_q_norm, gqa_k_norm, w_out_c, moe_router_w, moe_router_b, moe_w_gate_up, moe_b_gate_up, moe_w_down, moe_b_down):
    p = dict(locals())
    x = jnp.concatenate([x_prompt.reshape(N_PROMPT, D_MODEL), x_sample.reshape(N_SAMPLE, D_MODEL)], axis=0)

    cond = jnp.concatenate([c_ctx[None, :], c, jnp.zeros((COND_ROWS - N_COND, D_MODEL), F32)], axis=0)
    mods = _modulation(cond, w_ada, b_ada)
    tile_cond = jnp.concatenate([jnp.zeros((NT_PROMPT,), jnp.int32),
                                 1 + jnp.arange(NT - NT_PROMPT, dtype=jnp.int32) // TILES_PER_SAMPLE])
    mods = mods[:, tile_cond].reshape(DEPTH, NT, 6, 1, D_MODEL).transpose(0, 2, 1, 3, 4)

    consts = _constants()

    lat_out, k_out, v_out = [], [], []
    for l in range(DEPTH):
        sh1, sc1, g1, sh2, sc2, g2 = (mods[l, j] for j in range(6))
        i = l // 2
        gmix = norm_mix[l].reshape(1, D_MODEL)
        if l % 2 == 0:
            x, lat, _ = _even_mixer(x, p, i, gmix, sc1, sh1, g1, consts)
            lat_out.append(lat[:N_PROMPT].reshape(BATCH, SEQ, -1))
        else:
            x, k_plain, v_plain = _odd_mixer(x, p, i, gmix, sc1, sh1, g1, consts)
            k_out.append(k_plain[:N_PROMPT].reshape(BATCH, SEQ, GQA_KV_HEADS, GQA_HEAD_DIM))
            v_out.append(v_plain[:N_PROMPT].reshape(BATCH, SEQ, GQA_KV_HEADS, GQA_HEAD_DIM))
        x = _moe(l, x, norm_ffn[l].reshape(1, D_MODEL), sc2, sh2, g2, moe_router_w[l], moe_router_b[l],
                 moe_w_gate_up, moe_b_gate_up, moe_w_down, moe_b_down)

    y_prompt = x[:N_PROMPT].reshape(BATCH, SEQ, D_MODEL)
    y_sample = x[N_PROMPT:].reshape(DEC_BATCH, DEC_SEQ, D_MODEL)
    return (y_prompt, y_sample, jnp.stack(lat_out, axis=1), jnp.stack(k_out, axis=1), jnp.stack(v_out, axis=1))
```

```python
import functools
import math

import jax
import jax.numpy as jnp
import numpy as np
from jax import lax
from jax.experimental import pallas as pl
from jax.experimental.pallas import tpu as pltpu

F32 = jnp.float32
BF16 = jnp.bfloat16

D_MODEL = 1024
BATCH = 32
SEQ = 256
DEPTH = 4
DEC_BATCH = 4
DEC_SEQ = 1024
PAST_LEN = 256
GRID_W = 64
N_EVEN = (DEPTH + 1) // 2
N_ODD = DEPTH // 2
HY_D = D_MODEL // 2
HY_ORDER = 2
HY_BANDS = 16
HY_EMB = 2 * HY_BANDS + 1
HY_FILTER_HIDDEN = 64
MLA_HEADS = 8
MLA_NOPE = 64
MLA_ROPE = 32
MLA_QK = MLA_NOPE + MLA_ROPE
MLA_V = HY_D // MLA_HEADS
MLA_Q_LORA = 3 * D_MODEL // 8
MLA_KV_LORA = D_MODEL // 4
GQA_HEADS = 16
GQA_KV_HEADS = 4
GQA_HEAD_DIM = D_MODEL // GQA_HEADS
N_EXPERTS = 32
TOP_K = 4
D_EXPERT = D_MODEL
SWIGLU_LIMIT = 7.0
SWIGLU_ALPHA = 1.702
ROPE_THETA = 10000.0
RMS_EPS = 1e-6
GQA_SCALE = GQA_HEAD_DIM ** -0.5
assert math.frexp(GQA_SCALE)[0] == 0.5
IN_AB = (HY_ORDER + 1) * HY_D + MLA_Q_LORA + MLA_KV_LORA + MLA_ROPE

N_PROMPT = BATCH * SEQ
N_SAMPLE = DEC_BATCH * DEC_SEQ
N_TOK = N_PROMPT + N_SAMPLE

LANES = 128
SUBLANES = 8
VMEM_LIMIT_BYTES = 56 * 1024 * 1024

TM = 256
NT = N_TOK // TM
NT_PROMPT = N_PROMPT // TM
TILES_PER_SAMPLE = DEC_SEQ // TM
N_COND = 1 + DEC_BATCH
COND_ROWS = 8
HEAD_PAD = LANES
IN_AB_PAD = 2304
KV_PAD = 384
MOE_HALF = 256
MOE_BLK = 2 * MOE_HALF
N_SLOTS = N_TOK * TOP_K
N_BLOCKS = N_SLOTS // MOE_BLK + N_EXPERTS
CAP = N_BLOCKS * MOE_BLK
ROW_TILE = (SUBLANES, LANES)
assert SUBLANES * LANES == D_MODEL
HY_CH = 256
HY_PROMPT_SEQS = 8


def _dot(a, b):
    return jnp.dot(a, b, preferred_element_type=F32)


def _dot_nt(a, b):
    return lax.dot_general(a, b, (((1,), (1,)), ((), ())), preferred_element_type=F32)


def _split(x):
    hi = x.astype(BF16)
    lo = (x - hi.astype(F32)).astype(BF16)
    return hi, lo


def _dot3(a, b):
    ah, al = _split(a)
    bh, bl = _split(b)
    return _dot(ah, bh) + (_dot(ah, bl) + _dot(al, bh))


def _lane_iota(shape):
    return lax.broadcasted_iota(jnp.int32, shape, len(shape) - 1)


def _params(*sem, vmem=None):
    return pltpu.CompilerParams(dimension_semantics=sem, vmem_limit_bytes=vmem)


def _mod_kernel(c_ref, w_ref, b_ref, o_ref):
    c = c_ref[...]
    s = c * jax.nn.sigmoid(c)
    o_ref[0] = _dot(s.astype(BF16), w_ref[0].astype(BF16)) + b_ref[0]


def _modulation(cond, w_ada, b_ada):
    nblk = 6
    return pl.pallas_call(
        _mod_kernel,
        grid=(DEPTH, nblk),
        in_specs=[
            pl.BlockSpec((COND_ROWS, D_MODEL), lambda l, j: (0, 0)),
            pl.BlockSpec((1, D_MODEL, D_MODEL), lambda l, j: (l, 0, j)),
            pl.BlockSpec((1, 1, D_MODEL), lambda l, j: (l, 0, j)),
        ],
        out_specs=pl.BlockSpec((1, COND_ROWS, D_MODEL), lambda l, j: (l, 0, j)),
        out_shape=jax.ShapeDtypeStruct((DEPTH, COND_ROWS, nblk * D_MODEL), F32),
        compiler_params=_params("arbitrary", "arbitrary"),
        name="modulation",
    )(cond, w_ada, b_ada.reshape(DEPTH, 1, nblk * D_MODEL))


def _norm_mod(x, g, sc, sh):
    ms = jnp.mean(x * x, axis=-1, keepdims=True)
    y = x * lax.rsqrt(ms + RMS_EPS)
    return (y * g) * (1.0 + sc) + sh


def _row_spec(width):
    return pl.BlockSpec((TM, width), lambda i: (i, 0))


def _tile_vec_spec():
    return pl.BlockSpec((1, 1, D_MODEL), lambda i: (i, 0, 0))


def _full_spec(shape):
    nd = len(shape)
    return pl.BlockSpec(shape, lambda i: (0,) * nd)


def _normlin_kernel(x_ref, g_ref, sc_ref, sh_ref, w_ref, o_ref, w_scr):
    nin = w_ref.shape[1]

    @pl.when(pl.program_id(0) == 0)
    def _():
        w_scr[:, :nin] = w_ref[...].astype(BF16)
        w_scr[:, nin:] = jnp.zeros((D_MODEL, w_scr.shape[1] - nin), BF16)

    h = _norm_mod(x_ref[...], g_ref[...], sc_ref[0], sh_ref[0])
    o_ref[...] = _dot(h.astype(BF16), w_scr[...])


def _normlin(x, g, sc, sh, w_all, layer, nout):
    nin = w_all.shape[2]
    return pl.pallas_call(
        _normlin_kernel,
        grid=(NT,),
        in_specs=[_row_spec(D_MODEL), _full_spec((1, D_MODEL)), _tile_vec_spec(), _tile_vec_spec(),
                  pl.BlockSpec((None, D_MODEL, nin), lambda i: (layer, 0, 0))],
        out_specs=_row_spec(nout),
        out_shape=jax.ShapeDtypeStruct((N_TOK, nout), F32),
        scratch_shapes=[pltpu.VMEM((D_MODEL, nout), BF16)],
        compiler_params=_params("arbitrary", vmem=VMEM_LIMIT_BYTES),
        name="normlin",
    )(x, g, sc, sh, w_all)


def _hy_filter_kernel(z_ref, w1_ref, b1_ref, fr_ref, w2_ref, b2_ref, w3_ref, b3_ref, ed_ref,
                      c_ref, s_ref, kc_ref, ks_ref):
    L = z_ref.shape[0]
    z = z_ref[...]
    fr = fr_ref[...]
    hdn = jnp.sin(fr * (_dot3(z, w1_ref[...]) + b1_ref[...]))
    hdn = jnp.sin(fr * (_dot3(hdn, w2_ref[...]) + b2_ref[...]))
    filt = _dot3(hdn, w3_ref[...]) + b3_ref[...]
    t = z[:, 0:1]
    filt = filt * jnp.exp(-t * ed_ref[...])
    row = lax.broadcasted_iota(jnp.int32, (L, HY_D), 0)
    cm = c_ref[...]
    sm = s_ref[...]
    for o in range(HY_ORDER):
        fw = filt[:, (2 * o) * HY_D:(2 * o + 1) * HY_D]
        bw = filt[:, (2 * o + 1) * HY_D:(2 * o + 2) * HY_D]
        den = (jnp.sum(jnp.abs(fw), axis=0, keepdims=True)
               + jnp.sum(jnp.abs(bw), axis=0, keepdims=True)) + 1e-6
        fw = fw / den
        bw = jnp.where(row == 0, 0.0, bw / den)
        kc_ref[o] = _dot3(cm, fw + bw) * (1.0 / L)
        ks_ref[o] = _dot3(sm, fw - bw) * (1.0 / L)


def _dft_tables(L, transpose=False):
    m = jnp.arange(L, dtype=jnp.int32)
    odd = 2 * m + 1
    phase = (m[:, None] * odd[None, :] if transpose else odd[:, None] * m[None, :]) % (4 * L)
    ang = phase.astype(F32) * (2.0 * math.pi / (4 * L))
    return jnp.cos(ang), jnp.sin(ang)


def _filter_features(L):
    p = jnp.arange(L, dtype=F32)
    t = p / max(L - 1, 1)
    bands = jnp.linspace(1e-4, HY_BANDS - 1, HY_BANDS, dtype=F32)
    ang = (2.0 * math.pi / L) * p[:, None] * bands[None, :]
    z = jnp.concatenate([t[:, None], jnp.cos(ang), -jnp.sin(ang)], axis=-1)
    return jnp.pad(z, ((0, 0), (0, LANES - HY_EMB)))


def _hy_filter(L, cmat, smat, w1, b1, fr, w2, b2, w3, b3, log_decay):
    nf = HY_ORDER * 2 * HY_D
    args = (
        _filter_features(L),
        jnp.pad(w1, ((0, LANES - HY_EMB), (0, 0))),
        b1.reshape(1, -1), fr.reshape(1, -1), w2, b2.reshape(1, -1), w3, b3.reshape(1, -1),
        jnp.exp(log_decay.astype(F32)).reshape(1, nf),
        cmat, smat,
    )
    out_sds = jax.ShapeDtypeStruct((HY_ORDER, L, HY_D), F32)
    return pl.pallas_call(
        _hy_filter_kernel,
        grid=(1,),
        in_specs=[_full_spec(a.shape) for a in args],
        out_specs=[_full_spec(out_sds.shape)] * 2,
        out_shape=[out_sds, out_sds],
        compiler_params=_params("arbitrary", vmem=VMEM_LIMIT_BYTES),
        name=f"hy_filter_{L}",
    )(*args)


def _hy_mix_kernel(u0_ref, u1_ref, u2_ref, cw0_ref, cw1_ref, cw2_ref, cb0_ref, cb1_ref, cb2_ref,
                   kc_ref, ks_ref, hb_ref, f_ref, ft_ref, o_ref):
    S, L, _ = u0_ref.shape
    row = lax.broadcasted_iota(jnp.int32, (L, HY_CH), 0)

    def lanes(per_seq):
        return per_seq[0] if S == 1 else jnp.concatenate(per_seq, axis=1)

    def short_conv(u_ref, cw_ref, cb_ref):
        w = cw_ref[0]
        out = []
        for s in range(S):
            u = u_ref[s]
            prev = jnp.where(row == 0, 0.0, pltpu.roll(u, 1, 0))
            nxt = jnp.where(row == L - 1, 0.0, pltpu.roll(u, L - 1, 0))
            out.append((prev * w[0:1] + u * w[1:2]) + nxt * w[2:3] + cb_ref[0])
        return lanes(out)

    z = short_conv(u0_ref, cw0_ref, cb0_ref)
    gates = (short_conv(u1_ref, cw1_ref, cb1_ref), short_conv(u2_ref, cw2_ref, cb2_ref))
    for o in range(HY_ORDER):
        zz = _dot(f_ref[...], z.astype(BF16))
        cz, sz = zz[:L], zz[L:]
        kc, ks = lanes([kc_ref[o]] * S), lanes([ks_ref[o]] * S)
        w1 = cz * kc - sz * ks
        w2 = cz * ks + sz * kc
        ww = jnp.concatenate([w1, w2], axis=0).astype(BF16)
        conv = _dot(ft_ref[...], ww)
        z = gates[o] * (conv + z * lanes([hb_ref[0, o]] * S))
    for s in range(S):
        o_ref[s] = z[:, s * HY_CH:(s + 1) * HY_CH].astype(o_ref.dtype)


def _hy_mix(proj, row_block0, nseq, seqs_per_step, L, cw, cb, kc, ks, hb, fmat, ftmat):
    nch = HY_D // HY_CH
    nparts = HY_ORDER + 1
    S = seqs_per_step
    assert nseq % S == 0 and row_block0 % S == 0
    cw3 = cw.reshape(3, nparts * nch, HY_CH).transpose(1, 0, 2)
    cb3 = cb.reshape(nparts * nch, 1, HY_CH)
    hb3 = hb.reshape(HY_ORDER, nch, 1, HY_CH).transpose(1, 0, 2, 3)
    proj = proj.reshape(N_TOK // L, L, proj.shape[1])

    def u_spec(part):
        return pl.BlockSpec((S, L, HY_CH), lambda s, c: (row_block0 // S + s, 0, part * nch + c))

    def cw_spec(part):
        return pl.BlockSpec((1, 3, HY_CH), lambda s, c: (part * nch + c, 0, 0))

    def cb_spec(part):
        return pl.BlockSpec((1, 1, HY_CH), lambda s, c: (part * nch + c, 0, 0))

    return pl.pallas_call(
        _hy_mix_kernel,
        grid=(nseq // S, nch),
        in_specs=[
            u_spec(0), u_spec(1), u_spec(2),
            cw_spec(0), cw_spec(1), cw_spec(2),
            cb_spec(0), cb_spec(1), cb_spec(2),
            pl.BlockSpec((HY_ORDER, L, HY_CH), lambda s, c: (0, 0, c)),
            pl.BlockSpec((HY_ORDER, L, HY_CH), lambda s, c: (0, 0, c)),
            pl.BlockSpec((1, HY_ORDER, 1, HY_CH), lambda s, c: (c, 0, 0, 0)),
            pl.BlockSpec((2 * L, L), lambda s, c: (0, 0)),
            pl.BlockSpec((L, 2 * L), lambda s, c: (0, 0)),
        ],
        out_specs=pl.BlockSpec((S, L, HY_CH), lambda s, c: (s, 0, c)),
        out_shape=jax.ShapeDtypeStruct((nseq, L, HY_D), BF16),
        compiler_params=_params("arbitrary", "arbitrary", vmem=VMEM_LIMIT_BYTES),
        name=f"hy_mix_{L}",
    )(proj, proj, proj, cw3, cw3, cw3, cb3, cb3, cb3, kc, ks, hb3, fmat, ftmat).reshape(nseq * L, HY_D)


def _head_rms(xh, gain, dim):
    ms = jnp.sum(xh * xh, axis=-1, keepdims=True) * (1.0 / dim)
    return (xh * lax.rsqrt(ms + RMS_EPS)) * gain


def _per_group(context_fn, latent_fn):
    is_context = pl.program_id(0) < NT_PROMPT
    pl.when(is_context)(context_fn)
    pl.when(jnp.logical_not(is_context))(latent_fn)


def _rope(xh, cos, sin, half):
    lane = _lane_iota(xh.shape)
    first = (lane % (2 * half)) < half
    rot = jnp.where(first, pltpu.roll(xh, LANES - half, 1), pltpu.roll(xh, half, 1))
    return xh * cos + rot * sin


def _mla_keys_values(lat_n, kr_blk, wkv_ref, kg_ref, rope):
    kv = _dot(lat_n.astype(BF16), wkv_ref[...])
    kr = pltpu.roll(kr_blk, MLA_NOPE, 1)
    nope = _lane_iota(kr.shape) < MLA_NOPE
    ks = []
    for h in range(MLA_HEADS):
        kh = jnp.where(nope, kv[:, h * HEAD_PAD:(h + 1) * HEAD_PAD], 0.0) + kr
        kh = _head_rms(kh, kg_ref[...], MLA_QK)
        if rope is not None:
            kh = _rope(kh, rope[0], rope[1], MLA_ROPE // 4)
        ks.append(kh)
    return ks, kv


def _mla_prep_kernel(qa_ref, kva_ref, gq_ref, wq_ref, qg_ref, gkv_ref, wkv_ref, kg_ref,
                     cos_ref, sin_ref, q_ref, k_ref, v_ref, lat_ref):
    def body(rope):
        qa = qa_ref[...]
        ms = jnp.mean(qa * qa, axis=-1, keepdims=True)
        qn = (qa * lax.rsqrt(ms + RMS_EPS)) * gq_ref[...]
        q = _dot(qn.astype(BF16), wq_ref[...])
        for h in range(MLA_HEADS):
            qh = _head_rms(q[:, h * HEAD_PAD:(h + 1) * HEAD_PAD], qg_ref[...], MLA_QK)
            if rope is not None:
                qh = _rope(qh, rope[0], rope[1], MLA_ROPE // 4)
            q_ref[:, h * HEAD_PAD:(h + 1) * HEAD_PAD] = qh.astype(BF16)

        kva = kva_ref[...]
        lat = kva[:, :MLA_KV_LORA]
        ms = jnp.mean(lat * lat, axis=-1, keepdims=True)
        lat_n = (lat * lax.rsqrt(ms + RMS_EPS)) * gkv_ref[...]
        kr_blk = kva[:, MLA_KV_LORA:]
        lat_ref[:, :MLA_KV_LORA] = lat_n
        lat_ref[:, MLA_KV_LORA:] = kr_blk[:, :MLA_ROPE]
        ks, v = _mla_keys_values(lat_n, kr_blk, wkv_ref, kg_ref, rope)
        for h in range(MLA_HEADS):
            k_ref[:, h * HEAD_PAD:(h + 1) * HEAD_PAD] = ks[h].astype(BF16)
        v_ref[...] = v.astype(BF16)

    _per_group(lambda: body(None), lambda: body((cos_ref[...], sin_ref[...])))


def _mla_prep(proj, gq, wq, qg, gkv, wkv, kg, cos_t, sin_t):
    nq = MLA_HEADS * HEAD_PAD
    nv = MLA_HEADS * (MLA_NOPE + MLA_V)
    qa_blk = (HY_ORDER + 1) * HY_D // KV_PAD
    return pl.pallas_call(
        _mla_prep_kernel,
        grid=(NT,),
        in_specs=[
            pl.BlockSpec((TM, KV_PAD), lambda i: (i, qa_blk)),
            pl.BlockSpec((TM, KV_PAD), lambda i: (i, qa_blk + 1)),
            _full_spec(gq.shape), _full_spec(wq.shape), _full_spec(qg.shape), _full_spec(gkv.shape),
            _full_spec(wkv.shape), _full_spec(kg.shape),
            _rope_spec(), _rope_spec(),
        ],
        out_specs=[_row_spec(nq), _row_spec(nq), _row_spec(nv), _row_spec(MLA_KV_LORA + MLA_ROPE)],
        out_shape=[
            jax.ShapeDtypeStruct((N_TOK, nq), BF16),
            jax.ShapeDtypeStruct((N_TOK, nq), BF16),
            jax.ShapeDtypeStruct((N_TOK, nv), BF16),
            jax.ShapeDtypeStruct((N_TOK, MLA_KV_LORA + MLA_ROPE), F32),
        ],
        compiler_params=_params("arbitrary", vmem=VMEM_LIMIT_BYTES),
        name="mla_prep",
    )(proj, proj, gq, wq, qg, gkv, wkv, kg, cos_t, sin_t)


def _mla_ctx_kernel(lat_ref, wkv_ref, kg_ref, k_ref, v_ref):
    lat = lat_ref[...]
    ks, v = _mla_keys_values(lat[:, :MLA_KV_LORA], lat[:, MLA_KV_LORA:], wkv_ref, kg_ref, None)
    for h in range(MLA_HEADS):
        k_ref[:, h * HEAD_PAD:(h + 1) * HEAD_PAD] = ks[h].astype(BF16)
    v_ref[...] = v.astype(BF16)


def _mla_ctx(lat_pad, wkv, kg):
    n = lat_pad.shape[0]
    nq = MLA_HEADS * HEAD_PAD
    nv = MLA_HEADS * (MLA_NOPE + MLA_V)
    return pl.pallas_call(
        _mla_ctx_kernel,
        grid=(n // TM,),
        in_specs=[_row_spec(KV_PAD), _full_spec(wkv.shape), _full_spec(kg.shape)],
        out_specs=[_row_spec(nq), _row_spec(nv)],
        out_shape=[jax.ShapeDtypeStruct((n, nq), BF16), jax.ShapeDtypeStruct((n, nv), BF16)],
        compiler_params=_params("arbitrary"),
        name="mla_ctx",
    )(lat_pad, wkv, kg)


def _softmax_pv(q, keys, vals, scale):
    ss = [_dot_nt(q, k) for k in keys]
    if scale is not None:
        ss = [s * scale for s in ss]
    m = ss[0].max(axis=-1, keepdims=True)
    for s in ss[1:]:
        m = jnp.maximum(m, s.max(axis=-1, keepdims=True))
    ps = [jnp.exp(s - m) for s in ss]
    l = ps[0].sum(axis=-1, keepdims=True)
    for p in ps[1:]:
        l = l + p.sum(axis=-1, keepdims=True)
    o = _dot(ps[0].astype(BF16), vals[0])
    for p, v in zip(ps[1:], vals[1:]):
        o = o + _dot(p.astype(BF16), v)
    return o / l


def _mla_attn_kernel(*refs, has_ctx):
    if has_ctx:
        q_ref, k_ref, v_ref, kc_ref, vc_ref, o_ref = refs
    else:
        q_ref, k_ref, v_ref, o_ref = refs
    scale = MLA_QK ** -0.5
    lo = _lane_iota((q_ref.shape[0], LANES)) < MLA_V
    for j in range(MLA_HEADS // 2):
        outs = []
        for h in (2 * j, 2 * j + 1):
            sl = slice(h * HEAD_PAD, (h + 1) * HEAD_PAD)
            ks, vs = [k_ref[:, sl]], [v_ref[:, sl]]
            if has_ctx:
                ks.append(kc_ref[:, sl])
                vs.append(vc_ref[:, sl])
            outs.append(_softmax_pv(q_ref[:, sl], ks, vs, scale))
        pair = jnp.where(lo, pltpu.roll(outs[0], MLA_V, 1), outs[1])
        o_ref[:, j * LANES:(j + 1) * LANES] = pair.astype(o_ref.dtype)


def _gqa_attn_kernel(*refs, has_ctx):
    if has_ctx:
        q_ref, k_ref, v_ref, kc_ref, vc_ref, o_ref = refs
    else:
        q_ref, k_ref, v_ref, o_ref = refs
    scale = None
    lo = _lane_iota((q_ref.shape[0], LANES)) < GQA_HEAD_DIM
    pairs_per_kv = (GQA_HEADS // 2) // (GQA_KV_HEADS // 2)
    for p in range(GQA_HEADS // 2):
        kv = slice((p // pairs_per_kv) * LANES, (p // pairs_per_kv + 1) * LANES)
        ks, vs = [k_ref[:, kv]], [v_ref[:, kv]]
        if has_ctx:
            ks.append(kc_ref[:, kv])
            vs.append(vc_ref[:, kv])
        qp = q_ref[:, p * LANES:(p + 1) * LANES]
        zero = jnp.zeros_like(qp)
        o_lo = _softmax_pv(jnp.where(lo, qp, zero), ks, vs, scale)
        o_hi = _softmax_pv(jnp.where(lo, zero, qp), ks, vs, scale)
        o_ref[:, p * LANES:(p + 1) * LANES] = jnp.where(lo, o_lo, o_hi).astype(o_ref.dtype)


def _attention(body, q, k, v, kc, vc, wq, wk, wv, wo):
    outs = []
    outs.append(pl.pallas_call(
        functools.partial(body, has_ctx=False),
        grid=(BATCH,),
        in_specs=[pl.BlockSpec((SEQ, wq), lambda b: (b, 0)),
                  pl.BlockSpec((SEQ, wk), lambda b: (b, 0)),
                  pl.BlockSpec((SEQ, wv), lambda b: (b, 0))],
        out_specs=pl.BlockSpec((SEQ, wo), lambda b: (b, 0)),
        out_shape=jax.ShapeDtypeStruct((N_PROMPT, wo), BF16),
        compiler_params=_params("arbitrary", vmem=VMEM_LIMIT_BYTES),
        name="attn_prompt",
    )(q, k, v))
    qt = DEC_SEQ // TM
    q0 = N_PROMPT // TM
    s0 = N_PROMPT // DEC_SEQ
    outs.append(pl.pallas_call(
        functools.partial(body, has_ctx=True),
        grid=(DEC_BATCH, qt),
        in_specs=[pl.BlockSpec((TM, wq), lambda b, t: (q0 + b * qt + t, 0)),
                  pl.BlockSpec((DEC_SEQ, wk), lambda b, t: (s0 + b, 0)),
                  pl.BlockSpec((DEC_SEQ, wv), lambda b, t: (s0 + b, 0)),
                  pl.BlockSpec((PAST_LEN, wk), lambda b, t: (b, 0)),
                  pl.BlockSpec((PAST_LEN, wv), lambda b, t: (b, 0))],
        out_specs=pl.BlockSpec((TM, wo), lambda b, t: (b * qt + t, 0)),
        out_shape=jax.ShapeDtypeStruct((N_SAMPLE, wo), BF16),
        compiler_params=_params("arbitrary", "arbitrary", vmem=VMEM_LIMIT_BYTES),
        name="attn_sample",
    )(q, k, v, kc, vc))
    return outs


def _gqa_prep_kernel(x_ref, g_ref, sc_ref, sh_ref, w_ref, qg_ref, kg_ref, cos_ref, sin_ref,
                     q_ref, k_ref, v_ref, kp_ref, vp_ref):
    lo = _lane_iota((TM, LANES)) < GQA_HEAD_DIM
    nq = GQA_HEADS * GQA_HEAD_DIM
    nk = GQA_KV_HEADS * GQA_HEAD_DIM

    def pair_norm(xp, gain):
        sq = xp * xp
        ms_lo = jnp.sum(jnp.where(lo, sq, 0.0), axis=-1, keepdims=True)
        ms_hi = jnp.sum(jnp.where(lo, 0.0, sq), axis=-1, keepdims=True)
        ms = jnp.where(lo, ms_lo, ms_hi) * (1.0 / GQA_HEAD_DIM)
        return (xp * lax.rsqrt(ms + RMS_EPS)) * gain

    def body(rope):
        def rotate(xp):
            return xp if rope is None else _rope(xp, rope[0], rope[1], GQA_HEAD_DIM // 4)

        h = _norm_mod(x_ref[...], g_ref[...], sc_ref[0], sh_ref[0])
        qkv = _dot(h.astype(BF16), w_ref[...])
        for p in range(nq // LANES):
            sl = slice(p * LANES, (p + 1) * LANES)
            qn = pair_norm(qkv[:, sl], qg_ref[...])
            q_ref[:, sl] = (rotate(qn) * GQA_SCALE).astype(BF16)
        for p in range(nk // LANES):
            sl = slice(p * LANES, (p + 1) * LANES)
            kn = pair_norm(qkv[:, nq + p * LANES:nq + (p + 1) * LANES], kg_ref[...])
            kp_ref[:, sl] = kn
            k_ref[:, sl] = rotate(kn).astype(BF16)
        v = qkv[:, nq + nk:]
        vp_ref[...] = v
        v_ref[...] = v.astype(BF16)

    _per_group(lambda: body(None), lambda: body((cos_ref[...], sin_ref[...])))


def _gqa_prep(x, g, sc, sh, w, qg, kg, cos_t, sin_t):
    nq = GQA_HEADS * GQA_HEAD_DIM
    nk = GQA_KV_HEADS * GQA_HEAD_DIM
    return pl.pallas_call(
        _gqa_prep_kernel,
        grid=(NT,),
        in_specs=[_row_spec(D_MODEL), _full_spec((1, D_MODEL)), _tile_vec_spec(), _tile_vec_spec(),
                  _full_spec(w.shape), _full_spec(qg.shape), _full_spec(kg.shape),
                  _rope_spec(), _rope_spec()],
        out_specs=[_row_spec(nq), _row_spec(nk), _row_spec(nk), _row_spec(nk), _row_spec(nk)],
        out_shape=[
            jax.ShapeDtypeStruct((N_TOK, nq), BF16),
            jax.ShapeDtypeStruct((N_TOK, nk), BF16),
            jax.ShapeDtypeStruct((N_TOK, nk), BF16),
            jax.ShapeDtypeStruct((N_TOK, nk), F32),
            jax.ShapeDtypeStruct((N_TOK, nk), F32),
        ],
        compiler_params=_params("arbitrary", vmem=VMEM_LIMIT_BYTES),
        name="gqa_prep",
    )(x, g, sc, sh, w, qg, kg, cos_t, sin_t)


def _outproj_kernel(*refs, n_in):
    ap_refs = refs[:n_in]
    as_refs = refs[n_in:2 * n_in]
    w_refs = refs[2 * n_in:3 * n_in]
    x_ref, g_ref, o_ref = refs[3 * n_in:3 * n_in + 3]
    w_scr = refs[3 * n_in + 3:]
    is_prompt = pl.program_id(0) < NT_PROMPT

    @pl.when(pl.program_id(0) == 0)
    def _():
        for w, s in zip(w_refs, w_scr):
            s[...] = w[...].astype(BF16)

    y = None
    for ap, asm, s in zip(ap_refs, as_refs, w_scr):
        a = jnp.where(is_prompt, ap[...], asm[...])
        d = _dot(a, s[...])
        y = d if y is None else y + d
    o_ref[...] = x_ref[...] + g_ref[0] * y


def _outproj(acts, ws, x, gate):
    n_in = len(acts)

    def w_spec(index, rows):
        return pl.BlockSpec((None, rows, D_MODEL), lambda i: index)

    def prompt_spec(width):
        return pl.BlockSpec((TM, width), lambda i: (jnp.minimum(i, NT_PROMPT - 1), 0))

    def sample_spec(width):
        return pl.BlockSpec((TM, width), lambda i: (jnp.maximum(i - NT_PROMPT, 0), 0))

    return pl.pallas_call(
        functools.partial(_outproj_kernel, n_in=n_in),
        grid=(NT,),
        in_specs=([prompt_spec(ap.shape[1]) for ap, _ in acts] + [sample_spec(asm.shape[1]) for _, asm in acts]
                  + [w_spec(index, rows) for _, index, rows in ws] + [_row_spec(D_MODEL), _tile_vec_spec()]),
        out_specs=_row_spec(D_MODEL),
        out_shape=jax.ShapeDtypeStruct((N_TOK, D_MODEL), F32),
        scratch_shapes=[pltpu.VMEM((rows, D_MODEL), BF16) for _, _, rows in ws],
        compiler_params=_params("arbitrary", vmem=VMEM_LIMIT_BYTES),
        name="outproj",
    )(*(ap for ap, _ in acts), *(asm for _, asm in acts), *(w for w, _, _ in ws), x, gate)


def _router_kernel(x_ref, g_ref, sc_ref, sh_ref, wh_ref, wl_ref, br_ref, tri_ref,
                   h_ref, idx_ref, gate_ref, pos_ref, cnt_ref, run_ref):
    @pl.when(pl.program_id(0) == 0)
    def _():
        run_ref[...] = jnp.zeros_like(run_ref)

    h = _norm_mod(x_ref[...], g_ref[...], sc_ref[0], sh_ref[0])
    h_ref[...] = h.reshape((TM,) + ROW_TILE)
    hh, hl = _split(h)
    logits = _dot(hh, wh_ref[...]) + (_dot(hh, wl_ref[...]) + _dot(hl, wh_ref[...])) + br_ref[...]
    lane = _lane_iota((TM, LANES)).astype(F32)
    neg = jnp.float32(-jnp.inf)
    lg = jnp.where(lane < N_EXPERTS, logits, neg)
    tops, sels, hots = [], [], []
    for _ in range(TOP_K):
        m = lg.max(axis=-1, keepdims=True)
        sel = jnp.where(lg == m, lane, float(LANES)).min(axis=-1, keepdims=True)
        hot = lane == sel
        lg = jnp.where(hot, neg, lg)
        tops.append(m)
        sels.append(sel)
        hots.append(hot)
    es = [jnp.exp(t - tops[0]) for t in tops]
    den = es[0] + es[1] + es[2] + es[3]
    member = jnp.zeros((TM, LANES), F32)
    for hot in hots:
        member = member + hot.astype(F32)
    ranks = _dot(tri_ref[...], member.astype(BF16)) + run_ref[...]
    lane4 = _lane_iota((TM, TOP_K))
    idx4 = jnp.zeros((TM, TOP_K), F32)
    gate4 = jnp.zeros((TM, TOP_K), F32)
    pos4 = jnp.zeros((TM, TOP_K), F32)
    for k in range(TOP_K):
        pk = jnp.sum(jnp.where(hots[k], ranks, 0.0), axis=-1, keepdims=True)
        idx4 = jnp.where(lane4 == k, sels[k], idx4)
        gate4 = jnp.where(lane4 == k, es[k] / den, gate4)
        pos4 = jnp.where(lane4 == k, pk, pos4)
    idx_ref[...] = idx4.astype(jnp.int32)
    gate_ref[...] = gate4
    pos_ref[...] = pos4.astype(jnp.int32)
    run_ref[...] = run_ref[...] + jnp.sum(member, axis=0, keepdims=True)
    cnt_ref[...] = run_ref[...]


def _router(x, g, sc, sh, w_router, b_router):
    wpad = jnp.pad(w_router, ((0, 0), (0, LANES - N_EXPERTS)))
    wh = wpad.astype(BF16)
    wl = (wpad - wh.astype(F32)).astype(BF16)
    bpad = jnp.pad(b_router, (0, LANES - N_EXPERTS)).reshape(1, LANES)
    r = np.arange(TM)
    tri = jnp.asarray(r[None, :] < r[:, None], dtype=BF16)
    narrow = pl.BlockSpec((TM, TOP_K), lambda i: (i, 0))
    return pl.pallas_call(
        _router_kernel,
        grid=(NT,),
        in_specs=[_row_spec(D_MODEL), _full_spec((1, D_MODEL)), _tile_vec_spec(), _tile_vec_spec(),
                  _full_spec(wh.shape), _full_spec(wl.shape), _full_spec(bpad.shape), _full_spec(tri.shape)],
        out_specs=[pl.BlockSpec((TM,) + ROW_TILE, lambda i: (i, 0, 0)), narrow, narrow, narrow,
                   _full_spec((1, LANES))],
        out_shape=[
            jax.ShapeDtypeStruct((N_TOK,) + ROW_TILE, F32),
            jax.ShapeDtypeStruct((N_TOK, TOP_K), jnp.int32),
            jax.ShapeDtypeStruct((N_TOK, TOP_K), F32),
            jax.ShapeDtypeStruct((N_TOK, TOP_K), jnp.int32),
            jax.ShapeDtypeStruct((1, LANES), F32),
        ],
        scratch_shapes=[pltpu.VMEM((1, LANES), F32)],
        compiler_params=_params("arbitrary", vmem=VMEM_LIMIT_BYTES),
        name="router",
    )(x, g, sc, sh, wh, wl, bpad, tri)


DMA_QUEUES = 2
TRASH_ROWS = 2 * MOE_BLK
INVERT_UNROLL = 8


def _invert_kernel(dest_ref, pad_ref, inv_ref):
    def pad_range(e, carry):
        def fill(s, c):
            inv_ref[s] = N_SLOTS + (s & (TRASH_ROWS - 1))
            return c
        lax.fori_loop(pad_ref[2 * e], pad_ref[2 * e + 1], fill, 0)
        return carry

    lax.fori_loop(0, N_EXPERTS + 1, pad_range, 0)

    def place(j0, carry):
        for u in range(INVERT_UNROLL):
            j = j0 * INVERT_UNROLL + u
            inv_ref[dest_ref[j]] = j
        return carry

    lax.fori_loop(0, N_SLOTS // INVERT_UNROLL, place, 0)


def _invert(dest_flat, pad_ranges):
    return pl.pallas_call(
        _invert_kernel,
        grid_spec=pltpu.PrefetchScalarGridSpec(
            num_scalar_prefetch=2,
            grid=(1,),
            in_specs=[],
            out_specs=pl.BlockSpec(memory_space=pltpu.SMEM),
        ),
        out_shape=jax.ShapeDtypeStruct((CAP,), jnp.int32),
        compiler_params=_params("arbitrary"),
        name="moe_invert",
    )(dest_flat, pad_ranges)


def _expert_kernel(inv_ref, be_ref, used_ref, h_ref, wgu_ref, bgu_ref, wd_ref, bd_ref, ys_ref,
                   wgu_s, wd_s, xbuf, ybuf, gsem, ssem):
    b = pl.program_id(0)
    n_used = used_ref[0]
    cur = lax.rem(b, 2)
    nxt = 1 - cur

    def gather_row(blk, buf, r):
        j = inv_ref[blk * MOE_BLK + r]
        tok = jnp.minimum(lax.shift_right_logical(j, jnp.int32(2)), N_TOK - 1)
        return pltpu.make_async_copy(h_ref.at[pl.ds(tok, 1)], xbuf.at[buf, pl.ds(r, 1)], gsem.at[buf])

    def scatter_row(blk, buf, r):
        j = jnp.where(blk < 0, N_SLOTS + r, inv_ref[jnp.maximum(blk, 0) * MOE_BLK + r])
        return pltpu.make_async_copy(ybuf.at[buf, pl.ds(r, 1)], ys_ref.at[pl.ds(j, 1)], ssem.at[buf])

    def start_rows(row_copy):
        for r in range(MOE_BLK):
            row_copy(r).start(priority=r % DMA_QUEUES)

    def start_rows_loop(row_copy):
        def body(r2, carry):
            for u in range(DMA_QUEUES):
                row_copy(r2 * DMA_QUEUES + u).start(priority=u)
            return carry
        lax.fori_loop(0, MOE_BLK // DMA_QUEUES, body, 0)

    def wait_gathers(buf):
        pltpu.make_async_copy(h_ref.at[pl.ds(0, MOE_BLK)], xbuf.at[buf], gsem.at[buf]).wait()

    def wait_scatters(buf):
        pltpu.make_async_copy(ybuf.at[buf], ys_ref.at[pl.ds(0, MOE_BLK)], ssem.at[buf]).wait()

    next_blk = jnp.minimum(b + 1, n_used - 1)

    @pl.when(b == 0)
    def _():
        ybuf[...] = jnp.zeros_like(ybuf)
        spare = pltpu.make_async_copy(ybuf.at[0], ys_ref.at[pl.ds(N_SLOTS + MOE_BLK, MOE_BLK)], ssem.at[0])
        spare.start()
        spare.wait()
        start_rows_loop(functools.partial(gather_row, 0, 0))

    @pl.when((b > 0) & (b <= n_used))
    def _():
        wait_scatters(cur)

    @pl.when((b < n_used) & ((b == 0) | (be_ref[b] != be_ref[jnp.maximum(b - 1, 0)])))
    def _():
        wgu_s[...] = wgu_ref[...].astype(BF16)
        wd_s[...] = wd_ref[...].astype(BF16)

    @pl.when(b < n_used)
    def _():
        wait_gathers(cur)
        start_rows(functools.partial(gather_row, next_blk, nxt))
        start_rows(functools.partial(scatter_row, b - 1, nxt))
        x = xbuf[cur].reshape(MOE_BLK, D_MODEL)
        gu = _dot(x.astype(BF16), wgu_s[...]) + bgu_ref[...]
        g = jnp.minimum(gu[:, :D_EXPERT], SWIGLU_LIMIT)
        u = jnp.clip(gu[:, D_EXPERT:], -SWIGLU_LIMIT, SWIGLU_LIMIT)
        act = (u + 1.0) * (g * jax.nn.sigmoid(SWIGLU_ALPHA * g))
        y = _dot(act.astype(BF16), wd_s[...]) + bd_ref[...]
        ybuf[cur] = y.reshape((MOE_BLK,) + ROW_TILE)

    @pl.when(b == n_used)
    def _():
        wait_gathers(cur)
        start_rows_loop(functools.partial(scatter_row, b - 1, nxt))
        wait_scatters(nxt)


def _experts(layer, inv, block_e, n_used, h, w_gu, b_gu, w_down, b_down):
    def e_map(b, inv_r, be, nu):
        return (layer, be[b], 0, 0)

    hbm = pl.BlockSpec(memory_space=pl.ANY)
    return pl.pallas_call(
        _expert_kernel,
        grid_spec=pltpu.PrefetchScalarGridSpec(
            num_scalar_prefetch=3,
            grid=(N_BLOCKS + 1,),
            in_specs=[
                hbm,
                pl.BlockSpec((None, None, D_MODEL, 2 * D_EXPERT), e_map),
                pl.BlockSpec((None, None, 1, 2 * D_EXPERT), e_map),
                pl.BlockSpec((None, None, D_EXPERT, D_MODEL), e_map),
                pl.BlockSpec((None, None, 1, D_MODEL), e_map),
            ],
            out_specs=hbm,
            scratch_shapes=[
                pltpu.VMEM((D_MODEL, 2 * D_EXPERT), BF16), pltpu.VMEM((D_EXPERT, D_MODEL), BF16),
                pltpu.VMEM((2, MOE_BLK) + ROW_TILE, F32), pltpu.VMEM((2, MOE_BLK) + ROW_TILE, F32),
                pltpu.SemaphoreType.DMA((2,)), pltpu.SemaphoreType.DMA((2,)),
            ],
        ),
        out_shape=jax.ShapeDtypeStruct((N_SLOTS + TRASH_ROWS,) + ROW_TILE, F32),
        compiler_params=_params("arbitrary", vmem=VMEM_LIMIT_BYTES),
        name="moe_experts",
    )(inv, block_e, n_used, h, w_gu, b_gu.reshape(DEPTH, N_EXPERTS, 1, -1), w_down,
      b_down.reshape(DEPTH, N_EXPERTS, 1, -1))


def _combine_kernel(x_ref, g_ref, gate_ref, ys_ref, o_ref):
    gates = gate_ref[...]
    ff = gates[:, 0:1] * ys_ref[:, 0].reshape(TM, D_MODEL)
    for k in range(1, TOP_K):
        ff = ff + gates[:, k:k + 1] * ys_ref[:, k].reshape(TM, D_MODEL)
    o_ref[...] = x_ref[...] + g_ref[0] * ff


def _combine(x, gate_vec, gates, ys):
    ys = ys.reshape((ys.shape[0] // TOP_K, TOP_K) + ROW_TILE)
    return pl.pallas_call(
        _combine_kernel,
        grid=(NT,),
        in_specs=[
            _row_spec(D_MODEL), _tile_vec_spec(), pl.BlockSpec((TM, TOP_K), lambda i: (i, 0)),
            pl.BlockSpec((TM, TOP_K) + ROW_TILE, lambda i: (i, 0, 0, 0)),
        ],
        out_specs=_row_spec(D_MODEL),
        out_shape=jax.ShapeDtypeStruct((N_TOK, D_MODEL), F32),
        compiler_params=_params("arbitrary", vmem=VMEM_LIMIT_BYTES),
        name="moe_combine",
    )(x, gate_vec, gates, ys)


def _moe(layer, x, g, sc, sh, gate_vec, w_router, b_router, w_gu, b_gu, w_down, b_down):
    h, idx, gates, pos, counts = _router(x, g, sc, sh, w_router, b_router)
    cnt = counts[0, :N_EXPERTS].astype(jnp.int32)
    nblk = (cnt + MOE_BLK - 1) // MOE_BLK
    e_ids = jnp.arange(N_EXPERTS, dtype=jnp.int32)
    blk_end = jnp.sum(jnp.where(e_ids[None, :] <= e_ids[:, None], nblk[None, :], 0), axis=1)
    blk_start = blk_end - nblk
    n_used = blk_end[-1]
    dest = jnp.sum(jnp.where(idx[..., None] == e_ids, blk_start * MOE_BLK, 0), axis=-1) + pos
    b_ids = jnp.minimum(jnp.arange(N_BLOCKS + 1, dtype=jnp.int32), n_used - 1)
    block_e = jnp.sum((blk_end <= b_ids[:, None]).astype(jnp.int32), axis=1)
    pad_lo = jnp.concatenate([blk_start * MOE_BLK + cnt, n_used[None] * MOE_BLK])
    pad_hi = jnp.concatenate([blk_end * MOE_BLK, jnp.full((1,), CAP, jnp.int32)])
    pad_ranges = jnp.stack([pad_lo, pad_hi], axis=1).reshape(-1).astype(jnp.int32)
    inv = _invert(dest.reshape(-1).astype(jnp.int32), pad_ranges)
    ys = _experts(layer, inv, block_e.astype(jnp.int32), n_used.reshape(1).astype(jnp.int32), h,
                  w_gu, b_gu, w_down, b_down)
    return _combine(x, gate_vec, gates, ys)


def _rope_tables(d_rot, lane0, period):
    n_rows = DEC_SEQ // GRID_W
    rows = np.repeat(np.arange(n_rows), GRID_W).astype(np.float32)
    cols = np.tile(np.arange(GRID_W), n_rows).astype(np.float32)
    half = d_rot // 2
    lane = np.arange(LANES)
    i = (lane - lane0) % period
    active = (lane >= lane0) & (i < d_rot)
    w = i % half
    f = w % (half // 2)
    pos = np.where((i // half)[None, :] == 0, rows[:, None], cols[:, None])
    sign = np.where(w < half // 2, -1.0, 1.0).astype(np.float32)
    inv = ROPE_THETA ** (-jnp.arange(0, half, 2, dtype=F32) / half)
    ang = jnp.asarray(pos) * inv[f][None, :]
    cos = jnp.where(active[None, :], jnp.cos(ang), 1.0)
    sin = jnp.where(active[None, :], jnp.sin(ang) * sign[None, :], 0.0)
    return cos, sin


def _rope_spec():
    def index(i):
        return (jnp.maximum(i - NT_PROMPT, 0) % TILES_PER_SAMPLE, 0)
    return pl.BlockSpec((TM, LANES), index)


def _pad_heads(w, n_heads, width):
    lead = w.shape[:-1]
    w = w.reshape(lead + (n_heads, width))
    w = jnp.pad(w, [(0, 0)] * len(lead) + [(0, 0), (0, HEAD_PAD - width)])
    return w.reshape(lead + (n_heads * HEAD_PAD,))


_Q_ORDER = (0, 4, 1, 5, 2, 6, 3, 7, 8, 12, 9, 13, 10, 14, 11, 15)


def _perm_q_heads(w, axis):
    shape = w.shape
    n = shape[axis]
    w = jnp.moveaxis(w, axis, 0).reshape((GQA_HEADS, n // GQA_HEADS) + tuple(s for a, s in enumerate(shape) if a != axis))
    w = w[jnp.array(_Q_ORDER)]
    w = w.reshape((n,) + w.shape[2:])
    return jnp.moveaxis(w, 0, axis)


def _constants():
    mla_cos, mla_sin = _rope_tables(MLA_ROPE, MLA_NOPE, LANES)
    gqa_cos, gqa_sin = _rope_tables(GQA_HEAD_DIM, 0, GQA_HEAD_DIM)
    dft = {}
    for L in (SEQ, DEC_SEQ):
        cm, sm = _dft_tables(L)
        fmat = jnp.concatenate([cm, sm], axis=0).astype(BF16)
        ftmat = jnp.concatenate(_dft_tables(L, transpose=True), axis=1).astype(BF16)
        dft[L] = (cm, sm, fmat, ftmat)
    return dict(mla=(mla_cos, mla_sin), gqa=(gqa_cos, gqa_sin), dft=dft)


def _even_mixer(x, p, i, gmix, sc1, sh1, g1, consts):
    proj = _normlin(x, gmix, sc1, sh1, p["w_in_ab"], i, IN_AB_PAD)
    y_hy = []
    for L, blk0, nseq, per_step in ((SEQ, 0, BATCH, HY_PROMPT_SEQS), (DEC_SEQ, N_PROMPT // DEC_SEQ, DEC_BATCH, 1)):
        cm, sm, fmat, ftmat = consts["dft"][L]
        kc, ks = _hy_filter(L, cm, sm, p["hy_filter_w1"][i], p["hy_filter_b1"][i], p["hy_filter_freq"][i],
                            p["hy_filter_w2"][i], p["hy_filter_b2"][i], p["hy_filter_w3"][i],
                            p["hy_filter_b3"][i], p["hy_log_decay"][i])
        y_hy.append(_hy_mix(proj, blk0, nseq, per_step, L, p["hy_conv_w"][i], p["hy_conv_b"][i], kc, ks,
                            p["hy_bias"][i], fmat, ftmat))

    wq = _pad_heads(p["mla_wq_b"][i], MLA_HEADS, MLA_QK).astype(BF16)
    wkv = p["mla_wkv_b"][i].astype(BF16)
    qg = jnp.pad(p["mla_q_norm"][i], (0, HEAD_PAD - MLA_QK)).reshape(1, HEAD_PAD)
    kg = jnp.pad(p["mla_k_norm"][i], (0, HEAD_PAD - MLA_QK)).reshape(1, HEAD_PAD)
    mla_cos, mla_sin = consts["mla"]
    q, k, v, lat = _mla_prep(proj, p["mla_q_lora_norm"][i].reshape(1, -1), wq, qg,
                             p["mla_kv_lora_norm"][i].reshape(1, -1), wkv, kg, mla_cos, mla_sin)
    ctx = jnp.pad(p["cache_mla_latent"][:, i].reshape(DEC_BATCH * PAST_LEN, -1),
                  ((0, 0), (0, KV_PAD - MLA_KV_LORA - MLA_ROPE)))
    kc_ctx, vc_ctx = _mla_ctx(ctx, wkv, kg)
    nqk = MLA_HEADS * HEAD_PAD
    o = _attention(_mla_attn_kernel, q, k, v, kc_ctx, vc_ctx, nqk, nqk, nqk, MLA_HEADS * MLA_V)
    w_out = p["w_out_ab"]
    x = _outproj([y_hy, o], [(w_out, (i, 0, 0), HY_D), (w_out, (i, 1, 0), HY_D)], x, g1)
    return x, lat, dict(y_hy=jnp.concatenate(y_hy, axis=0), o=jnp.concatenate(o, axis=0))


def _odd_mixer(x, p, i, gmix, sc1, sh1, g1, consts):
    nq = GQA_HEADS * GQA_HEAD_DIM
    nk = GQA_KV_HEADS * GQA_HEAD_DIM
    w = p["w_qkv_c"][i]
    w_qkv = jnp.concatenate([_perm_q_heads(w[:, :nq], 1), w[:, nq:]], axis=1).astype(BF16)
    qg = jnp.tile(p["gqa_q_norm"][i], LANES // GQA_HEAD_DIM).reshape(1, LANES)
    kg = jnp.tile(p["gqa_k_norm"][i], LANES // GQA_HEAD_DIM).reshape(1, LANES)
    gqa_cos, gqa_sin = consts["gqa"]
    q, k, v, k_plain, v_plain = _gqa_prep(x, gmix, sc1, sh1, w_qkv, qg, kg, gqa_cos, gqa_sin)
    kc_ctx = p["cache_gqa_k"][:, i].reshape(DEC_BATCH * PAST_LEN, -1).astype(BF16)
    vc_ctx = p["cache_gqa_v"][:, i].reshape(DEC_BATCH * PAST_LEN, -1).astype(BF16)
    o = _attention(_gqa_attn_kernel, q, k, v, kc_ctx, vc_ctx, nq, nk, nk, nq)
    w_out = _perm_q_heads(p["w_out_c"][i], 0)[None]
    x = _outproj([o], [(w_out, (0, 0, 0), nq)], x, g1)
    return x, k_plain, v_plain


def kernel(x_prompt, x_sample, cache_mla_latent, cache_gqa_k, cache_gqa_v, c, c_ctx, w_ada, b_ada, norm_mix, norm_ffn, w_in_ab, hy_conv_w, hy_conv_b, hy_filter_w1, hy_filter_b1, hy_filter_freq, hy_filter_w2, hy_filter_b2, hy_filter_w3, hy_filter_b3, hy_log_decay, hy_bias, mla_q_lora_norm, mla_wq_b, mla_kv_lora_norm, mla_wkv_b, mla_q_norm, mla_k_norm, w_out_ab, w_qkv_c, gqa_q_norm, gqa_k_norm, w_out_c, moe_router_w, moe_router_b, moe_w_gate_up, moe_b_gate_up, moe_w_down, moe_b_down):
    p = dict(locals())
    x = jnp.concatenate([x_prompt.reshape(N_PROMPT, D_MODEL), x_sample.reshape(N_SAMPLE, D_MODEL)], axis=0)

    cond = jnp.concatenate([c_ctx[None, :], c, jnp.zeros((COND_ROWS - N_COND, D_MODEL), F32)], axis=0)
    mods = _modulation(cond, w_ada, b_ada)
    tile_cond = jnp.concatenate([jnp.zeros((NT_PROMPT,), jnp.int32),
                                 1 + jnp.arange(NT - NT_PROMPT, dtype=jnp.int32) // TILES_PER_SAMPLE])
    mods = mods[:, tile_cond].reshape(DEPTH, NT, 6, 1, D_MODEL).transpose(0, 2, 1, 3, 4)

    consts = _constants()

    lat_out, k_out, v_out = [], [], []
    for l in range(DEPTH):
        sh1, sc1, g1, sh2, sc2, g2 = (mods[l, j] for j in range(6))
        i = l // 2
        gmix = norm_mix[l].reshape(1, D_MODEL)
        if l % 2 == 0:
            x, lat, _ = _even_mixer(x, p, i, gmix, sc1, sh1, g1, consts)
            lat_out.append(lat[:N_PROMPT].reshape(BATCH, SEQ, -1))
        else:
            x, k_plain, v_plain = _odd_mixer(x, p, i, gmix, sc1, sh1, g1, consts)
            k_out.append(k_plain[:N_PROMPT].reshape(BATCH, SEQ, GQA_KV_HEADS, GQA_HEAD_DIM))
            v_out.append(v_plain[:N_PROMPT].reshape(BATCH, SEQ, GQA_KV_HEADS, GQA_HEAD_DIM))
        x = _moe(l, x, norm_ffn[l].reshape(1, D_MODEL), sc2, sh2, g2, moe_router_w[l], moe_router_b[l],
                 moe_w_gate_up, moe_b_gate_up, moe_w_down, moe_b_down)

    y_prompt = x[:N_PROMPT].reshape(BATCH, SEQ, D_MODEL)
    y_sample = x[N_PROMPT:].reshape(DEC_BATCH, DEC_SEQ, D_MODEL)
    return (y_prompt, y_sample, jnp.stack(lat_out, axis=1), jnp.stack(k_out, axis=1), jnp.stack(v_out, axis=1))
```

```python
import functools
import math

import jax
import jax.numpy as jnp
import numpy as np
from jax import lax
from jax.experimental import pallas as pl
from jax.experimental.pallas import tpu as pltpu

F32 = jnp.float32
BF16 = jnp.bfloat16

D_MODEL = 1024
BATCH = 32
SEQ = 256
DEPTH = 4
DEC_BATCH = 4
DEC_SEQ = 1024
PAST_LEN = 256
GRID_W = 64
N_EVEN = (DEPTH + 1) // 2
N_ODD = DEPTH // 2
HY_D = D_MODEL // 2
HY_ORDER = 2
HY_BANDS = 16
HY_EMB = 2 * HY_BANDS + 1
HY_FILTER_HIDDEN = 64
MLA_HEADS = 8
MLA_NOPE = 64
MLA_ROPE = 32
MLA_QK = MLA_NOPE + MLA_ROPE
MLA_V = HY_D // MLA_HEADS
MLA_Q_LORA = 3 * D_MODEL // 8
MLA_KV_LORA = D_MODEL // 4
GQA_HEADS = 16
GQA_KV_HEADS = 4
GQA_HEAD_DIM = D_MODEL // GQA_HEADS
N_EXPERTS = 32
TOP_K = 4
D_EXPERT = D_MODEL
SWIGLU_LIMIT = 7.0
SWIGLU_ALPHA = 1.702
ROPE_THETA = 10000.0
RMS_EPS = 1e-6
GQA_SCALE = GQA_HEAD_DIM ** -0.5
assert math.frexp(GQA_SCALE)[0] == 0.5
IN_AB = (HY_ORDER + 1) * HY_D + MLA_Q_LORA + MLA_KV_LORA + MLA_ROPE

N_PROMPT = BATCH * SEQ
N_SAMPLE = DEC_BATCH * DEC_SEQ
N_TOK = N_PROMPT + N_SAMPLE

LANES = 128
SUBLANES = 8
VMEM_LIMIT_BYTES = 56 * 1024 * 1024

TM = 256
NT = N_TOK // TM
NT_PROMPT = N_PROMPT // TM
TILES_PER_SAMPLE = DEC_SEQ // TM
N_COND = 1 + DEC_BATCH
COND_ROWS = 8
HEAD_PAD = LANES
IN_AB_PAD = 2304
KV_PAD = 384
MOE_HALF = 256
MOE_BLK = 2 * MOE_HALF
N_SLOTS = N_TOK * TOP_K
N_BLOCKS = N_SLOTS // MOE_BLK + N_EXPERTS
CAP = N_BLOCKS * MOE_BLK
ROW_TILE = (SUBLANES, LANES)
assert SUBLANES * LANES == D_MODEL
HY_CH = 256
HY_PROMPT_SEQS = 8


def _dot(a, b):
    return jnp.dot(a, b, preferred_element_type=F32)


def _dot_nt(a, b):
    return lax.dot_general(a, b, (((1,), (1,)), ((), ())), preferred_element_type=F32)


def _split(x):
    hi = x.astype(BF16)
    lo = (x - hi.astype(F32)).astype(BF16)
    return hi, lo


def _dot3(a, b):
    ah, al = _split(a)
    bh, bl = _split(b)
    return _dot(ah, bh) + (_dot(ah, bl) + _dot(al, bh))


def _lane_iota(shape):
    return lax.broadcasted_iota(jnp.int32, shape, len(shape) - 1)


def _params(*sem, vmem=None):
    return pltpu.CompilerParams(dimension_semantics=sem, vmem_limit_bytes=vmem)


def _mod_kernel(c_ref, w_ref, b_ref, o_ref):
    c = c_ref[...]
    s = c * jax.nn.sigmoid(c)
    o_ref[0] = _dot(s.astype(BF16), w_ref[0].astype(BF16)) + b_ref[0]


def _modulation(cond, w_ada, b_ada):
    nblk = 6
    return pl.pallas_call(
        _mod_kernel,
        grid=(DEPTH, nblk),
        in_specs=[
            pl.BlockSpec((COND_ROWS, D_MODEL), lambda l, j: (0, 0)),
            pl.BlockSpec((1, D_MODEL, D_MODEL), lambda l, j: (l, 0, j)),
            pl.BlockSpec((1, 1, D_MODEL), lambda l, j: (l, 0, j)),
        ],
        out_specs=pl.BlockSpec((1, COND_ROWS, D_MODEL), lambda l, j: (l, 0, j)),
        out_shape=jax.ShapeDtypeStruct((DEPTH, COND_ROWS, nblk * D_MODEL), F32),
        compiler_params=_params("arbitrary", "arbitrary"),
        name="modulation",
    )(cond, w_ada, b_ada.reshape(DEPTH, 1, nblk * D_MODEL))


def _norm_mod(x, g, sc, sh):
    ms = jnp.mean(x * x, axis=-1, keepdims=True)
    y = x * lax.rsqrt(ms + RMS_EPS)
    return (y * g) * (1.0 + sc) + sh


def _row_spec(width):
    return pl.BlockSpec((TM, width), lambda i: (i, 0))


def _tile_vec_spec():
    return pl.BlockSpec((1, 1, D_MODEL), lambda i: (i, 0, 0))


def _full_spec(shape):
    nd = len(shape)
    return pl.BlockSpec(shape, lambda i: (0,) * nd)


def _normlin_kernel(x_ref, g_ref, sc_ref, sh_ref, w_ref, o_ref, w_scr):
    nin = w_ref.shape[1]

    @pl.when(pl.program_id(0) == 0)
    def _():
        w_scr[:, :nin] = w_ref[...].astype(BF16)
        w_scr[:, nin:] = jnp.zeros((D_MODEL, w_scr.shape[1] - nin), BF16)

    h = _norm_mod(x_ref[...], g_ref[...], sc_ref[0], sh_ref[0])
    o_ref[...] = _dot(h.astype(BF16), w_scr[...])


def _normlin(x, g, sc, sh, w_all, layer, nout):
    nin = w_all.shape[2]
    return pl.pallas_call(
        _normlin_kernel,
        grid=(NT,),
        in_specs=[_row_spec(D_MODEL), _full_spec((1, D_MODEL)), _tile_vec_spec(), _tile_vec_spec(),
                  pl.BlockSpec((None, D_MODEL, nin), lambda i: (layer, 0, 0))],
        out_specs=_row_spec(nout),
        out_shape=jax.ShapeDtypeStruct((N_TOK, nout), F32),
        scratch_shapes=[pltpu.VMEM((D_MODEL, nout), BF16)],
        compiler_params=_params("arbitrary", vmem=VMEM_LIMIT_BYTES),
        name="normlin",
    )(x, g, sc, sh, w_all)


def _hy_filter_kernel(z_ref, w1_ref, b1_ref, fr_ref, w2_ref, b2_ref, w3_ref, b3_ref, ed_ref,
                      c_ref, s_ref, kc_ref, ks_ref):
    L = z_ref.shape[0]
    z = z_ref[...]
    fr = fr_ref[...]
    hdn = jnp.sin(fr * (_dot3(z, w1_ref[...]) + b1_ref[...]))
    hdn = jnp.sin(fr * (_dot3(hdn, w2_ref[...]) + b2_ref[...]))
    filt = _dot3(hdn, w3_ref[...]) + b3_ref[...]
    t = z[:, 0:1]
    filt = filt * jnp.exp(-t * ed_ref[...])
    row = lax.broadcasted_iota(jnp.int32, (L, HY_D), 0)
    cm = c_ref[...]
    sm = s_ref[...]
    for o in range(HY_ORDER):
        fw = filt[:, (2 * o) * HY_D:(2 * o + 1) * HY_D]
        bw = filt[:, (2 * o + 1) * HY_D:(2 * o + 2) * HY_D]
        den = (jnp.sum(jnp.abs(fw), axis=0, keepdims=True)
               + jnp.sum(jnp.abs(bw), axis=0, keepdims=True)) + 1e-6
        fw = fw / den
        bw = jnp.where(row == 0, 0.0, bw / den)
        kc_ref[o] = _dot3(cm, fw + bw) * (1.0 / L)
        ks_ref[o] = _dot3(sm, fw - bw) * (1.0 / L)


def _dft_tables(L, transpose=False):
    m = jnp.arange(L, dtype=jnp.int32)
    odd = 2 * m + 1
    phase = (m[:, None] * odd[None, :] if transpose else odd[:, None] * m[None, :]) % (4 * L)
    ang = phase.astype(F32) * (2.0 * math.pi / (4 * L))
    return jnp.cos(ang), jnp.sin(ang)


def _filter_features(L):
    p = jnp.arange(L, dtype=F32)
    t = p / max(L - 1, 1)
    bands = jnp.linspace(1e-4, HY_BANDS - 1, HY_BANDS, dtype=F32)
    ang = (2.0 * math.pi / L) * p[:, None] * bands[None, :]
    z = jnp.concatenate([t[:, None], jnp.cos(ang), -jnp.sin(ang)], axis=-1)
    return jnp.pad(z, ((0, 0), (0, LANES - HY_EMB)))


def _hy_filter(L, cmat, smat, w1, b1, fr, w2, b2, w3, b3, log_decay):
    nf = HY_ORDER * 2 * HY_D
    args = (
        _filter_features(L),
        jnp.pad(w1, ((0, LANES - HY_EMB), (0, 0))),
        b1.reshape(1, -1), fr.reshape(1, -1), w2, b2.reshape(1, -1), w3, b3.reshape(1, -1),
        jnp.exp(log_decay.astype(F32)).reshape(1, nf),
        cmat, smat,
    )
    out_sds = jax.ShapeDtypeStruct((HY_ORDER, L, HY_D), F32)
    return pl.pallas_call(
        _hy_filter_kernel,
        grid=(1,),
        in_specs=[_full_spec(a.shape) for a in args],
        out_specs=[_full_spec(out_sds.shape)] * 2,
        out_shape=[out_sds, out_sds],
        compiler_params=_params("arbitrary", vmem=VMEM_LIMIT_BYTES),
        name=f"hy_filter_{L}",
    )(*args)


def _hy_mix_kernel(u0_ref, u1_ref, u2_ref, cw0_ref, cw1_ref, cw2_ref, cb0_ref, cb1_ref, cb2_ref,
                   kc_ref, ks_ref, hb_ref, f_ref, ft_ref, o_ref):
    S, L, _ = u0_ref.shape
    row = lax.broadcasted_iota(jnp.int32, (L, HY_CH), 0)

    def lanes(per_seq):
        return per_seq[0] if S == 1 else jnp.concatenate(per_seq, axis=1)

    def short_conv(u_ref, cw_ref, cb_ref):
        w = cw_ref[0]
        out = []
        for s in range(S):
            u = u_ref[s]
            prev = jnp.where(row == 0, 0.0, pltpu.roll(u, 1, 0))
            nxt = jnp.where(row == L - 1, 0.0, pltpu.roll(u, L - 1, 0))
            out.append((prev * w[0:1] + u * w[1:2]) + nxt * w[2:3] + cb_ref[0])
        return lanes(out)

    z = short_conv(u0_ref, cw0_ref, cb0_ref)
    gates = (short_conv(u1_ref, cw1_ref, cb1_ref), short_conv(u2_ref, cw2_ref, cb2_ref))
    for o in range(HY_ORDER):
        zz = _dot(f_ref[...], z.astype(BF16))
        cz, sz = zz[:L], zz[L:]
        kc, ks = lanes([kc_ref[o]] * S), lanes([ks_ref[o]] * S)
        w1 = cz * kc - sz * ks
        w2 = cz * ks + sz * kc
        ww = jnp.concatenate([w1, w2], axis=0).astype(BF16)
        conv = _dot(ft_ref[...], ww)
        z = gates[o] * (conv + z * lanes([hb_ref[0, o]] * S))
    for s in range(S):
        o_ref[s] = z[:, s * HY_CH:(s + 1) * HY_CH].astype(o_ref.dtype)


def _hy_mix(proj, row_block0, nseq, seqs_per_step, L, cw, cb, kc, ks, hb, fmat, ftmat):
    nch = HY_D // HY_CH
    nparts = HY_ORDER + 1
    S = seqs_per_step
    assert nseq % S == 0 and row_block0 % S == 0
    cw3 = cw.reshape(3, nparts * nch, HY_CH).transpose(1, 0, 2)
    cb3 = cb.reshape(nparts * nch, 1, HY_CH)
    hb3 = hb.reshape(HY_ORDER, nch, 1, HY_CH).transpose(1, 0, 2, 3)
    proj = proj.reshape(N_TOK // L, L, proj.shape[1])

    def u_spec(part):
        return pl.BlockSpec((S, L, HY_CH), lambda s, c: (row_block0 // S + s, 0, part * nch + c))

    def cw_spec(part):
        return pl.BlockSpec((1, 3, HY_CH), lambda s, c: (part * nch + c, 0, 0))

    def cb_spec(part):
        return pl.BlockSpec((1, 1, HY_CH), lambda s, c: (part * nch + c, 0, 0))

    return pl.pallas_call(
        _hy_mix_kernel,
        grid=(nseq // S, nch),
        in_specs=[
            u_spec(0), u_spec(1), u_spec(2),
            cw_spec(0), cw_spec(1), cw_spec(2),
            cb_spec(0), cb_spec(1), cb_spec(2),
            pl.BlockSpec((HY_ORDER, L, HY_CH), lambda s, c: (0, 0, c)),
            pl.BlockSpec((HY_ORDER, L, HY_CH), lambda s, c: (0, 0, c)),
            pl.BlockSpec((1, HY_ORDER, 1, HY_CH), lambda s, c: (c, 0, 0, 0)),
            pl.BlockSpec((2 * L, L), lambda s, c: (0, 0)),
            pl.BlockSpec((L, 2 * L), lambda s, c: (0, 0)),
        ],
        out_specs=pl.BlockSpec((S, L, HY_CH), lambda s, c: (s, 0, c)),
        out_shape=jax.ShapeDtypeStruct((nseq, L, HY_D), BF16),
        compiler_params=_params("arbitrary", "arbitrary", vmem=VMEM_LIMIT_BYTES),
        name=f"hy_mix_{L}",
    )(proj, proj, proj, cw3, cw3, cw3, cb3, cb3, cb3, kc, ks, hb3, fmat, ftmat).reshape(nseq * L, HY_D)


def _head_rms(xh, gain, dim):
    ms = jnp.sum(xh * xh, axis=-1, keepdims=True) * (1.0 / dim)
    return (xh * lax.rsqrt(ms + RMS_EPS)) * gain


def _per_group(context_fn, latent_fn):
    is_context = pl.program_id(0) < NT_PROMPT
    pl.when(is_context)(context_fn)
    pl.when(jnp.logical_not(is_context))(latent_fn)


def _rope(xh, cos, sin, half):
    lane = _lane_iota(xh.shape)
    first = (lane % (2 * half)) < half
    rot = jnp.where(first, pltpu.roll(xh, LANES - half, 1), pltpu.roll(xh, half, 1))
    return xh * cos + rot * sin


def _mla_keys_values(lat_n, kr_blk, wkv_ref, kg_ref, rope):
    kv = _dot(lat_n.astype(BF16), wkv_ref[...])
    kr = pltpu.roll(kr_blk, MLA_NOPE, 1)
    nope = _lane_iota(kr.shape) < MLA_NOPE
    ks = []
    for h in range(MLA_HEADS):
        kh = jnp.where(nope, kv[:, h * HEAD_PAD:(h + 1) * HEAD_PAD], 0.0) + kr
        kh = _head_rms(kh, kg_ref[...], MLA_QK)
        if rope is not None:
            kh = _rope(kh, rope[0], rope[1], MLA_ROPE // 4)
        ks.append(kh)
    return ks, kv


def _mla_prep_kernel(qa_ref, kva_ref, gq_ref, wq_ref, qg_ref, gkv_ref, wkv_ref, kg_ref,
                     cos_ref, sin_ref, q_ref, k_ref, v_ref, lat_ref):
    def body(rope):
        qa = qa_ref[...]
        ms = jnp.mean(qa * qa, axis=-1, keepdims=True)
        qn = (qa * lax.rsqrt(ms + RMS_EPS)) * gq_ref[...]
        q = _dot(qn.astype(BF16), wq_ref[...])
        for h in range(MLA_HEADS):
            qh = _head_rms(q[:, h * HEAD_PAD:(h + 1) * HEAD_PAD], qg_ref[...], MLA_QK)
            if rope is not None:
                qh = _rope(qh, rope[0], rope[1], MLA_ROPE // 4)
            q_ref[:, h * HEAD_PAD:(h + 1) * HEAD_PAD] = qh.astype(BF16)

        kva = kva_ref[...]
        lat = kva[:, :MLA_KV_LORA]
        ms = jnp.mean(lat * lat, axis=-1, keepdims=True)
        lat_n = (lat * lax.rsqrt(ms + RMS_EPS)) * gkv_ref[...]
        kr_blk = kva[:, MLA_KV_LORA:]
        lat_ref[:, :MLA_KV_LORA] = lat_n
        lat_ref[:, MLA_KV_LORA:] = kr_blk[:, :MLA_ROPE]
        ks, v = _mla_keys_values(lat_n, kr_blk, wkv_ref, kg_ref, rope)
        for h in range(MLA_HEADS):
            k_ref[:, h * HEAD_PAD:(h + 1) * HEAD_PAD] = ks[h].astype(BF16)
        v_ref[...] = v.astype(BF16)

    _per_group(lambda: body(None), lambda: body((cos_ref[...], sin_ref[...])))


def _mla_prep(proj, gq, wq, qg, gkv, wkv, kg, cos_t, sin_t):
    nq = MLA_HEADS * HEAD_PAD
    nv = MLA_HEADS * (MLA_NOPE + MLA_V)
    qa_blk = (HY_ORDER + 1) * HY_D // KV_PAD
    return pl.pallas_call(
        _mla_prep_kernel,
        grid=(NT,),
        in_specs=[
            pl.BlockSpec((TM, KV_PAD), lambda i: (i, qa_blk)),
            pl.BlockSpec((TM, KV_PAD), lambda i: (i, qa_blk + 1)),
            _full_spec(gq.shape), _full_spec(wq.shape), _full_spec(qg.shape), _full_spec(gkv.shape),
            _full_spec(wkv.shape), _full_spec(kg.shape),
            _rope_spec(), _rope_spec(),
        ],
        out_specs=[_row_spec(nq), _row_spec(nq), _row_spec(nv), _row_spec(MLA_KV_LORA + MLA_ROPE)],
        out_shape=[
            jax.ShapeDtypeStruct((N_TOK, nq), BF16),
            jax.ShapeDtypeStruct((N_TOK, nq), BF16),
            jax.ShapeDtypeStruct((N_TOK, nv), BF16),
            jax.ShapeDtypeStruct((N_TOK, MLA_KV_LORA + MLA_ROPE), F32),
        ],
        compiler_params=_params("arbitrary", vmem=VMEM_LIMIT_BYTES),
        name="mla_prep",
    )(proj, proj, gq, wq, qg, gkv, wkv, kg, cos_t, sin_t)


def _mla_ctx_kernel(lat_ref, wkv_ref, kg_ref, k_ref, v_ref):
    lat = lat_ref[...]
    ks, v = _mla_keys_values(lat[:, :MLA_KV_LORA], lat[:, MLA_KV_LORA:], wkv_ref, kg_ref, None)
    for h in range(MLA_HEADS):
        k_ref[:, h * HEAD_PAD:(h + 1) * HEAD_PAD] = ks[h].astype(BF16)
    v_ref[...] = v.astype(BF16)


def _mla_ctx(lat_pad, wkv, kg):
    n = lat_pad.shape[0]
    nq = MLA_HEADS * HEAD_PAD
    nv = MLA_HEADS * (MLA_NOPE + MLA_V)
    return pl.pallas_call(
        _mla_ctx_kernel,
        grid=(n // TM,),
        in_specs=[_row_spec(KV_PAD), _full_spec(wkv.shape), _full_spec(kg.shape)],
        out_specs=[_row_spec(nq), _row_spec(nv)],
        out_shape=[jax.ShapeDtypeStruct((n, nq), BF16), jax.ShapeDtypeStruct((n, nv), BF16)],
        compiler_params=_params("arbitrary"),
        name="mla_ctx",
    )(lat_pad, wkv, kg)


def _softmax_pv(q, keys, vals, scale):
    ss = [_dot_nt(q, k) for k in keys]
    if scale is not None:
        ss = [s * scale for s in ss]
    m = ss[0].max(axis=-1, keepdims=True)
    for s in ss[1:]:
        m = jnp.maximum(m, s.max(axis=-1, keepdims=True))
    ps = [jnp.exp(s - m) for s in ss]
    l = ps[0].sum(axis=-1, keepdims=True)
    for p in ps[1:]:
        l = l + p.sum(axis=-1, keepdims=True)
    o = _dot(ps[0].astype(BF16), vals[0])
    for p, v in zip(ps[1:], vals[1:]):
        o = o + _dot(p.astype(BF16), v)
    return o / l


def _mla_attn_kernel(*refs, has_ctx):
    if has_ctx:
        q_ref, k_ref, v_ref, kc_ref, vc_ref, o_ref = refs
    else:
        q_ref, k_ref, v_ref, o_ref = refs
    scale = MLA_QK ** -0.5
    lo = _lane_iota((q_ref.shape[0], LANES)) < MLA_V
    for j in range(MLA_HEADS // 2):
        outs = []
        for h in (2 * j, 2 * j + 1):
            sl = slice(h * HEAD_PAD, (h + 1) * HEAD_PAD)
            ks, vs = [k_ref[:, sl]], [v_ref[:, sl]]
            if has_ctx:
                ks.append(kc_ref[:, sl])
                vs.append(vc_ref[:, sl])
            outs.append(_softmax_pv(q_ref[:, sl], ks, vs, scale))
        pair = jnp.where(lo, pltpu.roll(outs[0], MLA_V, 1), outs[1])
        o_ref[:, j * LANES:(j + 1) * LANES] = pair.astype(o_ref.dtype)


def _gqa_attn_kernel(*refs, has_ctx):
    if has_ctx:
        q_ref, k_ref, v_ref, kc_ref, vc_ref, o_ref = refs
    else:
        q_ref, k_ref, v_ref, o_ref = refs
    scale = None
    lo = _lane_iota((q_ref.shape[0], LANES)) < GQA_HEAD_DIM
    pairs_per_kv = (GQA_HEADS // 2) // (GQA_KV_HEADS // 2)
    for p in range(GQA_HEADS // 2):
        kv = slice((p // pairs_per_kv) * LANES, (p // pairs_per_kv + 1) * LANES)
        ks, vs = [k_ref[:, kv]], [v_ref[:, kv]]
        if has_ctx:
            ks.append(kc_ref[:, kv])
            vs.append(vc_ref[:, kv])
        qp = q_ref[:, p * LANES:(p + 1) * LANES]
        zero = jnp.zeros_like(qp)
        o_lo = _softmax_pv(jnp.where(lo, qp, zero), ks, vs, scale)
        o_hi = _softmax_pv(jnp.where(lo, zero, qp), ks, vs, scale)
        o_ref[:, p * LANES:(p + 1) * LANES] = jnp.where(lo, o_lo, o_hi).astype(o_ref.dtype)


def _attention(body, q, k, v, kc, vc, wq, wk, wv, wo):
    outs = []
    outs.append(pl.pallas_call(
        functools.partial(body, has_ctx=False),
        grid=(BATCH,),
        in_specs=[pl.BlockSpec((SEQ, wq), lambda b: (b, 0)),
                  pl.BlockSpec((SEQ, wk), lambda b: (b, 0)),
                  pl.BlockSpec((SEQ, wv), lambda b: (b, 0))],
        out_specs=pl.BlockSpec((SEQ, wo), lambda b: (b, 0)),
        out_shape=jax.ShapeDtypeStruct((N_PROMPT, wo), BF16),
        compiler_params=_params("arbitrary", vmem=VMEM_LIMIT_BYTES),
        name="attn_prompt",
    )(q, k, v))
    qt = DEC_SEQ // TM
    q0 = N_PROMPT // TM
    s0 = N_PROMPT // DEC_SEQ
    outs.append(pl.pallas_call(
        functools.partial(body, has_ctx=True),
        grid=(DEC_BATCH, qt),
        in_specs=[pl.BlockSpec((TM, wq), lambda b, t: (q0 + b * qt + t, 0)),
                  pl.BlockSpec((DEC_SEQ, wk), lambda b, t: (s0 + b, 0)),
                  pl.BlockSpec((DEC_SEQ, wv), lambda b, t: (s0 + b, 0)),
                  pl.BlockSpec((PAST_LEN, wk), lambda b, t: (b, 0)),
                  pl.BlockSpec((PAST_LEN, wv), lambda b, t: (b, 0))],
        out_specs=pl.BlockSpec((TM, wo), lambda b, t: (b * qt + t, 0)),
        out_shape=jax.ShapeDtypeStruct((N_SAMPLE, wo), BF16),
        compiler_params=_params("arbitrary", "arbitrary", vmem=VMEM_LIMIT_BYTES),
        name="attn_sample",
    )(q, k, v, kc, vc))
    return outs


def _gqa_prep_kernel(x_ref, g_ref, sc_ref, sh_ref, w_ref, qg_ref, kg_ref, cos_ref, sin_ref,
                     q_ref, k_ref, v_ref, kp_ref, vp_ref):
    lo = _lane_iota((TM, LANES)) < GQA_HEAD_DIM
    nq = GQA_HEADS * GQA_HEAD_DIM
    nk = GQA_KV_HEADS * GQA_HEAD_DIM

    def pair_norm(xp, gain):
        sq = xp * xp
        ms_lo = jnp.sum(jnp.where(lo, sq, 0.0), axis=-1, keepdims=True)
        ms_hi = jnp.sum(jnp.where(lo, 0.0, sq), axis=-1, keepdims=True)
        ms = jnp.where(lo, ms_lo, ms_hi) * (1.0 / GQA_HEAD_DIM)
        return (xp * lax.rsqrt(ms + RMS_EPS)) * gain

    def body(rope):
        def rotate(xp):
            return xp if rope is None else _rope(xp, rope[0], rope[1], GQA_HEAD_DIM // 4)

        h = _norm_mod(x_ref[...], g_ref[...], sc_ref[0], sh_ref[0])
        qkv = _dot(h.astype(BF16), w_ref[...])
        for p in range(nq // LANES):
            sl = slice(p * LANES, (p + 1) * LANES)
            qn = pair_norm(qkv[:, sl], qg_ref[...])
            q_ref[:, sl] = (rotate(qn) * GQA_SCALE).astype(BF16)
        for p in range(nk // LANES):
            sl = slice(p * LANES, (p + 1) * LANES)
            kn = pair_norm(qkv[:, nq + p * LANES:nq + (p + 1) * LANES], kg_ref[...])
            kp_ref[:, sl] = kn
            k_ref[:, sl] = rotate(kn).astype(BF16)
        v = qkv[:, nq + nk:]
        vp_ref[...] = v
        v_ref[...] = v.astype(BF16)

    _per_group(lambda: body(None), lambda: body((cos_ref[...], sin_ref[...])))


def _gqa_prep(x, g, sc, sh, w, qg, kg, cos_t, sin_t):
    nq = GQA_HEADS * GQA_HEAD_DIM
    nk = GQA_KV_HEADS * GQA_HEAD_DIM
    return pl.pallas_call(
        _gqa_prep_kernel,
        grid=(NT,),
        in_specs=[_row_spec(D_MODEL), _full_spec((1, D_MODEL)), _tile_vec_spec(), _tile_vec_spec(),
                  _full_spec(w.shape), _full_spec(qg.shape), _full_spec(kg.shape),
                  _rope_spec(), _rope_spec()],
        out_specs=[_row_spec(nq), _row_spec(nk), _row_spec(nk), _row_spec(nk), _row_spec(nk)],
        out_shape=[
            jax.ShapeDtypeStruct((N_TOK, nq), BF16),
            jax.ShapeDtypeStruct((N_TOK, nk), BF16),
            jax.ShapeDtypeStruct((N_TOK, nk), BF16),
            jax.ShapeDtypeStruct((N_TOK, nk), F32),
            jax.ShapeDtypeStruct((N_TOK, nk), F32),
        ],
        compiler_params=_params("arbitrary", vmem=VMEM_LIMIT_BYTES),
        name="gqa_prep",
    )(x, g, sc, sh, w, qg, kg, cos_t, sin_t)


def _outproj_kernel(*refs, n_in):
    ap_refs = refs[:n_in]
    as_refs = refs[n_in:2 * n_in]
    w_refs = refs[2 * n_in:3 * n_in]
    x_ref, g_ref, o_ref = refs[3 * n_in:3 * n_in + 3]
    w_scr = refs[3 * n_in + 3:]
    is_prompt = pl.program_id(0) < NT_PROMPT

    @pl.when(pl.program_id(0) == 0)
    def _():
        for w, s in zip(w_refs, w_scr):
            s[...] = w[...].astype(BF16)

    y = None
    for ap, asm, s in zip(ap_refs, as_refs, w_scr):
        a = jnp.where(is_prompt, ap[...], asm[...])
        d = _dot(a, s[...])
        y = d if y is None else y + d
    o_ref[...] = x_ref[...] + g_ref[0] * y


def _outproj(acts, ws, x, gate):
    n_in = len(acts)

    def w_spec(index, rows):
        return pl.BlockSpec((None, rows, D_MODEL), lambda i: index)

    def prompt_spec(width):
        return pl.BlockSpec((TM, width), lambda i: (jnp.minimum(i, NT_PROMPT - 1), 0))

    def sample_spec(width):
        return pl.BlockSpec((TM, width), lambda i: (jnp.maximum(i - NT_PROMPT, 0), 0))

    return pl.pallas_call(
        functools.partial(_outproj_kernel, n_in=n_in),
        grid=(NT,),
        in_specs=([prompt_spec(ap.shape[1]) for ap, _ in acts] + [sample_spec(asm.shape[1]) for _, asm in acts]
                  + [w_spec(index, rows) for _, index, rows in ws] + [_row_spec(D_MODEL), _tile_vec_spec()]),
        out_specs=_row_spec(D_MODEL),
        out_shape=jax.ShapeDtypeStruct((N_TOK, D_MODEL), F32),
        scratch_shapes=[pltpu.VMEM((rows, D_MODEL), BF16) for _, _, rows in ws],
        compiler_params=_params("arbitrary", vmem=VMEM_LIMIT_BYTES),
        name="outproj",
    )(*(ap for ap, _ in acts), *(asm for _, asm in acts), *(w for w, _, _ in ws), x, gate)


def _router_kernel(x_ref, g_ref, sc_ref, sh_ref, wh_ref, wl_ref, br_ref, tri_ref,
                   h_ref, idx_ref, gate_ref, pos_ref, cnt_ref, run_ref):
    @pl.when(pl.program_id(0) == 0)
    def _():
        run_ref[...] = jnp.zeros_like(run_ref)

    h = _norm_mod(x_ref[...], g_ref[...], sc_ref[0], sh_ref[0])
    h_ref[...] = h.reshape((TM,) + ROW_TILE)
    hh, hl = _split(h)
    logits = _dot(hh, wh_ref[...]) + (_dot(hh, wl_ref[...]) + _dot(hl, wh_ref[...])) + br_ref[...]
    lane = _lane_iota((TM, LANES)).astype(F32)
    neg = jnp.float32(-jnp.inf)
    lg = jnp.where(lane < N_EXPERTS, logits, neg)
    tops, sels, hots = [], [], []
    for _ in range(TOP_K):
        m = lg.max(axis=-1, keepdims=True)
        sel = jnp.where(lg == m, lane, float(LANES)).min(axis=-1, keepdims=True)
        hot = lane == sel
        lg = jnp.where(hot, neg, lg)
        tops.append(m)
        sels.append(sel)
        hots.append(hot)
    es = [jnp.exp(t - tops[0]) for t in tops]
    den = es[0] + es[1] + es[2] + es[3]
    member = jnp.zeros((TM, LANES), F32)
    for hot in hots:
        member = member + hot.astype(F32)
    ranks = _dot(tri_ref[...], member.astype(BF16)) + run_ref[...]
    lane4 = _lane_iota((TM, TOP_K))
    idx4 = jnp.zeros((TM, TOP_K), F32)
    gate4 = jnp.zeros((TM, TOP_K), F32)
    pos4 = jnp.zeros((TM, TOP_K), F32)
    for k in range(TOP_K):
        pk = jnp.sum(jnp.where(hots[k], ranks, 0.0), axis=-1, keepdims=True)
        idx4 = jnp.where(lane4 == k, sels[k], idx4)
        gate4 = jnp.where(lane4 == k, es[k] / den, gate4)
        pos4 = jnp.where(lane4 == k, pk, pos4)
    idx_ref[...] = idx4.astype(jnp.int32)
    gate_ref[...] = gate4
    pos_ref[...] = pos4.astype(jnp.int32)
    run_ref[...] = run_ref[...] + jnp.sum(member, axis=0, keepdims=True)
    cnt_ref[...] = run_ref[...]


def _router(x, g, sc, sh, w_router, b_router):
    wpad = jnp.pad(w_router, ((0, 0), (0, LANES - N_EXPERTS)))
    wh = wpad.astype(BF16)
    wl = (wpad - wh.astype(F32)).astype(BF16)
    bpad = jnp.pad(b_router, (0, LANES - N_EXPERTS)).reshape(1, LANES)
    r = np.arange(TM)
    tri = jnp.asarray(r[None, :] < r[:, None], dtype=BF16)
    narrow = pl.BlockSpec((TM, TOP_K), lambda i: (i, 0))
    return pl.pallas_call(
        _router_kernel,
        grid=(NT,),
        in_specs=[_row_spec(D_MODEL), _full_spec((1, D_MODEL)), _tile_vec_spec(), _tile_vec_spec(),
                  _full_spec(wh.shape), _full_spec(wl.shape), _full_spec(bpad.shape), _full_spec(tri.shape)],
        out_specs=[pl.BlockSpec((TM,) + ROW_TILE, lambda i: (i, 0, 0)), narrow, narrow, narrow,
                   _full_spec((1, LANES))],
        out_shape=[
            jax.ShapeDtypeStruct((N_TOK,) + ROW_TILE, F32),
            jax.ShapeDtypeStruct((N_TOK, TOP_K), jnp.int32),
            jax.ShapeDtypeStruct((N_TOK, TOP_K), F32),
            jax.ShapeDtypeStruct((N_TOK, TOP_K), jnp.int32),
            jax.ShapeDtypeStruct((1, LANES), F32),
        ],
        scratch_shapes=[pltpu.VMEM((1, LANES), F32)],
        compiler_params=_params("arbitrary", vmem=VMEM_LIMIT_BYTES),
        name="router",
    )(x, g, sc, sh, wh, wl, bpad, tri)


ROW_UNROLL = 4
DMA_QUEUES = 2


def _start_all_rows(row_copy):
    def start_rows(j, carry):
        for u in range(ROW_UNROLL):
            for k in range(TOP_K):
                row_copy(j * ROW_UNROLL + u, k).start(priority=k % DMA_QUEUES)
        return carry

    lax.fori_loop(0, TM // ROW_UNROLL, start_rows, 0)


def _dispatch_kernel(dest_ref, clear_ref, h_ref, xs_ref, zero_ref, sem_ref):
    def row_copy(t, k):
        d = dest_ref[(pl.program_id(0) * TM + t) * TOP_K + k]
        return pltpu.make_async_copy(h_ref.at[pl.ds(t, 1)], xs_ref.at[pl.ds(d, 1)], sem_ref.at[0])

    def zero_half(j):
        row0 = pl.multiple_of(j * MOE_HALF, MOE_HALF)
        return pltpu.make_async_copy(zero_ref, xs_ref.at[pl.ds(row0, MOE_HALF)], sem_ref.at[1])

    @pl.when(pl.program_id(0) == 0)
    def _():
        zero_ref[...] = jnp.zeros_like(zero_ref)

        def start(j, carry):
            @pl.when(clear_ref[j] > 0)
            def _():
                zero_half(j).start()
            return carry

        def wait(j, carry):
            @pl.when(clear_ref[j] > 0)
            def _():
                zero_half(j).wait()
            return carry

        lax.fori_loop(0, 2 * N_BLOCKS, start, 0)
        lax.fori_loop(0, 2 * N_BLOCKS, wait, 0)

    _start_all_rows(row_copy)
    for _ in range(TOP_K):
        pltpu.make_async_copy(h_ref, xs_ref.at[pl.ds(0, TM)], sem_ref.at[0]).wait()


def _dispatch(dest_flat, clear, h):
    return pl.pallas_call(
        _dispatch_kernel,
        grid_spec=pltpu.PrefetchScalarGridSpec(
            num_scalar_prefetch=2,
            grid=(NT,),
            in_specs=[pl.BlockSpec((TM,) + ROW_TILE, lambda i, d, c: (i, 0, 0))],
            out_specs=pl.BlockSpec(memory_space=pl.ANY),
            scratch_shapes=[pltpu.VMEM((MOE_HALF,) + ROW_TILE, F32), pltpu.SemaphoreType.DMA((2,))],
        ),
        out_shape=jax.ShapeDtypeStruct((CAP,) + ROW_TILE, F32),
        compiler_params=_params("arbitrary", vmem=VMEM_LIMIT_BYTES),
        name="moe_dispatch",
    )(dest_flat, clear, h)


def _expert_kernel(be_ref, nh_ref, last_ref, xs_ref, wgu_ref, bgu_ref, wd_ref, bd_ref, ys_ref, wgu_s, wd_s):
    del last_ref
    b = pl.program_id(0)
    nh = nh_ref[b]

    @pl.when((nh > 0) & ((b == 0) | (be_ref[b] != be_ref[jnp.maximum(b - 1, 0)])))
    def _():
        wgu_s[...] = wgu_ref[...].astype(BF16)
        wd_s[...] = wd_ref[...].astype(BF16)

    def ffn(rows):
        x = xs_ref[0:rows].reshape(rows, D_MODEL)
        gu = _dot(x.astype(BF16), wgu_s[...]) + bgu_ref[...]
        g = jnp.minimum(gu[:, :D_EXPERT], SWIGLU_LIMIT)
        u = jnp.clip(gu[:, D_EXPERT:], -SWIGLU_LIMIT, SWIGLU_LIMIT)
        act = (u + 1.0) * (g * jax.nn.sigmoid(SWIGLU_ALPHA * g))
        y = _dot(act.astype(BF16), wd_s[...]) + bd_ref[...]
        ys_ref[0:rows] = y.reshape((rows,) + ROW_TILE)

    @pl.when(nh == 2)
    def _():
        ffn(MOE_BLK)

    @pl.when(nh == 1)
    def _():
        ffn(MOE_HALF)
        ys_ref[MOE_HALF:] = jnp.zeros((MOE_BLK - MOE_HALF,) + ROW_TILE, F32)

    @pl.when(nh == 0)
    def _():
        ys_ref[...] = jnp.zeros_like(ys_ref)


def _experts(layer, block_e, block_nh, last_used, xs, w_gu, b_gu, w_down, b_down):
    def xs_map(b, be, nh, lu):
        return (jnp.minimum(b, lu[0]), 0, 0)

    def e_map(b, be, nh, lu):
        return (layer, be[b], 0, 0)

    return pl.pallas_call(
        _expert_kernel,
        grid_spec=pltpu.PrefetchScalarGridSpec(
            num_scalar_prefetch=3,
            grid=(N_BLOCKS,),
            in_specs=[
                pl.BlockSpec((MOE_BLK,) + ROW_TILE, xs_map),
                pl.BlockSpec((None, None, D_MODEL, 2 * D_EXPERT), e_map),
                pl.BlockSpec((None, None, 1, 2 * D_EXPERT), e_map),
                pl.BlockSpec((None, None, D_EXPERT, D_MODEL), e_map),
                pl.BlockSpec((None, None, 1, D_MODEL), e_map),
            ],
            out_specs=pl.BlockSpec((MOE_BLK,) + ROW_TILE, lambda b, be, nh, lu: (b, 0, 0)),
            scratch_shapes=[pltpu.VMEM((D_MODEL, 2 * D_EXPERT), BF16), pltpu.VMEM((D_EXPERT, D_MODEL), BF16)],
        ),
        out_shape=jax.ShapeDtypeStruct((CAP,) + ROW_TILE, F32),
        compiler_params=_params("arbitrary", vmem=VMEM_LIMIT_BYTES),
        name="moe_experts",
    )(block_e, block_nh, last_used, xs, w_gu, b_gu.reshape(DEPTH, N_EXPERTS, 1, -1), w_down,
      b_down.reshape(DEPTH, N_EXPERTS, 1, -1))


def _combine_kernel(dest_ref, x_ref, g_ref, gate_ref, ys_ref, o_ref, buf_ref, sem_ref):
    def row_copy(t, k):
        d = dest_ref[(pl.program_id(0) * TM + t) * TOP_K + k]
        return pltpu.make_async_copy(ys_ref.at[pl.ds(d, 1)], buf_ref.at[k, pl.ds(t, 1)], sem_ref.at[0])

    _start_all_rows(row_copy)
    for k in range(TOP_K):
        pltpu.make_async_copy(ys_ref.at[pl.ds(0, TM)], buf_ref.at[k], sem_ref.at[0]).wait()
    gates = gate_ref[...]
    ff = gates[:, 0:1] * buf_ref[0].reshape(TM, D_MODEL)
    for k in range(1, TOP_K):
        ff = ff + gates[:, k:k + 1] * buf_ref[k].reshape(TM, D_MODEL)
    o_ref[...] = x_ref[...] + g_ref[0] * ff


def _combine(dest_flat, x, gate_vec, gates, ys):
    return pl.pallas_call(
        _combine_kernel,
        grid_spec=pltpu.PrefetchScalarGridSpec(
            num_scalar_prefetch=1,
            grid=(NT,),
            in_specs=[
                pl.BlockSpec((TM, D_MODEL), lambda i, d: (i, 0)),
                pl.BlockSpec((1, 1, D_MODEL), lambda i, d: (i, 0, 0)),
                pl.BlockSpec((TM, TOP_K), lambda i, d: (i, 0)),
                pl.BlockSpec(memory_space=pl.ANY),
            ],
            out_specs=pl.BlockSpec((TM, D_MODEL), lambda i, d: (i, 0)),
            scratch_shapes=[pltpu.VMEM((TOP_K, TM) + ROW_TILE, F32), pltpu.SemaphoreType.DMA((1,))],
        ),
        out_shape=jax.ShapeDtypeStruct((N_TOK, D_MODEL), F32),
        compiler_params=_params("arbitrary", vmem=VMEM_LIMIT_BYTES),
        name="moe_combine",
    )(dest_flat, x, gate_vec, gates, ys)


def _moe(layer, x, g, sc, sh, gate_vec, w_router, b_router, w_gu, b_gu, w_down, b_down):
    h, idx, gates, pos, counts = _router(x, g, sc, sh, w_router, b_router)
    cnt = counts[0, :N_EXPERTS].astype(jnp.int32)
    nhalf = (cnt + MOE_HALF - 1) // MOE_HALF
    nblk = (nhalf + 1) // 2
    e_ids = jnp.arange(N_EXPERTS, dtype=jnp.int32)
    blk_end = jnp.sum(jnp.where(e_ids[None, :] <= e_ids[:, None], nblk[None, :], 0), axis=1)
    blk_start = blk_end - nblk
    last_used = blk_end[-1] - 1

    def per_expert(table, e):
        return jnp.sum(jnp.where(e[..., None] == e_ids, table, 0), axis=-1)

    def expert_of_block(blk):
        return jnp.sum((blk_end <= jnp.minimum(blk, last_used)[:, None]).astype(jnp.int32), axis=1)

    dest = per_expert(blk_start * MOE_BLK, idx) + pos
    b_ids = jnp.arange(N_BLOCKS, dtype=jnp.int32)
    block_e = expert_of_block(b_ids)
    block_nh = jnp.where(b_ids <= last_used,
                         jnp.clip(per_expert(nhalf, block_e) - 2 * (b_ids - per_expert(blk_start, block_e)), 0, 2), 0)
    h_ids = jnp.arange(2 * N_BLOCKS, dtype=jnp.int32)
    h_e = expert_of_block(h_ids // 2)
    h_nhalf = per_expert(nhalf, h_e)
    h_local = h_ids - 2 * per_expert(blk_start, h_e)
    holds_rows = (h_ids // 2 <= last_used) & (h_local < h_nhalf)
    clear = jnp.logical_not(holds_rows) | (h_local == h_nhalf - 1)
    dest_flat = dest.reshape(-1).astype(jnp.int32)
    xs = _dispatch(dest_flat, clear.astype(jnp.int32), h)
    ys = _experts(layer, block_e.astype(jnp.int32), block_nh.astype(jnp.int32),
                  last_used.reshape(1).astype(jnp.int32), xs, w_gu, b_gu, w_down, b_down)
    return _combine(dest_flat, x, gate_vec, gates, ys)


def _rope_tables(d_rot, lane0, period):
    n_rows = DEC_SEQ // GRID_W
    rows = np.repeat(np.arange(n_rows), GRID_W).astype(np.float32)
    cols = np.tile(np.arange(GRID_W), n_rows).astype(np.float32)
    half = d_rot // 2
    lane = np.arange(LANES)
    i = (lane - lane0) % period
    active = (lane >= lane0) & (i < d_rot)
    w = i % half
    f = w % (half // 2)
    pos = np.where((i // half)[None, :] == 0, rows[:, None], cols[:, None])
    sign = np.where(w < half // 2, -1.0, 1.0).astype(np.float32)
    inv = ROPE_THETA ** (-jnp.arange(0, half, 2, dtype=F32) / half)
    ang = jnp.asarray(pos) * inv[f][None, :]
    cos = jnp.where(active[None, :], jnp.cos(ang), 1.0)
    sin = jnp.where(active[None, :], jnp.sin(ang) * sign[None, :], 0.0)
    return cos, sin


def _rope_spec():
    def index(i):
        return (jnp.maximum(i - NT_PROMPT, 0) % TILES_PER_SAMPLE, 0)
    return pl.BlockSpec((TM, LANES), index)


def _pad_heads(w, n_heads, width):
    lead = w.shape[:-1]
    w = w.reshape(lead + (n_heads, width))
    w = jnp.pad(w, [(0, 0)] * len(lead) + [(0, 0), (0, HEAD_PAD - width)])
    return w.reshape(lead + (n_heads * HEAD_PAD,))


_Q_ORDER = (0, 4, 1, 5, 2, 6, 3, 7, 8, 12, 9, 13, 10, 14, 11, 15)


def _perm_q_heads(w, axis):
    shape = w.shape
    n = shape[axis]
    w = jnp.moveaxis(w, axis, 0).reshape((GQA_HEADS, n // GQA_HEADS) + tuple(s for a, s in enumerate(shape) if a != axis))
    w = w[jnp.array(_Q_ORDER)]
    w = w.reshape((n,) + w.shape[2:])
    return jnp.moveaxis(w, 0, axis)


def _constants():
    mla_cos, mla_sin = _rope_tables(MLA_ROPE, MLA_NOPE, LANES)
    gqa_cos, gqa_sin = _rope_tables(GQA_HEAD_DIM, 0, GQA_HEAD_DIM)
    dft = {}
    for L in (SEQ, DEC_SEQ):
        cm, sm = _dft_tables(L)
        fmat = jnp.concatenate([cm, sm], axis=0).astype(BF16)
        ftmat = jnp.concatenate(_dft_tables(L, transpose=True), axis=1).astype(BF16)
        dft[L] = (cm, sm, fmat, ftmat)
    return dict(mla=(mla_cos, mla_sin), gqa=(gqa_cos, gqa_sin), dft=dft)


def _even_mixer(x, p, i, gmix, sc1, sh1, g1, consts):
    proj = _normlin(x, gmix, sc1, sh1, p["w_in_ab"], i, IN_AB_PAD)
    y_hy = []
    for L, blk0, nseq, per_step in ((SEQ, 0, BATCH, HY_PROMPT_SEQS), (DEC_SEQ, N_PROMPT // DEC_SEQ, DEC_BATCH, 1)):
        cm, sm, fmat, ftmat = consts["dft"][L]
        kc, ks = _hy_filter(L, cm, sm, p["hy_filter_w1"][i], p["hy_filter_b1"][i], p["hy_filter_freq"][i],
                            p["hy_filter_w2"][i], p["hy_filter_b2"][i], p["hy_filter_w3"][i],
                            p["hy_filter_b3"][i], p["hy_log_decay"][i])
        y_hy.append(_hy_mix(proj, blk0, nseq, per_step, L, p["hy_conv_w"][i], p["hy_conv_b"][i], kc, ks,
                            p["hy_bias"][i], fmat, ftmat))

    wq = _pad_heads(p["mla_wq_b"][i], MLA_HEADS, MLA_QK).astype(BF16)
    wkv = p["mla_wkv_b"][i].astype(BF16)
    qg = jnp.pad(p["mla_q_norm"][i], (0, HEAD_PAD - MLA_QK)).reshape(1, HEAD_PAD)
    kg = jnp.pad(p["mla_k_norm"][i], (0, HEAD_PAD - MLA_QK)).reshape(1, HEAD_PAD)
    mla_cos, mla_sin = consts["mla"]
    q, k, v, lat = _mla_prep(proj, p["mla_q_lora_norm"][i].reshape(1, -1), wq, qg,
                             p["mla_kv_lora_norm"][i].reshape(1, -1), wkv, kg, mla_cos, mla_sin)
    ctx = jnp.pad(p["cache_mla_latent"][:, i].reshape(DEC_BATCH * PAST_LEN, -1),
                  ((0, 0), (0, KV_PAD - MLA_KV_LORA - MLA_ROPE)))
    kc_ctx, vc_ctx = _mla_ctx(ctx, wkv, kg)
    nqk = MLA_HEADS * HEAD_PAD
    o = _attention(_mla_attn_kernel, q, k, v, kc_ctx, vc_ctx, nqk, nqk, nqk, MLA_HEADS * MLA_V)
    w_out = p["w_out_ab"]
    x = _outproj([y_hy, o], [(w_out, (i, 0, 0), HY_D), (w_out, (i, 1, 0), HY_D)], x, g1)
    return x, lat, dict(y_hy=jnp.concatenate(y_hy, axis=0), o=jnp.concatenate(o, axis=0))


def _odd_mixer(x, p, i, gmix, sc1, sh1, g1, consts):
    nq = GQA_HEADS * GQA_HEAD_DIM
    nk = GQA_KV_HEADS * GQA_HEAD_DIM
    w = p["w_qkv_c"][i]
    w_qkv = jnp.concatenate([_perm_q_heads(w[:, :nq], 1), w[:, nq:]], axis=1).astype(BF16)
    qg = jnp.tile(p["gqa_q_norm"][i], LANES // GQA_HEAD_DIM).reshape(1, LANES)
    kg = jnp.tile(p["gqa_k_norm"][i], LANES // GQA_HEAD_DIM).reshape(1, LANES)
    gqa_cos, gqa_sin = consts["gqa"]
    q, k, v, k_plain, v_plain = _gqa_prep(x, gmix, sc1, sh1, w_qkv, qg, kg, gqa_cos, gqa_sin)
    kc_ctx = p["cache_gqa_k"][:, i].reshape(DEC_BATCH * PAST_LEN, -1).astype(BF16)
    vc_ctx = p["cache_gqa_v"][:, i].reshape(DEC_BATCH * PAST_LEN, -1).astype(BF16)
    o = _attention(_gqa_attn_kernel, q, k, v, kc_ctx, vc_ctx, nq, nk, nk, nq)
    w_out = _perm_q_heads(p["w_out_c"][i], 0)[None]
    x = _outproj([o], [(w_out, (0, 0, 0), nq)], x, g1)
    return x, k_plain, v_plain


def kernel(x_prompt, x_sample, cache_mla_latent, cache_gqa_k, cache_gqa_v, c, c_ctx, w_ada, b_ada, norm_mix, norm_ffn, w_in_ab, hy_conv_w, hy_conv_b, hy_filter_w1, hy_filter_b1, hy_filter_freq, hy_filter_w2, hy_filter_b2, hy_filter_w3, hy_filter_b3, hy_log_decay, hy_bias, mla_q_lora_norm, mla_wq_b, mla_kv_lora_norm, mla_wkv_b, mla_q_norm, mla_k_norm, w_out_ab, w_qkv_c, gqa_q_norm, gqa_k_norm, w_out_c, moe_router_w, moe_router_b, moe_w_gate_up, moe_b_gate_up, moe_w_down, moe_b_down):
    p = dict(locals())
    x = jnp.concatenate([x_prompt.reshape(N_PROMPT, D_MODEL), x_sample.reshape(N_SAMPLE, D_MODEL)], axis=0)

    cond = jnp.concatenate([c_ctx[None, :], c, jnp.zeros((COND_ROWS - N_COND, D_MODEL), F32)], axis=0)
    mods = _modulation(cond, w_ada, b_ada)
    tile_cond = jnp.concatenate([jnp.zeros((NT_PROMPT,), jnp.int32),
                                 1 + jnp.arange(NT - NT_PROMPT, dtype=jnp.int32) // TILES_PER_SAMPLE])
    mods = mods[:, tile_cond].reshape(DEPTH, NT, 6, 1, D_MODEL).transpose(0, 2, 1, 3, 4)

    consts = _constants()

    lat_out, k_out, v_out = [], [], []
    for l in range(DEPTH):
        sh1, sc1, g1, sh2, sc2, g2 = (mods[l, j] for j in range(6))
        i = l // 2
        gmix = norm_mix[l].reshape(1, D_MODEL)
        if l % 2 == 0:
            x, lat, _ = _even_mixer(x, p, i, gmix, sc1, sh1, g1, consts)
            lat_out.append(lat[:N_PROMPT].reshape(BATCH, SEQ, -1))
        else:
            x, k_plain, v_plain = _odd_mixer(x, p, i, gmix, sc1, sh1, g1, consts)
            k_out.append(k_plain[:N_PROMPT].reshape(BATCH, SEQ, GQA_KV_HEADS, GQA_HEAD_DIM))
            v_out.append(v_plain[:N_PROMPT].reshape(BATCH, SEQ, GQA_KV_HEADS, GQA_HEAD_DIM))
        x = _moe(l, x, norm_ffn[l].reshape(1, D_MODEL), sc2, sh2, g2, moe_router_w[l], moe_router_b[l],
                 moe_w_gate_up, moe_b_gate_up, moe_w_down, moe_b_down)

    y_prompt = x[:N_PROMPT].reshape(BATCH, SEQ, D_MODEL)
    y_sample = x[N_PROMPT:].reshape(DEC_BATCH, DEC_SEQ, D_MODEL)
    return (y_prompt, y_sample, jnp.stack(lat_out, axis=1), jnp.stack(k_out, axis=1), jnp.stack(v_out, axis=1))
```

```python
import functools
import math

import jax
import jax.numpy as jnp
import numpy as np
from jax import lax
from jax.experimental import pallas as pl
from jax.experimental.pallas import tpu as pltpu

F32 = jnp.float32
BF16 = jnp.bfloat16

D_MODEL = 1024
BATCH = 32
SEQ = 256
DEPTH = 4
DEC_BATCH = 4
DEC_SEQ = 1024
PAST_LEN = 256
GRID_W = 64
N_EVEN = (DEPTH + 1) // 2
N_ODD = DEPTH // 2
HY_D = D_MODEL // 2
HY_ORDER = 2
HY_BANDS = 16
HY_EMB = 2 * HY_BANDS + 1
HY_FILTER_HIDDEN = 64
MLA_HEADS = 8
MLA_NOPE = 64
MLA_ROPE = 32
MLA_QK = MLA_NOPE + MLA_ROPE
MLA_V = HY_D // MLA_HEADS
MLA_Q_LORA = 3 * D_MODEL // 8
MLA_KV_LORA = D_MODEL // 4
GQA_HEADS = 16
GQA_KV_HEADS = 4
GQA_HEAD_DIM = D_MODEL // GQA_HEADS
N_EXPERTS = 32
TOP_K = 4
D_EXPERT = D_MODEL
SWIGLU_LIMIT = 7.0
SWIGLU_ALPHA = 1.702
ROPE_THETA = 10000.0
RMS_EPS = 1e-6
GQA_SCALE = GQA_HEAD_DIM ** -0.5
assert math.frexp(GQA_SCALE)[0] == 0.5
IN_AB = (HY_ORDER + 1) * HY_D + MLA_Q_LORA + MLA_KV_LORA + MLA_ROPE

N_PROMPT = BATCH * SEQ
N_SAMPLE = DEC_BATCH * DEC_SEQ
N_TOK = N_PROMPT + N_SAMPLE

LANES = 128
SUBLANES = 8
VMEM_LIMIT_BYTES = 56 * 1024 * 1024

TM = 256
NT = N_TOK // TM
NT_PROMPT = N_PROMPT // TM
TILES_PER_SAMPLE = DEC_SEQ // TM
N_COND = 1 + DEC_BATCH
COND_ROWS = 8
HEAD_PAD = LANES
IN_AB_PAD = 2304
KV_PAD = 384
MOE_HALF = 256
MOE_BLK = 2 * MOE_HALF
N_SLOTS = N_TOK * TOP_K
N_BLOCKS = N_SLOTS // MOE_BLK + N_EXPERTS
CAP = N_BLOCKS * MOE_BLK
ROW_TILE = (SUBLANES, LANES)
assert SUBLANES * LANES == D_MODEL
HY_CH = 256
HY_PROMPT_SEQS = 8


def _dot(a, b):
    return jnp.dot(a, b, preferred_element_type=F32)


def _dot_nt(a, b):
    return lax.dot_general(a, b, (((1,), (1,)), ((), ())), preferred_element_type=F32)


def _split(x):
    hi = x.astype(BF16)
    lo = (x - hi.astype(F32)).astype(BF16)
    return hi, lo


def _dot3(a, b):
    ah, al = _split(a)
    bh, bl = _split(b)
    return _dot(ah, bh) + (_dot(ah, bl) + _dot(al, bh))


def _lane_iota(shape):
    return lax.broadcasted_iota(jnp.int32, shape, len(shape) - 1)


def _params(*sem, vmem=None):
    return pltpu.CompilerParams(dimension_semantics=sem, vmem_limit_bytes=vmem)


def _mod_kernel(c_ref, w_ref, b_ref, o_ref):
    c = c_ref[...]
    s = c * jax.nn.sigmoid(c)
    o_ref[0] = _dot(s.astype(BF16), w_ref[0].astype(BF16)) + b_ref[0]


def _modulation(cond, w_ada, b_ada):
    nblk = 6
    return pl.pallas_call(
        _mod_kernel,
        grid=(DEPTH, nblk),
        in_specs=[
            pl.BlockSpec((COND_ROWS, D_MODEL), lambda l, j: (0, 0)),
            pl.BlockSpec((1, D_MODEL, D_MODEL), lambda l, j: (l, 0, j)),
            pl.BlockSpec((1, 1, D_MODEL), lambda l, j: (l, 0, j)),
        ],
        out_specs=pl.BlockSpec((1, COND_ROWS, D_MODEL), lambda l, j: (l, 0, j)),
        out_shape=jax.ShapeDtypeStruct((DEPTH, COND_ROWS, nblk * D_MODEL), F32),
        compiler_params=_params("arbitrary", "arbitrary"),
        name="modulation",
    )(cond, w_ada, b_ada.reshape(DEPTH, 1, nblk * D_MODEL))


def _norm_mod(x, g, sc, sh):
    ms = jnp.mean(x * x, axis=-1, keepdims=True)
    y = x * lax.rsqrt(ms + RMS_EPS)
    return (y * g) * (1.0 + sc) + sh


def _row_spec(width):
    return pl.BlockSpec((TM, width), lambda i: (i, 0))


def _mod_spec(mod):
    _, layer, which = mod
    return pl.BlockSpec((None, None, 1, 1, D_MODEL), lambda i, *_: (layer, which, i, 0, 0))


def _gain_spec(gain):
    _, layer = gain
    return pl.BlockSpec((None, 1, D_MODEL), lambda i, *_: (layer, 0, 0))


def _full_spec(shape):
    nd = len(shape)
    return pl.BlockSpec(shape, lambda i: (0,) * nd)


def _normlin_kernel(x_ref, g_ref, sc_ref, sh_ref, w_ref, o_ref, w_scr):
    nin = w_ref.shape[1]

    @pl.when(pl.program_id(0) == 0)
    def _():
        w_scr[:, :nin] = w_ref[...].astype(BF16)
        w_scr[:, nin:] = jnp.zeros((D_MODEL, w_scr.shape[1] - nin), BF16)

    h = _norm_mod(x_ref[...], g_ref[...], sc_ref[0], sh_ref[0])
    o_ref[...] = _dot(h.astype(BF16), w_scr[...])


def _normlin(x, g, sc, sh, w_all, layer, nout):
    nin = w_all.shape[2]
    return pl.pallas_call(
        _normlin_kernel,
        grid=(NT,),
        in_specs=[_row_spec(D_MODEL), _gain_spec(g), _mod_spec(sc), _mod_spec(sh),
                  pl.BlockSpec((None, D_MODEL, nin), lambda i: (layer, 0, 0))],
        out_specs=_row_spec(nout),
        out_shape=jax.ShapeDtypeStruct((N_TOK, nout), F32),
        scratch_shapes=[pltpu.VMEM((D_MODEL, nout), BF16)],
        compiler_params=_params("arbitrary", vmem=VMEM_LIMIT_BYTES),
        name="normlin",
    )(x, g[0], sc[0], sh[0], w_all)


def _hy_filter_kernel(z_ref, w1_ref, b1_ref, fr_ref, w2_ref, b2_ref, w3_ref, b3_ref, ed_ref,
                      c_ref, s_ref, kc_ref, ks_ref):
    L = z_ref.shape[0]
    z = z_ref[...]
    fr = fr_ref[...]
    hdn = jnp.sin(fr * (_dot3(z, w1_ref[...]) + b1_ref[...]))
    hdn = jnp.sin(fr * (_dot3(hdn, w2_ref[...]) + b2_ref[...]))
    filt = _dot3(hdn, w3_ref[...]) + b3_ref[...]
    t = z[:, 0:1]
    filt = filt * jnp.exp(-t * ed_ref[...])
    row = lax.broadcasted_iota(jnp.int32, (L, HY_D), 0)
    cm = c_ref[...]
    sm = s_ref[...]
    for o in range(HY_ORDER):
        fw = filt[:, (2 * o) * HY_D:(2 * o + 1) * HY_D]
        bw = filt[:, (2 * o + 1) * HY_D:(2 * o + 2) * HY_D]
        den = (jnp.sum(jnp.abs(fw), axis=0, keepdims=True)
               + jnp.sum(jnp.abs(bw), axis=0, keepdims=True)) + 1e-6
        fw = fw / den
        bw = jnp.where(row == 0, 0.0, bw / den)
        kc_ref[o] = _dot3(cm, fw + bw) * (1.0 / L)
        ks_ref[o] = _dot3(sm, fw - bw) * (1.0 / L)


def _dft_tables(L, transpose=False):
    m = jnp.arange(L, dtype=jnp.int32)
    odd = 2 * m + 1
    phase = (m[:, None] * odd[None, :] if transpose else odd[:, None] * m[None, :]) % (4 * L)
    ang = phase.astype(F32) * (2.0 * math.pi / (4 * L))
    return jnp.cos(ang), jnp.sin(ang)


def _filter_features(L):
    p = jnp.arange(L, dtype=F32)
    t = p / max(L - 1, 1)
    bands = jnp.linspace(1e-4, HY_BANDS - 1, HY_BANDS, dtype=F32)
    ang = (2.0 * math.pi / L) * p[:, None] * bands[None, :]
    z = jnp.concatenate([t[:, None], jnp.cos(ang), -jnp.sin(ang)], axis=-1)
    return jnp.pad(z, ((0, 0), (0, LANES - HY_EMB)))


def _hy_filter(L, cmat, smat, w1, b1, fr, w2, b2, w3, b3, log_decay):
    nf = HY_ORDER * 2 * HY_D
    args = (
        _filter_features(L),
        jnp.pad(w1, ((0, LANES - HY_EMB), (0, 0))),
        b1.reshape(1, -1), fr.reshape(1, -1), w2, b2.reshape(1, -1), w3, b3.reshape(1, -1),
        jnp.exp(log_decay.astype(F32)).reshape(1, nf),
        cmat, smat,
    )
    out_sds = jax.ShapeDtypeStruct((HY_ORDER, L, HY_D), F32)
    return pl.pallas_call(
        _hy_filter_kernel,
        grid=(1,),
        in_specs=[_full_spec(a.shape) for a in args],
        out_specs=[_full_spec(out_sds.shape)] * 2,
        out_shape=[out_sds, out_sds],
        compiler_params=_params("arbitrary", vmem=VMEM_LIMIT_BYTES),
        name=f"hy_filter_{L}",
    )(*args)


def _hy_mix_kernel(u0_ref, u1_ref, u2_ref, cw0_ref, cw1_ref, cw2_ref, cb0_ref, cb1_ref, cb2_ref,
                   kc_ref, ks_ref, hb_ref, f_ref, ft_ref, o_ref):
    S, L, _ = u0_ref.shape
    row = lax.broadcasted_iota(jnp.int32, (L, HY_CH), 0)

    def lanes(per_seq):
        return per_seq[0] if S == 1 else jnp.concatenate(per_seq, axis=1)

    def short_conv(u_ref, cw_ref, cb_ref):
        w = cw_ref[0]
        out = []
        for s in range(S):
            u = u_ref[s]
            prev = jnp.where(row == 0, 0.0, pltpu.roll(u, 1, 0))
            nxt = jnp.where(row == L - 1, 0.0, pltpu.roll(u, L - 1, 0))
            out.append((prev * w[0:1] + u * w[1:2]) + nxt * w[2:3] + cb_ref[0])
        return lanes(out)

    z = short_conv(u0_ref, cw0_ref, cb0_ref)
    gates = (short_conv(u1_ref, cw1_ref, cb1_ref), short_conv(u2_ref, cw2_ref, cb2_ref))
    for o in range(HY_ORDER):
        zz = _dot(f_ref[...], z.astype(BF16))
        cz, sz = zz[:L], zz[L:]
        kc, ks = lanes([kc_ref[o]] * S), lanes([ks_ref[o]] * S)
        w1 = cz * kc - sz * ks
        w2 = cz * ks + sz * kc
        ww = jnp.concatenate([w1, w2], axis=0).astype(BF16)
        conv = _dot(ft_ref[...], ww)
        z = gates[o] * (conv + z * lanes([hb_ref[0, o]] * S))
    for s in range(S):
        o_ref[s] = z[:, s * HY_CH:(s + 1) * HY_CH].astype(o_ref.dtype)


def _hy_mix(proj, row_block0, nseq, seqs_per_step, L, cw, cb, kc, ks, hb, fmat, ftmat):
    nch = HY_D // HY_CH
    nparts = HY_ORDER + 1
    S = seqs_per_step
    assert nseq % S == 0 and row_block0 % S == 0
    cw3 = cw.reshape(3, nparts * nch, HY_CH).transpose(1, 0, 2)
    cb3 = cb.reshape(nparts * nch, 1, HY_CH)
    hb3 = hb.reshape(HY_ORDER, nch, 1, HY_CH).transpose(1, 0, 2, 3)
    proj = proj.reshape(N_TOK // L, L, proj.shape[1])

    def u_spec(part):
        return pl.BlockSpec((S, L, HY_CH), lambda s, c: (row_block0 // S + s, 0, part * nch + c))

    def cw_spec(part):
        return pl.BlockSpec((1, 3, HY_CH), lambda s, c: (part * nch + c, 0, 0))

    def cb_spec(part):
        return pl.BlockSpec((1, 1, HY_CH), lambda s, c: (part * nch + c, 0, 0))

    return pl.pallas_call(
        _hy_mix_kernel,
        grid=(nseq // S, nch),
        in_specs=[
            u_spec(0), u_spec(1), u_spec(2),
            cw_spec(0), cw_spec(1), cw_spec(2),
            cb_spec(0), cb_spec(1), cb_spec(2),
            pl.BlockSpec((HY_ORDER, L, HY_CH), lambda s, c: (0, 0, c)),
            pl.BlockSpec((HY_ORDER, L, HY_CH), lambda s, c: (0, 0, c)),
            pl.BlockSpec((1, HY_ORDER, 1, HY_CH), lambda s, c: (c, 0, 0, 0)),
            pl.BlockSpec((2 * L, L), lambda s, c: (0, 0)),
            pl.BlockSpec((L, 2 * L), lambda s, c: (0, 0)),
        ],
        out_specs=pl.BlockSpec((S, L, HY_CH), lambda s, c: (s, 0, c)),
        out_shape=jax.ShapeDtypeStruct((nseq, L, HY_D), BF16),
        compiler_params=_params("arbitrary", "arbitrary", vmem=VMEM_LIMIT_BYTES),
        name=f"hy_mix_{L}",
    )(proj, proj, proj, cw3, cw3, cw3, cb3, cb3, cb3, kc, ks, hb3, fmat, ftmat).reshape(nseq * L, HY_D)


def _head_rms(xh, gain, dim):
    ms = jnp.sum(xh * xh, axis=-1, keepdims=True) * (1.0 / dim)
    return (xh * lax.rsqrt(ms + RMS_EPS)) * gain


def _per_group(context_fn, latent_fn):
    is_context = pl.program_id(0) < NT_PROMPT
    pl.when(is_context)(context_fn)
    pl.when(jnp.logical_not(is_context))(latent_fn)


def _rope(xh, cos, sin, half):
    lane = _lane_iota(xh.shape)
    first = (lane % (2 * half)) < half
    rot = jnp.where(first, pltpu.roll(xh, LANES - half, 1), pltpu.roll(xh, half, 1))
    return xh * cos + rot * sin


def _mla_keys_values(lat_n, kr_blk, wkv_ref, kg_ref, rope):
    kv = _dot(lat_n.astype(BF16), wkv_ref[...])
    kr = pltpu.roll(kr_blk, MLA_NOPE, 1)
    nope = _lane_iota(kr.shape) < MLA_NOPE
    ks = []
    for h in range(MLA_HEADS):
        kh = jnp.where(nope, kv[:, h * HEAD_PAD:(h + 1) * HEAD_PAD], 0.0) + kr
        kh = _head_rms(kh, kg_ref[...], MLA_QK)
        if rope is not None:
            kh = _rope(kh, rope[0], rope[1], MLA_ROPE // 4)
        ks.append(kh)
    return ks, kv


def _mla_prep_kernel(qa_ref, kva_ref, gq_ref, wq_ref, qg_ref, gkv_ref, wkv_ref, kg_ref,
                     cos_ref, sin_ref, q_ref, k_ref, v_ref, lat_ref):
    def body(rope):
        qa = qa_ref[...]
        ms = jnp.mean(qa * qa, axis=-1, keepdims=True)
        qn = (qa * lax.rsqrt(ms + RMS_EPS)) * gq_ref[...]
        q = _dot(qn.astype(BF16), wq_ref[...])
        for h in range(MLA_HEADS):
            qh = _head_rms(q[:, h * HEAD_PAD:(h + 1) * HEAD_PAD], qg_ref[...], MLA_QK)
            if rope is not None:
                qh = _rope(qh, rope[0], rope[1], MLA_ROPE // 4)
            q_ref[:, h * HEAD_PAD:(h + 1) * HEAD_PAD] = qh.astype(BF16)

        kva = kva_ref[...]
        lat = kva[:, :MLA_KV_LORA]
        ms = jnp.mean(lat * lat, axis=-1, keepdims=True)
        lat_n = (lat * lax.rsqrt(ms + RMS_EPS)) * gkv_ref[...]
        kr_blk = kva[:, MLA_KV_LORA:]
        lat_ref[:, :MLA_KV_LORA] = lat_n
        lat_ref[:, MLA_KV_LORA:] = kr_blk[:, :MLA_ROPE]
        ks, v = _mla_keys_values(lat_n, kr_blk, wkv_ref, kg_ref, rope)
        for h in range(MLA_HEADS):
            k_ref[:, h * HEAD_PAD:(h + 1) * HEAD_PAD] = ks[h].astype(BF16)
        v_ref[...] = v.astype(BF16)

    _per_group(lambda: body(None), lambda: body((cos_ref[...], sin_ref[...])))


def _mla_prep(proj, gq, wq, qg, gkv, wkv, kg, cos_t, sin_t):
    nq = MLA_HEADS * HEAD_PAD
    nv = MLA_HEADS * (MLA_NOPE + MLA_V)
    qa_blk = (HY_ORDER + 1) * HY_D // KV_PAD
    return pl.pallas_call(
        _mla_prep_kernel,
        grid=(NT,),
        in_specs=[
            pl.BlockSpec((TM, KV_PAD), lambda i: (i, qa_blk)),
            pl.BlockSpec((TM, KV_PAD), lambda i: (i, qa_blk + 1)),
            _full_spec(gq.shape), _full_spec(wq.shape), _full_spec(qg.shape), _full_spec(gkv.shape),
            _full_spec(wkv.shape), _full_spec(kg.shape),
            _rope_spec(), _rope_spec(),
        ],
        out_specs=[_row_spec(nq), _row_spec(nq), _row_spec(nv), _row_spec(MLA_KV_LORA + MLA_ROPE)],
        out_shape=[
            jax.ShapeDtypeStruct((N_TOK, nq), BF16),
            jax.ShapeDtypeStruct((N_TOK, nq), BF16),
            jax.ShapeDtypeStruct((N_TOK, nv), BF16),
            jax.ShapeDtypeStruct((N_TOK, MLA_KV_LORA + MLA_ROPE), F32),
        ],
        compiler_params=_params("arbitrary", vmem=VMEM_LIMIT_BYTES),
        name="mla_prep",
    )(proj, proj, gq, wq, qg, gkv, wkv, kg, cos_t, sin_t)


def _mla_ctx_kernel(lat_ref, wkv_ref, kg_ref, k_ref, v_ref):
    lat = lat_ref[...]
    ks, v = _mla_keys_values(lat[:, :MLA_KV_LORA], lat[:, MLA_KV_LORA:], wkv_ref, kg_ref, None)
    for h in range(MLA_HEADS):
        k_ref[:, h * HEAD_PAD:(h + 1) * HEAD_PAD] = ks[h].astype(BF16)
    v_ref[...] = v.astype(BF16)


def _mla_ctx(lat_pad, wkv, kg):
    n = lat_pad.shape[0]
    nq = MLA_HEADS * HEAD_PAD
    nv = MLA_HEADS * (MLA_NOPE + MLA_V)
    return pl.pallas_call(
        _mla_ctx_kernel,
        grid=(n // TM,),
        in_specs=[_row_spec(KV_PAD), _full_spec(wkv.shape), _full_spec(kg.shape)],
        out_specs=[_row_spec(nq), _row_spec(nv)],
        out_shape=[jax.ShapeDtypeStruct((n, nq), BF16), jax.ShapeDtypeStruct((n, nv), BF16)],
        compiler_params=_params("arbitrary"),
        name="mla_ctx",
    )(lat_pad, wkv, kg)


def _softmax_pv(q, keys, vals, scale):
    ss = [_dot_nt(q, k) for k in keys]
    if scale is not None:
        ss = [s * scale for s in ss]
    m = ss[0].max(axis=-1, keepdims=True)
    for s in ss[1:]:
        m = jnp.maximum(m, s.max(axis=-1, keepdims=True))
    ps = [jnp.exp(s - m) for s in ss]
    l = ps[0].sum(axis=-1, keepdims=True)
    for p in ps[1:]:
        l = l + p.sum(axis=-1, keepdims=True)
    o = _dot(ps[0].astype(BF16), vals[0])
    for p, v in zip(ps[1:], vals[1:]):
        o = o + _dot(p.astype(BF16), v)
    return o / l


def _mla_attn_kernel(*refs, has_ctx):
    if has_ctx:
        q_ref, k_ref, v_ref, kc_ref, vc_ref, o_ref = refs
    else:
        q_ref, k_ref, v_ref, o_ref = refs
    scale = MLA_QK ** -0.5
    lo = _lane_iota((q_ref.shape[0], LANES)) < MLA_V
    for j in range(MLA_HEADS // 2):
        outs = []
        for h in (2 * j, 2 * j + 1):
            sl = slice(h * HEAD_PAD, (h + 1) * HEAD_PAD)
            ks, vs = [k_ref[:, sl]], [v_ref[:, sl]]
            if has_ctx:
                ks.append(kc_ref[:, sl])
                vs.append(vc_ref[:, sl])
            outs.append(_softmax_pv(q_ref[:, sl], ks, vs, scale))
        pair = jnp.where(lo, pltpu.roll(outs[0], MLA_V, 1), outs[1])
        o_ref[:, j * LANES:(j + 1) * LANES] = pair.astype(o_ref.dtype)


def _gqa_attn_kernel(*refs, has_ctx):
    if has_ctx:
        q_ref, k_ref, v_ref, kc_ref, vc_ref, o_ref = refs
    else:
        q_ref, k_ref, v_ref, o_ref = refs
    scale = None
    lo = _lane_iota((q_ref.shape[0], LANES)) < GQA_HEAD_DIM
    pairs_per_kv = (GQA_HEADS // 2) // (GQA_KV_HEADS // 2)
    for p in range(GQA_HEADS // 2):
        kv = slice((p // pairs_per_kv) * LANES, (p // pairs_per_kv + 1) * LANES)
        ks, vs = [k_ref[:, kv]], [v_ref[:, kv]]
        if has_ctx:
            ks.append(kc_ref[:, kv])
            vs.append(vc_ref[:, kv])
        qp = q_ref[:, p * LANES:(p + 1) * LANES]
        zero = jnp.zeros_like(qp)
        o_lo = _softmax_pv(jnp.where(lo, qp, zero), ks, vs, scale)
        o_hi = _softmax_pv(jnp.where(lo, zero, qp), ks, vs, scale)
        o_ref[:, p * LANES:(p + 1) * LANES] = jnp.where(lo, o_lo, o_hi).astype(o_ref.dtype)


def _attention(body, q, k, v, kc, vc, wq, wk, wv, wo):
    outs = []
    outs.append(pl.pallas_call(
        functools.partial(body, has_ctx=False),
        grid=(BATCH,),
        in_specs=[pl.BlockSpec((SEQ, wq), lambda b: (b, 0)),
                  pl.BlockSpec((SEQ, wk), lambda b: (b, 0)),
                  pl.BlockSpec((SEQ, wv), lambda b: (b, 0))],
        out_specs=pl.BlockSpec((SEQ, wo), lambda b: (b, 0)),
        out_shape=jax.ShapeDtypeStruct((N_PROMPT, wo), BF16),
        compiler_params=_params("arbitrary", vmem=VMEM_LIMIT_BYTES),
        name="attn_prompt",
    )(q, k, v))
    qt = DEC_SEQ // TM
    q0 = N_PROMPT // TM
    s0 = N_PROMPT // DEC_SEQ
    outs.append(pl.pallas_call(
        functools.partial(body, has_ctx=True),
        grid=(DEC_BATCH, qt),
        in_specs=[pl.BlockSpec((TM, wq), lambda b, t: (q0 + b * qt + t, 0)),
                  pl.BlockSpec((DEC_SEQ, wk), lambda b, t: (s0 + b, 0)),
                  pl.BlockSpec((DEC_SEQ, wv), lambda b, t: (s0 + b, 0)),
                  pl.BlockSpec((PAST_LEN, wk), lambda b, t: (b, 0)),
                  pl.BlockSpec((PAST_LEN, wv), lambda b, t: (b, 0))],
        out_specs=pl.BlockSpec((TM, wo), lambda b, t: (b * qt + t, 0)),
        out_shape=jax.ShapeDtypeStruct((N_SAMPLE, wo), BF16),
        compiler_params=_params("arbitrary", "arbitrary", vmem=VMEM_LIMIT_BYTES),
        name="attn_sample",
    )(q, k, v, kc, vc))
    return outs


def _gqa_prep_kernel(x_ref, g_ref, sc_ref, sh_ref, w_ref, qg_ref, kg_ref, cos_ref, sin_ref,
                     q_ref, k_ref, v_ref, kp_ref, vp_ref):
    lo = _lane_iota((TM, LANES)) < GQA_HEAD_DIM
    nq = GQA_HEADS * GQA_HEAD_DIM
    nk = GQA_KV_HEADS * GQA_HEAD_DIM

    def pair_norm(xp, gain):
        sq = xp * xp
        ms_lo = jnp.sum(jnp.where(lo, sq, 0.0), axis=-1, keepdims=True)
        ms_hi = jnp.sum(jnp.where(lo, 0.0, sq), axis=-1, keepdims=True)
        ms = jnp.where(lo, ms_lo, ms_hi) * (1.0 / GQA_HEAD_DIM)
        return (xp * lax.rsqrt(ms + RMS_EPS)) * gain

    def body(rope):
        def rotate(xp):
            return xp if rope is None else _rope(xp, rope[0], rope[1], GQA_HEAD_DIM // 4)

        h = _norm_mod(x_ref[...], g_ref[...], sc_ref[0], sh_ref[0])
        qkv = _dot(h.astype(BF16), w_ref[...])
        for p in range(nq // LANES):
            sl = slice(p * LANES, (p + 1) * LANES)
            qn = pair_norm(qkv[:, sl], qg_ref[...])
            q_ref[:, sl] = (rotate(qn) * GQA_SCALE).astype(BF16)
        for p in range(nk // LANES):
            sl = slice(p * LANES, (p + 1) * LANES)
            kn = pair_norm(qkv[:, nq + p * LANES:nq + (p + 1) * LANES], kg_ref[...])
            kp_ref[:, sl] = kn
            k_ref[:, sl] = rotate(kn).astype(BF16)
        v = qkv[:, nq + nk:]
        vp_ref[...] = v
        v_ref[...] = v.astype(BF16)

    _per_group(lambda: body(None), lambda: body((cos_ref[...], sin_ref[...])))


def _gqa_prep(x, g, sc, sh, w, qg, kg, cos_t, sin_t):
    nq = GQA_HEADS * GQA_HEAD_DIM
    nk = GQA_KV_HEADS * GQA_HEAD_DIM
    return pl.pallas_call(
        _gqa_prep_kernel,
        grid=(NT,),
        in_specs=[_row_spec(D_MODEL), _gain_spec(g), _mod_spec(sc), _mod_spec(sh),
                  _full_spec(w.shape), _full_spec(qg.shape), _full_spec(kg.shape),
                  _rope_spec(), _rope_spec()],
        out_specs=[_row_spec(nq), _row_spec(nk), _row_spec(nk), _row_spec(nk), _row_spec(nk)],
        out_shape=[
            jax.ShapeDtypeStruct((N_TOK, nq), BF16),
            jax.ShapeDtypeStruct((N_TOK, nk), BF16),
            jax.ShapeDtypeStruct((N_TOK, nk), BF16),
            jax.ShapeDtypeStruct((N_TOK, nk), F32),
            jax.ShapeDtypeStruct((N_TOK, nk), F32),
        ],
        compiler_params=_params("arbitrary", vmem=VMEM_LIMIT_BYTES),
        name="gqa_prep",
    )(x, g[0], sc[0], sh[0], w, qg, kg, cos_t, sin_t)


def _outproj_kernel(*refs, n_in):
    ap_refs = refs[:n_in]
    as_refs = refs[n_in:2 * n_in]
    w_refs = refs[2 * n_in:3 * n_in]
    x_ref, g_ref, o_ref = refs[3 * n_in:3 * n_in + 3]
    w_scr = refs[3 * n_in + 3:]
    is_prompt = pl.program_id(0) < NT_PROMPT

    @pl.when(pl.program_id(0) == 0)
    def _():
        for w, s in zip(w_refs, w_scr):
            s[...] = w[...].astype(BF16)

    y = None
    for ap, asm, s in zip(ap_refs, as_refs, w_scr):
        a = jnp.where(is_prompt, ap[...], asm[...])
        d = _dot(a, s[...])
        y = d if y is None else y + d
    o_ref[...] = x_ref[...] + g_ref[0] * y


def _outproj(acts, ws, x, gate):
    n_in = len(acts)

    def w_spec(index, rows):
        return pl.BlockSpec((None, rows, D_MODEL), lambda i: index)

    def prompt_spec(width):
        return pl.BlockSpec((TM, width), lambda i: (jnp.minimum(i, NT_PROMPT - 1), 0))

    def sample_spec(width):
        return pl.BlockSpec((TM, width), lambda i: (jnp.maximum(i - NT_PROMPT, 0), 0))

    return pl.pallas_call(
        functools.partial(_outproj_kernel, n_in=n_in),
        grid=(NT,),
        in_specs=([prompt_spec(ap.shape[1]) for ap, _ in acts] + [sample_spec(asm.shape[1]) for _, asm in acts]
                  + [w_spec(index, rows) for _, index, rows in ws] + [_row_spec(D_MODEL), _mod_spec(gate)]),
        out_specs=_row_spec(D_MODEL),
        out_shape=jax.ShapeDtypeStruct((N_TOK, D_MODEL), F32),
        scratch_shapes=[pltpu.VMEM((rows, D_MODEL), BF16) for _, _, rows in ws],
        compiler_params=_params("arbitrary", vmem=VMEM_LIMIT_BYTES),
        name="outproj",
    )(*(ap for ap, _ in acts), *(asm for _, asm in acts), *(w for w, _, _ in ws), x, gate[0])


def _router_kernel(x_ref, g_ref, sc_ref, sh_ref, wh_ref, wl_ref, br_ref, tri_ref,
                   h_ref, idx_ref, gate_ref, pos_ref, cnt_ref, run_ref):
    @pl.when(pl.program_id(0) == 0)
    def _():
        run_ref[...] = jnp.zeros_like(run_ref)

    h = _norm_mod(x_ref[...], g_ref[...], sc_ref[0], sh_ref[0])
    h_ref[...] = h.reshape((TM,) + ROW_TILE)
    hh, hl = _split(h)
    logits = _dot(hh, wh_ref[...]) + (_dot(hh, wl_ref[...]) + _dot(hl, wh_ref[...])) + br_ref[...]
    lane = _lane_iota((TM, LANES)).astype(F32)
    neg = jnp.float32(-jnp.inf)
    lg = jnp.where(lane < N_EXPERTS, logits, neg)
    tops, sels, hots = [], [], []
    for _ in range(TOP_K):
        m = lg.max(axis=-1, keepdims=True)
        sel = jnp.where(lg == m, lane, float(LANES)).min(axis=-1, keepdims=True)
        hot = lane == sel
        lg = jnp.where(hot, neg, lg)
        tops.append(m)
        sels.append(sel)
        hots.append(hot)
    es = [jnp.exp(t - tops[0]) for t in tops]
    den = es[0] + es[1] + es[2] + es[3]
    member = jnp.zeros((TM, LANES), F32)
    for hot in hots:
        member = member + hot.astype(F32)
    ranks = _dot(tri_ref[...], member.astype(BF16)) + run_ref[...]
    lane4 = _lane_iota((TM, TOP_K))
    idx4 = jnp.zeros((TM, TOP_K), F32)
    gate4 = jnp.zeros((TM, TOP_K), F32)
    pos4 = jnp.zeros((TM, TOP_K), F32)
    for k in range(TOP_K):
        pk = jnp.sum(jnp.where(hots[k], ranks, 0.0), axis=-1, keepdims=True)
        idx4 = jnp.where(lane4 == k, sels[k], idx4)
        gate4 = jnp.where(lane4 == k, es[k] / den, gate4)
        pos4 = jnp.where(lane4 == k, pk, pos4)
    idx_ref[...] = idx4.astype(jnp.int32)
    gate_ref[...] = gate4
    pos_ref[...] = pos4.astype(jnp.int32)
    run_ref[...] = run_ref[...] + jnp.sum(member, axis=0, keepdims=True)
    cnt_ref[...] = run_ref[...]


def _router_weights(w_router, b_router):
    wpad = jnp.pad(w_router, ((0, 0), (0, 0), (0, LANES - N_EXPERTS)))
    wh = wpad.astype(BF16)
    wl = (wpad - wh.astype(F32)).astype(BF16)
    bpad = jnp.pad(b_router, ((0, 0), (0, LANES - N_EXPERTS))).reshape(DEPTH, 1, LANES)
    return wh, wl, bpad


def _router(layer, x, g, sc, sh, router_weights):
    wh, wl, bpad = router_weights
    r = np.arange(TM)
    tri = jnp.asarray(r[None, :] < r[:, None], dtype=BF16)
    narrow = pl.BlockSpec((TM, TOP_K), lambda i: (i, 0))

    def layer_spec(a):
        return pl.BlockSpec((None,) + a.shape[1:], lambda i: (layer, 0, 0))

    return pl.pallas_call(
        _router_kernel,
        grid=(NT,),
        in_specs=[_row_spec(D_MODEL), _gain_spec(g), _mod_spec(sc), _mod_spec(sh),
                  layer_spec(wh), layer_spec(wl), layer_spec(bpad), _full_spec(tri.shape)],
        out_specs=[pl.BlockSpec((TM,) + ROW_TILE, lambda i: (i, 0, 0)), narrow, narrow, narrow,
                   _full_spec((1, LANES))],
        out_shape=[
            jax.ShapeDtypeStruct((N_TOK,) + ROW_TILE, F32),
            jax.ShapeDtypeStruct((N_TOK, TOP_K), jnp.int32),
            jax.ShapeDtypeStruct((N_TOK, TOP_K), F32),
            jax.ShapeDtypeStruct((N_TOK, TOP_K), jnp.int32),
            jax.ShapeDtypeStruct((1, LANES), F32),
        ],
        scratch_shapes=[pltpu.VMEM((1, LANES), F32)],
        compiler_params=_params("arbitrary", vmem=VMEM_LIMIT_BYTES),
        name="router",
    )(x, g[0], sc[0], sh[0], wh, wl, bpad, tri)


ROW_UNROLL = 4
DMA_QUEUES = 2


def _start_all_rows(row_copy):
    def start_rows(j, carry):
        for u in range(ROW_UNROLL):
            for k in range(TOP_K):
                row_copy(j * ROW_UNROLL + u, k).start(priority=k % DMA_QUEUES)
        return carry

    lax.fori_loop(0, TM // ROW_UNROLL, start_rows, 0)


def _dispatch_kernel(dest_ref, clear_ref, h_ref, xs_ref, zero_ref, sem_ref):
    def row_copy(t, k):
        d = dest_ref[(pl.program_id(0) * TM + t) * TOP_K + k]
        return pltpu.make_async_copy(h_ref.at[pl.ds(t, 1)], xs_ref.at[pl.ds(d, 1)], sem_ref.at[0])

    def zero_half(j):
        row0 = pl.multiple_of(j * MOE_HALF, MOE_HALF)
        return pltpu.make_async_copy(zero_ref, xs_ref.at[pl.ds(row0, MOE_HALF)], sem_ref.at[1])

    @pl.when(pl.program_id(0) == 0)
    def _():
        zero_ref[...] = jnp.zeros_like(zero_ref)

        def start(j, carry):
            @pl.when(clear_ref[j] > 0)
            def _():
                zero_half(j).start()
            return carry

        def wait(j, carry):
            @pl.when(clear_ref[j] > 0)
            def _():
                zero_half(j).wait()
            return carry

        lax.fori_loop(0, 2 * N_BLOCKS, start, 0)
        lax.fori_loop(0, 2 * N_BLOCKS, wait, 0)

    _start_all_rows(row_copy)
    for _ in range(TOP_K):
        pltpu.make_async_copy(h_ref, xs_ref.at[pl.ds(0, TM)], sem_ref.at[0]).wait()


def _dispatch(dest_flat, clear, h):
    return pl.pallas_call(
        _dispatch_kernel,
        grid_spec=pltpu.PrefetchScalarGridSpec(
            num_scalar_prefetch=2,
            grid=(NT,),
            in_specs=[pl.BlockSpec((TM,) + ROW_TILE, lambda i, d, c: (i, 0, 0))],
            out_specs=pl.BlockSpec(memory_space=pl.ANY),
            scratch_shapes=[pltpu.VMEM((MOE_HALF,) + ROW_TILE, F32), pltpu.SemaphoreType.DMA((2,))],
        ),
        out_shape=jax.ShapeDtypeStruct((CAP,) + ROW_TILE, F32),
        compiler_params=_params("arbitrary", vmem=VMEM_LIMIT_BYTES),
        name="moe_dispatch",
    )(dest_flat, clear, h)


def _expert_kernel(be_ref, nh_ref, last_ref, xs_ref, wgu_ref, bgu_ref, wd_ref, bd_ref, ys_ref, wgu_s, wd_s):
    del last_ref
    b = pl.program_id(0)
    nh = nh_ref[b]

    @pl.when((nh > 0) & ((b == 0) | (be_ref[b] != be_ref[jnp.maximum(b - 1, 0)])))
    def _():
        wgu_s[...] = wgu_ref[...].astype(BF16)
        wd_s[...] = wd_ref[...].astype(BF16)

    def ffn(rows):
        x = xs_ref[0:rows].reshape(rows, D_MODEL)
        gu = _dot(x.astype(BF16), wgu_s[...]) + bgu_ref[...]
        g = jnp.minimum(gu[:, :D_EXPERT], SWIGLU_LIMIT)
        u = jnp.clip(gu[:, D_EXPERT:], -SWIGLU_LIMIT, SWIGLU_LIMIT)
        act = (u + 1.0) * (g * jax.nn.sigmoid(SWIGLU_ALPHA * g))
        y = _dot(act.astype(BF16), wd_s[...]) + bd_ref[...]
        ys_ref[0:rows] = y.reshape((rows,) + ROW_TILE)

    @pl.when(nh == 2)
    def _():
        ffn(MOE_BLK)

    @pl.when(nh == 1)
    def _():
        ffn(MOE_HALF)
        ys_ref[MOE_HALF:] = jnp.zeros((MOE_BLK - MOE_HALF,) + ROW_TILE, F32)

    @pl.when(nh == 0)
    def _():
        ys_ref[...] = jnp.zeros_like(ys_ref)


def _experts(layer, block_e, block_nh, last_used, xs, w_gu, b_gu, w_down, b_down):
    def xs_map(b, be, nh, lu):
        return (jnp.minimum(b, lu[0]), 0, 0)

    def e_map(b, be, nh, lu):
        return (layer, be[b], 0, 0)

    return pl.pallas_call(
        _expert_kernel,
        grid_spec=pltpu.PrefetchScalarGridSpec(
            num_scalar_prefetch=3,
            grid=(N_BLOCKS,),
            in_specs=[
                pl.BlockSpec((MOE_BLK,) + ROW_TILE, xs_map),
                pl.BlockSpec((None, None, D_MODEL, 2 * D_EXPERT), e_map),
                pl.BlockSpec((None, None, 1, 2 * D_EXPERT), e_map),
                pl.BlockSpec((None, None, D_EXPERT, D_MODEL), e_map),
                pl.BlockSpec((None, None, 1, D_MODEL), e_map),
            ],
            out_specs=pl.BlockSpec((MOE_BLK,) + ROW_TILE, lambda b, be, nh, lu: (b, 0, 0)),
            scratch_shapes=[pltpu.VMEM((D_MODEL, 2 * D_EXPERT), BF16), pltpu.VMEM((D_EXPERT, D_MODEL), BF16)],
        ),
        out_shape=jax.ShapeDtypeStruct((CAP,) + ROW_TILE, F32),
        compiler_params=_params("arbitrary", vmem=VMEM_LIMIT_BYTES),
        name="moe_experts",
    )(block_e, block_nh, last_used, xs, w_gu, b_gu.reshape(DEPTH, N_EXPERTS, 1, -1), w_down,
      b_down.reshape(DEPTH, N_EXPERTS, 1, -1))


def _combine_kernel(dest_ref, x_ref, g_ref, gate_ref, ys_ref, o_ref, buf_ref, sem_ref):
    i = pl.program_id(0)
    cur = lax.rem(i, 2)

    def start_tile(tile, buf):
        def row_copy(t, k):
            d = dest_ref[(tile * TM + t) * TOP_K + k]
            return pltpu.make_async_copy(ys_ref.at[pl.ds(d, 1)], buf_ref.at[buf, k, pl.ds(t, 1)], sem_ref.at[buf])
        _start_all_rows(row_copy)

    @pl.when(i == 0)
    def _():
        start_tile(0, 0)

    @pl.when(i + 1 < NT)
    def _():
        start_tile(i + 1, 1 - cur)

    for k in range(TOP_K):
        pltpu.make_async_copy(ys_ref.at[pl.ds(0, TM)], buf_ref.at[cur, k], sem_ref.at[cur]).wait()
    gates = gate_ref[...]
    ff = gates[:, 0:1] * buf_ref[cur, 0].reshape(TM, D_MODEL)
    for k in range(1, TOP_K):
        ff = ff + gates[:, k:k + 1] * buf_ref[cur, k].reshape(TM, D_MODEL)
    o_ref[...] = x_ref[...] + g_ref[0] * ff


def _combine(dest_flat, x, gate_vec, gates, ys):
    return pl.pallas_call(
        _combine_kernel,
        grid_spec=pltpu.PrefetchScalarGridSpec(
            num_scalar_prefetch=1,
            grid=(NT,),
            in_specs=[
                pl.BlockSpec((TM, D_MODEL), lambda i, d: (i, 0)),
                _mod_spec(gate_vec),
                pl.BlockSpec((TM, TOP_K), lambda i, d: (i, 0)),
                pl.BlockSpec(memory_space=pl.ANY),
            ],
            out_specs=pl.BlockSpec((TM, D_MODEL), lambda i, d: (i, 0)),
            scratch_shapes=[pltpu.VMEM((2, TOP_K, TM) + ROW_TILE, F32), pltpu.SemaphoreType.DMA((2,))],
        ),
        out_shape=jax.ShapeDtypeStruct((N_TOK, D_MODEL), F32),
        compiler_params=_params("arbitrary", vmem=VMEM_LIMIT_BYTES),
        name="moe_combine",
    )(dest_flat, x, gate_vec[0], gates, ys)


def _moe(layer, x, g, sc, sh, gate_vec, router_weights, w_gu, b_gu, w_down, b_down):
    h, idx, gates, pos, counts = _router(layer, x, g, sc, sh, router_weights)
    cnt = counts[0, :N_EXPERTS].astype(jnp.int32)
    nhalf = (cnt + MOE_HALF - 1) // MOE_HALF
    nblk = (nhalf + 1) // 2
    e_ids = jnp.arange(N_EXPERTS, dtype=jnp.int32)
    blk_end = jnp.sum(jnp.where(e_ids[None, :] <= e_ids[:, None], nblk[None, :], 0), axis=1)
    blk_start = blk_end - nblk
    last_used = blk_end[-1] - 1

    def per_expert(table, e):
        return jnp.sum(jnp.where(e[..., None] == e_ids, table, 0), axis=-1)

    def expert_of_block(blk):
        return jnp.sum((blk_end <= jnp.minimum(blk, last_used)[:, None]).astype(jnp.int32), axis=1)

    dest = per_expert(blk_start * MOE_BLK, idx) + pos
    b_ids = jnp.arange(N_BLOCKS, dtype=jnp.int32)
    block_e = expert_of_block(b_ids)
    block_nh = jnp.where(b_ids <= last_used,
                         jnp.clip(per_expert(nhalf, block_e) - 2 * (b_ids - per_expert(blk_start, block_e)), 0, 2), 0)
    h_ids = jnp.arange(2 * N_BLOCKS, dtype=jnp.int32)
    h_e = expert_of_block(h_ids // 2)
    h_nhalf = per_expert(nhalf, h_e)
    h_local = h_ids - 2 * per_expert(blk_start, h_e)
    holds_rows = (h_ids // 2 <= last_used) & (h_local < h_nhalf)
    clear = jnp.logical_not(holds_rows) | (h_local == h_nhalf - 1)
    dest_flat = dest.reshape(-1).astype(jnp.int32)
    xs = _dispatch(dest_flat, clear.astype(jnp.int32), h)
    ys = _experts(layer, block_e.astype(jnp.int32), block_nh.astype(jnp.int32),
                  last_used.reshape(1).astype(jnp.int32), xs, w_gu, b_gu, w_down, b_down)
    return _combine(dest_flat, x, gate_vec, gates, ys)


def _rope_tables(d_rot, lane0, period):
    n_rows = DEC_SEQ // GRID_W
    rows = np.repeat(np.arange(n_rows), GRID_W).astype(np.float32)
    cols = np.tile(np.arange(GRID_W), n_rows).astype(np.float32)
    half = d_rot // 2
    lane = np.arange(LANES)
    i = (lane - lane0) % period
    active = (lane >= lane0) & (i < d_rot)
    w = i % half
    f = w % (half // 2)
    pos = np.where((i // half)[None, :] == 0, rows[:, None], cols[:, None])
    sign = np.where(w < half // 2, -1.0, 1.0).astype(np.float32)
    inv = ROPE_THETA ** (-jnp.arange(0, half, 2, dtype=F32) / half)
    ang = jnp.asarray(pos) * inv[f][None, :]
    cos = jnp.where(active[None, :], jnp.cos(ang), 1.0)
    sin = jnp.where(active[None, :], jnp.sin(ang) * sign[None, :], 0.0)
    return cos, sin


def _rope_spec():
    def index(i):
        return (jnp.maximum(i - NT_PROMPT, 0) % TILES_PER_SAMPLE, 0)
    return pl.BlockSpec((TM, LANES), index)


def _pad_heads(w, n_heads, width):
    lead = w.shape[:-1]
    w = w.reshape(lead + (n_heads, width))
    w = jnp.pad(w, [(0, 0)] * len(lead) + [(0, 0), (0, HEAD_PAD - width)])
    return w.reshape(lead + (n_heads * HEAD_PAD,))


_Q_ORDER = (0, 4, 1, 5, 2, 6, 3, 7, 8, 12, 9, 13, 10, 14, 11, 15)


def _perm_q_heads(w, axis):
    shape = w.shape
    n = shape[axis]
    w = jnp.moveaxis(w, axis, 0).reshape((GQA_HEADS, n // GQA_HEADS) + tuple(s for a, s in enumerate(shape) if a != axis))
    w = w[jnp.array(_Q_ORDER)]
    w = w.reshape((n,) + w.shape[2:])
    return jnp.moveaxis(w, 0, axis)


def _constants():
    mla_cos, mla_sin = _rope_tables(MLA_ROPE, MLA_NOPE, LANES)
    gqa_cos, gqa_sin = _rope_tables(GQA_HEAD_DIM, 0, GQA_HEAD_DIM)
    dft = {}
    for L in (SEQ, DEC_SEQ):
        cm, sm = _dft_tables(L)
        fmat = jnp.concatenate([cm, sm], axis=0).astype(BF16)
        ftmat = jnp.concatenate(_dft_tables(L, transpose=True), axis=1).astype(BF16)
        dft[L] = (cm, sm, fmat, ftmat)
    return dict(mla=(mla_cos, mla_sin), gqa=(gqa_cos, gqa_sin), dft=dft)


def _even_mixer(x, p, i, gmix, sc1, sh1, g1, consts):
    proj = _normlin(x, gmix, sc1, sh1, p["w_in_ab"], i, IN_AB_PAD)
    y_hy = []
    for L, blk0, nseq, per_step in ((SEQ, 0, BATCH, HY_PROMPT_SEQS), (DEC_SEQ, N_PROMPT // DEC_SEQ, DEC_BATCH, 1)):
        cm, sm, fmat, ftmat = consts["dft"][L]
        kc, ks = _hy_filter(L, cm, sm, p["hy_filter_w1"][i], p["hy_filter_b1"][i], p["hy_filter_freq"][i],
                            p["hy_filter_w2"][i], p["hy_filter_b2"][i], p["hy_filter_w3"][i],
                            p["hy_filter_b3"][i], p["hy_log_decay"][i])
        y_hy.append(_hy_mix(proj, blk0, nseq, per_step, L, p["hy_conv_w"][i], p["hy_conv_b"][i], kc, ks,
                            p["hy_bias"][i], fmat, ftmat))

    wq = _pad_heads(p["mla_wq_b"][i], MLA_HEADS, MLA_QK).astype(BF16)
    wkv = p["mla_wkv_b"][i].astype(BF16)
    qg = jnp.pad(p["mla_q_norm"][i], (0, HEAD_PAD - MLA_QK)).reshape(1, HEAD_PAD)
    kg = jnp.pad(p["mla_k_norm"][i], (0, HEAD_PAD - MLA_QK)).reshape(1, HEAD_PAD)
    mla_cos, mla_sin = consts["mla"]
    q, k, v, lat = _mla_prep(proj, p["mla_q_lora_norm"][i].reshape(1, -1), wq, qg,
                             p["mla_kv_lora_norm"][i].reshape(1, -1), wkv, kg, mla_cos, mla_sin)
    ctx = jnp.pad(p["cache_mla_latent"][:, i].reshape(DEC_BATCH * PAST_LEN, -1),
                  ((0, 0), (0, KV_PAD - MLA_KV_LORA - MLA_ROPE)))
    kc_ctx, vc_ctx = _mla_ctx(ctx, wkv, kg)
    nqk = MLA_HEADS * HEAD_PAD
    o = _attention(_mla_attn_kernel, q, k, v, kc_ctx, vc_ctx, nqk, nqk, nqk, MLA_HEADS * MLA_V)
    w_out = p["w_out_ab"]
    x = _outproj([y_hy, o], [(w_out, (i, 0, 0), HY_D), (w_out, (i, 1, 0), HY_D)], x, g1)
    return x, lat, dict(y_hy=jnp.concatenate(y_hy, axis=0), o=jnp.concatenate(o, axis=0))


def _odd_mixer(x, p, i, gmix, sc1, sh1, g1, consts):
    nq = GQA_HEADS * GQA_HEAD_DIM
    nk = GQA_KV_HEADS * GQA_HEAD_DIM
    w = p["w_qkv_c"][i]
    w_qkv = jnp.concatenate([_perm_q_heads(w[:, :nq], 1), w[:, nq:]], axis=1).astype(BF16)
    qg = jnp.tile(p["gqa_q_norm"][i], LANES // GQA_HEAD_DIM).reshape(1, LANES)
    kg = jnp.tile(p["gqa_k_norm"][i], LANES // GQA_HEAD_DIM).reshape(1, LANES)
    gqa_cos, gqa_sin = consts["gqa"]
    q, k, v, k_plain, v_plain = _gqa_prep(x, gmix, sc1, sh1, w_qkv, qg, kg, gqa_cos, gqa_sin)
    kc_ctx = p["cache_gqa_k"][:, i].reshape(DEC_BATCH * PAST_LEN, -1).astype(BF16)
    vc_ctx = p["cache_gqa_v"][:, i].reshape(DEC_BATCH * PAST_LEN, -1).astype(BF16)
    o = _attention(_gqa_attn_kernel, q, k, v, kc_ctx, vc_ctx, nq, nk, nk, nq)
    w_out = _perm_q_heads(p["w_out_c"][i], 0)[None]
    x = _outproj([o], [(w_out, (0, 0, 0), nq)], x, g1)
    return x, k_plain, v_plain


def kernel(x_prompt, x_sample, cache_mla_latent, cache_gqa_k, cache_gqa_v, c, c_ctx, w_ada, b_ada, norm_mix, norm_ffn, w_in_ab, hy_conv_w, hy_conv_b, hy_filter_w1, hy_filter_b1, hy_filter_freq, hy_filter_w2, hy_filter_b2, hy_filter_w3, hy_filter_b3, hy_log_decay, hy_bias, mla_q_lora_norm, mla_wq_b, mla_kv_lora_norm, mla_wkv_b, mla_q_norm, mla_k_norm, w_out_ab, w_qkv_c, gqa_q_norm, gqa_k_norm, w_out_c, moe_router_w, moe_router_b, moe_w_gate_up, moe_b_gate_up, moe_w_down, moe_b_down):
    p = dict(locals())
    x = jnp.concatenate([x_prompt.reshape(N_PROMPT, D_MODEL), x_sample.reshape(N_SAMPLE, D_MODEL)], axis=0)

    cond = jnp.concatenate([c_ctx[None, :], c, jnp.zeros((COND_ROWS - N_COND, D_MODEL), F32)], axis=0)
    mods = _modulation(cond, w_ada, b_ada)
    tile_cond = jnp.concatenate([jnp.zeros((NT_PROMPT,), jnp.int32),
                                 1 + jnp.arange(NT - NT_PROMPT, dtype=jnp.int32) // TILES_PER_SAMPLE])
    mods = mods[:, tile_cond].reshape(DEPTH, NT, 6, 1, D_MODEL).transpose(0, 2, 1, 3, 4)

    consts = _constants()
    router_weights = _router_weights(moe_router_w, moe_router_b)
    gains_mix = norm_mix.reshape(DEPTH, 1, D_MODEL)
    gains_ffn = norm_ffn.reshape(DEPTH, 1, D_MODEL)

    lat_out, k_out, v_out = [], [], []
    for l in range(DEPTH):
        sh1, sc1, g1, sh2, sc2, g2 = ((mods, l, j) for j in range(6))
        i = l // 2
        gmix = (gains_mix, l)
        if l % 2 == 0:
            x, lat, _ = _even_mixer(x, p, i, gmix, sc1, sh1, g1, consts)
            lat_out.append(lat[:N_PROMPT].reshape(BATCH, SEQ, -1))
        else:
            x, k_plain, v_plain = _odd_mixer(x, p, i, gmix, sc1, sh1, g1, consts)
            k_out.append(k_plain[:N_PROMPT].reshape(BATCH, SEQ, GQA_KV_HEADS, GQA_HEAD_DIM))
            v_out.append(v_plain[:N_PROMPT].reshape(BATCH, SEQ, GQA_KV_HEADS, GQA_HEAD_DIM))
        x = _moe(l, x, (gains_ffn, l), sc2, sh2, g2, router_weights,
                 moe_w_gate_up, moe_b_gate_up, moe_w_down, moe_b_down)

    y_prompt = x[:N_PROMPT].reshape(BATCH, SEQ, D_MODEL)
    y_sample = x[N_PROMPT:].reshape(DEC_BATCH, DEC_SEQ, D_MODEL)
    return (y_prompt, y_sample, jnp.stack(lat_out, axis=1), jnp.stack(k_out, axis=1), jnp.stack(v_out, axis=1))
```

```python
import functools
import math

import jax
import jax.numpy as jnp
import numpy as np
from jax import lax
from jax.experimental import pallas as pl
from jax.experimental.pallas import tpu as pltpu

F32 = jnp.float32
BF16 = jnp.bfloat16

D_MODEL = 1024
BATCH = 32
SEQ = 256
DEPTH = 4
DEC_BATCH = 4
DEC_SEQ = 1024
PAST_LEN = 256
GRID_W = 64
N_EVEN = (DEPTH + 1) // 2
N_ODD = DEPTH // 2
HY_D = D_MODEL // 2
HY_ORDER = 2
HY_BANDS = 16
HY_EMB = 2 * HY_BANDS + 1
HY_FILTER_HIDDEN = 64
MLA_HEADS = 8
MLA_NOPE = 64
MLA_ROPE = 32
MLA_QK = MLA_NOPE + MLA_ROPE
MLA_V = HY_D // MLA_HEADS
MLA_Q_LORA = 3 * D_MODEL // 8
MLA_KV_LORA = D_MODEL // 4
GQA_HEADS = 16
GQA_KV_HEADS = 4
GQA_HEAD_DIM = D_MODEL // GQA_HEADS
N_EXPERTS = 32
TOP_K = 4
D_EXPERT = D_MODEL
SWIGLU_LIMIT = 7.0
SWIGLU_ALPHA = 1.702
ROPE_THETA = 10000.0
RMS_EPS = 1e-6
GQA_SCALE = GQA_HEAD_DIM ** -0.5
assert math.frexp(GQA_SCALE)[0] == 0.5
IN_AB = (HY_ORDER + 1) * HY_D + MLA_Q_LORA + MLA_KV_LORA + MLA_ROPE

N_PROMPT = BATCH * SEQ
N_SAMPLE = DEC_BATCH * DEC_SEQ
N_TOK = N_PROMPT + N_SAMPLE

LANES = 128
SUBLANES = 8
VMEM_LIMIT_BYTES = 56 * 1024 * 1024

TM = 256
NT = N_TOK // TM
NT_PROMPT = N_PROMPT // TM
TILES_PER_SAMPLE = DEC_SEQ // TM
N_COND = 1 + DEC_BATCH
COND_ROWS = 8
HEAD_PAD = LANES
IN_AB_PAD = 2304
KV_PAD = 384
MOE_HALF = 256
MOE_BLK = 2 * MOE_HALF
N_SLOTS = N_TOK * TOP_K
N_BLOCKS = N_SLOTS // MOE_BLK + N_EXPERTS
CAP = N_BLOCKS * MOE_BLK
ROW_TILE = (SUBLANES, LANES)
assert SUBLANES * LANES == D_MODEL
HY_CH = 256
HY_PROMPT_SEQS = 8


def _dot(a, b):
    return jnp.dot(a, b, preferred_element_type=F32)


def _dot_nt(a, b):
    return lax.dot_general(a, b, (((1,), (1,)), ((), ())), preferred_element_type=F32)


def _split(x):
    hi = x.astype(BF16)
    lo = (x - hi.astype(F32)).astype(BF16)
    return hi, lo


def _dot3(a, b):
    ah, al = _split(a)
    bh, bl = _split(b)
    return _dot(ah, bh) + (_dot(ah, bl) + _dot(al, bh))


def _lane_iota(shape):
    return lax.broadcasted_iota(jnp.int32, shape, len(shape) - 1)


def _params(*sem, vmem=None):
    return pltpu.CompilerParams(dimension_semantics=sem, vmem_limit_bytes=vmem)


def _mod_kernel(c_ref, w_ref, b_ref, o_ref):
    c = c_ref[...]
    s = c * jax.nn.sigmoid(c)
    o_ref[0] = _dot(s.astype(BF16), w_ref[0].astype(BF16)) + b_ref[0]


def _modulation(cond, w_ada, b_ada):
    nblk = 6
    return pl.pallas_call(
        _mod_kernel,
        grid=(DEPTH, nblk),
        in_specs=[
            pl.BlockSpec((COND_ROWS, D_MODEL), lambda l, j: (0, 0)),
            pl.BlockSpec((1, D_MODEL, D_MODEL), lambda l, j: (l, 0, j)),
            pl.BlockSpec((1, 1, D_MODEL), lambda l, j: (l, 0, j)),
        ],
        out_specs=pl.BlockSpec((1, COND_ROWS, D_MODEL), lambda l, j: (l, 0, j)),
        out_shape=jax.ShapeDtypeStruct((DEPTH, COND_ROWS, nblk * D_MODEL), F32),
        compiler_params=_params("arbitrary", "arbitrary"),
        name="modulation",
    )(cond, w_ada, b_ada.reshape(DEPTH, 1, nblk * D_MODEL))


def _norm_mod(x, g, sc, sh):
    ms = jnp.mean(x * x, axis=-1, keepdims=True)
    y = x * lax.rsqrt(ms + RMS_EPS)
    return (y * g) * (1.0 + sc) + sh


def _row_spec(width):
    return pl.BlockSpec((TM, width), lambda i: (i, 0))


def _mod_spec(mod):
    _, layer, which = mod
    return pl.BlockSpec((None, None, 1, 1, D_MODEL), lambda i, *_: (layer, which, i, 0, 0))


def _gain_spec(gain):
    _, layer = gain
    return pl.BlockSpec((None, 1, D_MODEL), lambda i, *_: (layer, 0, 0))


def _full_spec(shape):
    nd = len(shape)
    return pl.BlockSpec(shape, lambda i: (0,) * nd)


def _normlin_kernel(x_ref, g_ref, sc_ref, sh_ref, w_ref, o_ref, w_scr):
    nin = w_ref.shape[1]

    @pl.when(pl.program_id(0) == 0)
    def _():
        w_scr[:, :nin] = w_ref[...].astype(BF16)
        w_scr[:, nin:] = jnp.zeros((D_MODEL, w_scr.shape[1] - nin), BF16)

    h = _norm_mod(x_ref[...], g_ref[...], sc_ref[0], sh_ref[0])
    o_ref[...] = _dot(h.astype(BF16), w_scr[...])


def _normlin(x, g, sc, sh, w_all, layer, nout):
    nin = w_all.shape[2]
    return pl.pallas_call(
        _normlin_kernel,
        grid=(NT,),
        in_specs=[_row_spec(D_MODEL), _gain_spec(g), _mod_spec(sc), _mod_spec(sh),
                  pl.BlockSpec((None, D_MODEL, nin), lambda i: (layer, 0, 0))],
        out_specs=_row_spec(nout),
        out_shape=jax.ShapeDtypeStruct((N_TOK, nout), F32),
        scratch_shapes=[pltpu.VMEM((D_MODEL, nout), BF16)],
        compiler_params=_params("arbitrary", vmem=VMEM_LIMIT_BYTES),
        name="normlin",
    )(x, g[0], sc[0], sh[0], w_all)


def _hy_filter_kernel(z_ref, w1_ref, b1_ref, fr_ref, w2_ref, b2_ref, w3_ref, b3_ref, ed_ref,
                      c_ref, s_ref, kc_ref, ks_ref):
    L = z_ref.shape[0]
    z = z_ref[...]
    fr = fr_ref[...]
    hdn = jnp.sin(fr * (_dot3(z, w1_ref[...]) + b1_ref[...]))
    hdn = jnp.sin(fr * (_dot3(hdn, w2_ref[...]) + b2_ref[...]))
    filt = _dot3(hdn, w3_ref[...]) + b3_ref[...]
    t = z[:, 0:1]
    filt = filt * jnp.exp(-t * ed_ref[...])
    row = lax.broadcasted_iota(jnp.int32, (L, HY_D), 0)
    cm = c_ref[...]
    sm = s_ref[...]
    for o in range(HY_ORDER):
        fw = filt[:, (2 * o) * HY_D:(2 * o + 1) * HY_D]
        bw = filt[:, (2 * o + 1) * HY_D:(2 * o + 2) * HY_D]
        den = (jnp.sum(jnp.abs(fw), axis=0, keepdims=True)
               + jnp.sum(jnp.abs(bw), axis=0, keepdims=True)) + 1e-6
        fw = fw / den
        bw = jnp.where(row == 0, 0.0, bw / den)
        kc_ref[o] = _dot3(cm, fw + bw) * (1.0 / L)
        ks_ref[o] = _dot3(sm, fw - bw) * (1.0 / L)


def _dft_tables(L):
    m = jnp.arange(L, dtype=jnp.int32)
    phase = ((2 * m + 1)[:, None] * m[None, :]) % (4 * L)
    ang = phase.astype(F32) * (2.0 * math.pi / (4 * L))
    return jnp.cos(ang), jnp.sin(ang)


def _filter_features(L):
    p = jnp.arange(L, dtype=F32)
    t = p / max(L - 1, 1)
    bands = jnp.linspace(1e-4, HY_BANDS - 1, HY_BANDS, dtype=F32)
    ang = (2.0 * math.pi / L) * p[:, None] * bands[None, :]
    z = jnp.concatenate([t[:, None], jnp.cos(ang), -jnp.sin(ang)], axis=-1)
    return jnp.pad(z, ((0, 0), (0, LANES - HY_EMB)))


def _hy_filter(L, cmat, smat, w1, b1, fr, w2, b2, w3, b3, log_decay):
    nf = HY_ORDER * 2 * HY_D
    args = (
        _filter_features(L),
        jnp.pad(w1, ((0, LANES - HY_EMB), (0, 0))),
        b1.reshape(1, -1), fr.reshape(1, -1), w2, b2.reshape(1, -1), w3, b3.reshape(1, -1),
        jnp.exp(log_decay.astype(F32)).reshape(1, nf),
        cmat, smat,
    )
    out_sds = jax.ShapeDtypeStruct((HY_ORDER, L, HY_D), F32)
    return pl.pallas_call(
        _hy_filter_kernel,
        grid=(1,),
        in_specs=[_full_spec(a.shape) for a in args],
        out_specs=[_full_spec(out_sds.shape)] * 2,
        out_shape=[out_sds, out_sds],
        compiler_params=_params("arbitrary", vmem=VMEM_LIMIT_BYTES),
        name=f"hy_filter_{L}",
    )(*args)


def _hy_mix_kernel(u0_ref, u1_ref, u2_ref, cw0_ref, cw1_ref, cw2_ref, cb0_ref, cb1_ref, cb2_ref,
                   kc_ref, ks_ref, hb_ref, f_ref, ft_ref, o_ref):
    S, L, _ = u0_ref.shape
    row = lax.broadcasted_iota(jnp.int32, (L, HY_CH), 0)

    def lanes(per_seq):
        return per_seq[0] if S == 1 else jnp.concatenate(per_seq, axis=1)

    def short_conv(u_ref, cw_ref, cb_ref):
        w = cw_ref[0]
        out = []
        for s in range(S):
            u = u_ref[s]
            prev = jnp.where(row == 0, 0.0, pltpu.roll(u, 1, 0))
            nxt = jnp.where(row == L - 1, 0.0, pltpu.roll(u, L - 1, 0))
            out.append((prev * w[0:1] + u * w[1:2]) + nxt * w[2:3] + cb_ref[0])
        return lanes(out)

    z = short_conv(u0_ref, cw0_ref, cb0_ref)
    gates = (short_conv(u1_ref, cw1_ref, cb1_ref), short_conv(u2_ref, cw2_ref, cb2_ref))
    for o in range(HY_ORDER):
        zz = _dot(f_ref[...], z.astype(BF16))
        cz, sz = zz[:L], zz[L:]
        kc, ks = lanes([kc_ref[o]] * S), lanes([ks_ref[o]] * S)
        w1 = cz * kc - sz * ks
        w2 = cz * ks + sz * kc
        ww = jnp.concatenate([w1, w2], axis=0).astype(BF16)
        conv = _dot(ft_ref[...], ww)
        z = gates[o] * (conv + z * lanes([hb_ref[0, o]] * S))
    for s in range(S):
        o_ref[s] = z[:, s * HY_CH:(s + 1) * HY_CH].astype(o_ref.dtype)


def _hy_mix(proj, row_block0, nseq, seqs_per_step, L, cw, cb, kc, ks, hb, fmat, ftmat):
    nch = HY_D // HY_CH
    nparts = HY_ORDER + 1
    S = seqs_per_step
    assert nseq % S == 0 and row_block0 % S == 0
    cw3 = cw.reshape(3, nparts * nch, HY_CH).transpose(1, 0, 2)
    cb3 = cb.reshape(nparts * nch, 1, HY_CH)
    hb3 = hb.reshape(HY_ORDER, nch, 1, HY_CH).transpose(1, 0, 2, 3)
    proj = proj.reshape(N_TOK // L, L, proj.shape[1])

    def u_spec(part):
        return pl.BlockSpec((S, L, HY_CH), lambda s, c: (row_block0 // S + s, 0, part * nch + c))

    def cw_spec(part):
        return pl.BlockSpec((1, 3, HY_CH), lambda s, c: (part * nch + c, 0, 0))

    def cb_spec(part):
        return pl.BlockSpec((1, 1, HY_CH), lambda s, c: (part * nch + c, 0, 0))

    return pl.pallas_call(
        _hy_mix_kernel,
        grid=(nseq // S, nch),
        in_specs=[
            u_spec(0), u_spec(1), u_spec(2),
            cw_spec(0), cw_spec(1), cw_spec(2),
            cb_spec(0), cb_spec(1), cb_spec(2),
            pl.BlockSpec((HY_ORDER, L, HY_CH), lambda s, c: (0, 0, c)),
            pl.BlockSpec((HY_ORDER, L, HY_CH), lambda s, c: (0, 0, c)),
            pl.BlockSpec((1, HY_ORDER, 1, HY_CH), lambda s, c: (c, 0, 0, 0)),
            pl.BlockSpec((2 * L, L), lambda s, c: (0, 0)),
            pl.BlockSpec((L, 2 * L), lambda s, c: (0, 0)),
        ],
        out_specs=pl.BlockSpec((S, L, HY_CH), lambda s, c: (s, 0, c)),
        out_shape=jax.ShapeDtypeStruct((nseq, L, HY_D), BF16),
        compiler_params=_params("arbitrary", "arbitrary", vmem=VMEM_LIMIT_BYTES),
        name=f"hy_mix_{L}",
    )(proj, proj, proj, cw3, cw3, cw3, cb3, cb3, cb3, kc, ks, hb3, fmat, ftmat).reshape(nseq * L, HY_D)


def _head_rms(xh, gain, dim):
    ms = jnp.sum(xh * xh, axis=-1, keepdims=True) * (1.0 / dim)
    return (xh * lax.rsqrt(ms + RMS_EPS)) * gain


def _context_rows_spec(width):
    return pl.BlockSpec((TM, width), lambda i, *_: (jnp.minimum(i, NT_PROMPT - 1), 0))


def _per_group(context_fn, latent_fn):
    is_context = pl.program_id(0) < NT_PROMPT
    pl.when(is_context)(context_fn)
    pl.when(jnp.logical_not(is_context))(latent_fn)


def _rope(xh, cos, sin, half):
    lane = _lane_iota(xh.shape)
    first = (lane % (2 * half)) < half
    rot = jnp.where(first, pltpu.roll(xh, LANES - half, 1), pltpu.roll(xh, half, 1))
    return xh * cos + rot * sin


def _mla_keys_values(lat_n, kr_blk, wkv_ref, kg_ref, rope):
    kv = _dot(lat_n.astype(BF16), wkv_ref[...])
    kr = pltpu.roll(kr_blk, MLA_NOPE, 1)
    nope = _lane_iota(kr.shape) < MLA_NOPE
    ks = []
    for h in range(MLA_HEADS):
        kh = jnp.where(nope, kv[:, h * HEAD_PAD:(h + 1) * HEAD_PAD], 0.0) + kr
        kh = _head_rms(kh, kg_ref[...], MLA_QK)
        if rope is not None:
            kh = _rope(kh, rope[0], rope[1], MLA_ROPE // 4)
        ks.append(kh)
    return ks, kv


def _mla_prep_kernel(qa_ref, kva_ref, gq_ref, wq_ref, qg_ref, gkv_ref, wkv_ref, kg_ref,
                     cos_ref, sin_ref, q_ref, k_ref, v_ref, lat_ref):
    def body(rope):
        qa = qa_ref[...]
        ms = jnp.mean(qa * qa, axis=-1, keepdims=True)
        qn = (qa * lax.rsqrt(ms + RMS_EPS)) * gq_ref[...]
        q = _dot(qn.astype(BF16), wq_ref[...])
        for h in range(MLA_HEADS):
            qh = _head_rms(q[:, h * HEAD_PAD:(h + 1) * HEAD_PAD], qg_ref[...], MLA_QK)
            if rope is not None:
                qh = _rope(qh, rope[0], rope[1], MLA_ROPE // 4)
            q_ref[:, h * HEAD_PAD:(h + 1) * HEAD_PAD] = qh.astype(BF16)

        kva = kva_ref[...]
        lat = kva[:, :MLA_KV_LORA]
        ms = jnp.mean(lat * lat, axis=-1, keepdims=True)
        lat_n = (lat * lax.rsqrt(ms + RMS_EPS)) * gkv_ref[...]
        kr_blk = kva[:, MLA_KV_LORA:]
        if rope is None:
            lat_ref[:, :MLA_KV_LORA] = lat_n
            lat_ref[:, MLA_KV_LORA:] = kr_blk[:, :MLA_ROPE]
        ks, v = _mla_keys_values(lat_n, kr_blk, wkv_ref, kg_ref, rope)
        for h in range(MLA_HEADS):
            k_ref[:, h * HEAD_PAD:(h + 1) * HEAD_PAD] = ks[h].astype(BF16)
        v_ref[...] = v.astype(BF16)

    _per_group(lambda: body(None), lambda: body((cos_ref[...], sin_ref[...])))


def _mla_prep(proj, gq, wq, qg, gkv, wkv, kg, cos_t, sin_t):
    nq = MLA_HEADS * HEAD_PAD
    nv = MLA_HEADS * (MLA_NOPE + MLA_V)
    qa_blk = (HY_ORDER + 1) * HY_D // KV_PAD
    return pl.pallas_call(
        _mla_prep_kernel,
        grid=(NT,),
        in_specs=[
            pl.BlockSpec((TM, KV_PAD), lambda i: (i, qa_blk)),
            pl.BlockSpec((TM, KV_PAD), lambda i: (i, qa_blk + 1)),
            _full_spec(gq.shape), _full_spec(wq.shape), _full_spec(qg.shape), _full_spec(gkv.shape),
            _full_spec(wkv.shape), _full_spec(kg.shape),
            _rope_spec(), _rope_spec(),
        ],
        out_specs=[_row_spec(nq), _row_spec(nq), _row_spec(nv), _context_rows_spec(MLA_KV_LORA + MLA_ROPE)],
        out_shape=[
            jax.ShapeDtypeStruct((N_TOK, nq), BF16),
            jax.ShapeDtypeStruct((N_TOK, nq), BF16),
            jax.ShapeDtypeStruct((N_TOK, nv), BF16),
            jax.ShapeDtypeStruct((N_PROMPT, MLA_KV_LORA + MLA_ROPE), F32),
        ],
        compiler_params=_params("arbitrary", vmem=VMEM_LIMIT_BYTES),
        name="mla_prep",
    )(proj, proj, gq, wq, qg, gkv, wkv, kg, cos_t, sin_t)


def _mla_ctx_kernel(lat_ref, wkv_ref, kg_ref, k_ref, v_ref):
    lat = lat_ref[...]
    ks, v = _mla_keys_values(lat[:, :MLA_KV_LORA], lat[:, MLA_KV_LORA:], wkv_ref, kg_ref, None)
    for h in range(MLA_HEADS):
        k_ref[:, h * HEAD_PAD:(h + 1) * HEAD_PAD] = ks[h].astype(BF16)
    v_ref[...] = v.astype(BF16)


def _mla_ctx(lat_pad, wkv, kg):
    n = lat_pad.shape[0]
    nq = MLA_HEADS * HEAD_PAD
    nv = MLA_HEADS * (MLA_NOPE + MLA_V)
    return pl.pallas_call(
        _mla_ctx_kernel,
        grid=(n // TM,),
        in_specs=[_row_spec(KV_PAD), _full_spec(wkv.shape), _full_spec(kg.shape)],
        out_specs=[_row_spec(nq), _row_spec(nv)],
        out_shape=[jax.ShapeDtypeStruct((n, nq), BF16), jax.ShapeDtypeStruct((n, nv), BF16)],
        compiler_params=_params("arbitrary"),
        name="mla_ctx",
    )(lat_pad, wkv, kg)


def _softmax_pv(q, keys, vals, scale):
    ss = [_dot_nt(q, k) for k in keys]
    if scale is not None:
        ss = [s * scale for s in ss]
    m = ss[0].max(axis=-1, keepdims=True)
    for s in ss[1:]:
        m = jnp.maximum(m, s.max(axis=-1, keepdims=True))
    ps = [jnp.exp(s - m) for s in ss]
    l = ps[0].sum(axis=-1, keepdims=True)
    for p in ps[1:]:
        l = l + p.sum(axis=-1, keepdims=True)
    o = _dot(ps[0].astype(BF16), vals[0])
    for p, v in zip(ps[1:], vals[1:]):
        o = o + _dot(p.astype(BF16), v)
    return o / l


def _mla_attn_kernel(*refs, has_ctx):
    if has_ctx:
        q_ref, k_ref, v_ref, kc_ref, vc_ref, o_ref = refs
    else:
        q_ref, k_ref, v_ref, o_ref = refs
    scale = MLA_QK ** -0.5
    lo = _lane_iota((q_ref.shape[0], LANES)) < MLA_V
    for j in range(MLA_HEADS // 2):
        outs = []
        for h in (2 * j, 2 * j + 1):
            sl = slice(h * HEAD_PAD, (h + 1) * HEAD_PAD)
            ks, vs = [k_ref[:, sl]], [v_ref[:, sl]]
            if has_ctx:
                ks.append(kc_ref[:, sl])
                vs.append(vc_ref[:, sl])
            outs.append(_softmax_pv(q_ref[:, sl], ks, vs, scale))
        pair = jnp.where(lo, pltpu.roll(outs[0], MLA_V, 1), outs[1])
        o_ref[:, j * LANES:(j + 1) * LANES] = pair.astype(o_ref.dtype)


def _gqa_attn_kernel(*refs, has_ctx):
    if has_ctx:
        q_ref, k_ref, v_ref, kc_ref, vc_ref, o_ref = refs
    else:
        q_ref, k_ref, v_ref, o_ref = refs
    scale = None
    lo = _lane_iota((q_ref.shape[0], LANES)) < GQA_HEAD_DIM
    pairs_per_kv = (GQA_HEADS // 2) // (GQA_KV_HEADS // 2)
    for p in range(GQA_HEADS // 2):
        kv = slice((p // pairs_per_kv) * LANES, (p // pairs_per_kv + 1) * LANES)
        ks, vs = [k_ref[:, kv]], [v_ref[:, kv]]
        if has_ctx:
            ks.append(kc_ref[:, kv])
            vs.append(vc_ref[:, kv])
        qp = q_ref[:, p * LANES:(p + 1) * LANES]
        zero = jnp.zeros_like(qp)
        o_lo = _softmax_pv(jnp.where(lo, qp, zero), ks, vs, scale)
        o_hi = _softmax_pv(jnp.where(lo, zero, qp), ks, vs, scale)
        o_ref[:, p * LANES:(p + 1) * LANES] = jnp.where(lo, o_lo, o_hi).astype(o_ref.dtype)


def _attention(body, q, k, v, kc, vc, wq, wk, wv, wo):
    outs = []
    outs.append(pl.pallas_call(
        functools.partial(body, has_ctx=False),
        grid=(BATCH,),
        in_specs=[pl.BlockSpec((SEQ, wq), lambda b: (b, 0)),
                  pl.BlockSpec((SEQ, wk), lambda b: (b, 0)),
                  pl.BlockSpec((SEQ, wv), lambda b: (b, 0))],
        out_specs=pl.BlockSpec((SEQ, wo), lambda b: (b, 0)),
        out_shape=jax.ShapeDtypeStruct((N_PROMPT, wo), BF16),
        compiler_params=_params("arbitrary", vmem=VMEM_LIMIT_BYTES),
        name="attn_prompt",
    )(q, k, v))
    qt = DEC_SEQ // TM
    q0 = N_PROMPT // TM
    s0 = N_PROMPT // DEC_SEQ
    outs.append(pl.pallas_call(
        functools.partial(body, has_ctx=True),
        grid=(DEC_BATCH, qt),
        in_specs=[pl.BlockSpec((TM, wq), lambda b, t: (q0 + b * qt + t, 0)),
                  pl.BlockSpec((DEC_SEQ, wk), lambda b, t: (s0 + b, 0)),
                  pl.BlockSpec((DEC_SEQ, wv), lambda b, t: (s0 + b, 0)),
                  pl.BlockSpec((PAST_LEN, wk), lambda b, t: (b, 0)),
                  pl.BlockSpec((PAST_LEN, wv), lambda b, t: (b, 0))],
        out_specs=pl.BlockSpec((TM, wo), lambda b, t: (b * qt + t, 0)),
        out_shape=jax.ShapeDtypeStruct((N_SAMPLE, wo), BF16),
        compiler_params=_params("arbitrary", "arbitrary", vmem=VMEM_LIMIT_BYTES),
        name="attn_sample",
    )(q, k, v, kc, vc))
    return outs


def _gqa_prep_kernel(x_ref, g_ref, sc_ref, sh_ref, w_ref, qg_ref, kg_ref, cos_ref, sin_ref,
                     q_ref, k_ref, v_ref, kp_ref, vp_ref):
    lo = _lane_iota((TM, LANES)) < GQA_HEAD_DIM
    nq = GQA_HEADS * GQA_HEAD_DIM
    nk = GQA_KV_HEADS * GQA_HEAD_DIM

    def pair_norm(xp, gain):
        sq = xp * xp
        ms_lo = jnp.sum(jnp.where(lo, sq, 0.0), axis=-1, keepdims=True)
        ms_hi = jnp.sum(jnp.where(lo, 0.0, sq), axis=-1, keepdims=True)
        ms = jnp.where(lo, ms_lo, ms_hi) * (1.0 / GQA_HEAD_DIM)
        return (xp * lax.rsqrt(ms + RMS_EPS)) * gain

    def body(rope):
        def rotate(xp):
            return xp if rope is None else _rope(xp, rope[0], rope[1], GQA_HEAD_DIM // 4)

        h = _norm_mod(x_ref[...], g_ref[...], sc_ref[0], sh_ref[0])
        qkv = _dot(h.astype(BF16), w_ref[...])
        for p in range(nq // LANES):
            sl = slice(p * LANES, (p + 1) * LANES)
            qn = pair_norm(qkv[:, sl], qg_ref[...])
            q_ref[:, sl] = (rotate(qn) * GQA_SCALE).astype(BF16)
        for p in range(nk // LANES):
            sl = slice(p * LANES, (p + 1) * LANES)
            kn = pair_norm(qkv[:, nq + p * LANES:nq + (p + 1) * LANES], kg_ref[...])
            if rope is None:
                kp_ref[:, sl] = kn
            k_ref[:, sl] = rotate(kn).astype(BF16)
        v = qkv[:, nq + nk:]
        if rope is None:
            vp_ref[...] = v
        v_ref[...] = v.astype(BF16)

    _per_group(lambda: body(None), lambda: body((cos_ref[...], sin_ref[...])))


def _gqa_prep(x, g, sc, sh, w, qg, kg, cos_t, sin_t):
    nq = GQA_HEADS * GQA_HEAD_DIM
    nk = GQA_KV_HEADS * GQA_HEAD_DIM
    return pl.pallas_call(
        _gqa_prep_kernel,
        grid=(NT,),
        in_specs=[_row_spec(D_MODEL), _gain_spec(g), _mod_spec(sc), _mod_spec(sh),
                  _full_spec(w.shape), _full_spec(qg.shape), _full_spec(kg.shape),
                  _rope_spec(), _rope_spec()],
        out_specs=[_row_spec(nq), _row_spec(nk), _row_spec(nk), _context_rows_spec(nk), _context_rows_spec(nk)],
        out_shape=[
            jax.ShapeDtypeStruct((N_TOK, nq), BF16),
            jax.ShapeDtypeStruct((N_TOK, nk), BF16),
            jax.ShapeDtypeStruct((N_TOK, nk), BF16),
            jax.ShapeDtypeStruct((N_PROMPT, nk), F32),
            jax.ShapeDtypeStruct((N_PROMPT, nk), F32),
        ],
        compiler_params=_params("arbitrary", vmem=VMEM_LIMIT_BYTES),
        name="gqa_prep",
    )(x, g[0], sc[0], sh[0], w, qg, kg, cos_t, sin_t)


def _outproj_kernel(*refs, n_in):
    ap_refs = refs[:n_in]
    as_refs = refs[n_in:2 * n_in]
    w_refs = refs[2 * n_in:3 * n_in]
    x_ref, g_ref, o_ref = refs[3 * n_in:3 * n_in + 3]
    w_scr = refs[3 * n_in + 3:]
    is_prompt = pl.program_id(0) < NT_PROMPT

    @pl.when(pl.program_id(0) == 0)
    def _():
        for w, s in zip(w_refs, w_scr):
            s[...] = w[...].astype(BF16)

    y = None
    for ap, asm, s in zip(ap_refs, as_refs, w_scr):
        a = jnp.where(is_prompt, ap[...], asm[...])
        d = _dot(a, s[...])
        y = d if y is None else y + d
    o_ref[...] = x_ref[...] + g_ref[0] * y


def _outproj(acts, ws, x, gate):
    n_in = len(acts)

    def w_spec(index, rows):
        return pl.BlockSpec((None, rows, D_MODEL), lambda i: index)

    def prompt_spec(width):
        return pl.BlockSpec((TM, width), lambda i: (jnp.minimum(i, NT_PROMPT - 1), 0))

    def sample_spec(width):
        return pl.BlockSpec((TM, width), lambda i: (jnp.maximum(i - NT_PROMPT, 0), 0))

    return pl.pallas_call(
        functools.partial(_outproj_kernel, n_in=n_in),
        grid=(NT,),
        in_specs=([prompt_spec(ap.shape[1]) for ap, _ in acts] + [sample_spec(asm.shape[1]) for _, asm in acts]
                  + [w_spec(index, rows) for _, index, rows in ws] + [_row_spec(D_MODEL), _mod_spec(gate)]),
        out_specs=_row_spec(D_MODEL),
        out_shape=jax.ShapeDtypeStruct((N_TOK, D_MODEL), F32),
        scratch_shapes=[pltpu.VMEM((rows, D_MODEL), BF16) for _, _, rows in ws],
        compiler_params=_params("arbitrary", vmem=VMEM_LIMIT_BYTES),
        name="outproj",
    )(*(ap for ap, _ in acts), *(asm for _, asm in acts), *(w for w, _, _ in ws), x, gate[0])


def _router_kernel(x_ref, g_ref, sc_ref, sh_ref, wh_ref, wl_ref, br_ref, tri_ref,
                   h_ref, idx_ref, gate_ref, pos_ref, cnt_ref, run_ref):
    @pl.when(pl.program_id(0) == 0)
    def _():
        run_ref[...] = jnp.zeros_like(run_ref)

    h = _norm_mod(x_ref[...], g_ref[...], sc_ref[0], sh_ref[0])
    h_ref[...] = h.reshape((TM,) + ROW_TILE)
    hh, hl = _split(h)
    logits = _dot(hh, wh_ref[...]) + (_dot(hh, wl_ref[...]) + _dot(hl, wh_ref[...])) + br_ref[...]
    lane = _lane_iota((TM, LANES)).astype(F32)
    neg = jnp.float32(-jnp.inf)
    lg = jnp.where(lane < N_EXPERTS, logits, neg)
    tops, sels, hots = [], [], []
    for _ in range(TOP_K):
        m = lg.max(axis=-1, keepdims=True)
        sel = jnp.where(lg == m, lane, float(LANES)).min(axis=-1, keepdims=True)
        hot = lane == sel
        lg = jnp.where(hot, neg, lg)
        tops.append(m)
        sels.append(sel)
        hots.append(hot)
    es = [jnp.exp(t - tops[0]) for t in tops]
    den = es[0] + es[1] + es[2] + es[3]
    member = jnp.zeros((TM, LANES), F32)
    for hot in hots:
        member = member + hot.astype(F32)
    ranks = _dot(tri_ref[...], member.astype(BF16)) + run_ref[...]
    lane4 = _lane_iota((TM, TOP_K))
    idx4 = jnp.zeros((TM, TOP_K), F32)
    gate4 = jnp.zeros((TM, TOP_K), F32)
    pos4 = jnp.zeros((TM, TOP_K), F32)
    for k in range(TOP_K):
        pk = jnp.sum(jnp.where(hots[k], ranks, 0.0), axis=-1, keepdims=True)
        idx4 = jnp.where(lane4 == k, sels[k], idx4)
        gate4 = jnp.where(lane4 == k, es[k] / den, gate4)
        pos4 = jnp.where(lane4 == k, pk, pos4)
    idx_ref[...] = idx4.astype(jnp.int32)
    gate_ref[...] = gate4
    pos_ref[...] = pos4.astype(jnp.int32)
    run_ref[...] = run_ref[...] + jnp.sum(member, axis=0, keepdims=True)
    cnt_ref[...] = run_ref[...]


def _router_weights(w_router, b_router):
    wpad = jnp.pad(w_router, ((0, 0), (0, 0), (0, LANES - N_EXPERTS)))
    wh = wpad.astype(BF16)
    wl = (wpad - wh.astype(F32)).astype(BF16)
    bpad = jnp.pad(b_router, ((0, 0), (0, LANES - N_EXPERTS))).reshape(DEPTH, 1, LANES)
    return wh, wl, bpad


def _router(layer, x, g, sc, sh, router_weights):
    wh, wl, bpad = router_weights
    r = np.arange(TM)
    tri = jnp.asarray(r[None, :] < r[:, None], dtype=BF16)
    narrow = pl.BlockSpec((TM, TOP_K), lambda i: (i, 0))

    def layer_spec(a):
        return pl.BlockSpec((None,) + a.shape[1:], lambda i: (layer, 0, 0))

    return pl.pallas_call(
        _router_kernel,
        grid=(NT,),
        in_specs=[_row_spec(D_MODEL), _gain_spec(g), _mod_spec(sc), _mod_spec(sh),
                  layer_spec(wh), layer_spec(wl), layer_spec(bpad), _full_spec(tri.shape)],
        out_specs=[pl.BlockSpec((TM,) + ROW_TILE, lambda i: (i, 0, 0)), narrow, narrow, narrow,
                   _full_spec((1, LANES))],
        out_shape=[
            jax.ShapeDtypeStruct((N_TOK,) + ROW_TILE, F32),
            jax.ShapeDtypeStruct((N_TOK, TOP_K), jnp.int32),
            jax.ShapeDtypeStruct((N_TOK, TOP_K), F32),
            jax.ShapeDtypeStruct((N_TOK, TOP_K), jnp.int32),
            jax.ShapeDtypeStruct((1, LANES), F32),
        ],
        scratch_shapes=[pltpu.VMEM((1, LANES), F32)],
        compiler_params=_params("arbitrary", vmem=VMEM_LIMIT_BYTES),
        name="router",
    )(x, g[0], sc[0], sh[0], wh, wl, bpad, tri)


ROW_UNROLL = 4
DMA_QUEUES = 2


def _start_all_rows(row_copy):
    def start_rows(j, carry):
        for u in range(ROW_UNROLL):
            for k in range(TOP_K):
                row_copy(j * ROW_UNROLL + u, k).start(priority=k % DMA_QUEUES)
        return carry

    lax.fori_loop(0, TM // ROW_UNROLL, start_rows, 0)


def _dispatch_kernel(dest_ref, clear_ref, h_ref, xs_ref, zero_ref, sem_ref):
    def row_copy(t, k):
        d = dest_ref[(pl.program_id(0) * TM + t) * TOP_K + k]
        return pltpu.make_async_copy(h_ref.at[pl.ds(t, 1)], xs_ref.at[pl.ds(d, 1)], sem_ref.at[0])

    def zero_half(j):
        row0 = pl.multiple_of(j * MOE_HALF, MOE_HALF)
        return pltpu.make_async_copy(zero_ref, xs_ref.at[pl.ds(row0, MOE_HALF)], sem_ref.at[1])

    @pl.when(pl.program_id(0) == 0)
    def _():
        zero_ref[...] = jnp.zeros_like(zero_ref)

        def start(j, carry):
            @pl.when(clear_ref[j] > 0)
            def _():
                zero_half(j).start()
            return carry

        def wait(j, carry):
            @pl.when(clear_ref[j] > 0)
            def _():
                zero_half(j).wait()
            return carry

        lax.fori_loop(0, 2 * N_BLOCKS, start, 0)
        lax.fori_loop(0, 2 * N_BLOCKS, wait, 0)

    _start_all_rows(row_copy)
    for _ in range(TOP_K):
        pltpu.make_async_copy(h_ref, xs_ref.at[pl.ds(0, TM)], sem_ref.at[0]).wait()


def _dispatch(dest_flat, clear, h):
    return pl.pallas_call(
        _dispatch_kernel,
        grid_spec=pltpu.PrefetchScalarGridSpec(
            num_scalar_prefetch=2,
            grid=(NT,),
            in_specs=[pl.BlockSpec((TM,) + ROW_TILE, lambda i, d, c: (i, 0, 0))],
            out_specs=pl.BlockSpec(memory_space=pl.ANY),
            scratch_shapes=[pltpu.VMEM((MOE_HALF,) + ROW_TILE, F32), pltpu.SemaphoreType.DMA((2,))],
        ),
        out_shape=jax.ShapeDtypeStruct((CAP,) + ROW_TILE, F32),
        compiler_params=_params("arbitrary", vmem=VMEM_LIMIT_BYTES),
        name="moe_dispatch",
    )(dest_flat, clear, h)


def _expert_kernel(be_ref, nh_ref, last_ref, xs_ref, wgu_ref, bgu_ref, wd_ref, bd_ref, ys_ref, wgu_s, wd_s):
    del last_ref
    b = pl.program_id(0)
    nh = nh_ref[b]

    @pl.when((nh > 0) & ((b == 0) | (be_ref[b] != be_ref[jnp.maximum(b - 1, 0)])))
    def _():
        wgu_s[...] = wgu_ref[...].astype(BF16)
        wd_s[...] = wd_ref[...].astype(BF16)

    def ffn(rows):
        x = xs_ref[0:rows].reshape(rows, D_MODEL)
        gu = _dot(x.astype(BF16), wgu_s[...]) + bgu_ref[...]
        g = jnp.minimum(gu[:, :D_EXPERT], SWIGLU_LIMIT)
        u = jnp.clip(gu[:, D_EXPERT:], -SWIGLU_LIMIT, SWIGLU_LIMIT)
        act = (u + 1.0) * (g * jax.nn.sigmoid(SWIGLU_ALPHA * g))
        y = _dot(act.astype(BF16), wd_s[...]) + bd_ref[...]
        ys_ref[0:rows] = y.reshape((rows,) + ROW_TILE)

    @pl.when(nh == 2)
    def _():
        ffn(MOE_BLK)

    @pl.when(nh == 1)
    def _():
        ffn(MOE_HALF)
        ys_ref[MOE_HALF:] = jnp.zeros((MOE_BLK - MOE_HALF,) + ROW_TILE, F32)

    @pl.when(nh == 0)
    def _():
        ys_ref[...] = jnp.zeros_like(ys_ref)


def _experts(layer, block_e, block_nh, last_used, xs, w_gu, b_gu, w_down, b_down):
    def xs_map(b, be, nh, lu):
        return (jnp.minimum(b, lu[0]), 0, 0)

    def e_map(b, be, nh, lu):
        return (layer, be[b], 0, 0)

    return pl.pallas_call(
        _expert_kernel,
        grid_spec=pltpu.PrefetchScalarGridSpec(
            num_scalar_prefetch=3,
            grid=(N_BLOCKS,),
            in_specs=[
                pl.BlockSpec((MOE_BLK,) + ROW_TILE, xs_map),
                pl.BlockSpec((None, None, D_MODEL, 2 * D_EXPERT), e_map),
                pl.BlockSpec((None, None, 1, 2 * D_EXPERT), e_map),
                pl.BlockSpec((None, None, D_EXPERT, D_MODEL), e_map),
                pl.BlockSpec((None, None, 1, D_MODEL), e_map),
            ],
            out_specs=pl.BlockSpec((MOE_BLK,) + ROW_TILE, lambda b, be, nh, lu: (b, 0, 0)),
            scratch_shapes=[pltpu.VMEM((D_MODEL, 2 * D_EXPERT), BF16), pltpu.VMEM((D_EXPERT, D_MODEL), BF16)],
        ),
        out_shape=jax.ShapeDtypeStruct((CAP,) + ROW_TILE, F32),
        compiler_params=_params("arbitrary", vmem=VMEM_LIMIT_BYTES),
        name="moe_experts",
    )(block_e, block_nh, last_used, xs, w_gu, b_gu.reshape(DEPTH, N_EXPERTS, 1, -1), w_down,
      b_down.reshape(DEPTH, N_EXPERTS, 1, -1))


def _combine_kernel(dest_ref, x_ref, g_ref, gate_ref, ys_ref, *rest, per_group):
    out_refs, (buf_ref, sem_ref) = rest[:-2], rest[-2:]
    i = pl.program_id(0)
    cur = lax.rem(i, 2)

    def start_tile(tile, buf):
        def row_copy(t, k):
            d = dest_ref[(tile * TM + t) * TOP_K + k]
            return pltpu.make_async_copy(ys_ref.at[pl.ds(d, 1)], buf_ref.at[buf, k, pl.ds(t, 1)], sem_ref.at[buf])
        _start_all_rows(row_copy)

    @pl.when(i == 0)
    def _():
        start_tile(0, 0)

    @pl.when(i + 1 < NT)
    def _():
        start_tile(i + 1, 1 - cur)

    for k in range(TOP_K):
        pltpu.make_async_copy(ys_ref.at[pl.ds(0, TM)], buf_ref.at[cur, k], sem_ref.at[cur]).wait()
    gates = gate_ref[...]
    ff = gates[:, 0:1] * buf_ref[cur, 0].reshape(TM, D_MODEL)
    for k in range(1, TOP_K):
        ff = ff + gates[:, k:k + 1] * buf_ref[cur, k].reshape(TM, D_MODEL)
    out = x_ref[...] + g_ref[0] * ff
    if per_group:
        def store(ref):
            ref[...] = out
        _per_group(functools.partial(store, out_refs[0]), functools.partial(store, out_refs[1]))
    else:
        out_refs[0][...] = out


def _combine(dest_flat, x, gate_vec, gates, ys, per_group):
    if per_group:
        out_specs = [_context_rows_spec(D_MODEL),
                     pl.BlockSpec((TM, D_MODEL), lambda i, *_: (jnp.maximum(i - NT_PROMPT, 0), 0))]
        out_shape = [jax.ShapeDtypeStruct((N_PROMPT, D_MODEL), F32), jax.ShapeDtypeStruct((N_SAMPLE, D_MODEL), F32)]
    else:
        out_specs = pl.BlockSpec((TM, D_MODEL), lambda i, d: (i, 0))
        out_shape = jax.ShapeDtypeStruct((N_TOK, D_MODEL), F32)
    return pl.pallas_call(
        functools.partial(_combine_kernel, per_group=per_group),
        grid_spec=pltpu.PrefetchScalarGridSpec(
            num_scalar_prefetch=1,
            grid=(NT,),
            in_specs=[
                pl.BlockSpec((TM, D_MODEL), lambda i, d: (i, 0)),
                _mod_spec(gate_vec),
                pl.BlockSpec((TM, TOP_K), lambda i, d: (i, 0)),
                pl.BlockSpec(memory_space=pl.ANY),
            ],
            out_specs=out_specs,
            scratch_shapes=[pltpu.VMEM((2, TOP_K, TM) + ROW_TILE, F32), pltpu.SemaphoreType.DMA((2,))],
        ),
        out_shape=out_shape,
        compiler_params=_params("arbitrary", vmem=VMEM_LIMIT_BYTES),
        name="moe_combine",
    )(dest_flat, x, gate_vec[0], gates, ys)


def _moe(layer, x, g, sc, sh, gate_vec, router_weights, w_gu, b_gu, w_down, b_down, per_group=False):
    h, idx, gates, pos, counts = _router(layer, x, g, sc, sh, router_weights)
    cnt = counts[0, :N_EXPERTS].astype(jnp.int32)
    nhalf = (cnt + MOE_HALF - 1) // MOE_HALF
    nblk = (nhalf + 1) // 2
    e_ids = jnp.arange(N_EXPERTS, dtype=jnp.int32)
    blk_end = jnp.sum(jnp.where(e_ids[None, :] <= e_ids[:, None], nblk[None, :], 0), axis=1)
    blk_start = blk_end - nblk
    last_used = blk_end[-1] - 1

    def per_expert(table, e):
        return jnp.sum(jnp.where(e[..., None] == e_ids, table, 0), axis=-1)

    def expert_of_block(blk):
        return jnp.sum((blk_end <= jnp.minimum(blk, last_used)[:, None]).astype(jnp.int32), axis=1)

    dest = per_expert(blk_start * MOE_BLK, idx) + pos
    b_ids = jnp.arange(N_BLOCKS, dtype=jnp.int32)
    block_e = expert_of_block(b_ids)
    block_nh = jnp.where(b_ids <= last_used,
                         jnp.clip(per_expert(nhalf, block_e) - 2 * (b_ids - per_expert(blk_start, block_e)), 0, 2), 0)
    h_ids = jnp.arange(2 * N_BLOCKS, dtype=jnp.int32)
    h_e = expert_of_block(h_ids // 2)
    h_nhalf = per_expert(nhalf, h_e)
    h_local = h_ids - 2 * per_expert(blk_start, h_e)
    holds_rows = (h_ids // 2 <= last_used) & (h_local < h_nhalf)
    clear = jnp.logical_not(holds_rows) | (h_local == h_nhalf - 1)
    dest_flat = dest.reshape(-1).astype(jnp.int32)
    xs = _dispatch(dest_flat, clear.astype(jnp.int32), h)
    ys = _experts(layer, block_e.astype(jnp.int32), block_nh.astype(jnp.int32),
                  last_used.reshape(1).astype(jnp.int32), xs, w_gu, b_gu, w_down, b_down)
    return _combine(dest_flat, x, gate_vec, gates, ys, per_group)


def _rope_tables(d_rot, lane0, period):
    n_rows = DEC_SEQ // GRID_W
    rows = np.repeat(np.arange(n_rows), GRID_W).astype(np.float32)
    cols = np.tile(np.arange(GRID_W), n_rows).astype(np.float32)
    half = d_rot // 2
    lane = np.arange(LANES)
    i = (lane - lane0) % period
    active = (lane >= lane0) & (i < d_rot)
    w = i % half
    f = w % (half // 2)
    pos = np.where((i // half)[None, :] == 0, rows[:, None], cols[:, None])
    sign = np.where(w < half // 2, -1.0, 1.0).astype(np.float32)
    inv = ROPE_THETA ** (-jnp.arange(0, half, 2, dtype=F32) / half)
    ang = jnp.asarray(pos) * inv[f][None, :]
    cos = jnp.where(active[None, :], jnp.cos(ang), 1.0)
    sin = jnp.where(active[None, :], jnp.sin(ang) * sign[None, :], 0.0)
    return cos, sin


def _rope_spec():
    def index(i):
        return (jnp.maximum(i - NT_PROMPT, 0) % TILES_PER_SAMPLE, 0)
    return pl.BlockSpec((TM, LANES), index)


def _pad_heads(w, n_heads, width):
    lead = w.shape[:-1]
    w = w.reshape(lead + (n_heads, width))
    w = jnp.pad(w, [(0, 0)] * len(lead) + [(0, 0), (0, HEAD_PAD - width)])
    return w.reshape(lead + (n_heads * HEAD_PAD,))


_Q_ORDER = (0, 4, 1, 5, 2, 6, 3, 7, 8, 12, 9, 13, 10, 14, 11, 15)


def _perm_q_heads(w, axis):
    shape = w.shape
    n = shape[axis]
    w = jnp.moveaxis(w, axis, 0).reshape((GQA_HEADS, n // GQA_HEADS) + tuple(s for a, s in enumerate(shape) if a != axis))
    w = w[jnp.array(_Q_ORDER)]
    w = w.reshape((n,) + w.shape[2:])
    return jnp.moveaxis(w, 0, axis)


def _constants():
    mla_cos, mla_sin = _rope_tables(MLA_ROPE, MLA_NOPE, LANES)
    gqa_cos, gqa_sin = _rope_tables(GQA_HEAD_DIM, 0, GQA_HEAD_DIM)
    dft = {}
    for L in (SEQ, DEC_SEQ):
        cm, sm = _dft_tables(L)
        fmat = jnp.concatenate([cm, sm], axis=0).astype(BF16)
        ftmat = fmat.T
        dft[L] = (cm, sm, fmat, ftmat)
    return dict(mla=(mla_cos, mla_sin), gqa=(gqa_cos, gqa_sin), dft=dft)


def _even_mixer(x, p, i, gmix, sc1, sh1, g1, consts):
    proj = _normlin(x, gmix, sc1, sh1, p["w_in_ab"], i, IN_AB_PAD)
    y_hy = []
    for L, blk0, nseq, per_step in ((SEQ, 0, BATCH, HY_PROMPT_SEQS), (DEC_SEQ, N_PROMPT // DEC_SEQ, DEC_BATCH, 1)):
        cm, sm, fmat, ftmat = consts["dft"][L]
        kc, ks = _hy_filter(L, cm, sm, p["hy_filter_w1"][i], p["hy_filter_b1"][i], p["hy_filter_freq"][i],
                            p["hy_filter_w2"][i], p["hy_filter_b2"][i], p["hy_filter_w3"][i],
                            p["hy_filter_b3"][i], p["hy_log_decay"][i])
        y_hy.append(_hy_mix(proj, blk0, nseq, per_step, L, p["hy_conv_w"][i], p["hy_conv_b"][i], kc, ks,
                            p["hy_bias"][i], fmat, ftmat))

    wq = _pad_heads(p["mla_wq_b"][i], MLA_HEADS, MLA_QK).astype(BF16)
    wkv = p["mla_wkv_b"][i].astype(BF16)
    qg = jnp.pad(p["mla_q_norm"][i], (0, HEAD_PAD - MLA_QK)).reshape(1, HEAD_PAD)
    kg = jnp.pad(p["mla_k_norm"][i], (0, HEAD_PAD - MLA_QK)).reshape(1, HEAD_PAD)
    mla_cos, mla_sin = consts["mla"]
    q, k, v, lat = _mla_prep(proj, p["mla_q_lora_norm"][i].reshape(1, -1), wq, qg,
                             p["mla_kv_lora_norm"][i].reshape(1, -1), wkv, kg, mla_cos, mla_sin)
    ctx = jnp.pad(p["cache_mla_latent"][:, i].reshape(DEC_BATCH * PAST_LEN, -1),
                  ((0, 0), (0, KV_PAD - MLA_KV_LORA - MLA_ROPE)))
    kc_ctx, vc_ctx = _mla_ctx(ctx, wkv, kg)
    nqk = MLA_HEADS * HEAD_PAD
    o = _attention(_mla_attn_kernel, q, k, v, kc_ctx, vc_ctx, nqk, nqk, nqk, MLA_HEADS * MLA_V)
    w_out = p["w_out_ab"]
    x = _outproj([y_hy, o], [(w_out, (i, 0, 0), HY_D), (w_out, (i, 1, 0), HY_D)], x, g1)
    return x, lat, dict(y_hy=jnp.concatenate(y_hy, axis=0), o=jnp.concatenate(o, axis=0))


def _odd_mixer(x, p, i, gmix, sc1, sh1, g1, consts):
    nq = GQA_HEADS * GQA_HEAD_DIM
    nk = GQA_KV_HEADS * GQA_HEAD_DIM
    w = p["w_qkv_c"][i]
    w_qkv = jnp.concatenate([_perm_q_heads(w[:, :nq], 1), w[:, nq:]], axis=1).astype(BF16)
    qg = jnp.tile(p["gqa_q_norm"][i], LANES // GQA_HEAD_DIM).reshape(1, LANES)
    kg = jnp.tile(p["gqa_k_norm"][i], LANES // GQA_HEAD_DIM).reshape(1, LANES)
    gqa_cos, gqa_sin = consts["gqa"]
    q, k, v, k_plain, v_plain = _gqa_prep(x, gmix, sc1, sh1, w_qkv, qg, kg, gqa_cos, gqa_sin)
    kc_ctx = p["cache_gqa_k"][:, i].reshape(DEC_BATCH * PAST_LEN, -1).astype(BF16)
    vc_ctx = p["cache_gqa_v"][:, i].reshape(DEC_BATCH * PAST_LEN, -1).astype(BF16)
    o = _attention(_gqa_attn_kernel, q, k, v, kc_ctx, vc_ctx, nq, nk, nk, nq)
    w_out = _perm_q_heads(p["w_out_c"][i], 0)[None]
    x = _outproj([o], [(w_out, (0, 0, 0), nq)], x, g1)
    return x, k_plain, v_plain


def kernel(x_prompt, x_sample, cache_mla_latent, cache_gqa_k, cache_gqa_v, c, c_ctx, w_ada, b_ada, norm_mix, norm_ffn, w_in_ab, hy_conv_w, hy_conv_b, hy_filter_w1, hy_filter_b1, hy_filter_freq, hy_filter_w2, hy_filter_b2, hy_filter_w3, hy_filter_b3, hy_log_decay, hy_bias, mla_q_lora_norm, mla_wq_b, mla_kv_lora_norm, mla_wkv_b, mla_q_norm, mla_k_norm, w_out_ab, w_qkv_c, gqa_q_norm, gqa_k_norm, w_out_c, moe_router_w, moe_router_b, moe_w_gate_up, moe_b_gate_up, moe_w_down, moe_b_down):
    p = dict(locals())
    x = jnp.concatenate([x_prompt.reshape(N_PROMPT, D_MODEL), x_sample.reshape(N_SAMPLE, D_MODEL)], axis=0)

    cond = jnp.concatenate([c_ctx[None, :], c, jnp.zeros((COND_ROWS - N_COND, D_MODEL), F32)], axis=0)
    mods = _modulation(cond, w_ada, b_ada)
    tile_cond = jnp.concatenate([jnp.zeros((NT_PROMPT,), jnp.int32),
                                 1 + jnp.arange(NT - NT_PROMPT, dtype=jnp.int32) // TILES_PER_SAMPLE])
    mods = mods[:, tile_cond].reshape(DEPTH, NT, 6, 1, D_MODEL).transpose(0, 2, 1, 3, 4)

    consts = _constants()
    router_weights = _router_weights(moe_router_w, moe_router_b)
    gains_mix = norm_mix.reshape(DEPTH, 1, D_MODEL)
    gains_ffn = norm_ffn.reshape(DEPTH, 1, D_MODEL)

    lat_out, k_out, v_out = [], [], []
    for l in range(DEPTH):
        sh1, sc1, g1, sh2, sc2, g2 = ((mods, l, j) for j in range(6))
        i = l // 2
        gmix = (gains_mix, l)
        if l % 2 == 0:
            x, lat, _ = _even_mixer(x, p, i, gmix, sc1, sh1, g1, consts)
            lat_out.append(lat.reshape(BATCH, SEQ, -1))
        else:
            x, k_plain, v_plain = _odd_mixer(x, p, i, gmix, sc1, sh1, g1, consts)
            k_out.append(k_plain.reshape(BATCH, SEQ, GQA_KV_HEADS, GQA_HEAD_DIM))
            v_out.append(v_plain.reshape(BATCH, SEQ, GQA_KV_HEADS, GQA_HEAD_DIM))
        x = _moe(l, x, (gains_ffn, l), sc2, sh2, g2, router_weights,
                 moe_w_gate_up, moe_b_gate_up, moe_w_down, moe_b_down, per_group=(l == DEPTH - 1))

    y_prompt = x[0].reshape(BATCH, SEQ, D_MODEL)
    y_sample = x[1].reshape(DEC_BATCH, DEC_SEQ, D_MODEL)
    return (y_prompt, y_sample, jnp.stack(lat_out, axis=1), jnp.stack(k_out, axis=1), jnp.stack(v_out, axis=1))
```

```python
import functools
import math

import jax
import jax.numpy as jnp
import numpy as np
from jax import lax
from jax.experimental import pallas as pl
from jax.experimental.pallas import tpu as pltpu

F32 = jnp.float32
BF16 = jnp.bfloat16

D_MODEL = 1024
BATCH = 32
SEQ = 256
DEPTH = 4
DEC_BATCH = 4
DEC_SEQ = 1024
PAST_LEN = 256
GRID_W = 64
N_EVEN = (DEPTH + 1) // 2
N_ODD = DEPTH // 2
HY_D = D_MODEL // 2
HY_ORDER = 2
HY_BANDS = 16
HY_EMB = 2 * HY_BANDS + 1
HY_FILTER_HIDDEN = 64
MLA_HEADS = 8
MLA_NOPE = 64
MLA_ROPE = 32
MLA_QK = MLA_NOPE + MLA_ROPE
MLA_V = HY_D // MLA_HEADS
MLA_Q_LORA = 3 * D_MODEL // 8
MLA_KV_LORA = D_MODEL // 4
GQA_HEADS = 16
GQA_KV_HEADS = 4
GQA_HEAD_DIM = D_MODEL // GQA_HEADS
N_EXPERTS = 32
TOP_K = 4
D_EXPERT = D_MODEL
SWIGLU_LIMIT = 7.0
SWIGLU_ALPHA = 1.702
ROPE_THETA = 10000.0
RMS_EPS = 1e-6
GQA_SCALE = GQA_HEAD_DIM ** -0.5
assert math.frexp(GQA_SCALE)[0] == 0.5
IN_AB = (HY_ORDER + 1) * HY_D + MLA_Q_LORA + MLA_KV_LORA + MLA_ROPE

N_PROMPT = BATCH * SEQ
N_SAMPLE = DEC_BATCH * DEC_SEQ
N_TOK = N_PROMPT + N_SAMPLE

LANES = 128
SUBLANES = 8
VMEM_LIMIT_BYTES = 56 * 1024 * 1024

TM = 256
NT = N_TOK // TM
NT_PROMPT = N_PROMPT // TM
TILES_PER_SAMPLE = DEC_SEQ // TM
N_COND = 1 + DEC_BATCH
COND_ROWS = 8
HEAD_PAD = LANES
IN_AB_PAD = 2304
KV_PAD = 384
MOE_BLK = 512
MOE_PARTS = 4
MOE_PART = MOE_BLK // MOE_PARTS
N_SLOTS = N_TOK * TOP_K
N_BLOCKS = N_SLOTS // MOE_BLK + N_EXPERTS
CAP = N_BLOCKS * MOE_BLK
ROW_TILE = (SUBLANES, LANES)
assert SUBLANES * LANES == D_MODEL
HY_CH = 256
HY_PROMPT_SEQS = 8


def _dot(a, b):
    return jnp.dot(a, b, preferred_element_type=F32)


def _dot_nt(a, b):
    return lax.dot_general(a, b, (((1,), (1,)), ((), ())), preferred_element_type=F32)


def _split(x):
    hi = x.astype(BF16)
    lo = (x - hi.astype(F32)).astype(BF16)
    return hi, lo


def _dot3(a, b):
    ah, al = _split(a)
    bh, bl = _split(b)
    return _dot(ah, bh) + (_dot(ah, bl) + _dot(al, bh))


def _lane_iota(shape):
    return lax.broadcasted_iota(jnp.int32, shape, len(shape) - 1)


def _params(*sem, vmem=None):
    return pltpu.CompilerParams(dimension_semantics=sem, vmem_limit_bytes=vmem)


def _mod_kernel(c_ref, w_ref, b_ref, o_ref):
    c = c_ref[...]
    s = c * jax.nn.sigmoid(c)
    o_ref[0] = _dot(s.astype(BF16), w_ref[0].astype(BF16)) + b_ref[0]


def _modulation(cond, w_ada, b_ada):
    nblk = 6
    return pl.pallas_call(
        _mod_kernel,
        grid=(DEPTH, nblk),
        in_specs=[
            pl.BlockSpec((COND_ROWS, D_MODEL), lambda l, j: (0, 0)),
            pl.BlockSpec((1, D_MODEL, D_MODEL), lambda l, j: (l, 0, j)),
            pl.BlockSpec((1, 1, D_MODEL), lambda l, j: (l, 0, j)),
        ],
        out_specs=pl.BlockSpec((1, COND_ROWS, D_MODEL), lambda l, j: (l, 0, j)),
        out_shape=jax.ShapeDtypeStruct((DEPTH, COND_ROWS, nblk * D_MODEL), F32),
        compiler_params=_params("arbitrary", "arbitrary"),
        name="modulation",
    )(cond, w_ada, b_ada.reshape(DEPTH, 1, nblk * D_MODEL))


def _norm_mod(x, g, sc, sh):
    ms = jnp.mean(x * x, axis=-1, keepdims=True)
    y = x * lax.rsqrt(ms + RMS_EPS)
    return (y * g) * (1.0 + sc) + sh


def _row_spec(width):
    return pl.BlockSpec((TM, width), lambda i: (i, 0))


def _mod_spec(mod):
    _, layer, which = mod
    return pl.BlockSpec((None, None, 1, 1, D_MODEL), lambda i, *_: (layer, which, i, 0, 0))


def _gain_spec(gain):
    _, layer = gain
    return pl.BlockSpec((None, 1, D_MODEL), lambda i, *_: (layer, 0, 0))


def _full_spec(shape):
    nd = len(shape)
    return pl.BlockSpec(shape, lambda i: (0,) * nd)


def _normlin_kernel(x_ref, g_ref, sc_ref, sh_ref, w_ref, o_ref, w_scr):
    nin = w_ref.shape[1]

    @pl.when(pl.program_id(0) == 0)
    def _():
        w_scr[:, :nin] = w_ref[...].astype(BF16)
        w_scr[:, nin:] = jnp.zeros((D_MODEL, w_scr.shape[1] - nin), BF16)

    h = _norm_mod(x_ref[...], g_ref[...], sc_ref[0], sh_ref[0])
    o_ref[...] = _dot(h.astype(BF16), w_scr[...])


def _normlin(x, g, sc, sh, w_all, layer, nout):
    nin = w_all.shape[2]
    return pl.pallas_call(
        _normlin_kernel,
        grid=(NT,),
        in_specs=[_row_spec(D_MODEL), _gain_spec(g), _mod_spec(sc), _mod_spec(sh),
                  pl.BlockSpec((None, D_MODEL, nin), lambda i: (layer, 0, 0))],
        out_specs=_row_spec(nout),
        out_shape=jax.ShapeDtypeStruct((N_TOK, nout), F32),
        scratch_shapes=[pltpu.VMEM((D_MODEL, nout), BF16)],
        compiler_params=_params("arbitrary", vmem=VMEM_LIMIT_BYTES),
        name="normlin",
    )(x, g[0], sc[0], sh[0], w_all)


def _hy_filter_kernel(z_ref, w1_ref, b1_ref, fr_ref, w2_ref, b2_ref, w3_ref, b3_ref, ed_ref,
                      c_ref, s_ref, kc_ref, ks_ref):
    L = z_ref.shape[0]
    z = z_ref[...]
    fr = fr_ref[...]
    hdn = jnp.sin(fr * (_dot3(z, w1_ref[...]) + b1_ref[...]))
    hdn = jnp.sin(fr * (_dot3(hdn, w2_ref[...]) + b2_ref[...]))
    filt = _dot3(hdn, w3_ref[...]) + b3_ref[...]
    t = z[:, 0:1]
    filt = filt * jnp.exp(-t * ed_ref[...])
    row = lax.broadcasted_iota(jnp.int32, (L, HY_D), 0)
    cm = c_ref[...]
    sm = s_ref[...]
    for o in range(HY_ORDER):
        fw = filt[:, (2 * o) * HY_D:(2 * o + 1) * HY_D]
        bw = filt[:, (2 * o + 1) * HY_D:(2 * o + 2) * HY_D]
        den = (jnp.sum(jnp.abs(fw), axis=0, keepdims=True)
               + jnp.sum(jnp.abs(bw), axis=0, keepdims=True)) + 1e-6
        fw = fw / den
        bw = jnp.where(row == 0, 0.0, bw / den)
        kc_ref[o] = _dot3(cm, fw + bw) * (1.0 / L)
        ks_ref[o] = _dot3(sm, fw - bw) * (1.0 / L)


def _dft_tables(L):
    m = jnp.arange(L, dtype=jnp.int32)
    phase = ((2 * m + 1)[:, None] * m[None, :]) % (4 * L)
    ang = phase.astype(F32) * (2.0 * math.pi / (4 * L))
    return jnp.cos(ang), jnp.sin(ang)


def _filter_features(L):
    p = jnp.arange(L, dtype=F32)
    t = p / max(L - 1, 1)
    bands = jnp.linspace(1e-4, HY_BANDS - 1, HY_BANDS, dtype=F32)
    ang = (2.0 * math.pi / L) * p[:, None] * bands[None, :]
    z = jnp.concatenate([t[:, None], jnp.cos(ang), -jnp.sin(ang)], axis=-1)
    return jnp.pad(z, ((0, 0), (0, LANES - HY_EMB)))


def _hy_filter(L, cmat, smat, w1, b1, fr, w2, b2, w3, b3, log_decay):
    nf = HY_ORDER * 2 * HY_D
    args = (
        _filter_features(L),
        jnp.pad(w1, ((0, LANES - HY_EMB), (0, 0))),
        b1.reshape(1, -1), fr.reshape(1, -1), w2, b2.reshape(1, -1), w3, b3.reshape(1, -1),
        jnp.exp(log_decay.astype(F32)).reshape(1, nf),
        cmat, smat,
    )
    out_sds = jax.ShapeDtypeStruct((HY_ORDER, L, HY_D), F32)
    return pl.pallas_call(
        _hy_filter_kernel,
        grid=(1,),
        in_specs=[_full_spec(a.shape) for a in args],
        out_specs=[_full_spec(out_sds.shape)] * 2,
        out_shape=[out_sds, out_sds],
        compiler_params=_params("arbitrary", vmem=VMEM_LIMIT_BYTES),
        name=f"hy_filter_{L}",
    )(*args)


def _hy_mix_kernel(u0_ref, u1_ref, u2_ref, cw0_ref, cw1_ref, cw2_ref, cb0_ref, cb1_ref, cb2_ref,
                   kc_ref, ks_ref, hb_ref, f_ref, ft_ref, o_ref):
    S, L, _ = u0_ref.shape
    row = lax.broadcasted_iota(jnp.int32, (L, HY_CH), 0)

    def lanes(per_seq):
        return per_seq[0] if S == 1 else jnp.concatenate(per_seq, axis=1)

    def short_conv(u_ref, cw_ref, cb_ref):
        w = cw_ref[0]
        out = []
        for s in range(S):
            u = u_ref[s]
            prev = jnp.where(row == 0, 0.0, pltpu.roll(u, 1, 0))
            nxt = jnp.where(row == L - 1, 0.0, pltpu.roll(u, L - 1, 0))
            out.append((prev * w[0:1] + u * w[1:2]) + nxt * w[2:3] + cb_ref[0])
        return lanes(out)

    z = short_conv(u0_ref, cw0_ref, cb0_ref)
    gates = (short_conv(u1_ref, cw1_ref, cb1_ref), short_conv(u2_ref, cw2_ref, cb2_ref))
    for o in range(HY_ORDER):
        zz = _dot(f_ref[...], z.astype(BF16))
        cz, sz = zz[:L], zz[L:]
        kc, ks = lanes([kc_ref[o]] * S), lanes([ks_ref[o]] * S)
        w1 = cz * kc - sz * ks
        w2 = cz * ks + sz * kc
        ww = jnp.concatenate([w1, w2], axis=0).astype(BF16)
        conv = _dot(ft_ref[...], ww)
        z = gates[o] * (conv + z * lanes([hb_ref[0, o]] * S))
    for s in range(S):
        o_ref[s] = z[:, s * HY_CH:(s + 1) * HY_CH].astype(o_ref.dtype)


def _hy_mix(proj, row_block0, nseq, seqs_per_step, L, cw, cb, kc, ks, hb, fmat, ftmat):
    nch = HY_D // HY_CH
    nparts = HY_ORDER + 1
    S = seqs_per_step
    assert nseq % S == 0 and row_block0 % S == 0
    cw3 = cw.reshape(3, nparts * nch, HY_CH).transpose(1, 0, 2)
    cb3 = cb.reshape(nparts * nch, 1, HY_CH)
    hb3 = hb.reshape(HY_ORDER, nch, 1, HY_CH).transpose(1, 0, 2, 3)
    proj = proj.reshape(N_TOK // L, L, proj.shape[1])

    def u_spec(part):
        return pl.BlockSpec((S, L, HY_CH), lambda s, c: (row_block0 // S + s, 0, part * nch + c))

    def cw_spec(part):
        return pl.BlockSpec((1, 3, HY_CH), lambda s, c: (part * nch + c, 0, 0))

    def cb_spec(part):
        return pl.BlockSpec((1, 1, HY_CH), lambda s, c: (part * nch + c, 0, 0))

    return pl.pallas_call(
        _hy_mix_kernel,
        grid=(nseq // S, nch),
        in_specs=[
            u_spec(0), u_spec(1), u_spec(2),
            cw_spec(0), cw_spec(1), cw_spec(2),
            cb_spec(0), cb_spec(1), cb_spec(2),
            pl.BlockSpec((HY_ORDER, L, HY_CH), lambda s, c: (0, 0, c)),
            pl.BlockSpec((HY_ORDER, L, HY_CH), lambda s, c: (0, 0, c)),
            pl.BlockSpec((1, HY_ORDER, 1, HY_CH), lambda s, c: (c, 0, 0, 0)),
            pl.BlockSpec((2 * L, L), lambda s, c: (0, 0)),
            pl.BlockSpec((L, 2 * L), lambda s, c: (0, 0)),
        ],
        out_specs=pl.BlockSpec((S, L, HY_CH), lambda s, c: (s, 0, c)),
        out_shape=jax.ShapeDtypeStruct((nseq, L, HY_D), BF16),
        compiler_params=_params("arbitrary", "arbitrary", vmem=VMEM_LIMIT_BYTES),
        name=f"hy_mix_{L}",
    )(proj, proj, proj, cw3, cw3, cw3, cb3, cb3, cb3, kc, ks, hb3, fmat, ftmat).reshape(nseq * L, HY_D)


def _head_rms(xh, gain, dim):
    ms = jnp.sum(xh * xh, axis=-1, keepdims=True) * (1.0 / dim)
    return (xh * lax.rsqrt(ms + RMS_EPS)) * gain


def _context_rows_spec(width):
    return pl.BlockSpec((TM, width), lambda i, *_: (jnp.minimum(i, NT_PROMPT - 1), 0))


def _per_group(context_fn, latent_fn):
    is_context = pl.program_id(0) < NT_PROMPT
    pl.when(is_context)(context_fn)
    pl.when(jnp.logical_not(is_context))(latent_fn)


def _rope(xh, cos, sin, half):
    lane = _lane_iota(xh.shape)
    first = (lane % (2 * half)) < half
    rot = jnp.where(first, pltpu.roll(xh, LANES - half, 1), pltpu.roll(xh, half, 1))
    return xh * cos + rot * sin


def _mla_keys_values(lat_n, kr_blk, wkv_ref, kg_ref, rope):
    kv = _dot(lat_n.astype(BF16), wkv_ref[...])
    kr = pltpu.roll(kr_blk, MLA_NOPE, 1)
    nope = _lane_iota(kr.shape) < MLA_NOPE
    ks = []
    for h in range(MLA_HEADS):
        kh = jnp.where(nope, kv[:, h * HEAD_PAD:(h + 1) * HEAD_PAD], 0.0) + kr
        kh = _head_rms(kh, kg_ref[...], MLA_QK)
        if rope is not None:
            kh = _rope(kh, rope[0], rope[1], MLA_ROPE // 4)
        ks.append(kh)
    return ks, kv


def _mla_prep_kernel(qa_ref, kva_ref, gq_ref, wq_ref, qg_ref, gkv_ref, wkv_ref, kg_ref,
                     cos_ref, sin_ref, q_ref, k_ref, v_ref, lat_ref):
    def body(rope):
        qa = qa_ref[...]
        ms = jnp.mean(qa * qa, axis=-1, keepdims=True)
        qn = (qa * lax.rsqrt(ms + RMS_EPS)) * gq_ref[...]
        q = _dot(qn.astype(BF16), wq_ref[...])
        for h in range(MLA_HEADS):
            qh = _head_rms(q[:, h * HEAD_PAD:(h + 1) * HEAD_PAD], qg_ref[...], MLA_QK)
            if rope is not None:
                qh = _rope(qh, rope[0], rope[1], MLA_ROPE // 4)
            q_ref[:, h * HEAD_PAD:(h + 1) * HEAD_PAD] = qh.astype(BF16)

        kva = kva_ref[...]
        lat = kva[:, :MLA_KV_LORA]
        ms = jnp.mean(lat * lat, axis=-1, keepdims=True)
        lat_n = (lat * lax.rsqrt(ms + RMS_EPS)) * gkv_ref[...]
        kr_blk = kva[:, MLA_KV_LORA:]
        if rope is None:
            lat_ref[:, :MLA_KV_LORA] = lat_n
            lat_ref[:, MLA_KV_LORA:] = kr_blk[:, :MLA_ROPE]
        ks, v = _mla_keys_values(lat_n, kr_blk, wkv_ref, kg_ref, rope)
        for h in range(MLA_HEADS):
            k_ref[:, h * HEAD_PAD:(h + 1) * HEAD_PAD] = ks[h].astype(BF16)
        v_ref[...] = v.astype(BF16)

    _per_group(lambda: body(None), lambda: body((cos_ref[...], sin_ref[...])))


def _mla_prep(proj, gq, wq, qg, gkv, wkv, kg, cos_t, sin_t):
    nq = MLA_HEADS * HEAD_PAD
    nv = MLA_HEADS * (MLA_NOPE + MLA_V)
    qa_blk = (HY_ORDER + 1) * HY_D // KV_PAD
    return pl.pallas_call(
        _mla_prep_kernel,
        grid=(NT,),
        in_specs=[
            pl.BlockSpec((TM, KV_PAD), lambda i: (i, qa_blk)),
            pl.BlockSpec((TM, KV_PAD), lambda i: (i, qa_blk + 1)),
            _full_spec(gq.shape), _full_spec(wq.shape), _full_spec(qg.shape), _full_spec(gkv.shape),
            _full_spec(wkv.shape), _full_spec(kg.shape),
            _rope_spec(), _rope_spec(),
        ],
        out_specs=[_row_spec(nq), _row_spec(nq), _row_spec(nv), _context_rows_spec(MLA_KV_LORA + MLA_ROPE)],
        out_shape=[
            jax.ShapeDtypeStruct((N_TOK, nq), BF16),
            jax.ShapeDtypeStruct((N_TOK, nq), BF16),
            jax.ShapeDtypeStruct((N_TOK, nv), BF16),
            jax.ShapeDtypeStruct((N_PROMPT, MLA_KV_LORA + MLA_ROPE), F32),
        ],
        compiler_params=_params("arbitrary", vmem=VMEM_LIMIT_BYTES),
        name="mla_prep",
    )(proj, proj, gq, wq, qg, gkv, wkv, kg, cos_t, sin_t)


def _mla_ctx_kernel(lat_ref, wkv_ref, kg_ref, k_ref, v_ref):
    lat = lat_ref[...]
    ks, v = _mla_keys_values(lat[:, :MLA_KV_LORA], lat[:, MLA_KV_LORA:], wkv_ref, kg_ref, None)
    for h in range(MLA_HEADS):
        k_ref[:, h * HEAD_PAD:(h + 1) * HEAD_PAD] = ks[h].astype(BF16)
    v_ref[...] = v.astype(BF16)


def _mla_ctx(lat_pad, wkv, kg):
    n = lat_pad.shape[0]
    nq = MLA_HEADS * HEAD_PAD
    nv = MLA_HEADS * (MLA_NOPE + MLA_V)
    return pl.pallas_call(
        _mla_ctx_kernel,
        grid=(n // TM,),
        in_specs=[_row_spec(KV_PAD), _full_spec(wkv.shape), _full_spec(kg.shape)],
        out_specs=[_row_spec(nq), _row_spec(nv)],
        out_shape=[jax.ShapeDtypeStruct((n, nq), BF16), jax.ShapeDtypeStruct((n, nv), BF16)],
        compiler_params=_params("arbitrary"),
        name="mla_ctx",
    )(lat_pad, wkv, kg)


def _softmax_pv(q, keys, vals, scale):
    ss = [_dot_nt(q, k) for k in keys]
    if scale is not None:
        ss = [s * scale for s in ss]
    m = ss[0].max(axis=-1, keepdims=True)
    for s in ss[1:]:
        m = jnp.maximum(m, s.max(axis=-1, keepdims=True))
    ps = [jnp.exp(s - m) for s in ss]
    l = ps[0].sum(axis=-1, keepdims=True)
    for p in ps[1:]:
        l = l + p.sum(axis=-1, keepdims=True)
    o = _dot(ps[0].astype(BF16), vals[0])
    for p, v in zip(ps[1:], vals[1:]):
        o = o + _dot(p.astype(BF16), v)
    return o / l


def _mla_attn_kernel(*refs, has_ctx):
    if has_ctx:
        q_ref, k_ref, v_ref, kc_ref, vc_ref, o_ref = refs
    else:
        q_ref, k_ref, v_ref, o_ref = refs
    scale = MLA_QK ** -0.5
    lo = _lane_iota((q_ref.shape[0], LANES)) < MLA_V
    for j in range(MLA_HEADS // 2):
        outs = []
        for h in (2 * j, 2 * j + 1):
            sl = slice(h * HEAD_PAD, (h + 1) * HEAD_PAD)
            ks, vs = [k_ref[:, sl]], [v_ref[:, sl]]
            if has_ctx:
                ks.append(kc_ref[:, sl])
                vs.append(vc_ref[:, sl])
            outs.append(_softmax_pv(q_ref[:, sl], ks, vs, scale))
        pair = jnp.where(lo, pltpu.roll(outs[0], MLA_V, 1), outs[1])
        o_ref[:, j * LANES:(j + 1) * LANES] = pair.astype(o_ref.dtype)


def _gqa_attn_kernel(*refs, has_ctx):
    if has_ctx:
        q_ref, k_ref, v_ref, kc_ref, vc_ref, o_ref = refs
    else:
        q_ref, k_ref, v_ref, o_ref = refs
    scale = None
    lo = _lane_iota((q_ref.shape[0], LANES)) < GQA_HEAD_DIM
    pairs_per_kv = (GQA_HEADS // 2) // (GQA_KV_HEADS // 2)
    for p in range(GQA_HEADS // 2):
        kv = slice((p // pairs_per_kv) * LANES, (p // pairs_per_kv + 1) * LANES)
        ks, vs = [k_ref[:, kv]], [v_ref[:, kv]]
        if has_ctx:
            ks.append(kc_ref[:, kv])
            vs.append(vc_ref[:, kv])
        qp = q_ref[:, p * LANES:(p + 1) * LANES]
        zero = jnp.zeros_like(qp)
        o_lo = _softmax_pv(jnp.where(lo, qp, zero), ks, vs, scale)
        o_hi = _softmax_pv(jnp.where(lo, zero, qp), ks, vs, scale)
        o_ref[:, p * LANES:(p + 1) * LANES] = jnp.where(lo, o_lo, o_hi).astype(o_ref.dtype)


def _attention(body, q, k, v, kc, vc, wq, wk, wv, wo):
    outs = []
    outs.append(pl.pallas_call(
        functools.partial(body, has_ctx=False),
        grid=(BATCH,),
        in_specs=[pl.BlockSpec((SEQ, wq), lambda b: (b, 0)),
                  pl.BlockSpec((SEQ, wk), lambda b: (b, 0)),
                  pl.BlockSpec((SEQ, wv), lambda b: (b, 0))],
        out_specs=pl.BlockSpec((SEQ, wo), lambda b: (b, 0)),
        out_shape=jax.ShapeDtypeStruct((N_PROMPT, wo), BF16),
        compiler_params=_params("arbitrary", vmem=VMEM_LIMIT_BYTES),
        name="attn_prompt",
    )(q, k, v))
    qt = DEC_SEQ // TM
    q0 = N_PROMPT // TM
    s0 = N_PROMPT // DEC_SEQ
    outs.append(pl.pallas_call(
        functools.partial(body, has_ctx=True),
        grid=(DEC_BATCH, qt),
        in_specs=[pl.BlockSpec((TM, wq), lambda b, t: (q0 + b * qt + t, 0)),
                  pl.BlockSpec((DEC_SEQ, wk), lambda b, t: (s0 + b, 0)),
                  pl.BlockSpec((DEC_SEQ, wv), lambda b, t: (s0 + b, 0)),
                  pl.BlockSpec((PAST_LEN, wk), lambda b, t: (b, 0)),
                  pl.BlockSpec((PAST_LEN, wv), lambda b, t: (b, 0))],
        out_specs=pl.BlockSpec((TM, wo), lambda b, t: (b * qt + t, 0)),
        out_shape=jax.ShapeDtypeStruct((N_SAMPLE, wo), BF16),
        compiler_params=_params("arbitrary", "arbitrary", vmem=VMEM_LIMIT_BYTES),
        name="attn_sample",
    )(q, k, v, kc, vc))
    return outs


def _gqa_prep_kernel(x_ref, g_ref, sc_ref, sh_ref, w_ref, qg_ref, kg_ref, cos_ref, sin_ref,
                     q_ref, k_ref, v_ref, kp_ref, vp_ref):
    lo = _lane_iota((TM, LANES)) < GQA_HEAD_DIM
    nq = GQA_HEADS * GQA_HEAD_DIM
    nk = GQA_KV_HEADS * GQA_HEAD_DIM

    def pair_norm(xp, gain):
        sq = xp * xp
        ms_lo = jnp.sum(jnp.where(lo, sq, 0.0), axis=-1, keepdims=True)
        ms_hi = jnp.sum(jnp.where(lo, 0.0, sq), axis=-1, keepdims=True)
        ms = jnp.where(lo, ms_lo, ms_hi) * (1.0 / GQA_HEAD_DIM)
        return (xp * lax.rsqrt(ms + RMS_EPS)) * gain

    def body(rope):
        def rotate(xp):
            return xp if rope is None else _rope(xp, rope[0], rope[1], GQA_HEAD_DIM // 4)

        h = _norm_mod(x_ref[...], g_ref[...], sc_ref[0], sh_ref[0])
        qkv = _dot(h.astype(BF16), w_ref[...])
        for p in range(nq // LANES):
            sl = slice(p * LANES, (p + 1) * LANES)
            qn = pair_norm(qkv[:, sl], qg_ref[...])
            q_ref[:, sl] = (rotate(qn) * GQA_SCALE).astype(BF16)
        for p in range(nk // LANES):
            sl = slice(p * LANES, (p + 1) * LANES)
            kn = pair_norm(qkv[:, nq + p * LANES:nq + (p + 1) * LANES], kg_ref[...])
            if rope is None:
                kp_ref[:, sl] = kn
            k_ref[:, sl] = rotate(kn).astype(BF16)
        v = qkv[:, nq + nk:]
        if rope is None:
            vp_ref[...] = v
        v_ref[...] = v.astype(BF16)

    _per_group(lambda: body(None), lambda: body((cos_ref[...], sin_ref[...])))


def _gqa_prep(x, g, sc, sh, w, qg, kg, cos_t, sin_t):
    nq = GQA_HEADS * GQA_HEAD_DIM
    nk = GQA_KV_HEADS * GQA_HEAD_DIM
    return pl.pallas_call(
        _gqa_prep_kernel,
        grid=(NT,),
        in_specs=[_row_spec(D_MODEL), _gain_spec(g), _mod_spec(sc), _mod_spec(sh),
                  _full_spec(w.shape), _full_spec(qg.shape), _full_spec(kg.shape),
                  _rope_spec(), _rope_spec()],
        out_specs=[_row_spec(nq), _row_spec(nk), _row_spec(nk), _context_rows_spec(nk), _context_rows_spec(nk)],
        out_shape=[
            jax.ShapeDtypeStruct((N_TOK, nq), BF16),
            jax.ShapeDtypeStruct((N_TOK, nk), BF16),
            jax.ShapeDtypeStruct((N_TOK, nk), BF16),
            jax.ShapeDtypeStruct((N_PROMPT, nk), F32),
            jax.ShapeDtypeStruct((N_PROMPT, nk), F32),
        ],
        compiler_params=_params("arbitrary", vmem=VMEM_LIMIT_BYTES),
        name="gqa_prep",
    )(x, g[0], sc[0], sh[0], w, qg, kg, cos_t, sin_t)


def _outproj_kernel(*refs, n_in):
    ap_refs = refs[:n_in]
    as_refs = refs[n_in:2 * n_in]
    w_refs = refs[2 * n_in:3 * n_in]
    x_ref, g_ref, o_ref = refs[3 * n_in:3 * n_in + 3]
    w_scr = refs[3 * n_in + 3:]
    is_prompt = pl.program_id(0) < NT_PROMPT

    @pl.when(pl.program_id(0) == 0)
    def _():
        for w, s in zip(w_refs, w_scr):
            s[...] = w[...].astype(BF16)

    y = None
    for ap, asm, s in zip(ap_refs, as_refs, w_scr):
        a = jnp.where(is_prompt, ap[...], asm[...])
        d = _dot(a, s[...])
        y = d if y is None else y + d
    o_ref[...] = x_ref[...] + g_ref[0] * y


def _outproj(acts, ws, x, gate):
    n_in = len(acts)

    def w_spec(index, rows):
        return pl.BlockSpec((None, rows, D_MODEL), lambda i: index)

    def prompt_spec(width):
        return pl.BlockSpec((TM, width), lambda i: (jnp.minimum(i, NT_PROMPT - 1), 0))

    def sample_spec(width):
        return pl.BlockSpec((TM, width), lambda i: (jnp.maximum(i - NT_PROMPT, 0), 0))

    return pl.pallas_call(
        functools.partial(_outproj_kernel, n_in=n_in),
        grid=(NT,),
        in_specs=([prompt_spec(ap.shape[1]) for ap, _ in acts] + [sample_spec(asm.shape[1]) for _, asm in acts]
                  + [w_spec(index, rows) for _, index, rows in ws] + [_row_spec(D_MODEL), _mod_spec(gate)]),
        out_specs=_row_spec(D_MODEL),
        out_shape=jax.ShapeDtypeStruct((N_TOK, D_MODEL), F32),
        scratch_shapes=[pltpu.VMEM((rows, D_MODEL), BF16) for _, _, rows in ws],
        compiler_params=_params("arbitrary", vmem=VMEM_LIMIT_BYTES),
        name="outproj",
    )(*(ap for ap, _ in acts), *(asm for _, asm in acts), *(w for w, _, _ in ws), x, gate[0])


def _router_kernel(x_ref, g_ref, sc_ref, sh_ref, wh_ref, wl_ref, br_ref, tri_ref,
                   h_ref, idx_ref, gate_ref, pos_ref, cnt_ref, run_ref):
    @pl.when(pl.program_id(0) == 0)
    def _():
        run_ref[...] = jnp.zeros_like(run_ref)

    h = _norm_mod(x_ref[...], g_ref[...], sc_ref[0], sh_ref[0])
    h_ref[...] = h.reshape((TM,) + ROW_TILE)
    hh, hl = _split(h)
    logits = _dot(hh, wh_ref[...]) + (_dot(hh, wl_ref[...]) + _dot(hl, wh_ref[...])) + br_ref[...]
    lane = _lane_iota((TM, LANES)).astype(F32)
    neg = jnp.float32(-jnp.inf)
    lg = jnp.where(lane < N_EXPERTS, logits, neg)
    tops, sels, hots = [], [], []
    for _ in range(TOP_K):
        m = lg.max(axis=-1, keepdims=True)
        sel = jnp.where(lg == m, lane, float(LANES)).min(axis=-1, keepdims=True)
        hot = lane == sel
        lg = jnp.where(hot, neg, lg)
        tops.append(m)
        sels.append(sel)
        hots.append(hot)
    es = [jnp.exp(t - tops[0]) for t in tops]
    den = es[0] + es[1] + es[2] + es[3]
    member = jnp.zeros((TM, LANES), F32)
    for hot in hots:
        member = member + hot.astype(F32)
    ranks = _dot(tri_ref[...], member.astype(BF16)) + run_ref[...]
    lane4 = _lane_iota((TM, TOP_K))
    idx4 = jnp.zeros((TM, TOP_K), F32)
    gate4 = jnp.zeros((TM, TOP_K), F32)
    pos4 = jnp.zeros((TM, TOP_K), F32)
    for k in range(TOP_K):
        pk = jnp.sum(jnp.where(hots[k], ranks, 0.0), axis=-1, keepdims=True)
        idx4 = jnp.where(lane4 == k, sels[k], idx4)
        gate4 = jnp.where(lane4 == k, es[k] / den, gate4)
        pos4 = jnp.where(lane4 == k, pk, pos4)
    idx_ref[...] = idx4.astype(jnp.int32)
    gate_ref[...] = gate4
    pos_ref[...] = pos4.astype(jnp.int32)
    run_ref[...] = run_ref[...] + jnp.sum(member, axis=0, keepdims=True)
    cnt_ref[...] = run_ref[...]


def _router_weights(w_router, b_router):
    wpad = jnp.pad(w_router, ((0, 0), (0, 0), (0, LANES - N_EXPERTS)))
    wh = wpad.astype(BF16)
    wl = (wpad - wh.astype(F32)).astype(BF16)
    bpad = jnp.pad(b_router, ((0, 0), (0, LANES - N_EXPERTS))).reshape(DEPTH, 1, LANES)
    return wh, wl, bpad


def _router(layer, x, g, sc, sh, router_weights):
    wh, wl, bpad = router_weights
    r = np.arange(TM)
    tri = jnp.asarray(r[None, :] < r[:, None], dtype=BF16)
    narrow = pl.BlockSpec((TM, TOP_K), lambda i: (i, 0))

    def layer_spec(a):
        return pl.BlockSpec((None,) + a.shape[1:], lambda i: (layer, 0, 0))

    return pl.pallas_call(
        _router_kernel,
        grid=(NT,),
        in_specs=[_row_spec(D_MODEL), _gain_spec(g), _mod_spec(sc), _mod_spec(sh),
                  layer_spec(wh), layer_spec(wl), layer_spec(bpad), _full_spec(tri.shape)],
        out_specs=[pl.BlockSpec((TM,) + ROW_TILE, lambda i: (i, 0, 0)), narrow, narrow, narrow,
                   _full_spec((1, LANES))],
        out_shape=[
            jax.ShapeDtypeStruct((N_TOK,) + ROW_TILE, F32),
            jax.ShapeDtypeStruct((N_TOK, TOP_K), jnp.int32),
            jax.ShapeDtypeStruct((N_TOK, TOP_K), F32),
            jax.ShapeDtypeStruct((N_TOK, TOP_K), jnp.int32),
            jax.ShapeDtypeStruct((1, LANES), F32),
        ],
        scratch_shapes=[pltpu.VMEM((1, LANES), F32)],
        compiler_params=_params("arbitrary", vmem=VMEM_LIMIT_BYTES),
        name="router",
    )(x, g[0], sc[0], sh[0], wh, wl, bpad, tri)


ROW_UNROLL = 4
DMA_QUEUES = 2


def _start_all_rows(row_copy):
    def start_rows(j, carry):
        for u in range(ROW_UNROLL):
            for k in range(TOP_K):
                row_copy(j * ROW_UNROLL + u, k).start(priority=k % DMA_QUEUES)
        return carry

    lax.fori_loop(0, TM // ROW_UNROLL, start_rows, 0)


def _dispatch_kernel(dest_ref, clear_ref, h_ref, xs_ref, zero_ref, sem_ref):
    def row_copy(t, k):
        d = dest_ref[(pl.program_id(0) * TM + t) * TOP_K + k]
        return pltpu.make_async_copy(h_ref.at[pl.ds(t, 1)], xs_ref.at[pl.ds(d, 1)], sem_ref.at[0])

    def zero_half(j):
        row0 = pl.multiple_of(j * MOE_PART, MOE_PART)
        return pltpu.make_async_copy(zero_ref, xs_ref.at[pl.ds(row0, MOE_PART)], sem_ref.at[1])

    @pl.when(pl.program_id(0) == 0)
    def _():
        zero_ref[...] = jnp.zeros_like(zero_ref)

        def start(j, carry):
            @pl.when(clear_ref[j] > 0)
            def _():
                zero_half(j).start()
            return carry

        def wait(j, carry):
            @pl.when(clear_ref[j] > 0)
            def _():
                zero_half(j).wait()
            return carry

        lax.fori_loop(0, MOE_PARTS * N_BLOCKS, start, 0)
        lax.fori_loop(0, MOE_PARTS * N_BLOCKS, wait, 0)

    _start_all_rows(row_copy)
    for _ in range(TOP_K):
        pltpu.make_async_copy(h_ref, xs_ref.at[pl.ds(0, TM)], sem_ref.at[0]).wait()


def _dispatch(dest_flat, clear, h):
    return pl.pallas_call(
        _dispatch_kernel,
        grid_spec=pltpu.PrefetchScalarGridSpec(
            num_scalar_prefetch=2,
            grid=(NT,),
            in_specs=[pl.BlockSpec((TM,) + ROW_TILE, lambda i, d, c: (i, 0, 0))],
            out_specs=pl.BlockSpec(memory_space=pl.ANY),
            scratch_shapes=[pltpu.VMEM((MOE_PART,) + ROW_TILE, F32), pltpu.SemaphoreType.DMA((2,))],
        ),
        out_shape=jax.ShapeDtypeStruct((CAP,) + ROW_TILE, F32),
        compiler_params=_params("arbitrary", vmem=VMEM_LIMIT_BYTES),
        name="moe_dispatch",
    )(dest_flat, clear, h)


def _expert_kernel(be_ref, nh_ref, last_ref, xs_ref, wgu_ref, bgu_ref, wd_ref, bd_ref, ys_ref, wgu_s, wd_s):
    del last_ref
    b = pl.program_id(0)
    nh = nh_ref[b]

    @pl.when((nh > 0) & ((b == 0) | (be_ref[b] != be_ref[jnp.maximum(b - 1, 0)])))
    def _():
        wgu_s[...] = wgu_ref[...].astype(BF16)
        wd_s[...] = wd_ref[...].astype(BF16)

    def ffn(rows):
        x = xs_ref[0:rows].reshape(rows, D_MODEL)
        gu = _dot(x.astype(BF16), wgu_s[...]) + bgu_ref[...]
        g = jnp.minimum(gu[:, :D_EXPERT], SWIGLU_LIMIT)
        u = jnp.clip(gu[:, D_EXPERT:], -SWIGLU_LIMIT, SWIGLU_LIMIT)
        act = (u + 1.0) * (g * jax.nn.sigmoid(SWIGLU_ALPHA * g))
        y = _dot(act.astype(BF16), wd_s[...]) + bd_ref[...]
        ys_ref[0:rows] = y.reshape((rows,) + ROW_TILE)

    def partial_block(parts):
        rows = parts * MOE_PART
        ffn(rows)
        if rows < MOE_BLK:
            ys_ref[rows:] = jnp.zeros((MOE_BLK - rows,) + ROW_TILE, F32)

    for parts in range(1, MOE_PARTS + 1):
        pl.when(nh == parts)(functools.partial(partial_block, parts))

    @pl.when(nh == 0)
    def _():
        ys_ref[...] = jnp.zeros_like(ys_ref)


def _experts(layer, block_e, block_nh, last_used, xs, w_gu, b_gu, w_down, b_down):
    def xs_map(b, be, nh, lu):
        return (jnp.minimum(b, lu[0]), 0, 0)

    def e_map(b, be, nh, lu):
        return (layer, be[b], 0, 0)

    return pl.pallas_call(
        _expert_kernel,
        grid_spec=pltpu.PrefetchScalarGridSpec(
            num_scalar_prefetch=3,
            grid=(N_BLOCKS,),
            in_specs=[
                pl.BlockSpec((MOE_BLK,) + ROW_TILE, xs_map),
                pl.BlockSpec((None, None, D_MODEL, 2 * D_EXPERT), e_map),
                pl.BlockSpec((None, None, 1, 2 * D_EXPERT), e_map),
                pl.BlockSpec((None, None, D_EXPERT, D_MODEL), e_map),
                pl.BlockSpec((None, None, 1, D_MODEL), e_map),
            ],
            out_specs=pl.BlockSpec((MOE_BLK,) + ROW_TILE, lambda b, be, nh, lu: (b, 0, 0)),
            scratch_shapes=[pltpu.VMEM((D_MODEL, 2 * D_EXPERT), BF16), pltpu.VMEM((D_EXPERT, D_MODEL), BF16)],
        ),
        out_shape=jax.ShapeDtypeStruct((CAP,) + ROW_TILE, F32),
        compiler_params=_params("arbitrary", vmem=VMEM_LIMIT_BYTES),
        name="moe_experts",
    )(block_e, block_nh, last_used, xs, w_gu, b_gu.reshape(DEPTH, N_EXPERTS, 1, -1), w_down,
      b_down.reshape(DEPTH, N_EXPERTS, 1, -1))


def _combine_kernel(dest_ref, x_ref, g_ref, gate_ref, ys_ref, *rest, per_group):
    out_refs, (buf_ref, sem_ref) = rest[:-2], rest[-2:]
    i = pl.program_id(0)
    cur = lax.rem(i, 2)

    def start_tile(tile, buf):
        def row_copy(t, k):
            d = dest_ref[(tile * TM + t) * TOP_K + k]
            return pltpu.make_async_copy(ys_ref.at[pl.ds(d, 1)], buf_ref.at[buf, k, pl.ds(t, 1)], sem_ref.at[buf])
        _start_all_rows(row_copy)

    @pl.when(i == 0)
    def _():
        start_tile(0, 0)

    @pl.when(i + 1 < NT)
    def _():
        start_tile(i + 1, 1 - cur)

    for k in range(TOP_K):
        pltpu.make_async_copy(ys_ref.at[pl.ds(0, TM)], buf_ref.at[cur, k], sem_ref.at[cur]).wait()
    gates = gate_ref[...]
    ff = gates[:, 0:1] * buf_ref[cur, 0].reshape(TM, D_MODEL)
    for k in range(1, TOP_K):
        ff = ff + gates[:, k:k + 1] * buf_ref[cur, k].reshape(TM, D_MODEL)
    out = x_ref[...] + g_ref[0] * ff
    if per_group:
        def store(ref):
            ref[...] = out
        _per_group(functools.partial(store, out_refs[0]), functools.partial(store, out_refs[1]))
    else:
        out_refs[0][...] = out


def _combine(dest_flat, x, gate_vec, gates, ys, per_group):
    if per_group:
        out_specs = [_context_rows_spec(D_MODEL),
                     pl.BlockSpec((TM, D_MODEL), lambda i, *_: (jnp.maximum(i - NT_PROMPT, 0), 0))]
        out_shape = [jax.ShapeDtypeStruct((N_PROMPT, D_MODEL), F32), jax.ShapeDtypeStruct((N_SAMPLE, D_MODEL), F32)]
    else:
        out_specs = pl.BlockSpec((TM, D_MODEL), lambda i, d: (i, 0))
        out_shape = jax.ShapeDtypeStruct((N_TOK, D_MODEL), F32)
    return pl.pallas_call(
        functools.partial(_combine_kernel, per_group=per_group),
        grid_spec=pltpu.PrefetchScalarGridSpec(
            num_scalar_prefetch=1,
            grid=(NT,),
            in_specs=[
                pl.BlockSpec((TM, D_MODEL), lambda i, d: (i, 0)),
                _mod_spec(gate_vec),
                pl.BlockSpec((TM, TOP_K), lambda i, d: (i, 0)),
                pl.BlockSpec(memory_space=pl.ANY),
            ],
            out_specs=out_specs,
            scratch_shapes=[pltpu.VMEM((2, TOP_K, TM) + ROW_TILE, F32), pltpu.SemaphoreType.DMA((2,))],
        ),
        out_shape=out_shape,
        compiler_params=_params("arbitrary", vmem=VMEM_LIMIT_BYTES),
        name="moe_combine",
    )(dest_flat, x, gate_vec[0], gates, ys)


def _moe(layer, x, g, sc, sh, gate_vec, router_weights, w_gu, b_gu, w_down, b_down, per_group=False):
    h, idx, gates, pos, counts = _router(layer, x, g, sc, sh, router_weights)
    cnt = counts[0, :N_EXPERTS].astype(jnp.int32)
    nparts = (cnt + MOE_PART - 1) // MOE_PART
    nblk = (nparts + MOE_PARTS - 1) // MOE_PARTS
    e_ids = jnp.arange(N_EXPERTS, dtype=jnp.int32)
    blk_end = jnp.sum(jnp.where(e_ids[None, :] <= e_ids[:, None], nblk[None, :], 0), axis=1)
    blk_start = blk_end - nblk
    last_used = blk_end[-1] - 1

    def per_expert(table, e):
        return jnp.sum(jnp.where(e[..., None] == e_ids, table, 0), axis=-1)

    def expert_of_block(blk):
        return jnp.sum((blk_end <= jnp.minimum(blk, last_used)[:, None]).astype(jnp.int32), axis=1)

    dest = per_expert(blk_start * MOE_BLK, idx) + pos
    b_ids = jnp.arange(N_BLOCKS, dtype=jnp.int32)
    block_e = expert_of_block(b_ids)
    block_nh = jnp.where(
        b_ids <= last_used,
        jnp.clip(per_expert(nparts, block_e) - MOE_PARTS * (b_ids - per_expert(blk_start, block_e)), 0, MOE_PARTS), 0)
    h_ids = jnp.arange(MOE_PARTS * N_BLOCKS, dtype=jnp.int32)
    h_e = expert_of_block(h_ids // MOE_PARTS)
    h_nparts = per_expert(nparts, h_e)
    h_local = h_ids - MOE_PARTS * per_expert(blk_start, h_e)
    holds_rows = (h_ids // MOE_PARTS <= last_used) & (h_local < h_nparts)
    clear = jnp.logical_not(holds_rows) | (h_local == h_nparts - 1)
    dest_flat = dest.reshape(-1).astype(jnp.int32)
    xs = _dispatch(dest_flat, clear.astype(jnp.int32), h)
    ys = _experts(layer, block_e.astype(jnp.int32), block_nh.astype(jnp.int32),
                  last_used.reshape(1).astype(jnp.int32), xs, w_gu, b_gu, w_down, b_down)
    return _combine(dest_flat, x, gate_vec, gates, ys, per_group)


def _rope_tables(d_rot, lane0, period):
    n_rows = DEC_SEQ // GRID_W
    rows = np.repeat(np.arange(n_rows), GRID_W).astype(np.float32)
    cols = np.tile(np.arange(GRID_W), n_rows).astype(np.float32)
    half = d_rot // 2
    lane = np.arange(LANES)
    i = (lane - lane0) % period
    active = (lane >= lane0) & (i < d_rot)
    w = i % half
    f = w % (half // 2)
    pos = np.where((i // half)[None, :] == 0, rows[:, None], cols[:, None])
    sign = np.where(w < half // 2, -1.0, 1.0).astype(np.float32)
    inv = ROPE_THETA ** (-jnp.arange(0, half, 2, dtype=F32) / half)
    ang = jnp.asarray(pos) * inv[f][None, :]
    cos = jnp.where(active[None, :], jnp.cos(ang), 1.0)
    sin = jnp.where(active[None, :], jnp.sin(ang) * sign[None, :], 0.0)
    return cos, sin


def _rope_spec():
    def index(i):
        return (jnp.maximum(i - NT_PROMPT, 0) % TILES_PER_SAMPLE, 0)
    return pl.BlockSpec((TM, LANES), index)


def _pad_heads(w, n_heads, width):
    lead = w.shape[:-1]
    w = w.reshape(lead + (n_heads, width))
    w = jnp.pad(w, [(0, 0)] * len(lead) + [(0, 0), (0, HEAD_PAD - width)])
    return w.reshape(lead + (n_heads * HEAD_PAD,))


_Q_ORDER = (0, 4, 1, 5, 2, 6, 3, 7, 8, 12, 9, 13, 10, 14, 11, 15)


def _perm_q_heads(w, axis):
    shape = w.shape
    n = shape[axis]
    w = jnp.moveaxis(w, axis, 0).reshape((GQA_HEADS, n // GQA_HEADS) + tuple(s for a, s in enumerate(shape) if a != axis))
    w = w[jnp.array(_Q_ORDER)]
    w = w.reshape((n,) + w.shape[2:])
    return jnp.moveaxis(w, 0, axis)


def _constants():
    mla_cos, mla_sin = _rope_tables(MLA_ROPE, MLA_NOPE, LANES)
    gqa_cos, gqa_sin = _rope_tables(GQA_HEAD_DIM, 0, GQA_HEAD_DIM)
    dft = {}
    for L in (SEQ, DEC_SEQ):
        cm, sm = _dft_tables(L)
        fmat = jnp.concatenate([cm, sm], axis=0).astype(BF16)
        ftmat = fmat.T
        dft[L] = (cm, sm, fmat, ftmat)
    return dict(mla=(mla_cos, mla_sin), gqa=(gqa_cos, gqa_sin), dft=dft)


def _even_mixer(x, p, i, gmix, sc1, sh1, g1, consts):
    proj = _normlin(x, gmix, sc1, sh1, p["w_in_ab"], i, IN_AB_PAD)
    y_hy = []
    for L, blk0, nseq, per_step in ((SEQ, 0, BATCH, HY_PROMPT_SEQS), (DEC_SEQ, N_PROMPT // DEC_SEQ, DEC_BATCH, 1)):
        cm, sm, fmat, ftmat = consts["dft"][L]
        kc, ks = _hy_filter(L, cm, sm, p["hy_filter_w1"][i], p["hy_filter_b1"][i], p["hy_filter_freq"][i],
                            p["hy_filter_w2"][i], p["hy_filter_b2"][i], p["hy_filter_w3"][i],
                            p["hy_filter_b3"][i], p["hy_log_decay"][i])
        y_hy.append(_hy_mix(proj, blk0, nseq, per_step, L, p["hy_conv_w"][i], p["hy_conv_b"][i], kc, ks,
                            p["hy_bias"][i], fmat, ftmat))

    wq = _pad_heads(p["mla_wq_b"][i], MLA_HEADS, MLA_QK).astype(BF16)
    wkv = p["mla_wkv_b"][i].astype(BF16)
    qg = jnp.pad(p["mla_q_norm"][i], (0, HEAD_PAD - MLA_QK)).reshape(1, HEAD_PAD)
    kg = jnp.pad(p["mla_k_norm"][i], (0, HEAD_PAD - MLA_QK)).reshape(1, HEAD_PAD)
    mla_cos, mla_sin = consts["mla"]
    q, k, v, lat = _mla_prep(proj, p["mla_q_lora_norm"][i].reshape(1, -1), wq, qg,
                             p["mla_kv_lora_norm"][i].reshape(1, -1), wkv, kg, mla_cos, mla_sin)
    ctx = jnp.pad(p["cache_mla_latent"][:, i].reshape(DEC_BATCH * PAST_LEN, -1),
                  ((0, 0), (0, KV_PAD - MLA_KV_LORA - MLA_ROPE)))
    kc_ctx, vc_ctx = _mla_ctx(ctx, wkv, kg)
    nqk = MLA_HEADS * HEAD_PAD
    o = _attention(_mla_attn_kernel, q, k, v, kc_ctx, vc_ctx, nqk, nqk, nqk, MLA_HEADS * MLA_V)
    w_out = p["w_out_ab"]
    x = _outproj([y_hy, o], [(w_out, (i, 0, 0), HY_D), (w_out, (i, 1, 0), HY_D)], x, g1)
    return x, lat, dict(y_hy=jnp.concatenate(y_hy, axis=0), o=jnp.concatenate(o, axis=0))


def _odd_mixer(x, p, i, gmix, sc1, sh1, g1, consts):
    nq = GQA_HEADS * GQA_HEAD_DIM
    nk = GQA_KV_HEADS * GQA_HEAD_DIM
    w = p["w_qkv_c"][i]
    w_qkv = jnp.concatenate([_perm_q_heads(w[:, :nq], 1), w[:, nq:]], axis=1).astype(BF16)
    qg = jnp.tile(p["gqa_q_norm"][i], LANES // GQA_HEAD_DIM).reshape(1, LANES)
    kg = jnp.tile(p["gqa_k_norm"][i], LANES // GQA_HEAD_DIM).reshape(1, LANES)
    gqa_cos, gqa_sin = consts["gqa"]
    q, k, v, k_plain, v_plain = _gqa_prep(x, gmix, sc1, sh1, w_qkv, qg, kg, gqa_cos, gqa_sin)
    kc_ctx = p["cache_gqa_k"][:, i].reshape(DEC_BATCH * PAST_LEN, -1).astype(BF16)
    vc_ctx = p["cache_gqa_v"][:, i].reshape(DEC_BATCH * PAST_LEN, -1).astype(BF16)
    o = _attention(_gqa_attn_kernel, q, k, v, kc_ctx, vc_ctx, nq, nk, nk, nq)
    w_out = _perm_q_heads(p["w_out_c"][i], 0)[None]
    x = _outproj([o], [(w_out, (0, 0, 0), nq)], x, g1)
    return x, k_plain, v_plain


def kernel(x_prompt, x_sample, cache_mla_latent, cache_gqa_k, cache_gqa_v, c, c_ctx, w_ada, b_ada, norm_mix, norm_ffn, w_in_ab, hy_conv_w, hy_conv_b, hy_filter_w1, hy_filter_b1, hy_filter_freq, hy_filter_w2, hy_filter_b2, hy_filter_w3, hy_filter_b3, hy_log_decay, hy_bias, mla_q_lora_norm, mla_wq_b, mla_kv_lora_norm, mla_wkv_b, mla_q_norm, mla_k_norm, w_out_ab, w_qkv_c, gqa_q_norm, gqa_k_norm, w_out_c, moe_router_w, moe_router_b, moe_w_gate_up, moe_b_gate_up, moe_w_down, moe_b_down):
    p = dict(locals())
    x = jnp.concatenate([x_prompt.reshape(N_PROMPT, D_MODEL), x_sample.reshape(N_SAMPLE, D_MODEL)], axis=0)

    cond = jnp.concatenate([c_ctx[None, :], c, jnp.zeros((COND_ROWS - N_COND, D_MODEL), F32)], axis=0)
    mods = _modulation(cond, w_ada, b_ada)
    tile_cond = jnp.concatenate([jnp.zeros((NT_PROMPT,), jnp.int32),
                                 1 + jnp.arange(NT - NT_PROMPT, dtype=jnp.int32) // TILES_PER_SAMPLE])
    mods = mods[:, tile_cond].reshape(DEPTH, NT, 6, 1, D_MODEL).transpose(0, 2, 1, 3, 4)

    consts = _constants()
    router_weights = _router_weights(moe_router_w, moe_router_b)
    gains_mix = norm_mix.reshape(DEPTH, 1, D_MODEL)
    gains_ffn = norm_ffn.reshape(DEPTH, 1, D_MODEL)

    lat_out, k_out, v_out = [], [], []
    for l in range(DEPTH):
        sh1, sc1, g1, sh2, sc2, g2 = ((mods, l, j) for j in range(6))
        i = l // 2
        gmix = (gains_mix, l)
        if l % 2 == 0:
            x, lat, _ = _even_mixer(x, p, i, gmix, sc1, sh1, g1, consts)
            lat_out.append(lat.reshape(BATCH, SEQ, -1))
        else:
            x, k_plain, v_plain = _odd_mixer(x, p, i, gmix, sc1, sh1, g1, consts)
            k_out.append(k_plain.reshape(BATCH, SEQ, GQA_KV_HEADS, GQA_HEAD_DIM))
            v_out.append(v_plain.reshape(BATCH, SEQ, GQA_KV_HEADS, GQA_HEAD_DIM))
        x = _moe(l, x, (gains_ffn, l), sc2, sh2, g2, router_weights,
                 moe_w_gate_up, moe_b_gate_up, moe_w_down, moe_b_down, per_group=(l == DEPTH - 1))

    y_prompt = x[0].reshape(BATCH, SEQ, D_MODEL)
    y_sample = x[1].reshape(DEC_BATCH, DEC_SEQ, D_MODEL)
    return (y_prompt, y_sample, jnp.stack(lat_out, axis=1), jnp.stack(k_out, axis=1), jnp.stack(v_out, axis=1))
```

```python
import functools
import math

import jax
import jax.numpy as jnp
import numpy as np
from jax import lax
from jax.experimental import pallas as pl
from jax.experimental.pallas import tpu as pltpu

F32 = jnp.float32
BF16 = jnp.bfloat16

D_MODEL = 1024
BATCH = 32
SEQ = 256
DEPTH = 4
DEC_BATCH = 4
DEC_SEQ = 1024
PAST_LEN = 256
GRID_W = 64
N_EVEN = (DEPTH + 1) // 2
N_ODD = DEPTH // 2
HY_D = D_MODEL // 2
HY_ORDER = 2
HY_BANDS = 16
HY_EMB = 2 * HY_BANDS + 1
HY_FILTER_HIDDEN = 64
MLA_HEADS = 8
MLA_NOPE = 64
MLA_ROPE = 32
MLA_QK = MLA_NOPE + MLA_ROPE
MLA_V = HY_D // MLA_HEADS
MLA_Q_LORA = 3 * D_MODEL // 8
MLA_KV_LORA = D_MODEL // 4
GQA_HEADS = 16
GQA_KV_HEADS = 4
GQA_HEAD_DIM = D_MODEL // GQA_HEADS
N_EXPERTS = 32
TOP_K = 4
D_EXPERT = D_MODEL
SWIGLU_LIMIT = 7.0
SWIGLU_ALPHA = 1.702
ROPE_THETA = 10000.0
RMS_EPS = 1e-6
GQA_SCALE = GQA_HEAD_DIM ** -0.5
assert math.frexp(GQA_SCALE)[0] == 0.5
IN_AB = (HY_ORDER + 1) * HY_D + MLA_Q_LORA + MLA_KV_LORA + MLA_ROPE

N_PROMPT = BATCH * SEQ
N_SAMPLE = DEC_BATCH * DEC_SEQ
N_TOK = N_PROMPT + N_SAMPLE

LANES = 128
SUBLANES = 8
VMEM_LIMIT_BYTES = 56 * 1024 * 1024

TM = 256
NT = N_TOK // TM
NT_PROMPT = N_PROMPT // TM
TILES_PER_SAMPLE = DEC_SEQ // TM
N_COND = 1 + DEC_BATCH
COND_ROWS = 8
HEAD_PAD = LANES
IN_AB_PAD = 2304
KV_PAD = 384
MOE_BLK = 512
MOE_PARTS = 4
MOE_PART = MOE_BLK // MOE_PARTS
N_SLOTS = N_TOK * TOP_K
N_BLOCKS = N_SLOTS // MOE_BLK + N_EXPERTS
CAP = N_BLOCKS * MOE_BLK
ROW_TILE = (SUBLANES, LANES)
assert SUBLANES * LANES == D_MODEL
HY_CH = 256
HY_PROMPT_SEQS = 8


def _dot(a, b):
    return jnp.dot(a, b, preferred_element_type=F32)


def _dot_nt(a, b):
    return lax.dot_general(a, b, (((1,), (1,)), ((), ())), preferred_element_type=F32)


def _split(x):
    hi = x.astype(BF16)
    lo = (x - hi.astype(F32)).astype(BF16)
    return hi, lo


def _dot3(a, b):
    ah, al = _split(a)
    bh, bl = _split(b)
    return _dot(ah, bh) + (_dot(ah, bl) + _dot(al, bh))


def _lane_iota(shape):
    return lax.broadcasted_iota(jnp.int32, shape, len(shape) - 1)


def _params(*sem, vmem=None):
    return pltpu.CompilerParams(dimension_semantics=sem, vmem_limit_bytes=vmem)


def _mod_kernel(c_ref, w_ref, b_ref, o_ref):
    c = c_ref[...]
    s = c * jax.nn.sigmoid(c)
    o_ref[0] = _dot(s.astype(BF16), w_ref[0].astype(BF16)) + b_ref[0]


def _modulation(cond, w_ada, b_ada):
    nblk = 6
    return pl.pallas_call(
        _mod_kernel,
        grid=(DEPTH, nblk),
        in_specs=[
            pl.BlockSpec((COND_ROWS, D_MODEL), lambda l, j: (0, 0)),
            pl.BlockSpec((1, D_MODEL, D_MODEL), lambda l, j: (l, 0, j)),
            pl.BlockSpec((1, 1, D_MODEL), lambda l, j: (l, 0, j)),
        ],
        out_specs=pl.BlockSpec((1, COND_ROWS, D_MODEL), lambda l, j: (l, 0, j)),
        out_shape=jax.ShapeDtypeStruct((DEPTH, COND_ROWS, nblk * D_MODEL), F32),
        compiler_params=_params("arbitrary", "arbitrary"),
        name="modulation",
    )(cond, w_ada, b_ada.reshape(DEPTH, 1, nblk * D_MODEL))


def _norm_mod(x, g, sc, sh):
    ms = jnp.mean(x * x, axis=-1, keepdims=True)
    y = x * lax.rsqrt(ms + RMS_EPS)
    return (y * g) * (1.0 + sc) + sh


def _row_spec(width):
    return pl.BlockSpec((TM, width), lambda i: (i, 0))


def _mod_spec(mod):
    _, layer, which = mod
    return pl.BlockSpec((None, None, 1, 1, D_MODEL), lambda i, *_: (layer, which, i, 0, 0))


def _gain_spec(gain):
    _, layer = gain
    return pl.BlockSpec((None, 1, D_MODEL), lambda i, *_: (layer, 0, 0))


def _full_spec(shape):
    nd = len(shape)
    return pl.BlockSpec(shape, lambda i: (0,) * nd)


def _normlin_kernel(x_ref, g_ref, sc_ref, sh_ref, w_ref, o_ref, w_scr):
    nin = w_ref.shape[1]

    @pl.when(pl.program_id(0) == 0)
    def _():
        w_scr[:, :nin] = w_ref[...].astype(BF16)
        w_scr[:, nin:] = jnp.zeros((D_MODEL, w_scr.shape[1] - nin), BF16)

    h = _norm_mod(x_ref[...], g_ref[...], sc_ref[0], sh_ref[0])
    o_ref[...] = _dot(h.astype(BF16), w_scr[...])


def _normlin(x, g, sc, sh, w_all, layer, nout):
    nin = w_all.shape[2]
    return pl.pallas_call(
        _normlin_kernel,
        grid=(NT,),
        in_specs=[_row_spec(D_MODEL), _gain_spec(g), _mod_spec(sc), _mod_spec(sh),
                  pl.BlockSpec((None, D_MODEL, nin), lambda i: (layer, 0, 0))],
        out_specs=_row_spec(nout),
        out_shape=jax.ShapeDtypeStruct((N_TOK, nout), F32),
        scratch_shapes=[pltpu.VMEM((D_MODEL, nout), BF16)],
        compiler_params=_params("arbitrary", vmem=VMEM_LIMIT_BYTES),
        name="normlin",
    )(x, g[0], sc[0], sh[0], w_all)


def _hy_filter_kernel(z_ref, w1_ref, b1_ref, fr_ref, w2_ref, b2_ref, w3_ref, b3_ref, ed_ref,
                      c_ref, s_ref, kc_ref, ks_ref):
    L = z_ref.shape[0]
    z = z_ref[...]
    fr = fr_ref[...]
    hdn = jnp.sin(fr * (_dot3(z, w1_ref[...]) + b1_ref[...]))
    hdn = jnp.sin(fr * (_dot3(hdn, w2_ref[...]) + b2_ref[...]))
    filt = _dot3(hdn, w3_ref[...]) + b3_ref[...]
    t = z[:, 0:1]
    filt = filt * jnp.exp(-t * ed_ref[...])
    row = lax.broadcasted_iota(jnp.int32, (L, HY_D), 0)
    cm = c_ref[...]
    sm = s_ref[...]
    for o in range(HY_ORDER):
        fw = filt[:, (2 * o) * HY_D:(2 * o + 1) * HY_D]
        bw = filt[:, (2 * o + 1) * HY_D:(2 * o + 2) * HY_D]
        den = (jnp.sum(jnp.abs(fw), axis=0, keepdims=True)
               + jnp.sum(jnp.abs(bw), axis=0, keepdims=True)) + 1e-6
        fw = fw / den
        bw = jnp.where(row == 0, 0.0, bw / den)
        kc_ref[o] = _dot3(cm, fw + bw) * (1.0 / L)
        ks_ref[o] = _dot3(sm, fw - bw) * (1.0 / L)


def _dft_tables(L):
    m = jnp.arange(L, dtype=jnp.int32)
    phase = ((2 * m + 1)[:, None] * m[None, :]) % (4 * L)
    ang = phase.astype(F32) * (2.0 * math.pi / (4 * L))
    return jnp.cos(ang), jnp.sin(ang)


def _filter_features(L):
    p = jnp.arange(L, dtype=F32)
    t = p / max(L - 1, 1)
    bands = jnp.linspace(1e-4, HY_BANDS - 1, HY_BANDS, dtype=F32)
    ang = (2.0 * math.pi / L) * p[:, None] * bands[None, :]
    z = jnp.concatenate([t[:, None], jnp.cos(ang), -jnp.sin(ang)], axis=-1)
    return jnp.pad(z, ((0, 0), (0, LANES - HY_EMB)))


def _hy_filter(L, cmat, smat, w1, b1, fr, w2, b2, w3, b3, log_decay):
    nf = HY_ORDER * 2 * HY_D
    args = (
        _filter_features(L),
        jnp.pad(w1, ((0, LANES - HY_EMB), (0, 0))),
        b1.reshape(1, -1), fr.reshape(1, -1), w2, b2.reshape(1, -1), w3, b3.reshape(1, -1),
        jnp.exp(log_decay.astype(F32)).reshape(1, nf),
        cmat, smat,
    )
    out_sds = jax.ShapeDtypeStruct((HY_ORDER, L, HY_D), F32)
    return pl.pallas_call(
        _hy_filter_kernel,
        grid=(1,),
        in_specs=[_full_spec(a.shape) for a in args],
        out_specs=[_full_spec(out_sds.shape)] * 2,
        out_shape=[out_sds, out_sds],
        compiler_params=_params("arbitrary", vmem=VMEM_LIMIT_BYTES),
        name=f"hy_filter_{L}",
    )(*args)


def _hy_mix_kernel(u0_ref, u1_ref, u2_ref, cw0_ref, cw1_ref, cw2_ref, cb0_ref, cb1_ref, cb2_ref,
                   kc_ref, ks_ref, hb_ref, f_ref, ft_ref, o_ref):
    S, L, _ = u0_ref.shape
    row = lax.broadcasted_iota(jnp.int32, (L, HY_CH), 0)

    def lanes(per_seq):
        return per_seq[0] if S == 1 else jnp.concatenate(per_seq, axis=1)

    def short_conv(u_ref, cw_ref, cb_ref):
        w = cw_ref[0]
        out = []
        for s in range(S):
            u = u_ref[s]
            prev = jnp.where(row == 0, 0.0, pltpu.roll(u, 1, 0))
            nxt = jnp.where(row == L - 1, 0.0, pltpu.roll(u, L - 1, 0))
            out.append((prev * w[0:1] + u * w[1:2]) + nxt * w[2:3] + cb_ref[0])
        return lanes(out)

    z = short_conv(u0_ref, cw0_ref, cb0_ref)
    gates = (short_conv(u1_ref, cw1_ref, cb1_ref), short_conv(u2_ref, cw2_ref, cb2_ref))
    for o in range(HY_ORDER):
        zz = _dot(f_ref[...], z.astype(BF16))
        cz, sz = zz[:L], zz[L:]
        kc, ks = lanes([kc_ref[o]] * S), lanes([ks_ref[o]] * S)
        w1 = cz * kc - sz * ks
        w2 = cz * ks + sz * kc
        ww = jnp.concatenate([w1, w2], axis=0).astype(BF16)
        conv = _dot(ft_ref[...], ww)
        z = gates[o] * (conv + z * lanes([hb_ref[0, o]] * S))
    for s in range(S):
        o_ref[s] = z[:, s * HY_CH:(s + 1) * HY_CH].astype(o_ref.dtype)


def _hy_mix(proj, row_block0, nseq, seqs_per_step, L, cw, cb, kc, ks, hb, fmat, ftmat):
    nch = HY_D // HY_CH
    nparts = HY_ORDER + 1
    S = seqs_per_step
    assert nseq % S == 0 and row_block0 % S == 0
    cw3 = cw.reshape(3, nparts * nch, HY_CH).transpose(1, 0, 2)
    cb3 = cb.reshape(nparts * nch, 1, HY_CH)
    hb3 = hb.reshape(HY_ORDER, nch, 1, HY_CH).transpose(1, 0, 2, 3)
    proj = proj.reshape(N_TOK // L, L, proj.shape[1])

    def u_spec(part):
        return pl.BlockSpec((S, L, HY_CH), lambda s, c: (row_block0 // S + s, 0, part * nch + c))

    def cw_spec(part):
        return pl.BlockSpec((1, 3, HY_CH), lambda s, c: (part * nch + c, 0, 0))

    def cb_spec(part):
        return pl.BlockSpec((1, 1, HY_CH), lambda s, c: (part * nch + c, 0, 0))

    return pl.pallas_call(
        _hy_mix_kernel,
        grid=(nseq // S, nch),
        in_specs=[
            u_spec(0), u_spec(1), u_spec(2),
            cw_spec(0), cw_spec(1), cw_spec(2),
            cb_spec(0), cb_spec(1), cb_spec(2),
            pl.BlockSpec((HY_ORDER, L, HY_CH), lambda s, c: (0, 0, c)),
            pl.BlockSpec((HY_ORDER, L, HY_CH), lambda s, c: (0, 0, c)),
            pl.BlockSpec((1, HY_ORDER, 1, HY_CH), lambda s, c: (c, 0, 0, 0)),
            pl.BlockSpec((2 * L, L), lambda s, c: (0, 0)),
            pl.BlockSpec((L, 2 * L), lambda s, c: (0, 0)),
        ],
        out_specs=pl.BlockSpec((S, L, HY_CH), lambda s, c: (s, 0, c)),
        out_shape=jax.ShapeDtypeStruct((nseq, L, HY_D), BF16),
        compiler_params=_params("arbitrary", "arbitrary", vmem=VMEM_LIMIT_BYTES),
        name=f"hy_mix_{L}",
    )(proj, proj, proj, cw3, cw3, cw3, cb3, cb3, cb3, kc, ks, hb3, fmat, ftmat).reshape(nseq * L, HY_D)


def _head_rms(xh, gain, dim):
    ms = jnp.sum(xh * xh, axis=-1, keepdims=True) * (1.0 / dim)
    return (xh * lax.rsqrt(ms + RMS_EPS)) * gain


def _context_rows_spec(width):
    return pl.BlockSpec((TM, width), lambda i, *_: (jnp.minimum(i, NT_PROMPT - 1), 0))


def _per_group(context_fn, latent_fn):
    is_context = pl.program_id(0) < NT_PROMPT
    pl.when(is_context)(context_fn)
    pl.when(jnp.logical_not(is_context))(latent_fn)


def _rope(xh, cos, sin, half):
    lane = _lane_iota(xh.shape)
    first = (lane % (2 * half)) < half
    rot = jnp.where(first, pltpu.roll(xh, LANES - half, 1), pltpu.roll(xh, half, 1))
    return xh * cos + rot * sin


def _mla_keys_values(lat_n, kr_blk, wkv_ref, kg_ref, rope):
    kv = _dot(lat_n.astype(BF16), wkv_ref[...])
    kr = pltpu.roll(kr_blk, MLA_NOPE, 1)
    nope = _lane_iota(kr.shape) < MLA_NOPE
    ks = []
    for h in range(MLA_HEADS):
        kh = jnp.where(nope, kv[:, h * HEAD_PAD:(h + 1) * HEAD_PAD], 0.0) + kr
        kh = _head_rms(kh, kg_ref[...], MLA_QK)
        if rope is not None:
            kh = _rope(kh, rope[0], rope[1], MLA_ROPE // 4)
        ks.append(kh)
    return ks, kv


def _mla_prep_kernel(qa_ref, kva_ref, gq_ref, wq_ref, qg_ref, gkv_ref, wkv_ref, kg_ref,
                     cos_ref, sin_ref, q_ref, k_ref, v_ref, lat_ref):
    def body(rope):
        qa = qa_ref[...]
        ms = jnp.mean(qa * qa, axis=-1, keepdims=True)
        qn = (qa * lax.rsqrt(ms + RMS_EPS)) * gq_ref[...]
        q = _dot(qn.astype(BF16), wq_ref[...])
        for h in range(MLA_HEADS):
            qh = _head_rms(q[:, h * HEAD_PAD:(h + 1) * HEAD_PAD], qg_ref[...], MLA_QK)
            if rope is not None:
                qh = _rope(qh, rope[0], rope[1], MLA_ROPE // 4)
            q_ref[:, h * HEAD_PAD:(h + 1) * HEAD_PAD] = qh.astype(BF16)

        kva = kva_ref[...]
        lat = kva[:, :MLA_KV_LORA]
        ms = jnp.mean(lat * lat, axis=-1, keepdims=True)
        lat_n = (lat * lax.rsqrt(ms + RMS_EPS)) * gkv_ref[...]
        kr_blk = kva[:, MLA_KV_LORA:]
        if rope is None:
            lat_ref[:, :MLA_KV_LORA] = lat_n
            lat_ref[:, MLA_KV_LORA:] = kr_blk[:, :MLA_ROPE]
        ks, v = _mla_keys_values(lat_n, kr_blk, wkv_ref, kg_ref, rope)
        for h in range(MLA_HEADS):
            k_ref[:, h * HEAD_PAD:(h + 1) * HEAD_PAD] = ks[h].astype(BF16)
        v_ref[...] = v.astype(BF16)

    _per_group(lambda: body(None), lambda: body((cos_ref[...], sin_ref[...])))


def _mla_prep(proj, gq, wq, qg, gkv, wkv, kg, cos_t, sin_t):
    nq = MLA_HEADS * HEAD_PAD
    nv = MLA_HEADS * (MLA_NOPE + MLA_V)
    qa_blk = (HY_ORDER + 1) * HY_D // KV_PAD
    return pl.pallas_call(
        _mla_prep_kernel,
        grid=(NT,),
        in_specs=[
            pl.BlockSpec((TM, KV_PAD), lambda i: (i, qa_blk)),
            pl.BlockSpec((TM, KV_PAD), lambda i: (i, qa_blk + 1)),
            _full_spec(gq.shape), _full_spec(wq.shape), _full_spec(qg.shape), _full_spec(gkv.shape),
            _full_spec(wkv.shape), _full_spec(kg.shape),
            _rope_spec(), _rope_spec(),
        ],
        out_specs=[_row_spec(nq), _row_spec(nq), _row_spec(nv), _context_rows_spec(MLA_KV_LORA + MLA_ROPE)],
        out_shape=[
            jax.ShapeDtypeStruct((N_TOK, nq), BF16),
            jax.ShapeDtypeStruct((N_TOK, nq), BF16),
            jax.ShapeDtypeStruct((N_TOK, nv), BF16),
            jax.ShapeDtypeStruct((N_PROMPT, MLA_KV_LORA + MLA_ROPE), F32),
        ],
        compiler_params=_params("arbitrary", vmem=VMEM_LIMIT_BYTES),
        name="mla_prep",
    )(proj, proj, gq, wq, qg, gkv, wkv, kg, cos_t, sin_t)


def _mla_ctx_kernel(lat_ref, wkv_ref, kg_ref, k_ref, v_ref):
    lat = lat_ref[...]
    ks, v = _mla_keys_values(lat[:, :MLA_KV_LORA], lat[:, MLA_KV_LORA:], wkv_ref, kg_ref, None)
    for h in range(MLA_HEADS):
        k_ref[:, h * HEAD_PAD:(h + 1) * HEAD_PAD] = ks[h].astype(BF16)
    v_ref[...] = v.astype(BF16)


def _mla_ctx(lat_pad, wkv, kg):
    n = lat_pad.shape[0]
    nq = MLA_HEADS * HEAD_PAD
    nv = MLA_HEADS * (MLA_NOPE + MLA_V)
    return pl.pallas_call(
        _mla_ctx_kernel,
        grid=(n // TM,),
        in_specs=[_row_spec(KV_PAD), _full_spec(wkv.shape), _full_spec(kg.shape)],
        out_specs=[_row_spec(nq), _row_spec(nv)],
        out_shape=[jax.ShapeDtypeStruct((n, nq), BF16), jax.ShapeDtypeStruct((n, nv), BF16)],
        compiler_params=_params("arbitrary"),
        name="mla_ctx",
    )(lat_pad, wkv, kg)


def _softmax_pv(q, keys, vals, scale):
    ss = [_dot_nt(q, k) for k in keys]
    if scale is not None:
        ss = [s * scale for s in ss]
    m = ss[0].max(axis=-1, keepdims=True)
    for s in ss[1:]:
        m = jnp.maximum(m, s.max(axis=-1, keepdims=True))
    ps = [jnp.exp(s - m) for s in ss]
    l = ps[0].sum(axis=-1, keepdims=True)
    for p in ps[1:]:
        l = l + p.sum(axis=-1, keepdims=True)
    o = _dot(ps[0].astype(BF16), vals[0])
    for p, v in zip(ps[1:], vals[1:]):
        o = o + _dot(p.astype(BF16), v)
    return o / l


def _mla_attn_kernel(*refs, has_ctx):
    if has_ctx:
        q_ref, k_ref, v_ref, kc_ref, vc_ref, o_ref = refs
    else:
        q_ref, k_ref, v_ref, o_ref = refs
    scale = MLA_QK ** -0.5
    lo = _lane_iota((q_ref.shape[0], LANES)) < MLA_V
    for j in range(MLA_HEADS // 2):
        outs = []
        for h in (2 * j, 2 * j + 1):
            sl = slice(h * HEAD_PAD, (h + 1) * HEAD_PAD)
            ks, vs = [k_ref[:, sl]], [v_ref[:, sl]]
            if has_ctx:
                ks.append(kc_ref[:, sl])
                vs.append(vc_ref[:, sl])
            outs.append(_softmax_pv(q_ref[:, sl], ks, vs, scale))
        pair = jnp.where(lo, pltpu.roll(outs[0], MLA_V, 1), outs[1])
        o_ref[:, j * LANES:(j + 1) * LANES] = pair.astype(o_ref.dtype)


def _gqa_attn_kernel(*refs, has_ctx):
    if has_ctx:
        q_ref, k_ref, v_ref, kc_ref, vc_ref, o_ref = refs
    else:
        q_ref, k_ref, v_ref, o_ref = refs
    scale = None
    lo = _lane_iota((q_ref.shape[0], LANES)) < GQA_HEAD_DIM
    pairs_per_kv = (GQA_HEADS // 2) // (GQA_KV_HEADS // 2)
    for p in range(GQA_HEADS // 2):
        kv = slice((p // pairs_per_kv) * LANES, (p // pairs_per_kv + 1) * LANES)
        ks, vs = [k_ref[:, kv]], [v_ref[:, kv]]
        if has_ctx:
            ks.append(kc_ref[:, kv])
            vs.append(vc_ref[:, kv])
        qp = q_ref[:, p * LANES:(p + 1) * LANES]
        zero = jnp.zeros_like(qp)
        o_lo = _softmax_pv(jnp.where(lo, qp, zero), ks, vs, scale)
        o_hi = _softmax_pv(jnp.where(lo, zero, qp), ks, vs, scale)
        o_ref[:, p * LANES:(p + 1) * LANES] = jnp.where(lo, o_lo, o_hi).astype(o_ref.dtype)


def _attention(body, q, k, v, kc, vc, wq, wk, wv, wo):
    outs = []
    outs.append(pl.pallas_call(
        functools.partial(body, has_ctx=False),
        grid=(BATCH,),
        in_specs=[pl.BlockSpec((SEQ, wq), lambda b: (b, 0)),
                  pl.BlockSpec((SEQ, wk), lambda b: (b, 0)),
                  pl.BlockSpec((SEQ, wv), lambda b: (b, 0))],
        out_specs=pl.BlockSpec((SEQ, wo), lambda b: (b, 0)),
        out_shape=jax.ShapeDtypeStruct((N_PROMPT, wo), BF16),
        compiler_params=_params("arbitrary", vmem=VMEM_LIMIT_BYTES),
        name="attn_prompt",
    )(q, k, v))
    qt = DEC_SEQ // TM
    q0 = N_PROMPT // TM
    s0 = N_PROMPT // DEC_SEQ
    outs.append(pl.pallas_call(
        functools.partial(body, has_ctx=True),
        grid=(DEC_BATCH, qt),
        in_specs=[pl.BlockSpec((TM, wq), lambda b, t: (q0 + b * qt + t, 0)),
                  pl.BlockSpec((DEC_SEQ, wk), lambda b, t: (s0 + b, 0)),
                  pl.BlockSpec((DEC_SEQ, wv), lambda b, t: (s0 + b, 0)),
                  pl.BlockSpec((PAST_LEN, wk), lambda b, t: (b, 0)),
                  pl.BlockSpec((PAST_LEN, wv), lambda b, t: (b, 0))],
        out_specs=pl.BlockSpec((TM, wo), lambda b, t: (b * qt + t, 0)),
        out_shape=jax.ShapeDtypeStruct((N_SAMPLE, wo), BF16),
        compiler_params=_params("arbitrary", "arbitrary", vmem=VMEM_LIMIT_BYTES),
        name="attn_sample",
    )(q, k, v, kc, vc))
    return outs


def _gqa_prep_kernel(x_ref, g_ref, sc_ref, sh_ref, w_ref, qg_ref, kg_ref, cos_ref, sin_ref,
                     q_ref, k_ref, v_ref, kp_ref, vp_ref):
    lo = _lane_iota((TM, LANES)) < GQA_HEAD_DIM
    nq = GQA_HEADS * GQA_HEAD_DIM
    nk = GQA_KV_HEADS * GQA_HEAD_DIM

    def pair_norm(xp, gain):
        sq = xp * xp
        ms_lo = jnp.sum(jnp.where(lo, sq, 0.0), axis=-1, keepdims=True)
        ms_hi = jnp.sum(jnp.where(lo, 0.0, sq), axis=-1, keepdims=True)
        ms = jnp.where(lo, ms_lo, ms_hi) * (1.0 / GQA_HEAD_DIM)
        return (xp * lax.rsqrt(ms + RMS_EPS)) * gain

    def body(rope):
        def rotate(xp):
            return xp if rope is None else _rope(xp, rope[0], rope[1], GQA_HEAD_DIM // 4)

        h = _norm_mod(x_ref[...], g_ref[...], sc_ref[0], sh_ref[0])
        qkv = _dot(h.astype(BF16), w_ref[...])
        for p in range(nq // LANES):
            sl = slice(p * LANES, (p + 1) * LANES)
            qn = pair_norm(qkv[:, sl], qg_ref[...])
            q_ref[:, sl] = (rotate(qn) * GQA_SCALE).astype(BF16)
        for p in range(nk // LANES):
            sl = slice(p * LANES, (p + 1) * LANES)
            kn = pair_norm(qkv[:, nq + p * LANES:nq + (p + 1) * LANES], kg_ref[...])
            if rope is None:
                kp_ref[:, sl] = kn
            k_ref[:, sl] = rotate(kn).astype(BF16)
        v = qkv[:, nq + nk:]
        if rope is None:
            vp_ref[...] = v
        v_ref[...] = v.astype(BF16)

    _per_group(lambda: body(None), lambda: body((cos_ref[...], sin_ref[...])))


def _gqa_prep(x, g, sc, sh, w, qg, kg, cos_t, sin_t):
    nq = GQA_HEADS * GQA_HEAD_DIM
    nk = GQA_KV_HEADS * GQA_HEAD_DIM
    return pl.pallas_call(
        _gqa_prep_kernel,
        grid=(NT,),
        in_specs=[_row_spec(D_MODEL), _gain_spec(g), _mod_spec(sc), _mod_spec(sh),
                  _full_spec(w.shape), _full_spec(qg.shape), _full_spec(kg.shape),
                  _rope_spec(), _rope_spec()],
        out_specs=[_row_spec(nq), _row_spec(nk), _row_spec(nk), _context_rows_spec(nk), _context_rows_spec(nk)],
        out_shape=[
            jax.ShapeDtypeStruct((N_TOK, nq), BF16),
            jax.ShapeDtypeStruct((N_TOK, nk), BF16),
            jax.ShapeDtypeStruct((N_TOK, nk), BF16),
            jax.ShapeDtypeStruct((N_PROMPT, nk), F32),
            jax.ShapeDtypeStruct((N_PROMPT, nk), F32),
        ],
        compiler_params=_params("arbitrary", vmem=VMEM_LIMIT_BYTES),
        name="gqa_prep",
    )(x, g[0], sc[0], sh[0], w, qg, kg, cos_t, sin_t)


def _outproj_kernel(*refs, n_in):
    ap_refs = refs[:n_in]
    as_refs = refs[n_in:2 * n_in]
    w_refs = refs[2 * n_in:3 * n_in]
    x_ref, g_ref, o_ref = refs[3 * n_in:3 * n_in + 3]
    w_scr = refs[3 * n_in + 3:]
    is_prompt = pl.program_id(0) < NT_PROMPT

    @pl.when(pl.program_id(0) == 0)
    def _():
        for w, s in zip(w_refs, w_scr):
            s[...] = w[...].astype(BF16)

    y = None
    for ap, asm, s in zip(ap_refs, as_refs, w_scr):
        a = jnp.where(is_prompt, ap[...], asm[...])
        d = _dot(a, s[...])
        y = d if y is None else y + d
    o_ref[...] = x_ref[...] + g_ref[0] * y


def _outproj(acts, ws, x, gate):
    n_in = len(acts)

    def w_spec(index, rows):
        return pl.BlockSpec((None, rows, D_MODEL), lambda i: index)

    def prompt_spec(width):
        return pl.BlockSpec((TM, width), lambda i: (jnp.minimum(i, NT_PROMPT - 1), 0))

    def sample_spec(width):
        return pl.BlockSpec((TM, width), lambda i: (jnp.maximum(i - NT_PROMPT, 0), 0))

    return pl.pallas_call(
        functools.partial(_outproj_kernel, n_in=n_in),
        grid=(NT,),
        in_specs=([prompt_spec(ap.shape[1]) for ap, _ in acts] + [sample_spec(asm.shape[1]) for _, asm in acts]
                  + [w_spec(index, rows) for _, index, rows in ws] + [_row_spec(D_MODEL), _mod_spec(gate)]),
        out_specs=_row_spec(D_MODEL),
        out_shape=jax.ShapeDtypeStruct((N_TOK, D_MODEL), F32),
        scratch_shapes=[pltpu.VMEM((rows, D_MODEL), BF16) for _, _, rows in ws],
        compiler_params=_params("arbitrary", vmem=VMEM_LIMIT_BYTES),
        name="outproj",
    )(*(ap for ap, _ in acts), *(asm for _, asm in acts), *(w for w, _, _ in ws), x, gate[0])


def _router_kernel(x_ref, g_ref, sc_ref, sh_ref, wh_ref, wl_ref, br_ref, tri_ref,
                   h_ref, idx_ref, gate_ref, pos_ref, cnt_ref, run_ref):
    @pl.when(pl.program_id(0) == 0)
    def _():
        run_ref[...] = jnp.zeros_like(run_ref)

    h = _norm_mod(x_ref[...], g_ref[...], sc_ref[0], sh_ref[0])
    h_ref[...] = h.reshape((TM,) + ROW_TILE)
    hh, hl = _split(h)
    logits = _dot(hh, wh_ref[...]) + (_dot(hh, wl_ref[...]) + _dot(hl, wh_ref[...])) + br_ref[...]
    lane = _lane_iota((TM, LANES)).astype(F32)
    neg = jnp.float32(-jnp.inf)
    lg = jnp.where(lane < N_EXPERTS, logits, neg)
    tops, sels, hots = [], [], []
    for _ in range(TOP_K):
        m = lg.max(axis=-1, keepdims=True)
        sel = jnp.where(lg == m, lane, float(LANES)).min(axis=-1, keepdims=True)
        hot = lane == sel
        lg = jnp.where(hot, neg, lg)
        tops.append(m)
        sels.append(sel)
        hots.append(hot)
    es = [jnp.exp(t - tops[0]) for t in tops]
    den = es[0] + es[1] + es[2] + es[3]
    member = jnp.zeros((TM, LANES), F32)
    for hot in hots:
        member = member + hot.astype(F32)
    ranks = _dot(tri_ref[...], member.astype(BF16)) + run_ref[...]
    lane4 = _lane_iota((TM, TOP_K))
    idx4 = jnp.zeros((TM, TOP_K), F32)
    gate4 = jnp.zeros((TM, TOP_K), F32)
    pos4 = jnp.zeros((TM, TOP_K), F32)
    for k in range(TOP_K):
        pk = jnp.sum(jnp.where(hots[k], ranks, 0.0), axis=-1, keepdims=True)
        idx4 = jnp.where(lane4 == k, sels[k], idx4)
        gate4 = jnp.where(lane4 == k, es[k] / den, gate4)
        pos4 = jnp.where(lane4 == k, pk, pos4)
    idx_ref[...] = idx4.astype(jnp.int32)
    gate_ref[...] = gate4
    pos_ref[...] = pos4.astype(jnp.int32)
    run_ref[...] = run_ref[...] + jnp.sum(member, axis=0, keepdims=True)
    cnt_ref[...] = run_ref[...]


def _router_weights(w_router, b_router):
    wpad = jnp.pad(w_router, ((0, 0), (0, 0), (0, LANES - N_EXPERTS)))
    wh = wpad.astype(BF16)
    wl = (wpad - wh.astype(F32)).astype(BF16)
    bpad = jnp.pad(b_router, ((0, 0), (0, LANES - N_EXPERTS))).reshape(DEPTH, 1, LANES)
    return wh, wl, bpad


def _router(layer, x, g, sc, sh, router_weights):
    wh, wl, bpad = router_weights
    r = np.arange(TM)
    tri = jnp.asarray(r[None, :] < r[:, None], dtype=BF16)
    narrow = pl.BlockSpec((TM, TOP_K), lambda i: (i, 0))

    def layer_spec(a):
        return pl.BlockSpec((None,) + a.shape[1:], lambda i: (layer, 0, 0))

    return pl.pallas_call(
        _router_kernel,
        grid=(NT,),
        in_specs=[_row_spec(D_MODEL), _gain_spec(g), _mod_spec(sc), _mod_spec(sh),
                  layer_spec(wh), layer_spec(wl), layer_spec(bpad), _full_spec(tri.shape)],
        out_specs=[pl.BlockSpec((TM,) + ROW_TILE, lambda i: (i, 0, 0)), narrow, narrow, narrow,
                   _full_spec((1, LANES))],
        out_shape=[
            jax.ShapeDtypeStruct((N_TOK,) + ROW_TILE, F32),
            jax.ShapeDtypeStruct((N_TOK, TOP_K), jnp.int32),
            jax.ShapeDtypeStruct((N_TOK, TOP_K), F32),
            jax.ShapeDtypeStruct((N_TOK, TOP_K), jnp.int32),
            jax.ShapeDtypeStruct((1, LANES), F32),
        ],
        scratch_shapes=[pltpu.VMEM((1, LANES), F32)],
        compiler_params=_params("arbitrary", vmem=VMEM_LIMIT_BYTES),
        name="router",
    )(x, g[0], sc[0], sh[0], wh, wl, bpad, tri)


ROW_UNROLL = 4
DMA_QUEUES = 2


def _start_all_rows(row_copy):
    def start_rows(j, carry):
        for u in range(ROW_UNROLL):
            for k in range(TOP_K):
                row_copy(j * ROW_UNROLL + u, k).start(priority=k % DMA_QUEUES)
        return carry

    lax.fori_loop(0, TM // ROW_UNROLL, start_rows, 0)


def _dispatch_kernel(dest_ref, clear_ref, h_ref, xs_ref, zero_ref, sem_ref):
    def row_copy(t, k):
        d = dest_ref[(pl.program_id(0) * TM + t) * TOP_K + k]
        return pltpu.make_async_copy(h_ref.at[pl.ds(t, 1)], xs_ref.at[pl.ds(d, 1)], sem_ref.at[0])

    def zero_half(j):
        row0 = pl.multiple_of(j * MOE_PART, MOE_PART)
        return pltpu.make_async_copy(zero_ref, xs_ref.at[pl.ds(row0, MOE_PART)], sem_ref.at[1])

    @pl.when(pl.program_id(0) == 0)
    def _():
        zero_ref[...] = jnp.zeros_like(zero_ref)

        def start(j, carry):
            @pl.when(clear_ref[j] > 0)
            def _():
                zero_half(j).start()
            return carry

        def wait(j, carry):
            @pl.when(clear_ref[j] > 0)
            def _():
                zero_half(j).wait()
            return carry

        lax.fori_loop(0, MOE_PARTS * N_BLOCKS, start, 0)
        lax.fori_loop(0, MOE_PARTS * N_BLOCKS, wait, 0)

    _start_all_rows(row_copy)
    for _ in range(TOP_K):
        pltpu.make_async_copy(h_ref, xs_ref.at[pl.ds(0, TM)], sem_ref.at[0]).wait()


def _dispatch(dest_flat, clear, h):
    return pl.pallas_call(
        _dispatch_kernel,
        grid_spec=pltpu.PrefetchScalarGridSpec(
            num_scalar_prefetch=2,
            grid=(NT,),
            in_specs=[pl.BlockSpec((TM,) + ROW_TILE, lambda i, d, c: (i, 0, 0))],
            out_specs=pl.BlockSpec(memory_space=pl.ANY),
            scratch_shapes=[pltpu.VMEM((MOE_PART,) + ROW_TILE, F32), pltpu.SemaphoreType.DMA((2,))],
        ),
        out_shape=jax.ShapeDtypeStruct((CAP,) + ROW_TILE, F32),
        compiler_params=_params("arbitrary", vmem=VMEM_LIMIT_BYTES),
        name="moe_dispatch",
    )(dest_flat, clear, h)


def _expert_kernel(be_ref, nh_ref, last_ref, ebuf_ref, enext_ref, xs_ref, wgu_hbm, bgu_ref, wd_hbm, bd_ref, ys_ref,
                   wgu_s, wd_s, wgu_f, wd_f, sem_ref, *, layer):
    del last_ref
    b = pl.program_id(0)
    nh = nh_ref[b]

    def weight_copies(e, buf):
        return (pltpu.make_async_copy(wgu_hbm.at[layer, e], wgu_f.at[buf], sem_ref.at[buf, 0]),
                pltpu.make_async_copy(wd_hbm.at[layer, e], wd_f.at[buf], sem_ref.at[buf, 1]))

    @pl.when((nh > 0) & ((b == 0) | (be_ref[b] != be_ref[jnp.maximum(b - 1, 0)])))
    def _():
        buf = ebuf_ref[b]

        @pl.when(b == 0)
        def _():
            for cp in weight_copies(be_ref[b], buf):
                cp.start()

        for cp in weight_copies(be_ref[b], buf):
            cp.wait()

        @pl.when(enext_ref[b] >= 0)
        def _():
            for cp in weight_copies(enext_ref[b], 1 - buf):
                cp.start()

        wgu_s[...] = wgu_f[buf].astype(BF16)
        wd_s[...] = wd_f[buf].astype(BF16)

    def ffn(rows):
        x = xs_ref[0:rows].reshape(rows, D_MODEL)
        gu = _dot(x.astype(BF16), wgu_s[...]) + bgu_ref[...]
        g = jnp.minimum(gu[:, :D_EXPERT], SWIGLU_LIMIT)
        u = jnp.clip(gu[:, D_EXPERT:], -SWIGLU_LIMIT, SWIGLU_LIMIT)
        act = (u + 1.0) * (g * jax.nn.sigmoid(SWIGLU_ALPHA * g))
        y = _dot(act.astype(BF16), wd_s[...]) + bd_ref[...]
        ys_ref[0:rows] = y.reshape((rows,) + ROW_TILE)

    def partial_block(parts):
        rows = parts * MOE_PART
        ffn(rows)
        if rows < MOE_BLK:
            ys_ref[rows:] = jnp.zeros((MOE_BLK - rows,) + ROW_TILE, F32)

    for parts in range(1, MOE_PARTS + 1):
        pl.when(nh == parts)(functools.partial(partial_block, parts))

    @pl.when(nh == 0)
    def _():
        ys_ref[...] = jnp.zeros_like(ys_ref)


def _experts(layer, block_e, block_nh, last_used, block_buf, block_next, xs, w_gu, b_gu, w_down, b_down):
    def xs_map(b, be, nh, lu, eb, en):
        return (jnp.minimum(b, lu[0]), 0, 0)

    def e_map(b, be, nh, lu, eb, en):
        return (layer, be[b], 0, 0)

    hbm = pl.BlockSpec(memory_space=pl.ANY)
    return pl.pallas_call(
        functools.partial(_expert_kernel, layer=layer),
        grid_spec=pltpu.PrefetchScalarGridSpec(
            num_scalar_prefetch=5,
            grid=(N_BLOCKS,),
            in_specs=[
                pl.BlockSpec((MOE_BLK,) + ROW_TILE, xs_map),
                hbm,
                pl.BlockSpec((None, None, 1, 2 * D_EXPERT), e_map),
                hbm,
                pl.BlockSpec((None, None, 1, D_MODEL), e_map),
            ],
            out_specs=pl.BlockSpec((MOE_BLK,) + ROW_TILE, lambda b, be, nh, lu, eb, en: (b, 0, 0)),
            scratch_shapes=[
                pltpu.VMEM((D_MODEL, 2 * D_EXPERT), BF16), pltpu.VMEM((D_EXPERT, D_MODEL), BF16),
                pltpu.VMEM((2, D_MODEL, 2 * D_EXPERT), F32), pltpu.VMEM((2, D_EXPERT, D_MODEL), F32),
                pltpu.SemaphoreType.DMA((2, 2)),
            ],
        ),
        out_shape=jax.ShapeDtypeStruct((CAP,) + ROW_TILE, F32),
        compiler_params=_params("arbitrary", vmem=VMEM_LIMIT_BYTES),
        name="moe_experts",
    )(block_e, block_nh, last_used, block_buf, block_next, xs, w_gu, b_gu.reshape(DEPTH, N_EXPERTS, 1, -1),
      w_down, b_down.reshape(DEPTH, N_EXPERTS, 1, -1))


def _combine_kernel(dest_ref, x_ref, g_ref, gate_ref, ys_ref, *rest, per_group):
    out_refs, (buf_ref, sem_ref) = rest[:-2], rest[-2:]
    i = pl.program_id(0)
    cur = lax.rem(i, 2)

    def start_tile(tile, buf):
        def row_copy(t, k):
            d = dest_ref[(tile * TM + t) * TOP_K + k]
            return pltpu.make_async_copy(ys_ref.at[pl.ds(d, 1)], buf_ref.at[buf, k, pl.ds(t, 1)], sem_ref.at[buf])
        _start_all_rows(row_copy)

    @pl.when(i == 0)
    def _():
        start_tile(0, 0)

    @pl.when(i + 1 < NT)
    def _():
        start_tile(i + 1, 1 - cur)

    for k in range(TOP_K):
        pltpu.make_async_copy(ys_ref.at[pl.ds(0, TM)], buf_ref.at[cur, k], sem_ref.at[cur]).wait()
    gates = gate_ref[...]
    ff = gates[:, 0:1] * buf_ref[cur, 0].reshape(TM, D_MODEL)
    for k in range(1, TOP_K):
        ff = ff + gates[:, k:k + 1] * buf_ref[cur, k].reshape(TM, D_MODEL)
    out = x_ref[...] + g_ref[0] * ff
    if per_group:
        def store(ref):
            ref[...] = out
        _per_group(functools.partial(store, out_refs[0]), functools.partial(store, out_refs[1]))
    else:
        out_refs[0][...] = out


def _combine(dest_flat, x, gate_vec, gates, ys, per_group):
    if per_group:
        out_specs = [_context_rows_spec(D_MODEL),
                     pl.BlockSpec((TM, D_MODEL), lambda i, *_: (jnp.maximum(i - NT_PROMPT, 0), 0))]
        out_shape = [jax.ShapeDtypeStruct((N_PROMPT, D_MODEL), F32), jax.ShapeDtypeStruct((N_SAMPLE, D_MODEL), F32)]
    else:
        out_specs = pl.BlockSpec((TM, D_MODEL), lambda i, d: (i, 0))
        out_shape = jax.ShapeDtypeStruct((N_TOK, D_MODEL), F32)
    return pl.pallas_call(
        functools.partial(_combine_kernel, per_group=per_group),
        grid_spec=pltpu.PrefetchScalarGridSpec(
            num_scalar_prefetch=1,
            grid=(NT,),
            in_specs=[
                pl.BlockSpec((TM, D_MODEL), lambda i, d: (i, 0)),
                _mod_spec(gate_vec),
                pl.BlockSpec((TM, TOP_K), lambda i, d: (i, 0)),
                pl.BlockSpec(memory_space=pl.ANY),
            ],
            out_specs=out_specs,
            scratch_shapes=[pltpu.VMEM((2, TOP_K, TM) + ROW_TILE, F32), pltpu.SemaphoreType.DMA((2,))],
        ),
        out_shape=out_shape,
        compiler_params=_params("arbitrary", vmem=VMEM_LIMIT_BYTES),
        name="moe_combine",
    )(dest_flat, x, gate_vec[0], gates, ys)


def _moe(layer, x, g, sc, sh, gate_vec, router_weights, w_gu, b_gu, w_down, b_down, per_group=False):
    h, idx, gates, pos, counts = _router(layer, x, g, sc, sh, router_weights)
    cnt = counts[0, :N_EXPERTS].astype(jnp.int32)
    nparts = (cnt + MOE_PART - 1) // MOE_PART
    nblk = (nparts + MOE_PARTS - 1) // MOE_PARTS
    e_ids = jnp.arange(N_EXPERTS, dtype=jnp.int32)
    blk_end = jnp.sum(jnp.where(e_ids[None, :] <= e_ids[:, None], nblk[None, :], 0), axis=1)
    blk_start = blk_end - nblk
    last_used = blk_end[-1] - 1

    def per_expert(table, e):
        return jnp.sum(jnp.where(e[..., None] == e_ids, table, 0), axis=-1)

    def expert_of_block(blk):
        return jnp.sum((blk_end <= jnp.minimum(blk, last_used)[:, None]).astype(jnp.int32), axis=1)

    dest = per_expert(blk_start * MOE_BLK, idx) + pos
    b_ids = jnp.arange(N_BLOCKS, dtype=jnp.int32)
    block_e = expert_of_block(b_ids)
    block_nh = jnp.where(
        b_ids <= last_used,
        jnp.clip(per_expert(nparts, block_e) - MOE_PARTS * (b_ids - per_expert(blk_start, block_e)), 0, MOE_PARTS), 0)
    h_ids = jnp.arange(MOE_PARTS * N_BLOCKS, dtype=jnp.int32)
    h_e = expert_of_block(h_ids // MOE_PARTS)
    h_nparts = per_expert(nparts, h_e)
    h_local = h_ids - MOE_PARTS * per_expert(blk_start, h_e)
    holds_rows = (h_ids // MOE_PARTS <= last_used) & (h_local < h_nparts)
    clear = jnp.logical_not(holds_rows) | (h_local == h_nparts - 1)
    dest_flat = dest.reshape(-1).astype(jnp.int32)
    xs = _dispatch(dest_flat, clear.astype(jnp.int32), h)
    used = nblk > 0
    before = e_ids[None, :] < e_ids[:, None]
    expert_buf = jnp.sum(jnp.where(before & used[None, :], 1, 0), axis=1) % 2
    expert_next = jnp.min(jnp.where((e_ids[None, :] > e_ids[:, None]) & used[None, :], e_ids[None, :], N_EXPERTS),
                          axis=1)
    expert_next = jnp.where(expert_next == N_EXPERTS, -1, expert_next)
    ys = _experts(layer, block_e.astype(jnp.int32), block_nh.astype(jnp.int32),
                  last_used.reshape(1).astype(jnp.int32), per_expert(expert_buf, block_e).astype(jnp.int32),
                  per_expert(expert_next, block_e).astype(jnp.int32), xs, w_gu, b_gu, w_down, b_down)
    return _combine(dest_flat, x, gate_vec, gates, ys, per_group)


def _rope_tables(d_rot, lane0, period):
    n_rows = DEC_SEQ // GRID_W
    rows = np.repeat(np.arange(n_rows), GRID_W).astype(np.float32)
    cols = np.tile(np.arange(GRID_W), n_rows).astype(np.float32)
    half = d_rot // 2
    lane = np.arange(LANES)
    i = (lane - lane0) % period
    active = (lane >= lane0) & (i < d_rot)
    w = i % half
    f = w % (half // 2)
    pos = np.where((i // half)[None, :] == 0, rows[:, None], cols[:, None])
    sign = np.where(w < half // 2, -1.0, 1.0).astype(np.float32)
    inv = ROPE_THETA ** (-jnp.arange(0, half, 2, dtype=F32) / half)
    ang = jnp.asarray(pos) * inv[f][None, :]
    cos = jnp.where(active[None, :], jnp.cos(ang), 1.0)
    sin = jnp.where(active[None, :], jnp.sin(ang) * sign[None, :], 0.0)
    return cos, sin


def _rope_spec():
    def index(i):
        return (jnp.maximum(i - NT_PROMPT, 0) % TILES_PER_SAMPLE, 0)
    return pl.BlockSpec((TM, LANES), index)


def _pad_heads(w, n_heads, width):
    lead = w.shape[:-1]
    w = w.reshape(lead + (n_heads, width))
    w = jnp.pad(w, [(0, 0)] * len(lead) + [(0, 0), (0, HEAD_PAD - width)])
    return w.reshape(lead + (n_heads * HEAD_PAD,))


_Q_ORDER = (0, 4, 1, 5, 2, 6, 3, 7, 8, 12, 9, 13, 10, 14, 11, 15)


def _perm_q_heads(w, axis):
    shape = w.shape
    n = shape[axis]
    w = jnp.moveaxis(w, axis, 0).reshape((GQA_HEADS, n // GQA_HEADS) + tuple(s for a, s in enumerate(shape) if a != axis))
    w = w[jnp.array(_Q_ORDER)]
    w = w.reshape((n,) + w.shape[2:])
    return jnp.moveaxis(w, 0, axis)


def _constants():
    mla_cos, mla_sin = _rope_tables(MLA_ROPE, MLA_NOPE, LANES)
    gqa_cos, gqa_sin = _rope_tables(GQA_HEAD_DIM, 0, GQA_HEAD_DIM)
    dft = {}
    for L in (SEQ, DEC_SEQ):
        cm, sm = _dft_tables(L)
        fmat = jnp.concatenate([cm, sm], axis=0).astype(BF16)
        ftmat = fmat.T
        dft[L] = (cm, sm, fmat, ftmat)
    return dict(mla=(mla_cos, mla_sin), gqa=(gqa_cos, gqa_sin), dft=dft)


def _even_mixer(x, p, i, gmix, sc1, sh1, g1, consts):
    proj = _normlin(x, gmix, sc1, sh1, p["w_in_ab"], i, IN_AB_PAD)
    y_hy = []
    for L, blk0, nseq, per_step in ((SEQ, 0, BATCH, HY_PROMPT_SEQS), (DEC_SEQ, N_PROMPT // DEC_SEQ, DEC_BATCH, 1)):
        cm, sm, fmat, ftmat = consts["dft"][L]
        kc, ks = _hy_filter(L, cm, sm, p["hy_filter_w1"][i], p["hy_filter_b1"][i], p["hy_filter_freq"][i],
                            p["hy_filter_w2"][i], p["hy_filter_b2"][i], p["hy_filter_w3"][i],
                            p["hy_filter_b3"][i], p["hy_log_decay"][i])
        y_hy.append(_hy_mix(proj, blk0, nseq, per_step, L, p["hy_conv_w"][i], p["hy_conv_b"][i], kc, ks,
                            p["hy_bias"][i], fmat, ftmat))

    wq = _pad_heads(p["mla_wq_b"][i], MLA_HEADS, MLA_QK).astype(BF16)
    wkv = p["mla_wkv_b"][i].astype(BF16)
    qg = jnp.pad(p["mla_q_norm"][i], (0, HEAD_PAD - MLA_QK)).reshape(1, HEAD_PAD)
    kg = jnp.pad(p["mla_k_norm"][i], (0, HEAD_PAD - MLA_QK)).reshape(1, HEAD_PAD)
    mla_cos, mla_sin = consts["mla"]
    q, k, v, lat = _mla_prep(proj, p["mla_q_lora_norm"][i].reshape(1, -1), wq, qg,
                             p["mla_kv_lora_norm"][i].reshape(1, -1), wkv, kg, mla_cos, mla_sin)
    ctx = jnp.pad(p["cache_mla_latent"][:, i].reshape(DEC_BATCH * PAST_LEN, -1),
                  ((0, 0), (0, KV_PAD - MLA_KV_LORA - MLA_ROPE)))
    kc_ctx, vc_ctx = _mla_ctx(ctx, wkv, kg)
    nqk = MLA_HEADS * HEAD_PAD
    o = _attention(_mla_attn_kernel, q, k, v, kc_ctx, vc_ctx, nqk, nqk, nqk, MLA_HEADS * MLA_V)
    w_out = p["w_out_ab"]
    x = _outproj([y_hy, o], [(w_out, (i, 0, 0), HY_D), (w_out, (i, 1, 0), HY_D)], x, g1)
    return x, lat, dict(y_hy=jnp.concatenate(y_hy, axis=0), o=jnp.concatenate(o, axis=0))


def _odd_mixer(x, p, i, gmix, sc1, sh1, g1, consts):
    nq = GQA_HEADS * GQA_HEAD_DIM
    nk = GQA_KV_HEADS * GQA_HEAD_DIM
    w = p["w_qkv_c"][i]
    w_qkv = jnp.concatenate([_perm_q_heads(w[:, :nq], 1), w[:, nq:]], axis=1).astype(BF16)
    qg = jnp.tile(p["gqa_q_norm"][i], LANES // GQA_HEAD_DIM).reshape(1, LANES)
    kg = jnp.tile(p["gqa_k_norm"][i], LANES // GQA_HEAD_DIM).reshape(1, LANES)
    gqa_cos, gqa_sin = consts["gqa"]
    q, k, v, k_plain, v_plain = _gqa_prep(x, gmix, sc1, sh1, w_qkv, qg, kg, gqa_cos, gqa_sin)
    kc_ctx = p["cache_gqa_k"][:, i].reshape(DEC_BATCH * PAST_LEN, -1).astype(BF16)
    vc_ctx = p["cache_gqa_v"][:, i].reshape(DEC_BATCH * PAST_LEN, -1).astype(BF16)
    o = _attention(_gqa_attn_kernel, q, k, v, kc_ctx, vc_ctx, nq, nk, nk, nq)
    w_out = _perm_q_heads(p["w_out_c"][i], 0)[None]
    x = _outproj([o], [(w_out, (0, 0, 0), nq)], x, g1)
    return x, k_plain, v_plain


def kernel(x_prompt, x_sample, cache_mla_latent, cache_gqa_k, cache_gqa_v, c, c_ctx, w_ada, b_ada, norm_mix, norm_ffn, w_in_ab, hy_conv_w, hy_conv_b, hy_filter_w1, hy_filter_b1, hy_filter_freq, hy_filter_w2, hy_filter_b2, hy_filter_w3, hy_filter_b3, hy_log_decay, hy_bias, mla_q_lora_norm, mla_wq_b, mla_kv_lora_norm, mla_wkv_b, mla_q_norm, mla_k_norm, w_out_ab, w_qkv_c, gqa_q_norm, gqa_k_norm, w_out_c, moe_router_w, moe_router_b, moe_w_gate_up, moe_b_gate_up, moe_w_down, moe_b_down):
    p = dict(locals())
    x = jnp.concatenate([x_prompt.reshape(N_PROMPT, D_MODEL), x_sample.reshape(N_SAMPLE, D_MODEL)], axis=0)

    cond = jnp.concatenate([c_ctx[None, :], c, jnp.zeros((COND_ROWS - N_COND, D_MODEL), F32)], axis=0)
    mods = _modulation(cond, w_ada, b_ada)
    tile_cond = jnp.concatenate([jnp.zeros((NT_PROMPT,), jnp.int32),
                                 1 + jnp.arange(NT - NT_PROMPT, dtype=jnp.int32) // TILES_PER_SAMPLE])
    mods = mods[:, tile_cond].reshape(DEPTH, NT, 6, 1, D_MODEL).transpose(0, 2, 1, 3, 4)

    consts = _constants()
    router_weights = _router_weights(moe_router_w, moe_router_b)
    gains_mix = norm_mix.reshape(DEPTH, 1, D_MODEL)
    gains_ffn = norm_ffn.reshape(DEPTH, 1, D_MODEL)

    lat_out, k_out, v_out = [], [], []
    for l in range(DEPTH):
        sh1, sc1, g1, sh2, sc2, g2 = ((mods, l, j) for j in range(6))
        i = l // 2
        gmix = (gains_mix, l)
        if l % 2 == 0:
            x, lat, _ = _even_mixer(x, p, i, gmix, sc1, sh1, g1, consts)
            lat_out.append(lat.reshape(BATCH, SEQ, -1))
        else:
            x, k_plain, v_plain = _odd_mixer(x, p, i, gmix, sc1, sh1, g1, consts)
            k_out.append(k_plain.reshape(BATCH, SEQ, GQA_KV_HEADS, GQA_HEAD_DIM))
            v_out.append(v_plain.reshape(BATCH, SEQ, GQA_KV_HEADS, GQA_HEAD_DIM))
        x = _moe(l, x, (gains_ffn, l), sc2, sh2, g2, router_weights,
                 moe_w_gate_up, moe_b_gate_up, moe_w_down, moe_b_down, per_group=(l == DEPTH - 1))

    y_prompt = x[0].reshape(BATCH, SEQ, D_MODEL)
    y_sample = x[1].reshape(DEC_BATCH, DEC_SEQ, D_MODEL)
    return (y_prompt, y_sample, jnp.stack(lat_out, axis=1), jnp.stack(k_out, axis=1), jnp.stack(v_out, axis=1))
```

```python
import functools
import math

import jax
import jax.numpy as jnp
import numpy as np
from jax import lax
from jax.experimental import pallas as pl
from jax.experimental.pallas import tpu as pltpu

F32 = jnp.float32
BF16 = jnp.bfloat16

D_MODEL = 1024
BATCH = 32
SEQ = 256
DEPTH = 4
DEC_BATCH = 4
DEC_SEQ = 1024
PAST_LEN = 256
GRID_W = 64
N_EVEN = (DEPTH + 1) // 2
N_ODD = DEPTH // 2
HY_D = D_MODEL // 2
HY_ORDER = 2
HY_BANDS = 16
HY_EMB = 2 * HY_BANDS + 1
HY_FILTER_HIDDEN = 64
MLA_HEADS = 8
MLA_NOPE = 64
MLA_ROPE = 32
MLA_QK = MLA_NOPE + MLA_ROPE
MLA_V = HY_D // MLA_HEADS
MLA_Q_LORA = 3 * D_MODEL // 8
MLA_KV_LORA = D_MODEL // 4
GQA_HEADS = 16
GQA_KV_HEADS = 4
GQA_HEAD_DIM = D_MODEL // GQA_HEADS
N_EXPERTS = 32
TOP_K = 4
D_EXPERT = D_MODEL
SWIGLU_LIMIT = 7.0
SWIGLU_ALPHA = 1.702
ROPE_THETA = 10000.0
RMS_EPS = 1e-6
GQA_SCALE = GQA_HEAD_DIM ** -0.5
assert math.frexp(GQA_SCALE)[0] == 0.5
IN_AB = (HY_ORDER + 1) * HY_D + MLA_Q_LORA + MLA_KV_LORA + MLA_ROPE

N_PROMPT = BATCH * SEQ
N_SAMPLE = DEC_BATCH * DEC_SEQ
N_TOK = N_PROMPT + N_SAMPLE

LANES = 128
SUBLANES = 8
VMEM_LIMIT_BYTES = 56 * 1024 * 1024

TM = 256
NT = N_TOK // TM
NT_PROMPT = N_PROMPT // TM
TILES_PER_SAMPLE = DEC_SEQ // TM
N_COND = 1 + DEC_BATCH
COND_ROWS = 8
HEAD_PAD = LANES
IN_AB_PAD = 2304
KV_PAD = 384
MOE_BLK = 512
MOE_PARTS = 4
MOE_PART = MOE_BLK // MOE_PARTS
N_SLOTS = N_TOK * TOP_K
N_BLOCKS = N_SLOTS // MOE_BLK + N_EXPERTS
CAP = N_BLOCKS * MOE_BLK
ROW_TILE = (SUBLANES, LANES)
assert SUBLANES * LANES == D_MODEL
HY_CH = 256
HY_PROMPT_SEQS = 8


def _dot(a, b):
    return jnp.dot(a, b, preferred_element_type=F32)


def _dot_nt(a, b):
    return lax.dot_general(a, b, (((1,), (1,)), ((), ())), preferred_element_type=F32)


def _split(x):
    hi = x.astype(BF16)
    lo = (x - hi.astype(F32)).astype(BF16)
    return hi, lo


def _dot3(a, b):
    ah, al = _split(a)
    bh, bl = _split(b)
    return _dot(ah, bh) + (_dot(ah, bl) + _dot(al, bh))


def _lane_iota(shape):
    return lax.broadcasted_iota(jnp.int32, shape, len(shape) - 1)


def _params(*sem, vmem=None):
    return pltpu.CompilerParams(dimension_semantics=sem, vmem_limit_bytes=vmem)


def _mod_kernel(c_ref, w_ref, b_ref, o_ref):
    c = c_ref[...]
    s = c * jax.nn.sigmoid(c)
    o_ref[0] = _dot(s.astype(BF16), w_ref[0].astype(BF16)) + b_ref[0]


def _modulation(cond, w_ada, b_ada):
    nblk = 6
    return pl.pallas_call(
        _mod_kernel,
        grid=(DEPTH, nblk),
        in_specs=[
            pl.BlockSpec((COND_ROWS, D_MODEL), lambda l, j: (0, 0)),
            pl.BlockSpec((1, D_MODEL, D_MODEL), lambda l, j: (l, 0, j)),
            pl.BlockSpec((1, 1, D_MODEL), lambda l, j: (l, 0, j)),
        ],
        out_specs=pl.BlockSpec((1, COND_ROWS, D_MODEL), lambda l, j: (l, 0, j)),
        out_shape=jax.ShapeDtypeStruct((DEPTH, COND_ROWS, nblk * D_MODEL), F32),
        compiler_params=_params("arbitrary", "arbitrary"),
        name="modulation",
    )(cond, w_ada, b_ada.reshape(DEPTH, 1, nblk * D_MODEL))


def _norm_mod(x, g, sc, sh):
    ms = jnp.mean(x * x, axis=-1, keepdims=True)
    y = x * lax.rsqrt(ms + RMS_EPS)
    return (y * g) * (1.0 + sc) + sh


def _row_spec(width):
    return pl.BlockSpec((TM, width), lambda i: (i, 0))


def _mod_spec(mod):
    _, layer, which = mod
    return pl.BlockSpec((None, None, 1, 1, D_MODEL), lambda i, *_: (layer, which, i, 0, 0))


def _gain_spec(gain):
    _, layer = gain
    return pl.BlockSpec((None, 1, D_MODEL), lambda i, *_: (layer, 0, 0))


def _full_spec(shape):
    nd = len(shape)
    return pl.BlockSpec(shape, lambda i: (0,) * nd)


def _normlin_kernel(x_ref, g_ref, sc_ref, sh_ref, w_ref, o_ref, w_scr):
    nin = w_ref.shape[1]

    @pl.when(pl.program_id(0) == 0)
    def _():
        w_scr[:, :nin] = w_ref[...].astype(BF16)
        w_scr[:, nin:] = jnp.zeros((D_MODEL, w_scr.shape[1] - nin), BF16)

    h = _norm_mod(x_ref[...], g_ref[...], sc_ref[0], sh_ref[0])
    o_ref[...] = _dot(h.astype(BF16), w_scr[...])


def _normlin(x, g, sc, sh, w_all, layer, nout):
    nin = w_all.shape[2]
    return pl.pallas_call(
        _normlin_kernel,
        grid=(NT,),
        in_specs=[_row_spec(D_MODEL), _gain_spec(g), _mod_spec(sc), _mod_spec(sh),
                  pl.BlockSpec((None, D_MODEL, nin), lambda i: (layer, 0, 0))],
        out_specs=_row_spec(nout),
        out_shape=jax.ShapeDtypeStruct((N_TOK, nout), F32),
        scratch_shapes=[pltpu.VMEM((D_MODEL, nout), BF16)],
        compiler_params=_params("arbitrary", vmem=VMEM_LIMIT_BYTES),
        name="normlin",
    )(x, g[0], sc[0], sh[0], w_all)


def _hy_filter_kernel(z_ref, w1_ref, b1_ref, fr_ref, w2_ref, b2_ref, w3_ref, b3_ref, ed_ref,
                      c_ref, s_ref, kc_ref, ks_ref):
    L = z_ref.shape[0]
    z = z_ref[...]
    fr = fr_ref[...]
    hdn = jnp.sin(fr * (_dot3(z, w1_ref[...]) + b1_ref[...]))
    hdn = jnp.sin(fr * (_dot3(hdn, w2_ref[...]) + b2_ref[...]))
    filt = _dot3(hdn, w3_ref[...]) + b3_ref[...]
    t = z[:, 0:1]
    filt = filt * jnp.exp(-t * ed_ref[...])
    row = lax.broadcasted_iota(jnp.int32, (L, HY_D), 0)
    cm = c_ref[...]
    sm = s_ref[...]
    for o in range(HY_ORDER):
        fw = filt[:, (2 * o) * HY_D:(2 * o + 1) * HY_D]
        bw = filt[:, (2 * o + 1) * HY_D:(2 * o + 2) * HY_D]
        den = (jnp.sum(jnp.abs(fw), axis=0, keepdims=True)
               + jnp.sum(jnp.abs(bw), axis=0, keepdims=True)) + 1e-6
        fw = fw / den
        bw = jnp.where(row == 0, 0.0, bw / den)
        kc_ref[o] = _dot3(cm, fw + bw) * (1.0 / L)
        ks_ref[o] = _dot3(sm, fw - bw) * (1.0 / L)


def _dft_tables(L):
    m = jnp.arange(L, dtype=jnp.int32)
    phase = ((2 * m + 1)[:, None] * m[None, :]) % (4 * L)
    ang = phase.astype(F32) * (2.0 * math.pi / (4 * L))
    return jnp.cos(ang), jnp.sin(ang)


def _filter_features(L):
    p = jnp.arange(L, dtype=F32)
    t = p / max(L - 1, 1)
    bands = jnp.linspace(1e-4, HY_BANDS - 1, HY_BANDS, dtype=F32)
    ang = (2.0 * math.pi / L) * p[:, None] * bands[None, :]
    z = jnp.concatenate([t[:, None], jnp.cos(ang), -jnp.sin(ang)], axis=-1)
    return jnp.pad(z, ((0, 0), (0, LANES - HY_EMB)))


def _hy_filter(L, cmat, smat, w1, b1, fr, w2, b2, w3, b3, log_decay):
    nf = HY_ORDER * 2 * HY_D
    args = (
        _filter_features(L),
        jnp.pad(w1, ((0, LANES - HY_EMB), (0, 0))),
        b1.reshape(1, -1), fr.reshape(1, -1), w2, b2.reshape(1, -1), w3, b3.reshape(1, -1),
        jnp.exp(log_decay.astype(F32)).reshape(1, nf),
        cmat, smat,
    )
    out_sds = jax.ShapeDtypeStruct((HY_ORDER, L, HY_D), F32)
    return pl.pallas_call(
        _hy_filter_kernel,
        grid=(1,),
        in_specs=[_full_spec(a.shape) for a in args],
        out_specs=[_full_spec(out_sds.shape)] * 2,
        out_shape=[out_sds, out_sds],
        compiler_params=_params("arbitrary", vmem=VMEM_LIMIT_BYTES),
        name=f"hy_filter_{L}",
    )(*args)


def _hy_mix_kernel(u0_ref, u1_ref, u2_ref, cw0_ref, cw1_ref, cw2_ref, cb0_ref, cb1_ref, cb2_ref,
                   kc_ref, ks_ref, hb_ref, f_ref, ft_ref, o_ref):
    S, L, _ = u0_ref.shape
    row = lax.broadcasted_iota(jnp.int32, (L, HY_CH), 0)

    def lanes(per_seq):
        return per_seq[0] if S == 1 else jnp.concatenate(per_seq, axis=1)

    def short_conv(u_ref, cw_ref, cb_ref):
        w = cw_ref[0]
        out = []
        for s in range(S):
            u = u_ref[s]
            prev = jnp.where(row == 0, 0.0, pltpu.roll(u, 1, 0))
            nxt = jnp.where(row == L - 1, 0.0, pltpu.roll(u, L - 1, 0))
            out.append((prev * w[0:1] + u * w[1:2]) + nxt * w[2:3] + cb_ref[0])
        return lanes(out)

    z = short_conv(u0_ref, cw0_ref, cb0_ref)
    gates = (short_conv(u1_ref, cw1_ref, cb1_ref), short_conv(u2_ref, cw2_ref, cb2_ref))
    for o in range(HY_ORDER):
        zz = _dot(f_ref[...], z.astype(BF16))
        cz, sz = zz[:L], zz[L:]
        kc, ks = lanes([kc_ref[o]] * S), lanes([ks_ref[o]] * S)
        w1 = cz * kc - sz * ks
        w2 = cz * ks + sz * kc
        ww = jnp.concatenate([w1, w2], axis=0).astype(BF16)
        conv = _dot(ft_ref[...], ww)
        z = gates[o] * (conv + z * lanes([hb_ref[0, o]] * S))
    for s in range(S):
        o_ref[s] = z[:, s * HY_CH:(s + 1) * HY_CH].astype(o_ref.dtype)


def _hy_mix(proj, row_block0, nseq, seqs_per_step, L, cw, cb, kc, ks, hb, fmat, ftmat):
    nch = HY_D // HY_CH
    nparts = HY_ORDER + 1
    S = seqs_per_step
    assert nseq % S == 0 and row_block0 % S == 0
    cw3 = cw.reshape(3, nparts * nch, HY_CH).transpose(1, 0, 2)
    cb3 = cb.reshape(nparts * nch, 1, HY_CH)
    hb3 = hb.reshape(HY_ORDER, nch, 1, HY_CH).transpose(1, 0, 2, 3)
    proj = proj.reshape(N_TOK // L, L, proj.shape[1])

    def u_spec(part):
        return pl.BlockSpec((S, L, HY_CH), lambda s, c: (row_block0 // S + s, 0, part * nch + c))

    def cw_spec(part):
        return pl.BlockSpec((1, 3, HY_CH), lambda s, c: (part * nch + c, 0, 0))

    def cb_spec(part):
        return pl.BlockSpec((1, 1, HY_CH), lambda s, c: (part * nch + c, 0, 0))

    return pl.pallas_call(
        _hy_mix_kernel,
        grid=(nseq // S, nch),
        in_specs=[
            u_spec(0), u_spec(1), u_spec(2),
            cw_spec(0), cw_spec(1), cw_spec(2),
            cb_spec(0), cb_spec(1), cb_spec(2),
            pl.BlockSpec((HY_ORDER, L, HY_CH), lambda s, c: (0, 0, c)),
            pl.BlockSpec((HY_ORDER, L, HY_CH), lambda s, c: (0, 0, c)),
            pl.BlockSpec((1, HY_ORDER, 1, HY_CH), lambda s, c: (c, 0, 0, 0)),
            pl.BlockSpec((2 * L, L), lambda s, c: (0, 0)),
            pl.BlockSpec((L, 2 * L), lambda s, c: (0, 0)),
        ],
        out_specs=pl.BlockSpec((S, L, HY_CH), lambda s, c: (s, 0, c)),
        out_shape=jax.ShapeDtypeStruct((nseq, L, HY_D), BF16),
        compiler_params=_params("arbitrary", "arbitrary", vmem=VMEM_LIMIT_BYTES),
        name=f"hy_mix_{L}",
    )(proj, proj, proj, cw3, cw3, cw3, cb3, cb3, cb3, kc, ks, hb3, fmat, ftmat).reshape(nseq * L, HY_D)


def _head_rms(xh, gain, dim):
    ms = jnp.sum(xh * xh, axis=-1, keepdims=True) * (1.0 / dim)
    return (xh * lax.rsqrt(ms + RMS_EPS)) * gain


def _context_rows_spec(width):
    return pl.BlockSpec((TM, width), lambda i, *_: (jnp.minimum(i, NT_PROMPT - 1), 0))


def _per_group(context_fn, latent_fn):
    is_context = pl.program_id(0) < NT_PROMPT
    pl.when(is_context)(context_fn)
    pl.when(jnp.logical_not(is_context))(latent_fn)


def _rope(xh, cos, sin, half):
    lane = _lane_iota(xh.shape)
    first = (lane % (2 * half)) < half
    rot = jnp.where(first, pltpu.roll(xh, LANES - half, 1), pltpu.roll(xh, half, 1))
    return xh * cos + rot * sin


def _mla_keys_values(lat_n, kr_blk, wkv_ref, kg_ref, rope):
    kv = _dot(lat_n.astype(BF16), wkv_ref[...])
    kr = pltpu.roll(kr_blk, MLA_NOPE, 1)
    nope = _lane_iota(kr.shape) < MLA_NOPE
    ks = []
    for h in range(MLA_HEADS):
        kh = jnp.where(nope, kv[:, h * HEAD_PAD:(h + 1) * HEAD_PAD], 0.0) + kr
        kh = _head_rms(kh, kg_ref[...], MLA_QK)
        if rope is not None:
            kh = _rope(kh, rope[0], rope[1], MLA_ROPE // 4)
        ks.append(kh)
    return ks, kv


def _mla_prep_kernel(qa_ref, kva_ref, gq_ref, wq_ref, qg_ref, gkv_ref, wkv_ref, kg_ref,
                     cos_ref, sin_ref, q_ref, k_ref, v_ref, lat_ref):
    def body(rope):
        qa = qa_ref[...]
        ms = jnp.mean(qa * qa, axis=-1, keepdims=True)
        qn = (qa * lax.rsqrt(ms + RMS_EPS)) * gq_ref[...]
        q = _dot(qn.astype(BF16), wq_ref[...])
        for h in range(MLA_HEADS):
            qh = _head_rms(q[:, h * HEAD_PAD:(h + 1) * HEAD_PAD], qg_ref[...], MLA_QK)
            if rope is not None:
                qh = _rope(qh, rope[0], rope[1], MLA_ROPE // 4)
            q_ref[:, h * HEAD_PAD:(h + 1) * HEAD_PAD] = qh.astype(BF16)

        kva = kva_ref[...]
        lat = kva[:, :MLA_KV_LORA]
        ms = jnp.mean(lat * lat, axis=-1, keepdims=True)
        lat_n = (lat * lax.rsqrt(ms + RMS_EPS)) * gkv_ref[...]
        kr_blk = kva[:, MLA_KV_LORA:]
        if rope is None:
            lat_ref[:, :MLA_KV_LORA] = lat_n
            lat_ref[:, MLA_KV_LORA:] = kr_blk[:, :MLA_ROPE]
        ks, v = _mla_keys_values(lat_n, kr_blk, wkv_ref, kg_ref, rope)
        for h in range(MLA_HEADS):
            k_ref[:, h * HEAD_PAD:(h + 1) * HEAD_PAD] = ks[h].astype(BF16)
        v_ref[...] = v.astype(BF16)

    _per_group(lambda: body(None), lambda: body((cos_ref[...], sin_ref[...])))


def _mla_prep(proj, gq, wq, qg, gkv, wkv, kg, cos_t, sin_t):
    nq = MLA_HEADS * HEAD_PAD
    nv = MLA_HEADS * (MLA_NOPE + MLA_V)
    qa_blk = (HY_ORDER + 1) * HY_D // KV_PAD
    return pl.pallas_call(
        _mla_prep_kernel,
        grid=(NT,),
        in_specs=[
            pl.BlockSpec((TM, KV_PAD), lambda i: (i, qa_blk)),
            pl.BlockSpec((TM, KV_PAD), lambda i: (i, qa_blk + 1)),
            _full_spec(gq.shape), _full_spec(wq.shape), _full_spec(qg.shape), _full_spec(gkv.shape),
            _full_spec(wkv.shape), _full_spec(kg.shape),
            _rope_spec(), _rope_spec(),
        ],
        out_specs=[_row_spec(nq), _row_spec(nq), _row_spec(nv), _context_rows_spec(MLA_KV_LORA + MLA_ROPE)],
        out_shape=[
            jax.ShapeDtypeStruct((N_TOK, nq), BF16),
            jax.ShapeDtypeStruct((N_TOK, nq), BF16),
            jax.ShapeDtypeStruct((N_TOK, nv), BF16),
            jax.ShapeDtypeStruct((N_PROMPT, MLA_KV_LORA + MLA_ROPE), F32),
        ],
        compiler_params=_params("arbitrary", vmem=VMEM_LIMIT_BYTES),
        name="mla_prep",
    )(proj, proj, gq, wq, qg, gkv, wkv, kg, cos_t, sin_t)


def _mla_ctx_kernel(lat_ref, wkv_ref, kg_ref, k_ref, v_ref):
    lat = lat_ref[...]
    ks, v = _mla_keys_values(lat[:, :MLA_KV_LORA], lat[:, MLA_KV_LORA:], wkv_ref, kg_ref, None)
    for h in range(MLA_HEADS):
        k_ref[:, h * HEAD_PAD:(h + 1) * HEAD_PAD] = ks[h].astype(BF16)
    v_ref[...] = v.astype(BF16)


def _mla_ctx(lat_pad, wkv, kg):
    n = lat_pad.shape[0]
    nq = MLA_HEADS * HEAD_PAD
    nv = MLA_HEADS * (MLA_NOPE + MLA_V)
    return pl.pallas_call(
        _mla_ctx_kernel,
        grid=(n // TM,),
        in_specs=[_row_spec(KV_PAD), _full_spec(wkv.shape), _full_spec(kg.shape)],
        out_specs=[_row_spec(nq), _row_spec(nv)],
        out_shape=[jax.ShapeDtypeStruct((n, nq), BF16), jax.ShapeDtypeStruct((n, nv), BF16)],
        compiler_params=_params("arbitrary"),
        name="mla_ctx",
    )(lat_pad, wkv, kg)


def _softmax_pv(q, keys, vals, scale):
    ss = [_dot_nt(q, k) for k in keys]
    if scale is not None:
        ss = [s * scale for s in ss]
    m = ss[0].max(axis=-1, keepdims=True)
    for s in ss[1:]:
        m = jnp.maximum(m, s.max(axis=-1, keepdims=True))
    ps = [jnp.exp(s - m) for s in ss]
    l = ps[0].sum(axis=-1, keepdims=True)
    for p in ps[1:]:
        l = l + p.sum(axis=-1, keepdims=True)
    o = _dot(ps[0].astype(BF16), vals[0])
    for p, v in zip(ps[1:], vals[1:]):
        o = o + _dot(p.astype(BF16), v)
    return o / l


def _mla_attn_kernel(*refs, has_ctx):
    if has_ctx:
        q_ref, k_ref, v_ref, kc_ref, vc_ref, o_ref = refs
    else:
        q_ref, k_ref, v_ref, o_ref = refs
    scale = MLA_QK ** -0.5
    lo = _lane_iota((q_ref.shape[0], LANES)) < MLA_V
    for j in range(MLA_HEADS // 2):
        outs = []
        for h in (2 * j, 2 * j + 1):
            sl = slice(h * HEAD_PAD, (h + 1) * HEAD_PAD)
            ks, vs = [k_ref[:, sl]], [v_ref[:, sl]]
            if has_ctx:
                ks.append(kc_ref[:, sl])
                vs.append(vc_ref[:, sl])
            outs.append(_softmax_pv(q_ref[:, sl], ks, vs, scale))
        pair = jnp.where(lo, pltpu.roll(outs[0], MLA_V, 1), outs[1])
        o_ref[:, j * LANES:(j + 1) * LANES] = pair.astype(o_ref.dtype)


def _gqa_attn_kernel(*refs, has_ctx):
    if has_ctx:
        q_ref, k_ref, v_ref, kc_ref, vc_ref, o_ref = refs
    else:
        q_ref, k_ref, v_ref, o_ref = refs
    scale = None
    lo = _lane_iota((q_ref.shape[0], LANES)) < GQA_HEAD_DIM
    pairs_per_kv = (GQA_HEADS // 2) // (GQA_KV_HEADS // 2)
    for p in range(GQA_HEADS // 2):
        kv = slice((p // pairs_per_kv) * LANES, (p // pairs_per_kv + 1) * LANES)
        ks, vs = [k_ref[:, kv]], [v_ref[:, kv]]
        if has_ctx:
            ks.append(kc_ref[:, kv])
            vs.append(vc_ref[:, kv])
        qp = q_ref[:, p * LANES:(p + 1) * LANES]
        zero = jnp.zeros_like(qp)
        o_lo = _softmax_pv(jnp.where(lo, qp, zero), ks, vs, scale)
        o_hi = _softmax_pv(jnp.where(lo, zero, qp), ks, vs, scale)
        o_ref[:, p * LANES:(p + 1) * LANES] = jnp.where(lo, o_lo, o_hi).astype(o_ref.dtype)


def _attention(body, q, k, v, kc, vc, wq, wk, wv, wo):
    outs = []
    outs.append(pl.pallas_call(
        functools.partial(body, has_ctx=False),
        grid=(BATCH,),
        in_specs=[pl.BlockSpec((SEQ, wq), lambda b: (b, 0)),
                  pl.BlockSpec((SEQ, wk), lambda b: (b, 0)),
                  pl.BlockSpec((SEQ, wv), lambda b: (b, 0))],
        out_specs=pl.BlockSpec((SEQ, wo), lambda b: (b, 0)),
        out_shape=jax.ShapeDtypeStruct((N_PROMPT, wo), BF16),
        compiler_params=_params("arbitrary", vmem=VMEM_LIMIT_BYTES),
        name="attn_prompt",
    )(q, k, v))
    qt = DEC_SEQ // TM
    q0 = N_PROMPT // TM
    s0 = N_PROMPT // DEC_SEQ
    outs.append(pl.pallas_call(
        functools.partial(body, has_ctx=True),
        grid=(DEC_BATCH, qt),
        in_specs=[pl.BlockSpec((TM, wq), lambda b, t: (q0 + b * qt + t, 0)),
                  pl.BlockSpec((DEC_SEQ, wk), lambda b, t: (s0 + b, 0)),
                  pl.BlockSpec((DEC_SEQ, wv), lambda b, t: (s0 + b, 0)),
                  pl.BlockSpec((PAST_LEN, wk), lambda b, t: (b, 0)),
                  pl.BlockSpec((PAST_LEN, wv), lambda b, t: (b, 0))],
        out_specs=pl.BlockSpec((TM, wo), lambda b, t: (b * qt + t, 0)),
        out_shape=jax.ShapeDtypeStruct((N_SAMPLE, wo), BF16),
        compiler_params=_params("arbitrary", "arbitrary", vmem=VMEM_LIMIT_BYTES),
        name="attn_sample",
    )(q, k, v, kc, vc))
    return outs


def _gqa_prep_kernel(x_ref, g_ref, sc_ref, sh_ref, w_ref, qg_ref, kg_ref, cos_ref, sin_ref,
                     q_ref, k_ref, v_ref, kp_ref, vp_ref):
    lo = _lane_iota((TM, LANES)) < GQA_HEAD_DIM
    nq = GQA_HEADS * GQA_HEAD_DIM
    nk = GQA_KV_HEADS * GQA_HEAD_DIM

    def pair_norm(xp, gain):
        sq = xp * xp
        ms_lo = jnp.sum(jnp.where(lo, sq, 0.0), axis=-1, keepdims=True)
        ms_hi = jnp.sum(jnp.where(lo, 0.0, sq), axis=-1, keepdims=True)
        ms = jnp.where(lo, ms_lo, ms_hi) * (1.0 / GQA_HEAD_DIM)
        return (xp * lax.rsqrt(ms + RMS_EPS)) * gain

    def body(rope):
        def rotate(xp):
            return xp if rope is None else _rope(xp, rope[0], rope[1], GQA_HEAD_DIM // 4)

        h = _norm_mod(x_ref[...], g_ref[...], sc_ref[0], sh_ref[0])
        qkv = _dot(h.astype(BF16), w_ref[...])
        for p in range(nq // LANES):
            sl = slice(p * LANES, (p + 1) * LANES)
            qn = pair_norm(qkv[:, sl], qg_ref[...])
            q_ref[:, sl] = (rotate(qn) * GQA_SCALE).astype(BF16)
        for p in range(nk // LANES):
            sl = slice(p * LANES, (p + 1) * LANES)
            kn = pair_norm(qkv[:, nq + p * LANES:nq + (p + 1) * LANES], kg_ref[...])
            if rope is None:
                kp_ref[:, sl] = kn
            k_ref[:, sl] = rotate(kn).astype(BF16)
        v = qkv[:, nq + nk:]
        if rope is None:
            vp_ref[...] = v
        v_ref[...] = v.astype(BF16)

    _per_group(lambda: body(None), lambda: body((cos_ref[...], sin_ref[...])))


def _gqa_prep(x, g, sc, sh, w, qg, kg, cos_t, sin_t):
    nq = GQA_HEADS * GQA_HEAD_DIM
    nk = GQA_KV_HEADS * GQA_HEAD_DIM
    return pl.pallas_call(
        _gqa_prep_kernel,
        grid=(NT,),
        in_specs=[_row_spec(D_MODEL), _gain_spec(g), _mod_spec(sc), _mod_spec(sh),
                  _full_spec(w.shape), _full_spec(qg.shape), _full_spec(kg.shape),
                  _rope_spec(), _rope_spec()],
        out_specs=[_row_spec(nq), _row_spec(nk), _row_spec(nk), _context_rows_spec(nk), _context_rows_spec(nk)],
        out_shape=[
            jax.ShapeDtypeStruct((N_TOK, nq), BF16),
            jax.ShapeDtypeStruct((N_TOK, nk), BF16),
            jax.ShapeDtypeStruct((N_TOK, nk), BF16),
            jax.ShapeDtypeStruct((N_PROMPT, nk), F32),
            jax.ShapeDtypeStruct((N_PROMPT, nk), F32),
        ],
        compiler_params=_params("arbitrary", vmem=VMEM_LIMIT_BYTES),
        name="gqa_prep",
    )(x, g[0], sc[0], sh[0], w, qg, kg, cos_t, sin_t)


def _outproj_kernel(*refs, n_in):
    ap_refs = refs[:n_in]
    as_refs = refs[n_in:2 * n_in]
    w_refs = refs[2 * n_in:3 * n_in]
    x_ref, g_ref, o_ref = refs[3 * n_in:3 * n_in + 3]
    w_scr = refs[3 * n_in + 3:]
    is_prompt = pl.program_id(0) < NT_PROMPT

    @pl.when(pl.program_id(0) == 0)
    def _():
        for w, s in zip(w_refs, w_scr):
            s[...] = w[...].astype(BF16)

    y = None
    for ap, asm, s in zip(ap_refs, as_refs, w_scr):
        a = jnp.where(is_prompt, ap[...], asm[...])
        d = _dot(a, s[...])
        y = d if y is None else y + d
    o_ref[...] = x_ref[...] + g_ref[0] * y


def _outproj(acts, ws, x, gate):
    n_in = len(acts)

    def w_spec(index, rows):
        return pl.BlockSpec((None, rows, D_MODEL), lambda i: index)

    def prompt_spec(width):
        return pl.BlockSpec((TM, width), lambda i: (jnp.minimum(i, NT_PROMPT - 1), 0))

    def sample_spec(width):
        return pl.BlockSpec((TM, width), lambda i: (jnp.maximum(i - NT_PROMPT, 0), 0))

    return pl.pallas_call(
        functools.partial(_outproj_kernel, n_in=n_in),
        grid=(NT,),
        in_specs=([prompt_spec(ap.shape[1]) for ap, _ in acts] + [sample_spec(asm.shape[1]) for _, asm in acts]
                  + [w_spec(index, rows) for _, index, rows in ws] + [_row_spec(D_MODEL), _mod_spec(gate)]),
        out_specs=_row_spec(D_MODEL),
        out_shape=jax.ShapeDtypeStruct((N_TOK, D_MODEL), F32),
        scratch_shapes=[pltpu.VMEM((rows, D_MODEL), BF16) for _, _, rows in ws],
        compiler_params=_params("arbitrary", vmem=VMEM_LIMIT_BYTES),
        name="outproj",
    )(*(ap for ap, _ in acts), *(asm for _, asm in acts), *(w for w, _, _ in ws), x, gate[0])


def _router_kernel(x_ref, g_ref, sc_ref, sh_ref, wh_ref, wl_ref, br_ref, tri_ref,
                   h_ref, idx_ref, gate_ref, pos_ref, cnt_ref, run_ref):
    @pl.when(pl.program_id(0) == 0)
    def _():
        run_ref[...] = jnp.zeros_like(run_ref)

    h = _norm_mod(x_ref[...], g_ref[...], sc_ref[0], sh_ref[0])
    h_ref[...] = h.reshape((TM,) + ROW_TILE)
    hh, hl = _split(h)
    logits = _dot(hh, wh_ref[...]) + (_dot(hh, wl_ref[...]) + _dot(hl, wh_ref[...])) + br_ref[...]
    lane = _lane_iota((TM, LANES)).astype(F32)
    neg = jnp.float32(-jnp.inf)
    lg = jnp.where(lane < N_EXPERTS, logits, neg)
    tops, sels, hots = [], [], []
    for _ in range(TOP_K):
        m = lg.max(axis=-1, keepdims=True)
        sel = jnp.where(lg == m, lane, float(LANES)).min(axis=-1, keepdims=True)
        hot = lane == sel
        lg = jnp.where(hot, neg, lg)
        tops.append(m)
        sels.append(sel)
        hots.append(hot)
    es = [jnp.exp(t - tops[0]) for t in tops]
    den = es[0] + es[1] + es[2] + es[3]
    member = jnp.zeros((TM, LANES), F32)
    for hot in hots:
        member = member + hot.astype(F32)
    ranks = _dot(tri_ref[...], member.astype(BF16)) + run_ref[...]
    lane4 = _lane_iota((TM, TOP_K))
    idx4 = jnp.zeros((TM, TOP_K), F32)
    gate4 = jnp.zeros((TM, TOP_K), F32)
    pos4 = jnp.zeros((TM, TOP_K), F32)
    for k in range(TOP_K):
        pk = jnp.sum(jnp.where(hots[k], ranks, 0.0), axis=-1, keepdims=True)
        idx4 = jnp.where(lane4 == k, sels[k], idx4)
        gate4 = jnp.where(lane4 == k, es[k] / den, gate4)
        pos4 = jnp.where(lane4 == k, pk, pos4)
    idx_ref[...] = idx4.astype(jnp.int32)
    gate_ref[...] = gate4
    pos_ref[...] = pos4.astype(jnp.int32)
    run_ref[...] = run_ref[...] + jnp.sum(member, axis=0, keepdims=True)
    cnt_ref[...] = run_ref[...]


def _router_weights(w_router, b_router):
    wpad = jnp.pad(w_router, ((0, 0), (0, 0), (0, LANES - N_EXPERTS)))
    wh = wpad.astype(BF16)
    wl = (wpad - wh.astype(F32)).astype(BF16)
    bpad = jnp.pad(b_router, ((0, 0), (0, LANES - N_EXPERTS))).reshape(DEPTH, 1, LANES)
    return wh, wl, bpad


def _router(layer, x, g, sc, sh, router_weights):
    wh, wl, bpad = router_weights
    r = np.arange(TM)
    tri = jnp.asarray(r[None, :] < r[:, None], dtype=BF16)
    narrow = pl.BlockSpec((TM, TOP_K), lambda i: (i, 0))

    def layer_spec(a):
        return pl.BlockSpec((None,) + a.shape[1:], lambda i: (layer, 0, 0))

    return pl.pallas_call(
        _router_kernel,
        grid=(NT,),
        in_specs=[_row_spec(D_MODEL), _gain_spec(g), _mod_spec(sc), _mod_spec(sh),
                  layer_spec(wh), layer_spec(wl), layer_spec(bpad), _full_spec(tri.shape)],
        out_specs=[pl.BlockSpec((TM,) + ROW_TILE, lambda i: (i, 0, 0)), narrow, narrow, narrow,
                   _full_spec((1, LANES))],
        out_shape=[
            jax.ShapeDtypeStruct((N_TOK,) + ROW_TILE, F32),
            jax.ShapeDtypeStruct((N_TOK, TOP_K), jnp.int32),
            jax.ShapeDtypeStruct((N_TOK, TOP_K), F32),
            jax.ShapeDtypeStruct((N_TOK, TOP_K), jnp.int32),
            jax.ShapeDtypeStruct((1, LANES), F32),
        ],
        scratch_shapes=[pltpu.VMEM((1, LANES), F32)],
        compiler_params=_params("arbitrary", vmem=VMEM_LIMIT_BYTES),
        name="router",
    )(x, g[0], sc[0], sh[0], wh, wl, bpad, tri)


CLEAR_PADDED, CLEAR_EMPTY = 1, 2
ROW_UNROLL = 4
DMA_QUEUES = 2


def _start_all_rows(row_copy):
    def start_rows(j, carry):
        for u in range(ROW_UNROLL):
            for k in range(TOP_K):
                row_copy(j * ROW_UNROLL + u, k).start(priority=k % DMA_QUEUES)
        return carry

    lax.fori_loop(0, TM // ROW_UNROLL, start_rows, 0)


def _dispatch_kernel(dest_ref, clear_ref, h_ref, xs_ref, zero_ref, sem_ref):
    def row_copy(t, k):
        d = dest_ref[(pl.program_id(0) * TM + t) * TOP_K + k]
        return pltpu.make_async_copy(h_ref.at[pl.ds(t, 1)], xs_ref.at[pl.ds(d, 1)], sem_ref.at[0])

    def zero_part(j, kind):
        row0 = pl.multiple_of(j * MOE_PART, MOE_PART)
        return pltpu.make_async_copy(zero_ref, xs_ref.at[pl.ds(row0, MOE_PART)], sem_ref.at[kind])

    def for_parts(kind, fn):
        def body(j, carry):
            @pl.when(clear_ref[j] == kind)
            def _():
                fn(zero_part(j, kind))
            return carry
        lax.fori_loop(0, MOE_PARTS * N_BLOCKS, body, 0)

    @pl.when(pl.program_id(0) == 0)
    def _():
        zero_ref[...] = jnp.zeros_like(zero_ref)
        for_parts(CLEAR_PADDED, lambda cp: cp.start())
        for_parts(CLEAR_EMPTY, lambda cp: cp.start())
        for_parts(CLEAR_PADDED, lambda cp: cp.wait())

    _start_all_rows(row_copy)
    for _ in range(TOP_K):
        pltpu.make_async_copy(h_ref, xs_ref.at[pl.ds(0, TM)], sem_ref.at[0]).wait()

    @pl.when(pl.program_id(0) == NT - 1)
    def _():
        for_parts(CLEAR_EMPTY, lambda cp: cp.wait())


def _dispatch(dest_flat, clear, h):
    return pl.pallas_call(
        _dispatch_kernel,
        grid_spec=pltpu.PrefetchScalarGridSpec(
            num_scalar_prefetch=2,
            grid=(NT,),
            in_specs=[pl.BlockSpec((TM,) + ROW_TILE, lambda i, d, c: (i, 0, 0))],
            out_specs=pl.BlockSpec(memory_space=pl.ANY),
            scratch_shapes=[pltpu.VMEM((MOE_PART,) + ROW_TILE, F32), pltpu.SemaphoreType.DMA((3,))],
        ),
        out_shape=jax.ShapeDtypeStruct((CAP,) + ROW_TILE, F32),
        compiler_params=_params("arbitrary", vmem=VMEM_LIMIT_BYTES),
        name="moe_dispatch",
    )(dest_flat, clear, h)


def _expert_kernel(be_ref, nh_ref, last_ref, ebuf_ref, enext_ref, xs_ref, wgu_hbm, bgu_ref, wd_hbm, bd_ref, ys_ref,
                   wgu_s, wd_s, wgu_f, wd_f, sem_ref, *, layer):
    del last_ref
    b = pl.program_id(0)
    nh = nh_ref[b]

    def weight_copies(e, buf):
        return (pltpu.make_async_copy(wgu_hbm.at[layer, e], wgu_f.at[buf], sem_ref.at[buf, 0]),
                pltpu.make_async_copy(wd_hbm.at[layer, e], wd_f.at[buf], sem_ref.at[buf, 1]))

    @pl.when((nh > 0) & ((b == 0) | (be_ref[b] != be_ref[jnp.maximum(b - 1, 0)])))
    def _():
        buf = ebuf_ref[b]

        @pl.when(b == 0)
        def _():
            for cp in weight_copies(be_ref[b], buf):
                cp.start()

        for cp in weight_copies(be_ref[b], buf):
            cp.wait()

        @pl.when(enext_ref[b] >= 0)
        def _():
            for cp in weight_copies(enext_ref[b], 1 - buf):
                cp.start()

        wgu_s[...] = wgu_f[buf].astype(BF16)
        wd_s[...] = wd_f[buf].astype(BF16)

    def ffn(rows):
        x = xs_ref[0:rows].reshape(rows, D_MODEL)
        gu = _dot(x.astype(BF16), wgu_s[...]) + bgu_ref[...]
        g = jnp.minimum(gu[:, :D_EXPERT], SWIGLU_LIMIT)
        u = jnp.clip(gu[:, D_EXPERT:], -SWIGLU_LIMIT, SWIGLU_LIMIT)
        act = (u + 1.0) * (g * jax.nn.sigmoid(SWIGLU_ALPHA * g))
        y = _dot(act.astype(BF16), wd_s[...]) + bd_ref[...]
        ys_ref[0:rows] = y.reshape((rows,) + ROW_TILE)

    def partial_block(parts):
        rows = parts * MOE_PART
        ffn(rows)
        if rows < MOE_BLK:
            ys_ref[rows:] = jnp.zeros((MOE_BLK - rows,) + ROW_TILE, F32)

    for parts in range(1, MOE_PARTS + 1):
        pl.when(nh == parts)(functools.partial(partial_block, parts))

    @pl.when(nh == 0)
    def _():
        ys_ref[...] = jnp.zeros_like(ys_ref)


def _experts(layer, block_e, block_nh, last_used, block_buf, block_next, xs, w_gu, b_gu, w_down, b_down):
    def xs_map(b, be, nh, lu, eb, en):
        return (jnp.minimum(b, lu[0]), 0, 0)

    def e_map(b, be, nh, lu, eb, en):
        return (layer, be[b], 0, 0)

    hbm = pl.BlockSpec(memory_space=pl.ANY)
    return pl.pallas_call(
        functools.partial(_expert_kernel, layer=layer),
        grid_spec=pltpu.PrefetchScalarGridSpec(
            num_scalar_prefetch=5,
            grid=(N_BLOCKS,),
            in_specs=[
                pl.BlockSpec((MOE_BLK,) + ROW_TILE, xs_map),
                hbm,
                pl.BlockSpec((None, None, 1, 2 * D_EXPERT), e_map),
                hbm,
                pl.BlockSpec((None, None, 1, D_MODEL), e_map),
            ],
            out_specs=pl.BlockSpec((MOE_BLK,) + ROW_TILE, lambda b, be, nh, lu, eb, en: (b, 0, 0)),
            scratch_shapes=[
                pltpu.VMEM((D_MODEL, 2 * D_EXPERT), BF16), pltpu.VMEM((D_EXPERT, D_MODEL), BF16),
                pltpu.VMEM((2, D_MODEL, 2 * D_EXPERT), F32), pltpu.VMEM((2, D_EXPERT, D_MODEL), F32),
                pltpu.SemaphoreType.DMA((2, 2)),
            ],
        ),
        out_shape=jax.ShapeDtypeStruct((CAP,) + ROW_TILE, F32),
        compiler_params=_params("arbitrary", vmem=VMEM_LIMIT_BYTES),
        name="moe_experts",
    )(block_e, block_nh, last_used, block_buf, block_next, xs, w_gu, b_gu.reshape(DEPTH, N_EXPERTS, 1, -1),
      w_down, b_down.reshape(DEPTH, N_EXPERTS, 1, -1))


def _combine_kernel(dest_ref, x_ref, g_ref, gate_ref, ys_ref, *rest, per_group):
    out_refs, (buf_ref, sem_ref) = rest[:-2], rest[-2:]
    i = pl.program_id(0)
    cur = lax.rem(i, 2)

    def start_tile(tile, buf):
        def row_copy(t, k):
            d = dest_ref[(tile * TM + t) * TOP_K + k]
            return pltpu.make_async_copy(ys_ref.at[pl.ds(d, 1)], buf_ref.at[buf, k, pl.ds(t, 1)], sem_ref.at[buf])
        _start_all_rows(row_copy)

    @pl.when(i == 0)
    def _():
        start_tile(0, 0)

    @pl.when(i + 1 < NT)
    def _():
        start_tile(i + 1, 1 - cur)

    for k in range(TOP_K):
        pltpu.make_async_copy(ys_ref.at[pl.ds(0, TM)], buf_ref.at[cur, k], sem_ref.at[cur]).wait()
    gates = gate_ref[...]
    ff = gates[:, 0:1] * buf_ref[cur, 0].reshape(TM, D_MODEL)
    for k in range(1, TOP_K):
        ff = ff + gates[:, k:k + 1] * buf_ref[cur, k].reshape(TM, D_MODEL)
    out = x_ref[...] + g_ref[0] * ff
    if per_group:
        def store(ref):
            ref[...] = out
        _per_group(functools.partial(store, out_refs[0]), functools.partial(store, out_refs[1]))
    else:
        out_refs[0][...] = out


def _combine(dest_flat, x, gate_vec, gates, ys, per_group):
    if per_group:
        out_specs = [_context_rows_spec(D_MODEL),
                     pl.BlockSpec((TM, D_MODEL), lambda i, *_: (jnp.maximum(i - NT_PROMPT, 0), 0))]
        out_shape = [jax.ShapeDtypeStruct((N_PROMPT, D_MODEL), F32), jax.ShapeDtypeStruct((N_SAMPLE, D_MODEL), F32)]
    else:
        out_specs = pl.BlockSpec((TM, D_MODEL), lambda i, d: (i, 0))
        out_shape = jax.ShapeDtypeStruct((N_TOK, D_MODEL), F32)
    return pl.pallas_call(
        functools.partial(_combine_kernel, per_group=per_group),
        grid_spec=pltpu.PrefetchScalarGridSpec(
            num_scalar_prefetch=1,
            grid=(NT,),
            in_specs=[
                pl.BlockSpec((TM, D_MODEL), lambda i, d: (i, 0)),
                _mod_spec(gate_vec),
                pl.BlockSpec((TM, TOP_K), lambda i, d: (i, 0)),
                pl.BlockSpec(memory_space=pl.ANY),
            ],
            out_specs=out_specs,
            scratch_shapes=[pltpu.VMEM((2, TOP_K, TM) + ROW_TILE, F32), pltpu.SemaphoreType.DMA((2,))],
        ),
        out_shape=out_shape,
        compiler_params=_params("arbitrary", vmem=VMEM_LIMIT_BYTES),
        name="moe_combine",
    )(dest_flat, x, gate_vec[0], gates, ys)


def _moe(layer, x, g, sc, sh, gate_vec, router_weights, w_gu, b_gu, w_down, b_down, per_group=False):
    h, idx, gates, pos, counts = _router(layer, x, g, sc, sh, router_weights)
    cnt = counts[0, :N_EXPERTS].astype(jnp.int32)
    nparts = (cnt + MOE_PART - 1) // MOE_PART
    nblk = (nparts + MOE_PARTS - 1) // MOE_PARTS
    e_ids = jnp.arange(N_EXPERTS, dtype=jnp.int32)
    blk_end = jnp.sum(jnp.where(e_ids[None, :] <= e_ids[:, None], nblk[None, :], 0), axis=1)
    blk_start = blk_end - nblk
    last_used = blk_end[-1] - 1

    def per_expert(table, e):
        return jnp.sum(jnp.where(e[..., None] == e_ids, table, 0), axis=-1)

    def expert_of_block(blk):
        return jnp.sum((blk_end <= jnp.minimum(blk, last_used)[:, None]).astype(jnp.int32), axis=1)

    dest = per_expert(blk_start * MOE_BLK, idx) + pos
    b_ids = jnp.arange(N_BLOCKS, dtype=jnp.int32)
    block_e = expert_of_block(b_ids)
    block_nh = jnp.where(
        b_ids <= last_used,
        jnp.clip(per_expert(nparts, block_e) - MOE_PARTS * (b_ids - per_expert(blk_start, block_e)), 0, MOE_PARTS), 0)
    h_ids = jnp.arange(MOE_PARTS * N_BLOCKS, dtype=jnp.int32)
    h_e = expert_of_block(h_ids // MOE_PARTS)
    h_nparts = per_expert(nparts, h_e)
    h_local = h_ids - MOE_PARTS * per_expert(blk_start, h_e)
    holds_rows = (h_ids // MOE_PARTS <= last_used) & (h_local < h_nparts)
    clear = jnp.where(holds_rows, jnp.where(h_local == h_nparts - 1, CLEAR_PADDED, 0), CLEAR_EMPTY)
    dest_flat = dest.reshape(-1).astype(jnp.int32)
    xs = _dispatch(dest_flat, clear.astype(jnp.int32), h)
    used = nblk > 0
    before = e_ids[None, :] < e_ids[:, None]
    expert_buf = jnp.sum(jnp.where(before & used[None, :], 1, 0), axis=1) % 2
    expert_next = jnp.min(jnp.where((e_ids[None, :] > e_ids[:, None]) & used[None, :], e_ids[None, :], N_EXPERTS),
                          axis=1)
    expert_next = jnp.where(expert_next == N_EXPERTS, -1, expert_next)
    ys = _experts(layer, block_e.astype(jnp.int32), block_nh.astype(jnp.int32),
                  last_used.reshape(1).astype(jnp.int32), per_expert(expert_buf, block_e).astype(jnp.int32),
                  per_expert(expert_next, block_e).astype(jnp.int32), xs, w_gu, b_gu, w_down, b_down)
    return _combine(dest_flat, x, gate_vec, gates, ys, per_group)


def _rope_tables(d_rot, lane0, period):
    n_rows = DEC_SEQ // GRID_W
    rows = np.repeat(np.arange(n_rows), GRID_W).astype(np.float32)
    cols = np.tile(np.arange(GRID_W), n_rows).astype(np.float32)
    half = d_rot // 2
    lane = np.arange(LANES)
    i = (lane - lane0) % period
    active = (lane >= lane0) & (i < d_rot)
    w = i % half
    f = w % (half // 2)
    pos = np.where((i // half)[None, :] == 0, rows[:, None], cols[:, None])
    sign = np.where(w < half // 2, -1.0, 1.0).astype(np.float32)
    inv = ROPE_THETA ** (-jnp.arange(0, half, 2, dtype=F32) / half)
    ang = jnp.asarray(pos) * inv[f][None, :]
    cos = jnp.where(active[None, :], jnp.cos(ang), 1.0)
    sin = jnp.where(active[None, :], jnp.sin(ang) * sign[None, :], 0.0)
    return cos, sin


def _rope_spec():
    def index(i):
        return (jnp.maximum(i - NT_PROMPT, 0) % TILES_PER_SAMPLE, 0)
    return pl.BlockSpec((TM, LANES), index)


def _pad_heads(w, n_heads, width):
    lead = w.shape[:-1]
    w = w.reshape(lead + (n_heads, width))
    w = jnp.pad(w, [(0, 0)] * len(lead) + [(0, 0), (0, HEAD_PAD - width)])
    return w.reshape(lead + (n_heads * HEAD_PAD,))


_Q_ORDER = (0, 4, 1, 5, 2, 6, 3, 7, 8, 12, 9, 13, 10, 14, 11, 15)


def _perm_q_heads(w, axis):
    shape = w.shape
    n = shape[axis]
    w = jnp.moveaxis(w, axis, 0).reshape((GQA_HEADS, n // GQA_HEADS) + tuple(s for a, s in enumerate(shape) if a != axis))
    w = w[jnp.array(_Q_ORDER)]
    w = w.reshape((n,) + w.shape[2:])
    return jnp.moveaxis(w, 0, axis)


def _constants():
    mla_cos, mla_sin = _rope_tables(MLA_ROPE, MLA_NOPE, LANES)
    gqa_cos, gqa_sin = _rope_tables(GQA_HEAD_DIM, 0, GQA_HEAD_DIM)
    dft = {}
    for L in (SEQ, DEC_SEQ):
        cm, sm = _dft_tables(L)
        fmat = jnp.concatenate([cm, sm], axis=0).astype(BF16)
        ftmat = fmat.T
        dft[L] = (cm, sm, fmat, ftmat)
    return dict(mla=(mla_cos, mla_sin), gqa=(gqa_cos, gqa_sin), dft=dft)


def _even_mixer(x, p, i, gmix, sc1, sh1, g1, consts):
    proj = _normlin(x, gmix, sc1, sh1, p["w_in_ab"], i, IN_AB_PAD)
    y_hy = []
    for L, blk0, nseq, per_step in ((SEQ, 0, BATCH, HY_PROMPT_SEQS), (DEC_SEQ, N_PROMPT // DEC_SEQ, DEC_BATCH, 1)):
        cm, sm, fmat, ftmat = consts["dft"][L]
        kc, ks = _hy_filter(L, cm, sm, p["hy_filter_w1"][i], p["hy_filter_b1"][i], p["hy_filter_freq"][i],
                            p["hy_filter_w2"][i], p["hy_filter_b2"][i], p["hy_filter_w3"][i],
                            p["hy_filter_b3"][i], p["hy_log_decay"][i])
        y_hy.append(_hy_mix(proj, blk0, nseq, per_step, L, p["hy_conv_w"][i], p["hy_conv_b"][i], kc, ks,
                            p["hy_bias"][i], fmat, ftmat))

    wq = _pad_heads(p["mla_wq_b"][i], MLA_HEADS, MLA_QK).astype(BF16)
    wkv = p["mla_wkv_b"][i].astype(BF16)
    qg = jnp.pad(p["mla_q_norm"][i], (0, HEAD_PAD - MLA_QK)).reshape(1, HEAD_PAD)
    kg = jnp.pad(p["mla_k_norm"][i], (0, HEAD_PAD - MLA_QK)).reshape(1, HEAD_PAD)
    mla_cos, mla_sin = consts["mla"]
    q, k, v, lat = _mla_prep(proj, p["mla_q_lora_norm"][i].reshape(1, -1), wq, qg,
                             p["mla_kv_lora_norm"][i].reshape(1, -1), wkv, kg, mla_cos, mla_sin)
    ctx = jnp.pad(p["cache_mla_latent"][:, i].reshape(DEC_BATCH * PAST_LEN, -1),
                  ((0, 0), (0, KV_PAD - MLA_KV_LORA - MLA_ROPE)))
    kc_ctx, vc_ctx = _mla_ctx(ctx, wkv, kg)
    nqk = MLA_HEADS * HEAD_PAD
    o = _attention(_mla_attn_kernel, q, k, v, kc_ctx, vc_ctx, nqk, nqk, nqk, MLA_HEADS * MLA_V)
    w_out = p["w_out_ab"]
    x = _outproj([y_hy, o], [(w_out, (i, 0, 0), HY_D), (w_out, (i, 1, 0), HY_D)], x, g1)
    return x, lat, dict(y_hy=jnp.concatenate(y_hy, axis=0), o=jnp.concatenate(o, axis=0))


def _odd_mixer(x, p, i, gmix, sc1, sh1, g1, consts):
    nq = GQA_HEADS * GQA_HEAD_DIM
    nk = GQA_KV_HEADS * GQA_HEAD_DIM
    w = p["w_qkv_c"][i]
    w_qkv = jnp.concatenate([_perm_q_heads(w[:, :nq], 1), w[:, nq:]], axis=1).astype(BF16)
    qg = jnp.tile(p["gqa_q_norm"][i], LANES // GQA_HEAD_DIM).reshape(1, LANES)
    kg = jnp.tile(p["gqa_k_norm"][i], LANES // GQA_HEAD_DIM).reshape(1, LANES)
    gqa_cos, gqa_sin = consts["gqa"]
    q, k, v, k_plain, v_plain = _gqa_prep(x, gmix, sc1, sh1, w_qkv, qg, kg, gqa_cos, gqa_sin)
    kc_ctx = p["cache_gqa_k"][:, i].reshape(DEC_BATCH * PAST_LEN, -1).astype(BF16)
    vc_ctx = p["cache_gqa_v"][:, i].reshape(DEC_BATCH * PAST_LEN, -1).astype(BF16)
    o = _attention(_gqa_attn_kernel, q, k, v, kc_ctx, vc_ctx, nq, nk, nk, nq)
    w_out = _perm_q_heads(p["w_out_c"][i], 0)[None]
    x = _outproj([o], [(w_out, (0, 0, 0), nq)], x, g1)
    return x, k_plain, v_plain


def kernel(x_prompt, x_sample, cache_mla_latent, cache_gqa_k, cache_gqa_v, c, c_ctx, w_ada, b_ada, norm_mix, norm_ffn, w_in_ab, hy_conv_w, hy_conv_b, hy_filter_w1, hy_filter_b1, hy_filter_freq, hy_filter_w2, hy_filter_b2, hy_filter_w3, hy_filter_b3, hy_log_decay, hy_bias, mla_q_lora_norm, mla_wq_b, mla_kv_lora_norm, mla_wkv_b, mla_q_norm, mla_k_norm, w_out_ab, w_qkv_c, gqa_q_norm, gqa_k_norm, w_out_c, moe_router_w, moe_router_b, moe_w_gate_up, moe_b_gate_up, moe_w_down, moe_b_down):
    p = dict(locals())
    x = jnp.concatenate([x_prompt.reshape(N_PROMPT, D_MODEL), x_sample.reshape(N_SAMPLE, D_MODEL)], axis=0)

    cond = jnp.concatenate([c_ctx[None, :], c, jnp.zeros((COND_ROWS - N_COND, D_MODEL), F32)], axis=0)
    mods = _modulation(cond, w_ada, b_ada)
    tile_cond = jnp.concatenate([jnp.zeros((NT_PROMPT,), jnp.int32),
                                 1 + jnp.arange(NT - NT_PROMPT, dtype=jnp.int32) // TILES_PER_SAMPLE])
    mods = mods[:, tile_cond].reshape(DEPTH, NT, 6, 1, D_MODEL).transpose(0, 2, 1, 3, 4)

    consts = _constants()
    router_weights = _router_weights(moe_router_w, moe_router_b)
    gains_mix = norm_mix.reshape(DEPTH, 1, D_MODEL)
    gains_ffn = norm_ffn.reshape(DEPTH, 1, D_MODEL)

    lat_out, k_out, v_out = [], [], []
    for l in range(DEPTH):
        sh1, sc1, g1, sh2, sc2, g2 = ((mods, l, j) for j in range(6))
        i = l // 2
        gmix = (gains_mix, l)
        if l % 2 == 0:
            x, lat, _ = _even_mixer(x, p, i, gmix, sc1, sh1, g1, consts)
            lat_out.append(lat.reshape(BATCH, SEQ, -1))
        else:
            x, k_plain, v_plain = _odd_mixer(x, p, i, gmix, sc1, sh1, g1, consts)
            k_out.append(k_plain.reshape(BATCH, SEQ, GQA_KV_HEADS, GQA_HEAD_DIM))
            v_out.append(v_plain.reshape(BATCH, SEQ, GQA_KV_HEADS, GQA_HEAD_DIM))
        x = _moe(l, x, (gains_ffn, l), sc2, sh2, g2, router_weights,
                 moe_w_gate_up, moe_b_gate_up, moe_w_down, moe_b_down, per_group=(l == DEPTH - 1))

    y_prompt = x[0].reshape(BATCH, SEQ, D_MODEL)
    y_sample = x[1].reshape(DEC_BATCH, DEC_SEQ, D_MODEL)
    return (y_prompt, y_sample, jnp.stack(lat_out, axis=1), jnp.stack(k_out, axis=1), jnp.stack(v_out, axis=1))
```

```python
import functools
import math

import jax
import jax.numpy as jnp
import numpy as np
from jax import lax
from jax.experimental import pallas as pl
from jax.experimental.pallas import tpu as pltpu

F32 = jnp.float32
BF16 = jnp.bfloat16

D_MODEL = 1024
BATCH = 32
SEQ = 256
DEPTH = 4
DEC_BATCH = 4
DEC_SEQ = 1024
PAST_LEN = 256
GRID_W = 64
N_EVEN = (DEPTH + 1) // 2
N_ODD = DEPTH // 2
HY_D = D_MODEL // 2
HY_ORDER = 2
HY_BANDS = 16
HY_EMB = 2 * HY_BANDS + 1
HY_FILTER_HIDDEN = 64
MLA_HEADS = 8
MLA_NOPE = 64
MLA_ROPE = 32
MLA_QK = MLA_NOPE + MLA_ROPE
MLA_V = HY_D // MLA_HEADS
MLA_Q_LORA = 3 * D_MODEL // 8
MLA_KV_LORA = D_MODEL // 4
GQA_HEADS = 16
GQA_KV_HEADS = 4
GQA_HEAD_DIM = D_MODEL // GQA_HEADS
N_EXPERTS = 32
TOP_K = 4
D_EXPERT = D_MODEL
SWIGLU_LIMIT = 7.0
SWIGLU_ALPHA = 1.702
ROPE_THETA = 10000.0
RMS_EPS = 1e-6
GQA_SCALE = GQA_HEAD_DIM ** -0.5
assert math.frexp(GQA_SCALE)[0] == 0.5
IN_AB = (HY_ORDER + 1) * HY_D + MLA_Q_LORA + MLA_KV_LORA + MLA_ROPE

N_PROMPT = BATCH * SEQ
N_SAMPLE = DEC_BATCH * DEC_SEQ
N_TOK = N_PROMPT + N_SAMPLE

LANES = 128
SUBLANES = 8
VMEM_LIMIT_BYTES = 56 * 1024 * 1024

TM = 512
NT = N_TOK // TM
NT_PROMPT = N_PROMPT // TM
TILES_PER_SAMPLE = DEC_SEQ // TM
N_COND = 1 + DEC_BATCH
COND_ROWS = 8
HEAD_PAD = LANES
IN_AB_PAD = 2304
KV_PAD = 384
MOE_BLK = 512
MOE_PARTS = 4
MOE_PART = MOE_BLK // MOE_PARTS
N_SLOTS = N_TOK * TOP_K
N_BLOCKS = N_SLOTS // MOE_BLK + N_EXPERTS
CAP = N_BLOCKS * MOE_BLK
ROW_TILE = (SUBLANES, LANES)
assert SUBLANES * LANES == D_MODEL
HY_CH = 256
HY_PROMPT_SEQS = 8


def _dot(a, b):
    return jnp.dot(a, b, preferred_element_type=F32)


def _dot_nt(a, b):
    return lax.dot_general(a, b, (((1,), (1,)), ((), ())), preferred_element_type=F32)


def _split(x):
    hi = x.astype(BF16)
    lo = (x - hi.astype(F32)).astype(BF16)
    return hi, lo


def _dot3(a, b):
    ah, al = _split(a)
    bh, bl = _split(b)
    return _dot(ah, bh) + (_dot(ah, bl) + _dot(al, bh))


def _lane_iota(shape):
    return lax.broadcasted_iota(jnp.int32, shape, len(shape) - 1)


def _params(*sem, vmem=None):
    return pltpu.CompilerParams(dimension_semantics=sem, vmem_limit_bytes=vmem)


def _mod_kernel(c_ref, w_ref, b_ref, o_ref):
    c = c_ref[...]
    s = c * jax.nn.sigmoid(c)
    o_ref[0] = _dot(s.astype(BF16), w_ref[0].astype(BF16)) + b_ref[0]


def _modulation(cond, w_ada, b_ada):
    nblk = 6
    return pl.pallas_call(
        _mod_kernel,
        grid=(DEPTH, nblk),
        in_specs=[
            pl.BlockSpec((COND_ROWS, D_MODEL), lambda l, j: (0, 0)),
            pl.BlockSpec((1, D_MODEL, D_MODEL), lambda l, j: (l, 0, j)),
            pl.BlockSpec((1, 1, D_MODEL), lambda l, j: (l, 0, j)),
        ],
        out_specs=pl.BlockSpec((1, COND_ROWS, D_MODEL), lambda l, j: (l, 0, j)),
        out_shape=jax.ShapeDtypeStruct((DEPTH, COND_ROWS, nblk * D_MODEL), F32),
        compiler_params=_params("arbitrary", "arbitrary"),
        name="modulation",
    )(cond, w_ada, b_ada.reshape(DEPTH, 1, nblk * D_MODEL))


def _norm_mod(x, g, sc, sh):
    ms = jnp.mean(x * x, axis=-1, keepdims=True)
    y = x * lax.rsqrt(ms + RMS_EPS)
    return (y * g) * (1.0 + sc) + sh


def _row_spec(width):
    return pl.BlockSpec((TM, width), lambda i: (i, 0))


def _mod_spec(mod):
    _, layer, which = mod
    return pl.BlockSpec((None, None, 1, 1, D_MODEL), lambda i, *_: (layer, which, i, 0, 0))


def _gain_spec(gain):
    _, layer = gain
    return pl.BlockSpec((None, 1, D_MODEL), lambda i, *_: (layer, 0, 0))


def _full_spec(shape):
    nd = len(shape)
    return pl.BlockSpec(shape, lambda i: (0,) * nd)


def _normlin_kernel(x_ref, g_ref, sc_ref, sh_ref, w_ref, o_ref, w_scr):
    nin = w_ref.shape[1]

    @pl.when(pl.program_id(0) == 0)
    def _():
        w_scr[:, :nin] = w_ref[...].astype(BF16)
        w_scr[:, nin:] = jnp.zeros((D_MODEL, w_scr.shape[1] - nin), BF16)

    h = _norm_mod(x_ref[...], g_ref[...], sc_ref[0], sh_ref[0])
    o_ref[...] = _dot(h.astype(BF16), w_scr[...])


def _normlin(x, g, sc, sh, w_all, layer, nout):
    nin = w_all.shape[2]
    return pl.pallas_call(
        _normlin_kernel,
        grid=(NT,),
        in_specs=[_row_spec(D_MODEL), _gain_spec(g), _mod_spec(sc), _mod_spec(sh),
                  pl.BlockSpec((None, D_MODEL, nin), lambda i: (layer, 0, 0))],
        out_specs=_row_spec(nout),
        out_shape=jax.ShapeDtypeStruct((N_TOK, nout), F32),
        scratch_shapes=[pltpu.VMEM((D_MODEL, nout), BF16)],
        compiler_params=_params("arbitrary", vmem=VMEM_LIMIT_BYTES),
        name="normlin",
    )(x, g[0], sc[0], sh[0], w_all)


def _hy_filter_kernel(z_ref, w1_ref, b1_ref, fr_ref, w2_ref, b2_ref, w3_ref, b3_ref, ed_ref,
                      c_ref, s_ref, kc_ref, ks_ref):
    L = z_ref.shape[0]
    z = z_ref[...]
    fr = fr_ref[...]
    hdn = jnp.sin(fr * (_dot3(z, w1_ref[...]) + b1_ref[...]))
    hdn = jnp.sin(fr * (_dot3(hdn, w2_ref[...]) + b2_ref[...]))
    filt = _dot3(hdn, w3_ref[...]) + b3_ref[...]
    t = z[:, 0:1]
    filt = filt * jnp.exp(-t * ed_ref[...])
    row = lax.broadcasted_iota(jnp.int32, (L, HY_D), 0)
    cm = c_ref[...]
    sm = s_ref[...]
    for o in range(HY_ORDER):
        fw = filt[:, (2 * o) * HY_D:(2 * o + 1) * HY_D]
        bw = filt[:, (2 * o + 1) * HY_D:(2 * o + 2) * HY_D]
        den = (jnp.sum(jnp.abs(fw), axis=0, keepdims=True)
               + jnp.sum(jnp.abs(bw), axis=0, keepdims=True)) + 1e-6
        fw = fw / den
        bw = jnp.where(row == 0, 0.0, bw / den)
        kc_ref[o] = _dot3(cm, fw + bw) * (1.0 / L)
        ks_ref[o] = _dot3(sm, fw - bw) * (1.0 / L)


def _dft_tables(L):
    m = jnp.arange(L, dtype=jnp.int32)
    phase = ((2 * m + 1)[:, None] * m[None, :]) % (4 * L)
    ang = phase.astype(F32) * (2.0 * math.pi / (4 * L))
    return jnp.cos(ang), jnp.sin(ang)


def _filter_features(L):
    p = jnp.arange(L, dtype=F32)
    t = p / max(L - 1, 1)
    bands = jnp.linspace(1e-4, HY_BANDS - 1, HY_BANDS, dtype=F32)
    ang = (2.0 * math.pi / L) * p[:, None] * bands[None, :]
    z = jnp.concatenate([t[:, None], jnp.cos(ang), -jnp.sin(ang)], axis=-1)
    return jnp.pad(z, ((0, 0), (0, LANES - HY_EMB)))


def _hy_filter(L, cmat, smat, w1, b1, fr, w2, b2, w3, b3, log_decay):
    nf = HY_ORDER * 2 * HY_D
    args = (
        _filter_features(L),
        jnp.pad(w1, ((0, LANES - HY_EMB), (0, 0))),
        b1.reshape(1, -1), fr.reshape(1, -1), w2, b2.reshape(1, -1), w3, b3.reshape(1, -1),
        jnp.exp(log_decay.astype(F32)).reshape(1, nf),
        cmat, smat,
    )
    out_sds = jax.ShapeDtypeStruct((HY_ORDER, L, HY_D), F32)
    return pl.pallas_call(
        _hy_filter_kernel,
        grid=(1,),
        in_specs=[_full_spec(a.shape) for a in args],
        out_specs=[_full_spec(out_sds.shape)] * 2,
        out_shape=[out_sds, out_sds],
        compiler_params=_params("arbitrary", vmem=VMEM_LIMIT_BYTES),
        name=f"hy_filter_{L}",
    )(*args)


def _hy_mix_kernel(u0_ref, u1_ref, u2_ref, cw0_ref, cw1_ref, cw2_ref, cb0_ref, cb1_ref, cb2_ref,
                   kc_ref, ks_ref, hb_ref, f_ref, ft_ref, o_ref):
    S, L, _ = u0_ref.shape
    row = lax.broadcasted_iota(jnp.int32, (L, HY_CH), 0)

    def lanes(per_seq):
        return per_seq[0] if S == 1 else jnp.concatenate(per_seq, axis=1)

    def short_conv(u_ref, cw_ref, cb_ref):
        w = cw_ref[0]
        out = []
        for s in range(S):
            u = u_ref[s]
            prev = jnp.where(row == 0, 0.0, pltpu.roll(u, 1, 0))
            nxt = jnp.where(row == L - 1, 0.0, pltpu.roll(u, L - 1, 0))
            out.append((prev * w[0:1] + u * w[1:2]) + nxt * w[2:3] + cb_ref[0])
        return lanes(out)

    z = short_conv(u0_ref, cw0_ref, cb0_ref)
    gates = (short_conv(u1_ref, cw1_ref, cb1_ref), short_conv(u2_ref, cw2_ref, cb2_ref))
    for o in range(HY_ORDER):
        zz = _dot(f_ref[...], z.astype(BF16))
        cz, sz = zz[:L], zz[L:]
        kc, ks = lanes([kc_ref[o]] * S), lanes([ks_ref[o]] * S)
        w1 = cz * kc - sz * ks
        w2 = cz * ks + sz * kc
        ww = jnp.concatenate([w1, w2], axis=0).astype(BF16)
        conv = _dot(ft_ref[...], ww)
        z = gates[o] * (conv + z * lanes([hb_ref[0, o]] * S))
    for s in range(S):
        o_ref[s] = z[:, s * HY_CH:(s + 1) * HY_CH].astype(o_ref.dtype)


def _hy_mix(proj, row_block0, nseq, seqs_per_step, L, cw, cb, kc, ks, hb, fmat, ftmat):
    nch = HY_D // HY_CH
    nparts = HY_ORDER + 1
    S = seqs_per_step
    assert nseq % S == 0 and row_block0 % S == 0
    cw3 = cw.reshape(3, nparts * nch, HY_CH).transpose(1, 0, 2)
    cb3 = cb.reshape(nparts * nch, 1, HY_CH)
    hb3 = hb.reshape(HY_ORDER, nch, 1, HY_CH).transpose(1, 0, 2, 3)
    proj = proj.reshape(N_TOK // L, L, proj.shape[1])

    def u_spec(part):
        return pl.BlockSpec((S, L, HY_CH), lambda s, c: (row_block0 // S + s, 0, part * nch + c))

    def cw_spec(part):
        return pl.BlockSpec((1, 3, HY_CH), lambda s, c: (part * nch + c, 0, 0))

    def cb_spec(part):
        return pl.BlockSpec((1, 1, HY_CH), lambda s, c: (part * nch + c, 0, 0))

    return pl.pallas_call(
        _hy_mix_kernel,
        grid=(nseq // S, nch),
        in_specs=[
            u_spec(0), u_spec(1), u_spec(2),
            cw_spec(0), cw_spec(1), cw_spec(2),
            cb_spec(0), cb_spec(1), cb_spec(2),
            pl.BlockSpec((HY_ORDER, L, HY_CH), lambda s, c: (0, 0, c)),
            pl.BlockSpec((HY_ORDER, L, HY_CH), lambda s, c: (0, 0, c)),
            pl.BlockSpec((1, HY_ORDER, 1, HY_CH), lambda s, c: (c, 0, 0, 0)),
            pl.BlockSpec((2 * L, L), lambda s, c: (0, 0)),
            pl.BlockSpec((L, 2 * L), lambda s, c: (0, 0)),
        ],
        out_specs=pl.BlockSpec((S, L, HY_CH), lambda s, c: (s, 0, c)),
        out_shape=jax.ShapeDtypeStruct((nseq, L, HY_D), BF16),
        compiler_params=_params("arbitrary", "arbitrary", vmem=VMEM_LIMIT_BYTES),
        name=f"hy_mix_{L}",
    )(proj, proj, proj, cw3, cw3, cw3, cb3, cb3, cb3, kc, ks, hb3, fmat, ftmat).reshape(nseq * L, HY_D)


def _head_rms(xh, gain, dim):
    ms = jnp.sum(xh * xh, axis=-1, keepdims=True) * (1.0 / dim)
    return (xh * lax.rsqrt(ms + RMS_EPS)) * gain


def _context_rows_spec(width):
    return pl.BlockSpec((TM, width), lambda i, *_: (jnp.minimum(i, NT_PROMPT - 1), 0))


def _per_group(context_fn, latent_fn):
    is_context = pl.program_id(0) < NT_PROMPT
    pl.when(is_context)(context_fn)
    pl.when(jnp.logical_not(is_context))(latent_fn)


def _rope(xh, cos, sin, half):
    lane = _lane_iota(xh.shape)
    first = (lane % (2 * half)) < half
    rot = jnp.where(first, pltpu.roll(xh, LANES - half, 1), pltpu.roll(xh, half, 1))
    return xh * cos + rot * sin


def _mla_keys_values(lat_n, kr_blk, wkv_ref, kg_ref, rope):
    kv = _dot(lat_n.astype(BF16), wkv_ref[...])
    kr = pltpu.roll(kr_blk, MLA_NOPE, 1)
    nope = _lane_iota(kr.shape) < MLA_NOPE
    ks = []
    for h in range(MLA_HEADS):
        kh = jnp.where(nope, kv[:, h * HEAD_PAD:(h + 1) * HEAD_PAD], 0.0) + kr
        kh = _head_rms(kh, kg_ref[...], MLA_QK)
        if rope is not None:
            kh = _rope(kh, rope[0], rope[1], MLA_ROPE // 4)
        ks.append(kh)
    return ks, kv


def _mla_prep_kernel(qa_ref, kva_ref, gq_ref, wq_ref, qg_ref, gkv_ref, wkv_ref, kg_ref,
                     cos_ref, sin_ref, q_ref, k_ref, v_ref, lat_ref):
    def body(rope):
        qa = qa_ref[...]
        ms = jnp.mean(qa * qa, axis=-1, keepdims=True)
        qn = (qa * lax.rsqrt(ms + RMS_EPS)) * gq_ref[...]
        q = _dot(qn.astype(BF16), wq_ref[...])
        for h in range(MLA_HEADS):
            qh = _head_rms(q[:, h * HEAD_PAD:(h + 1) * HEAD_PAD], qg_ref[...], MLA_QK)
            if rope is not None:
                qh = _rope(qh, rope[0], rope[1], MLA_ROPE // 4)
            q_ref[:, h * HEAD_PAD:(h + 1) * HEAD_PAD] = qh.astype(BF16)

        kva = kva_ref[...]
        lat = kva[:, :MLA_KV_LORA]
        ms = jnp.mean(lat * lat, axis=-1, keepdims=True)
        lat_n = (lat * lax.rsqrt(ms + RMS_EPS)) * gkv_ref[...]
        kr_blk = kva[:, MLA_KV_LORA:]
        if rope is None:
            lat_ref[:, :MLA_KV_LORA] = lat_n
            lat_ref[:, MLA_KV_LORA:] = kr_blk[:, :MLA_ROPE]
        ks, v = _mla_keys_values(lat_n, kr_blk, wkv_ref, kg_ref, rope)
        for h in range(MLA_HEADS):
            k_ref[:, h * HEAD_PAD:(h + 1) * HEAD_PAD] = ks[h].astype(BF16)
        v_ref[...] = v.astype(BF16)

    _per_group(lambda: body(None), lambda: body((cos_ref[...], sin_ref[...])))


def _mla_prep(proj, gq, wq, qg, gkv, wkv, kg, cos_t, sin_t):
    nq = MLA_HEADS * HEAD_PAD
    nv = MLA_HEADS * (MLA_NOPE + MLA_V)
    qa_blk = (HY_ORDER + 1) * HY_D // KV_PAD
    return pl.pallas_call(
        _mla_prep_kernel,
        grid=(NT,),
        in_specs=[
            pl.BlockSpec((TM, KV_PAD), lambda i: (i, qa_blk)),
            pl.BlockSpec((TM, KV_PAD), lambda i: (i, qa_blk + 1)),
            _full_spec(gq.shape), _full_spec(wq.shape), _full_spec(qg.shape), _full_spec(gkv.shape),
            _full_spec(wkv.shape), _full_spec(kg.shape),
            _rope_spec(), _rope_spec(),
        ],
        out_specs=[_row_spec(nq), _row_spec(nq), _row_spec(nv), _context_rows_spec(MLA_KV_LORA + MLA_ROPE)],
        out_shape=[
            jax.ShapeDtypeStruct((N_TOK, nq), BF16),
            jax.ShapeDtypeStruct((N_TOK, nq), BF16),
            jax.ShapeDtypeStruct((N_TOK, nv), BF16),
            jax.ShapeDtypeStruct((N_PROMPT, MLA_KV_LORA + MLA_ROPE), F32),
        ],
        compiler_params=_params("arbitrary", vmem=VMEM_LIMIT_BYTES),
        name="mla_prep",
    )(proj, proj, gq, wq, qg, gkv, wkv, kg, cos_t, sin_t)


def _mla_ctx_kernel(lat_ref, wkv_ref, kg_ref, k_ref, v_ref):
    lat = lat_ref[...]
    ks, v = _mla_keys_values(lat[:, :MLA_KV_LORA], lat[:, MLA_KV_LORA:], wkv_ref, kg_ref, None)
    for h in range(MLA_HEADS):
        k_ref[:, h * HEAD_PAD:(h + 1) * HEAD_PAD] = ks[h].astype(BF16)
    v_ref[...] = v.astype(BF16)


def _mla_ctx(lat_pad, wkv, kg):
    n = lat_pad.shape[0]
    nq = MLA_HEADS * HEAD_PAD
    nv = MLA_HEADS * (MLA_NOPE + MLA_V)
    return pl.pallas_call(
        _mla_ctx_kernel,
        grid=(n // TM,),
        in_specs=[_row_spec(KV_PAD), _full_spec(wkv.shape), _full_spec(kg.shape)],
        out_specs=[_row_spec(nq), _row_spec(nv)],
        out_shape=[jax.ShapeDtypeStruct((n, nq), BF16), jax.ShapeDtypeStruct((n, nv), BF16)],
        compiler_params=_params("arbitrary"),
        name="mla_ctx",
    )(lat_pad, wkv, kg)


def _softmax_pv(q, keys, vals, scale):
    ss = [_dot_nt(q, k) for k in keys]
    if scale is not None:
        ss = [s * scale for s in ss]
    m = ss[0].max(axis=-1, keepdims=True)
    for s in ss[1:]:
        m = jnp.maximum(m, s.max(axis=-1, keepdims=True))
    ps = [jnp.exp(s - m) for s in ss]
    l = ps[0].sum(axis=-1, keepdims=True)
    for p in ps[1:]:
        l = l + p.sum(axis=-1, keepdims=True)
    o = _dot(ps[0].astype(BF16), vals[0])
    for p, v in zip(ps[1:], vals[1:]):
        o = o + _dot(p.astype(BF16), v)
    return o / l


def _mla_attn_kernel(*refs, has_ctx):
    if has_ctx:
        q_ref, k_ref, v_ref, kc_ref, vc_ref, o_ref = refs
    else:
        q_ref, k_ref, v_ref, o_ref = refs
    scale = MLA_QK ** -0.5
    lo = _lane_iota((q_ref.shape[0], LANES)) < MLA_V
    for j in range(MLA_HEADS // 2):
        outs = []
        for h in (2 * j, 2 * j + 1):
            sl = slice(h * HEAD_PAD, (h + 1) * HEAD_PAD)
            ks, vs = [k_ref[:, sl]], [v_ref[:, sl]]
            if has_ctx:
                ks.append(kc_ref[:, sl])
                vs.append(vc_ref[:, sl])
            outs.append(_softmax_pv(q_ref[:, sl], ks, vs, scale))
        pair = jnp.where(lo, pltpu.roll(outs[0], MLA_V, 1), outs[1])
        o_ref[:, j * LANES:(j + 1) * LANES] = pair.astype(o_ref.dtype)


def _gqa_attn_kernel(*refs, has_ctx):
    if has_ctx:
        q_ref, k_ref, v_ref, kc_ref, vc_ref, o_ref = refs
    else:
        q_ref, k_ref, v_ref, o_ref = refs
    scale = None
    lo = _lane_iota((q_ref.shape[0], LANES)) < GQA_HEAD_DIM
    pairs_per_kv = (GQA_HEADS // 2) // (GQA_KV_HEADS // 2)
    for p in range(GQA_HEADS // 2):
        kv = slice((p // pairs_per_kv) * LANES, (p // pairs_per_kv + 1) * LANES)
        ks, vs = [k_ref[:, kv]], [v_ref[:, kv]]
        if has_ctx:
            ks.append(kc_ref[:, kv])
            vs.append(vc_ref[:, kv])
        qp = q_ref[:, p * LANES:(p + 1) * LANES]
        zero = jnp.zeros_like(qp)
        o_lo = _softmax_pv(jnp.where(lo, qp, zero), ks, vs, scale)
        o_hi = _softmax_pv(jnp.where(lo, zero, qp), ks, vs, scale)
        o_ref[:, p * LANES:(p + 1) * LANES] = jnp.where(lo, o_lo, o_hi).astype(o_ref.dtype)


def _attention(body, q, k, v, kc, vc, wq, wk, wv, wo):
    outs = []
    outs.append(pl.pallas_call(
        functools.partial(body, has_ctx=False),
        grid=(BATCH,),
        in_specs=[pl.BlockSpec((SEQ, wq), lambda b: (b, 0)),
                  pl.BlockSpec((SEQ, wk), lambda b: (b, 0)),
                  pl.BlockSpec((SEQ, wv), lambda b: (b, 0))],
        out_specs=pl.BlockSpec((SEQ, wo), lambda b: (b, 0)),
        out_shape=jax.ShapeDtypeStruct((N_PROMPT, wo), BF16),
        compiler_params=_params("arbitrary", vmem=VMEM_LIMIT_BYTES),
        name="attn_prompt",
    )(q, k, v))
    qt = DEC_SEQ // TM
    q0 = N_PROMPT // TM
    s0 = N_PROMPT // DEC_SEQ
    outs.append(pl.pallas_call(
        functools.partial(body, has_ctx=True),
        grid=(DEC_BATCH, qt),
        in_specs=[pl.BlockSpec((TM, wq), lambda b, t: (q0 + b * qt + t, 0)),
                  pl.BlockSpec((DEC_SEQ, wk), lambda b, t: (s0 + b, 0)),
                  pl.BlockSpec((DEC_SEQ, wv), lambda b, t: (s0 + b, 0)),
                  pl.BlockSpec((PAST_LEN, wk), lambda b, t: (b, 0)),
                  pl.BlockSpec((PAST_LEN, wv), lambda b, t: (b, 0))],
        out_specs=pl.BlockSpec((TM, wo), lambda b, t: (b * qt + t, 0)),
        out_shape=jax.ShapeDtypeStruct((N_SAMPLE, wo), BF16),
        compiler_params=_params("arbitrary", "arbitrary", vmem=VMEM_LIMIT_BYTES),
        name="attn_sample",
    )(q, k, v, kc, vc))
    return outs


def _gqa_prep_kernel(x_ref, g_ref, sc_ref, sh_ref, w_ref, qg_ref, kg_ref, cos_ref, sin_ref,
                     q_ref, k_ref, v_ref, kp_ref, vp_ref):
    lo = _lane_iota((TM, LANES)) < GQA_HEAD_DIM
    nq = GQA_HEADS * GQA_HEAD_DIM
    nk = GQA_KV_HEADS * GQA_HEAD_DIM

    def pair_norm(xp, gain):
        sq = xp * xp
        ms_lo = jnp.sum(jnp.where(lo, sq, 0.0), axis=-1, keepdims=True)
        ms_hi = jnp.sum(jnp.where(lo, 0.0, sq), axis=-1, keepdims=True)
        ms = jnp.where(lo, ms_lo, ms_hi) * (1.0 / GQA_HEAD_DIM)
        return (xp * lax.rsqrt(ms + RMS_EPS)) * gain

    def body(rope):
        def rotate(xp):
            return xp if rope is None else _rope(xp, rope[0], rope[1], GQA_HEAD_DIM // 4)

        h = _norm_mod(x_ref[...], g_ref[...], sc_ref[0], sh_ref[0])
        qkv = _dot(h.astype(BF16), w_ref[...])
        for p in range(nq // LANES):
            sl = slice(p * LANES, (p + 1) * LANES)
            qn = pair_norm(qkv[:, sl], qg_ref[...])
            q_ref[:, sl] = (rotate(qn) * GQA_SCALE).astype(BF16)
        for p in range(nk // LANES):
            sl = slice(p * LANES, (p + 1) * LANES)
            kn = pair_norm(qkv[:, nq + p * LANES:nq + (p + 1) * LANES], kg_ref[...])
            if rope is None:
                kp_ref[:, sl] = kn
            k_ref[:, sl] = rotate(kn).astype(BF16)
        v = qkv[:, nq + nk:]
        if rope is None:
            vp_ref[...] = v
        v_ref[...] = v.astype(BF16)

    _per_group(lambda: body(None), lambda: body((cos_ref[...], sin_ref[...])))


def _gqa_prep(x, g, sc, sh, w, qg, kg, cos_t, sin_t):
    nq = GQA_HEADS * GQA_HEAD_DIM
    nk = GQA_KV_HEADS * GQA_HEAD_DIM
    return pl.pallas_call(
        _gqa_prep_kernel,
        grid=(NT,),
        in_specs=[_row_spec(D_MODEL), _gain_spec(g), _mod_spec(sc), _mod_spec(sh),
                  _full_spec(w.shape), _full_spec(qg.shape), _full_spec(kg.shape),
                  _rope_spec(), _rope_spec()],
        out_specs=[_row_spec(nq), _row_spec(nk), _row_spec(nk), _context_rows_spec(nk), _context_rows_spec(nk)],
        out_shape=[
            jax.ShapeDtypeStruct((N_TOK, nq), BF16),
            jax.ShapeDtypeStruct((N_TOK, nk), BF16),
            jax.ShapeDtypeStruct((N_TOK, nk), BF16),
            jax.ShapeDtypeStruct((N_PROMPT, nk), F32),
            jax.ShapeDtypeStruct((N_PROMPT, nk), F32),
        ],
        compiler_params=_params("arbitrary", vmem=VMEM_LIMIT_BYTES),
        name="gqa_prep",
    )(x, g[0], sc[0], sh[0], w, qg, kg, cos_t, sin_t)


def _outproj_kernel(*refs, n_in):
    ap_refs = refs[:n_in]
    as_refs = refs[n_in:2 * n_in]
    w_refs = refs[2 * n_in:3 * n_in]
    x_ref, g_ref, o_ref = refs[3 * n_in:3 * n_in + 3]
    w_scr = refs[3 * n_in + 3:]
    is_prompt = pl.program_id(0) < NT_PROMPT

    @pl.when(pl.program_id(0) == 0)
    def _():
        for w, s in zip(w_refs, w_scr):
            s[...] = w[...].astype(BF16)

    y = None
    for ap, asm, s in zip(ap_refs, as_refs, w_scr):
        a = jnp.where(is_prompt, ap[...], asm[...])
        d = _dot(a, s[...])
        y = d if y is None else y + d
    o_ref[...] = x_ref[...] + g_ref[0] * y


def _outproj(acts, ws, x, gate):
    n_in = len(acts)

    def w_spec(index, rows):
        return pl.BlockSpec((None, rows, D_MODEL), lambda i: index)

    def prompt_spec(width):
        return pl.BlockSpec((TM, width), lambda i: (jnp.minimum(i, NT_PROMPT - 1), 0))

    def sample_spec(width):
        return pl.BlockSpec((TM, width), lambda i: (jnp.maximum(i - NT_PROMPT, 0), 0))

    return pl.pallas_call(
        functools.partial(_outproj_kernel, n_in=n_in),
        grid=(NT,),
        in_specs=([prompt_spec(ap.shape[1]) for ap, _ in acts] + [sample_spec(asm.shape[1]) for _, asm in acts]
                  + [w_spec(index, rows) for _, index, rows in ws] + [_row_spec(D_MODEL), _mod_spec(gate)]),
        out_specs=_row_spec(D_MODEL),
        out_shape=jax.ShapeDtypeStruct((N_TOK, D_MODEL), F32),
        scratch_shapes=[pltpu.VMEM((rows, D_MODEL), BF16) for _, _, rows in ws],
        compiler_params=_params("arbitrary", vmem=VMEM_LIMIT_BYTES),
        name="outproj",
    )(*(ap for ap, _ in acts), *(asm for _, asm in acts), *(w for w, _, _ in ws), x, gate[0])


def _router_kernel(x_ref, g_ref, sc_ref, sh_ref, wh_ref, wl_ref, br_ref, tri_ref,
                   h_ref, idx_ref, gate_ref, pos_ref, cnt_ref, run_ref):
    @pl.when(pl.program_id(0) == 0)
    def _():
        run_ref[...] = jnp.zeros_like(run_ref)

    h = _norm_mod(x_ref[...], g_ref[...], sc_ref[0], sh_ref[0])
    h_ref[...] = h.reshape((TM,) + ROW_TILE)
    hh, hl = _split(h)
    logits = _dot(hh, wh_ref[...]) + (_dot(hh, wl_ref[...]) + _dot(hl, wh_ref[...])) + br_ref[...]
    lane = _lane_iota((TM, LANES)).astype(F32)
    neg = jnp.float32(-jnp.inf)
    lg = jnp.where(lane < N_EXPERTS, logits, neg)
    tops, sels, hots = [], [], []
    for _ in range(TOP_K):
        m = lg.max(axis=-1, keepdims=True)
        sel = jnp.where(lg == m, lane, float(LANES)).min(axis=-1, keepdims=True)
        hot = lane == sel
        lg = jnp.where(hot, neg, lg)
        tops.append(m)
        sels.append(sel)
        hots.append(hot)
    es = [jnp.exp(t - tops[0]) for t in tops]
    den = es[0] + es[1] + es[2] + es[3]
    member = jnp.zeros((TM, LANES), F32)
    for hot in hots:
        member = member + hot.astype(F32)
    ranks = _dot(tri_ref[...], member.astype(BF16)) + run_ref[...]
    lane4 = _lane_iota((TM, TOP_K))
    idx4 = jnp.zeros((TM, TOP_K), F32)
    gate4 = jnp.zeros((TM, TOP_K), F32)
    pos4 = jnp.zeros((TM, TOP_K), F32)
    for k in range(TOP_K):
        pk = jnp.sum(jnp.where(hots[k], ranks, 0.0), axis=-1, keepdims=True)
        idx4 = jnp.where(lane4 == k, sels[k], idx4)
        gate4 = jnp.where(lane4 == k, es[k] / den, gate4)
        pos4 = jnp.where(lane4 == k, pk, pos4)
    idx_ref[...] = idx4.astype(jnp.int32)
    gate_ref[...] = gate4
    pos_ref[...] = pos4.astype(jnp.int32)
    run_ref[...] = run_ref[...] + jnp.sum(member, axis=0, keepdims=True)
    cnt_ref[...] = run_ref[...]


def _router_weights(w_router, b_router):
    wpad = jnp.pad(w_router, ((0, 0), (0, 0), (0, LANES - N_EXPERTS)))
    wh = wpad.astype(BF16)
    wl = (wpad - wh.astype(F32)).astype(BF16)
    bpad = jnp.pad(b_router, ((0, 0), (0, LANES - N_EXPERTS))).reshape(DEPTH, 1, LANES)
    return wh, wl, bpad


def _router(layer, x, g, sc, sh, router_weights):
    wh, wl, bpad = router_weights
    r = np.arange(TM)
    tri = jnp.asarray(r[None, :] < r[:, None], dtype=BF16)
    narrow = pl.BlockSpec((TM, TOP_K), lambda i: (i, 0))

    def layer_spec(a):
        return pl.BlockSpec((None,) + a.shape[1:], lambda i: (layer, 0, 0))

    return pl.pallas_call(
        _router_kernel,
        grid=(NT,),
        in_specs=[_row_spec(D_MODEL), _gain_spec(g), _mod_spec(sc), _mod_spec(sh),
                  layer_spec(wh), layer_spec(wl), layer_spec(bpad), _full_spec(tri.shape)],
        out_specs=[pl.BlockSpec((TM,) + ROW_TILE, lambda i: (i, 0, 0)), narrow, narrow, narrow,
                   _full_spec((1, LANES))],
        out_shape=[
            jax.ShapeDtypeStruct((N_TOK,) + ROW_TILE, F32),
            jax.ShapeDtypeStruct((N_TOK, TOP_K), jnp.int32),
            jax.ShapeDtypeStruct((N_TOK, TOP_K), F32),
            jax.ShapeDtypeStruct((N_TOK, TOP_K), jnp.int32),
            jax.ShapeDtypeStruct((1, LANES), F32),
        ],
        scratch_shapes=[pltpu.VMEM((1, LANES), F32)],
        compiler_params=_params("arbitrary", vmem=VMEM_LIMIT_BYTES),
        name="router",
    )(x, g[0], sc[0], sh[0], wh, wl, bpad, tri)


CLEAR_PADDED, CLEAR_EMPTY = 1, 2
ROW_UNROLL = 4
DMA_QUEUES = 2


def _start_all_rows(row_copy):
    def start_rows(j, carry):
        for u in range(ROW_UNROLL):
            for k in range(TOP_K):
                row_copy(j * ROW_UNROLL + u, k).start(priority=k % DMA_QUEUES)
        return carry

    lax.fori_loop(0, TM // ROW_UNROLL, start_rows, 0)


def _dispatch_kernel(dest_ref, clear_ref, h_ref, xs_ref, zero_ref, sem_ref):
    def row_copy(t, k):
        d = dest_ref[(pl.program_id(0) * TM + t) * TOP_K + k]
        return pltpu.make_async_copy(h_ref.at[pl.ds(t, 1)], xs_ref.at[pl.ds(d, 1)], sem_ref.at[0])

    def zero_part(j, kind):
        row0 = pl.multiple_of(j * MOE_PART, MOE_PART)
        return pltpu.make_async_copy(zero_ref, xs_ref.at[pl.ds(row0, MOE_PART)], sem_ref.at[kind])

    def for_parts(kind, fn):
        def body(j, carry):
            @pl.when(clear_ref[j] == kind)
            def _():
                fn(zero_part(j, kind))
            return carry
        lax.fori_loop(0, MOE_PARTS * N_BLOCKS, body, 0)

    @pl.when(pl.program_id(0) == 0)
    def _():
        zero_ref[...] = jnp.zeros_like(zero_ref)
        for_parts(CLEAR_PADDED, lambda cp: cp.start())
        for_parts(CLEAR_EMPTY, lambda cp: cp.start())
        for_parts(CLEAR_PADDED, lambda cp: cp.wait())

    _start_all_rows(row_copy)
    for _ in range(TOP_K):
        pltpu.make_async_copy(h_ref, xs_ref.at[pl.ds(0, TM)], sem_ref.at[0]).wait()

    @pl.when(pl.program_id(0) == NT - 1)
    def _():
        for_parts(CLEAR_EMPTY, lambda cp: cp.wait())


def _dispatch(dest_flat, clear, h):
    return pl.pallas_call(
        _dispatch_kernel,
        grid_spec=pltpu.PrefetchScalarGridSpec(
            num_scalar_prefetch=2,
            grid=(NT,),
            in_specs=[pl.BlockSpec((TM,) + ROW_TILE, lambda i, d, c: (i, 0, 0))],
            out_specs=pl.BlockSpec(memory_space=pl.ANY),
            scratch_shapes=[pltpu.VMEM((MOE_PART,) + ROW_TILE, F32), pltpu.SemaphoreType.DMA((3,))],
        ),
        out_shape=jax.ShapeDtypeStruct((CAP,) + ROW_TILE, F32),
        compiler_params=_params("arbitrary", vmem=VMEM_LIMIT_BYTES),
        name="moe_dispatch",
    )(dest_flat, clear, h)


def _expert_kernel(be_ref, nh_ref, last_ref, ebuf_ref, enext_ref, xs_ref, wgu_hbm, bgu_ref, wd_hbm, bd_ref, ys_ref,
                   wgu_s, wd_s, wgu_f, wd_f, sem_ref, *, layer):
    del last_ref
    b = pl.program_id(0)
    nh = nh_ref[b]

    def weight_copies(e, buf):
        return (pltpu.make_async_copy(wgu_hbm.at[layer, e], wgu_f.at[buf], sem_ref.at[buf, 0]),
                pltpu.make_async_copy(wd_hbm.at[layer, e], wd_f.at[buf], sem_ref.at[buf, 1]))

    @pl.when((nh > 0) & ((b == 0) | (be_ref[b] != be_ref[jnp.maximum(b - 1, 0)])))
    def _():
        buf = ebuf_ref[b]

        @pl.when(b == 0)
        def _():
            for cp in weight_copies(be_ref[b], buf):
                cp.start()

        for cp in weight_copies(be_ref[b], buf):
            cp.wait()

        @pl.when(enext_ref[b] >= 0)
        def _():
            for cp in weight_copies(enext_ref[b], 1 - buf):
                cp.start()

        wgu_s[...] = wgu_f[buf].astype(BF16)
        wd_s[...] = wd_f[buf].astype(BF16)

    def ffn(rows):
        x = xs_ref[0:rows].reshape(rows, D_MODEL)
        gu = _dot(x.astype(BF16), wgu_s[...]) + bgu_ref[...]
        g = jnp.minimum(gu[:, :D_EXPERT], SWIGLU_LIMIT)
        u = jnp.clip(gu[:, D_EXPERT:], -SWIGLU_LIMIT, SWIGLU_LIMIT)
        act = (u + 1.0) * (g * jax.nn.sigmoid(SWIGLU_ALPHA * g))
        y = _dot(act.astype(BF16), wd_s[...]) + bd_ref[...]
        ys_ref[0:rows] = y.reshape((rows,) + ROW_TILE)

    def partial_block(parts):
        rows = parts * MOE_PART
        ffn(rows)
        if rows < MOE_BLK:
            ys_ref[rows:] = jnp.zeros((MOE_BLK - rows,) + ROW_TILE, F32)

    for parts in range(1, MOE_PARTS + 1):
        pl.when(nh == parts)(functools.partial(partial_block, parts))

    @pl.when(nh == 0)
    def _():
        ys_ref[...] = jnp.zeros_like(ys_ref)


def _experts(layer, block_e, block_nh, last_used, block_buf, block_next, xs, w_gu, b_gu, w_down, b_down):
    def xs_map(b, be, nh, lu, eb, en):
        return (jnp.minimum(b, lu[0]), 0, 0)

    def e_map(b, be, nh, lu, eb, en):
        return (layer, be[b], 0, 0)

    hbm = pl.BlockSpec(memory_space=pl.ANY)
    return pl.pallas_call(
        functools.partial(_expert_kernel, layer=layer),
        grid_spec=pltpu.PrefetchScalarGridSpec(
            num_scalar_prefetch=5,
            grid=(N_BLOCKS,),
            in_specs=[
                pl.BlockSpec((MOE_BLK,) + ROW_TILE, xs_map),
                hbm,
                pl.BlockSpec((None, None, 1, 2 * D_EXPERT), e_map),
                hbm,
                pl.BlockSpec((None, None, 1, D_MODEL), e_map),
            ],
            out_specs=pl.BlockSpec((MOE_BLK,) + ROW_TILE, lambda b, be, nh, lu, eb, en: (b, 0, 0)),
            scratch_shapes=[
                pltpu.VMEM((D_MODEL, 2 * D_EXPERT), BF16), pltpu.VMEM((D_EXPERT, D_MODEL), BF16),
                pltpu.VMEM((2, D_MODEL, 2 * D_EXPERT), F32), pltpu.VMEM((2, D_EXPERT, D_MODEL), F32),
                pltpu.SemaphoreType.DMA((2, 2)),
            ],
        ),
        out_shape=jax.ShapeDtypeStruct((CAP,) + ROW_TILE, F32),
        compiler_params=_params("arbitrary", vmem=VMEM_LIMIT_BYTES),
        name="moe_experts",
    )(block_e, block_nh, last_used, block_buf, block_next, xs, w_gu, b_gu.reshape(DEPTH, N_EXPERTS, 1, -1),
      w_down, b_down.reshape(DEPTH, N_EXPERTS, 1, -1))


def _combine_kernel(dest_ref, x_ref, g_ref, gate_ref, ys_ref, *rest, per_group):
    out_refs, (buf_ref, sem_ref) = rest[:-2], rest[-2:]
    i = pl.program_id(0)
    cur = lax.rem(i, 2)

    def start_tile(tile, buf):
        def row_copy(t, k):
            d = dest_ref[(tile * TM + t) * TOP_K + k]
            return pltpu.make_async_copy(ys_ref.at[pl.ds(d, 1)], buf_ref.at[buf, k, pl.ds(t, 1)], sem_ref.at[buf])
        _start_all_rows(row_copy)

    @pl.when(i == 0)
    def _():
        start_tile(0, 0)

    @pl.when(i + 1 < NT)
    def _():
        start_tile(i + 1, 1 - cur)

    for k in range(TOP_K):
        pltpu.make_async_copy(ys_ref.at[pl.ds(0, TM)], buf_ref.at[cur, k], sem_ref.at[cur]).wait()
    gates = gate_ref[...]
    ff = gates[:, 0:1] * buf_ref[cur, 0].reshape(TM, D_MODEL)
    for k in range(1, TOP_K):
        ff = ff + gates[:, k:k + 1] * buf_ref[cur, k].reshape(TM, D_MODEL)
    out = x_ref[...] + g_ref[0] * ff
    if per_group:
        def store(ref):
            ref[...] = out
        _per_group(functools.partial(store, out_refs[0]), functools.partial(store, out_refs[1]))
    else:
        out_refs[0][...] = out


def _combine(dest_flat, x, gate_vec, gates, ys, per_group):
    if per_group:
        out_specs = [_context_rows_spec(D_MODEL),
                     pl.BlockSpec((TM, D_MODEL), lambda i, *_: (jnp.maximum(i - NT_PROMPT, 0), 0))]
        out_shape = [jax.ShapeDtypeStruct((N_PROMPT, D_MODEL), F32), jax.ShapeDtypeStruct((N_SAMPLE, D_MODEL), F32)]
    else:
        out_specs = pl.BlockSpec((TM, D_MODEL), lambda i, d: (i, 0))
        out_shape = jax.ShapeDtypeStruct((N_TOK, D_MODEL), F32)
    return pl.pallas_call(
        functools.partial(_combine_kernel, per_group=per_group),
        grid_spec=pltpu.PrefetchScalarGridSpec(
            num_scalar_prefetch=1,
            grid=(NT,),
            in_specs=[
                pl.BlockSpec((TM, D_MODEL), lambda i, d: (i, 0)),
                _mod_spec(gate_vec),
                pl.BlockSpec((TM, TOP_K), lambda i, d: (i, 0)),
                pl.BlockSpec(memory_space=pl.ANY),
            ],
            out_specs=out_specs,
            scratch_shapes=[pltpu.VMEM((2, TOP_K, TM) + ROW_TILE, F32), pltpu.SemaphoreType.DMA((2,))],
        ),
        out_shape=out_shape,
        compiler_params=_params("arbitrary", vmem=VMEM_LIMIT_BYTES),
        name="moe_combine",
    )(dest_flat, x, gate_vec[0], gates, ys)


def _moe(layer, x, g, sc, sh, gate_vec, router_weights, w_gu, b_gu, w_down, b_down, per_group=False):
    h, idx, gates, pos, counts = _router(layer, x, g, sc, sh, router_weights)
    cnt = counts[0, :N_EXPERTS].astype(jnp.int32)
    nparts = (cnt + MOE_PART - 1) // MOE_PART
    nblk = (nparts + MOE_PARTS - 1) // MOE_PARTS
    e_ids = jnp.arange(N_EXPERTS, dtype=jnp.int32)
    blk_end = jnp.sum(jnp.where(e_ids[None, :] <= e_ids[:, None], nblk[None, :], 0), axis=1)
    blk_start = blk_end - nblk
    last_used = blk_end[-1] - 1

    def per_expert(table, e):
        return jnp.sum(jnp.where(e[..., None] == e_ids, table, 0), axis=-1)

    def expert_of_block(blk):
        return jnp.sum((blk_end <= jnp.minimum(blk, last_used)[:, None]).astype(jnp.int32), axis=1)

    dest = per_expert(blk_start * MOE_BLK, idx) + pos
    b_ids = jnp.arange(N_BLOCKS, dtype=jnp.int32)
    block_e = expert_of_block(b_ids)
    block_nh = jnp.where(
        b_ids <= last_used,
        jnp.clip(per_expert(nparts, block_e) - MOE_PARTS * (b_ids - per_expert(blk_start, block_e)), 0, MOE_PARTS), 0)
    h_ids = jnp.arange(MOE_PARTS * N_BLOCKS, dtype=jnp.int32)
    h_e = expert_of_block(h_ids // MOE_PARTS)
    h_nparts = per_expert(nparts, h_e)
    h_local = h_ids - MOE_PARTS * per_expert(blk_start, h_e)
    holds_rows = (h_ids // MOE_PARTS <= last_used) & (h_local < h_nparts)
    clear = jnp.where(holds_rows, jnp.where(h_local == h_nparts - 1, CLEAR_PADDED, 0), CLEAR_EMPTY)
    dest_flat = dest.reshape(-1).astype(jnp.int32)
    xs = _dispatch(dest_flat, clear.astype(jnp.int32), h)
    used = nblk > 0
    before = e_ids[None, :] < e_ids[:, None]
    expert_buf = jnp.sum(jnp.where(before & used[None, :], 1, 0), axis=1) % 2
    expert_next = jnp.min(jnp.where((e_ids[None, :] > e_ids[:, None]) & used[None, :], e_ids[None, :], N_EXPERTS),
                          axis=1)
    expert_next = jnp.where(expert_next == N_EXPERTS, -1, expert_next)
    ys = _experts(layer, block_e.astype(jnp.int32), block_nh.astype(jnp.int32),
                  last_used.reshape(1).astype(jnp.int32), per_expert(expert_buf, block_e).astype(jnp.int32),
                  per_expert(expert_next, block_e).astype(jnp.int32), xs, w_gu, b_gu, w_down, b_down)
    return _combine(dest_flat, x, gate_vec, gates, ys, per_group)


def _rope_tables(d_rot, lane0, period):
    n_rows = DEC_SEQ // GRID_W
    rows = np.repeat(np.arange(n_rows), GRID_W).astype(np.float32)
    cols = np.tile(np.arange(GRID_W), n_rows).astype(np.float32)
    half = d_rot // 2
    lane = np.arange(LANES)
    i = (lane - lane0) % period
    active = (lane >= lane0) & (i < d_rot)
    w = i % half
    f = w % (half // 2)
    pos = np.where((i // half)[None, :] == 0, rows[:, None], cols[:, None])
    sign = np.where(w < half // 2, -1.0, 1.0).astype(np.float32)
    inv = ROPE_THETA ** (-jnp.arange(0, half, 2, dtype=F32) / half)
    ang = jnp.asarray(pos) * inv[f][None, :]
    cos = jnp.where(active[None, :], jnp.cos(ang), 1.0)
    sin = jnp.where(active[None, :], jnp.sin(ang) * sign[None, :], 0.0)
    return cos, sin


def _rope_spec():
    def index(i):
        return (jnp.maximum(i - NT_PROMPT, 0) % TILES_PER_SAMPLE, 0)
    return pl.BlockSpec((TM, LANES), index)


def _pad_heads(w, n_heads, width):
    lead = w.shape[:-1]
    w = w.reshape(lead + (n_heads, width))
    w = jnp.pad(w, [(0, 0)] * len(lead) + [(0, 0), (0, HEAD_PAD - width)])
    return w.reshape(lead + (n_heads * HEAD_PAD,))


_Q_ORDER = (0, 4, 1, 5, 2, 6, 3, 7, 8, 12, 9, 13, 10, 14, 11, 15)


def _perm_q_heads(w, axis):
    shape = w.shape
    n = shape[axis]
    w = jnp.moveaxis(w, axis, 0).reshape((GQA_HEADS, n // GQA_HEADS) + tuple(s for a, s in enumerate(shape) if a != axis))
    w = w[jnp.array(_Q_ORDER)]
    w = w.reshape((n,) + w.shape[2:])
    return jnp.moveaxis(w, 0, axis)


def _constants():
    mla_cos, mla_sin = _rope_tables(MLA_ROPE, MLA_NOPE, LANES)
    gqa_cos, gqa_sin = _rope_tables(GQA_HEAD_DIM, 0, GQA_HEAD_DIM)
    dft = {}
    for L in (SEQ, DEC_SEQ):
        cm, sm = _dft_tables(L)
        fmat = jnp.concatenate([cm, sm], axis=0).astype(BF16)
        ftmat = fmat.T
        dft[L] = (cm, sm, fmat, ftmat)
    return dict(mla=(mla_cos, mla_sin), gqa=(gqa_cos, gqa_sin), dft=dft)


def _even_mixer(x, p, i, gmix, sc1, sh1, g1, consts):
    proj = _normlin(x, gmix, sc1, sh1, p["w_in_ab"], i, IN_AB_PAD)
    y_hy = []
    for L, blk0, nseq, per_step in ((SEQ, 0, BATCH, HY_PROMPT_SEQS), (DEC_SEQ, N_PROMPT // DEC_SEQ, DEC_BATCH, 1)):
        cm, sm, fmat, ftmat = consts["dft"][L]
        kc, ks = _hy_filter(L, cm, sm, p["hy_filter_w1"][i], p["hy_filter_b1"][i], p["hy_filter_freq"][i],
                            p["hy_filter_w2"][i], p["hy_filter_b2"][i], p["hy_filter_w3"][i],
                            p["hy_filter_b3"][i], p["hy_log_decay"][i])
        y_hy.append(_hy_mix(proj, blk0, nseq, per_step, L, p["hy_conv_w"][i], p["hy_conv_b"][i], kc, ks,
                            p["hy_bias"][i], fmat, ftmat))

    wq = _pad_heads(p["mla_wq_b"][i], MLA_HEADS, MLA_QK).astype(BF16)
    wkv = p["mla_wkv_b"][i].astype(BF16)
    qg = jnp.pad(p["mla_q_norm"][i], (0, HEAD_PAD - MLA_QK)).reshape(1, HEAD_PAD)
    kg = jnp.pad(p["mla_k_norm"][i], (0, HEAD_PAD - MLA_QK)).reshape(1, HEAD_PAD)
    mla_cos, mla_sin = consts["mla"]
    q, k, v, lat = _mla_prep(proj, p["mla_q_lora_norm"][i].reshape(1, -1), wq, qg,
                             p["mla_kv_lora_norm"][i].reshape(1, -1), wkv, kg, mla_cos, mla_sin)
    ctx = jnp.pad(p["cache_mla_latent"][:, i].reshape(DEC_BATCH * PAST_LEN, -1),
                  ((0, 0), (0, KV_PAD - MLA_KV_LORA - MLA_ROPE)))
    kc_ctx, vc_ctx = _mla_ctx(ctx, wkv, kg)
    nqk = MLA_HEADS * HEAD_PAD
    o = _attention(_mla_attn_kernel, q, k, v, kc_ctx, vc_ctx, nqk, nqk, nqk, MLA_HEADS * MLA_V)
    w_out = p["w_out_ab"]
    x = _outproj([y_hy, o], [(w_out, (i, 0, 0), HY_D), (w_out, (i, 1, 0), HY_D)], x, g1)
    return x, lat, dict(y_hy=jnp.concatenate(y_hy, axis=0), o=jnp.concatenate(o, axis=0))


def _odd_mixer(x, p, i, gmix, sc1, sh1, g1, consts):
    nq = GQA_HEADS * GQA_HEAD_DIM
    nk = GQA_KV_HEADS * GQA_HEAD_DIM
    w = p["w_qkv_c"][i]
    w_qkv = jnp.concatenate([_perm_q_heads(w[:, :nq], 1), w[:, nq:]], axis=1).astype(BF16)
    qg = jnp.tile(p["gqa_q_norm"][i], LANES // GQA_HEAD_DIM).reshape(1, LANES)
    kg = jnp.tile(p["gqa_k_norm"][i], LANES // GQA_HEAD_DIM).reshape(1, LANES)
    gqa_cos, gqa_sin = consts["gqa"]
    q, k, v, k_plain, v_plain = _gqa_prep(x, gmix, sc1, sh1, w_qkv, qg, kg, gqa_cos, gqa_sin)
    kc_ctx = p["cache_gqa_k"][:, i].reshape(DEC_BATCH * PAST_LEN, -1).astype(BF16)
    vc_ctx = p["cache_gqa_v"][:, i].reshape(DEC_BATCH * PAST_LEN, -1).astype(BF16)
    o = _attention(_gqa_attn_kernel, q, k, v, kc_ctx, vc_ctx, nq, nk, nk, nq)
    w_out = _perm_q_heads(p["w_out_c"][i], 0)[None]
    x = _outproj([o], [(w_out, (0, 0, 0), nq)], x, g1)
    return x, k_plain, v_plain


def kernel(x_prompt, x_sample, cache_mla_latent, cache_gqa_k, cache_gqa_v, c, c_ctx, w_ada, b_ada, norm_mix, norm_ffn, w_in_ab, hy_conv_w, hy_conv_b, hy_filter_w1, hy_filter_b1, hy_filter_freq, hy_filter_w2, hy_filter_b2, hy_filter_w3, hy_filter_b3, hy_log_decay, hy_bias, mla_q_lora_norm, mla_wq_b, mla_kv_lora_norm, mla_wkv_b, mla_q_norm, mla_k_norm, w_out_ab, w_qkv_c, gqa_q_norm, gqa_k_norm, w_out_c, moe_router_w, moe_router_b, moe_w_gate_up, moe_b_gate_up, moe_w_down, moe_b_down):
    p = dict(locals())
    x = jnp.concatenate([x_prompt.reshape(N_PROMPT, D_MODEL), x_sample.reshape(N_SAMPLE, D_MODEL)], axis=0)

    cond = jnp.concatenate([c_ctx[None, :], c, jnp.zeros((COND_ROWS - N_COND, D_MODEL), F32)], axis=0)
    mods = _modulation(cond, w_ada, b_ada)
    tile_cond = jnp.concatenate([jnp.zeros((NT_PROMPT,), jnp.int32),
                                 1 + jnp.arange(NT - NT_PROMPT, dtype=jnp.int32) // TILES_PER_SAMPLE])
    mods = mods[:, tile_cond].reshape(DEPTH, NT, 6, 1, D_MODEL).transpose(0, 2, 1, 3, 4)

    consts = _constants()
    router_weights = _router_weights(moe_router_w, moe_router_b)
    gains_mix = norm_mix.reshape(DEPTH, 1, D_MODEL)
    gains_ffn = norm_ffn.reshape(DEPTH, 1, D_MODEL)

    lat_out, k_out, v_out = [], [], []
    for l in range(DEPTH):
        sh1, sc1, g1, sh2, sc2, g2 = ((mods, l, j) for j in range(6))
        i = l // 2
        gmix = (gains_mix, l)
        if l % 2 == 0:
            x, lat, _ = _even_mixer(x, p, i, gmix, sc1, sh1, g1, consts)
            lat_out.append(lat.reshape(BATCH, SEQ, -1))
        else:
            x, k_plain, v_plain = _odd_mixer(x, p, i, gmix, sc1, sh1, g1, consts)
            k_out.append(k_plain.reshape(BATCH, SEQ, GQA_KV_HEADS, GQA_HEAD_DIM))
            v_out.append(v_plain.reshape(BATCH, SEQ, GQA_KV_HEADS, GQA_HEAD_DIM))
        x = _moe(l, x, (gains_ffn, l), sc2, sh2, g2, router_weights,
                 moe_w_gate_up, moe_b_gate_up, moe_w_down, moe_b_down, per_group=(l == DEPTH - 1))

    y_prompt = x[0].reshape(BATCH, SEQ, D_MODEL)
    y_sample = x[1].reshape(DEC_BATCH, DEC_SEQ, D_MODEL)
    return (y_prompt, y_sample, jnp.stack(lat_out, axis=1), jnp.stack(k_out, axis=1), jnp.stack(v_out, axis=1))
```

```python
import functools
import math

import jax
import jax.numpy as jnp
import numpy as np
from jax import lax
from jax.experimental import pallas as pl
from jax.experimental.pallas import tpu as pltpu

F32 = jnp.float32
BF16 = jnp.bfloat16

D_MODEL = 1024
BATCH = 32
SEQ = 256
DEPTH = 4
DEC_BATCH = 4
DEC_SEQ = 1024
PAST_LEN = 256
GRID_W = 64
N_EVEN = (DEPTH + 1) // 2
N_ODD = DEPTH // 2
HY_D = D_MODEL // 2
HY_ORDER = 2
HY_BANDS = 16
HY_EMB = 2 * HY_BANDS + 1
HY_FILTER_HIDDEN = 64
MLA_HEADS = 8
MLA_NOPE = 64
MLA_ROPE = 32
MLA_QK = MLA_NOPE + MLA_ROPE
MLA_V = HY_D // MLA_HEADS
MLA_Q_LORA = 3 * D_MODEL // 8
MLA_KV_LORA = D_MODEL // 4
GQA_HEADS = 16
GQA_KV_HEADS = 4
GQA_HEAD_DIM = D_MODEL // GQA_HEADS
N_EXPERTS = 32
TOP_K = 4
D_EXPERT = D_MODEL
SWIGLU_LIMIT = 7.0
SWIGLU_ALPHA = 1.702
ROPE_THETA = 10000.0
RMS_EPS = 1e-6
GQA_SCALE = GQA_HEAD_DIM ** -0.5
assert math.frexp(GQA_SCALE)[0] == 0.5
IN_AB = (HY_ORDER + 1) * HY_D + MLA_Q_LORA + MLA_KV_LORA + MLA_ROPE

N_PROMPT = BATCH * SEQ
N_SAMPLE = DEC_BATCH * DEC_SEQ
N_TOK = N_PROMPT + N_SAMPLE

LANES = 128
SUBLANES = 8
VMEM_LIMIT_BYTES = 56 * 1024 * 1024

TM = 512
NT = N_TOK // TM
NT_PROMPT = N_PROMPT // TM
TILES_PER_SAMPLE = DEC_SEQ // TM
N_COND = 1 + DEC_BATCH
COND_ROWS = 8
HEAD_PAD = LANES
IN_AB_PAD = 2304
KV_PAD = 384
MOE_BLK = 512
MOE_PARTS = 4
MOE_PART = MOE_BLK // MOE_PARTS
N_SLOTS = N_TOK * TOP_K
N_BLOCKS = N_SLOTS // MOE_BLK + N_EXPERTS
CAP = N_BLOCKS * MOE_BLK
ROW_TILE = (SUBLANES, LANES)
assert SUBLANES * LANES == D_MODEL
HY_CH = 256
HY_PROMPT_SEQS = 8


def _dot(a, b):
    return jnp.dot(a, b, preferred_element_type=F32)


def _dot_nt(a, b):
    return lax.dot_general(a, b, (((1,), (1,)), ((), ())), preferred_element_type=F32)


def _split(x):
    hi = x.astype(BF16)
    lo = (x - hi.astype(F32)).astype(BF16)
    return hi, lo


def _dot3(a, b):
    ah, al = _split(a)
    bh, bl = _split(b)
    return _dot(ah, bh) + (_dot(ah, bl) + _dot(al, bh))


def _lane_iota(shape):
    return lax.broadcasted_iota(jnp.int32, shape, len(shape) - 1)


def _params(*sem, vmem=None):
    return pltpu.CompilerParams(dimension_semantics=sem, vmem_limit_bytes=vmem)


def _mod_kernel(c_ref, w_ref, b_ref, o_ref):
    c = c_ref[...]
    s = c * jax.nn.sigmoid(c)
    o_ref[0] = _dot(s.astype(BF16), w_ref[0].astype(BF16)) + b_ref[0]


def _modulation(cond, w_ada, b_ada):
    nblk = 6
    return pl.pallas_call(
        _mod_kernel,
        grid=(DEPTH, nblk),
        in_specs=[
            pl.BlockSpec((COND_ROWS, D_MODEL), lambda l, j: (0, 0)),
            pl.BlockSpec((1, D_MODEL, D_MODEL), lambda l, j: (l, 0, j)),
            pl.BlockSpec((1, 1, D_MODEL), lambda l, j: (l, 0, j)),
        ],
        out_specs=pl.BlockSpec((1, COND_ROWS, D_MODEL), lambda l, j: (l, 0, j)),
        out_shape=jax.ShapeDtypeStruct((DEPTH, COND_ROWS, nblk * D_MODEL), F32),
        compiler_params=_params("arbitrary", "arbitrary"),
        name="modulation",
    )(cond, w_ada, b_ada.reshape(DEPTH, 1, nblk * D_MODEL))


def _norm_mod(x, g, sc, sh):
    ms = jnp.mean(x * x, axis=-1, keepdims=True)
    y = x * lax.rsqrt(ms + RMS_EPS)
    return (y * g) * (1.0 + sc) + sh


def _row_spec(width):
    return pl.BlockSpec((TM, width), lambda i: (i, 0))


def _mod_spec(mod):
    _, layer, which = mod
    return pl.BlockSpec((None, None, 1, 1, D_MODEL), lambda i, *_: (layer, which, i, 0, 0))


def _gain_spec(gain):
    _, layer = gain
    return pl.BlockSpec((None, 1, D_MODEL), lambda i, *_: (layer, 0, 0))


def _full_spec(shape):
    nd = len(shape)
    return pl.BlockSpec(shape, lambda i: (0,) * nd)


def _normlin_kernel(x_ref, g_ref, sc_ref, sh_ref, w_ref, o_ref, w_scr):
    nin = w_ref.shape[1]

    @pl.when(pl.program_id(0) == 0)
    def _():
        w_scr[:, :nin] = w_ref[...].astype(BF16)
        w_scr[:, nin:] = jnp.zeros((D_MODEL, w_scr.shape[1] - nin), BF16)

    h = _norm_mod(x_ref[...], g_ref[...], sc_ref[0], sh_ref[0])
    o_ref[...] = _dot(h.astype(BF16), w_scr[...])


def _normlin(x, g, sc, sh, w_all, layer, nout):
    nin = w_all.shape[2]
    return pl.pallas_call(
        _normlin_kernel,
        grid=(NT,),
        in_specs=[_row_spec(D_MODEL), _gain_spec(g), _mod_spec(sc), _mod_spec(sh),
                  pl.BlockSpec((None, D_MODEL, nin), lambda i: (layer, 0, 0))],
        out_specs=_row_spec(nout),
        out_shape=jax.ShapeDtypeStruct((N_TOK, nout), F32),
        scratch_shapes=[pltpu.VMEM((D_MODEL, nout), BF16)],
        compiler_params=_params("arbitrary", vmem=VMEM_LIMIT_BYTES),
        name="normlin",
    )(x, g[0], sc[0], sh[0], w_all)


def _hy_filter_kernel(z_ref, w1_ref, b1_ref, fr_ref, w2_ref, b2_ref, w3_ref, b3_ref, ed_ref,
                      c_ref, s_ref, kc_ref, ks_ref):
    L = z_ref.shape[0]
    z = z_ref[...]
    fr = fr_ref[...]
    hdn = jnp.sin(fr * (_dot3(z, w1_ref[...]) + b1_ref[...]))
    hdn = jnp.sin(fr * (_dot3(hdn, w2_ref[...]) + b2_ref[...]))
    filt = _dot3(hdn, w3_ref[...]) + b3_ref[...]
    t = z[:, 0:1]
    filt = filt * jnp.exp(-t * ed_ref[...])
    row = lax.broadcasted_iota(jnp.int32, (L, HY_D), 0)
    cm = c_ref[...]
    sm = s_ref[...]
    for o in range(HY_ORDER):
        fw = filt[:, (2 * o) * HY_D:(2 * o + 1) * HY_D]
        bw = filt[:, (2 * o + 1) * HY_D:(2 * o + 2) * HY_D]
        den = (jnp.sum(jnp.abs(fw), axis=0, keepdims=True)
               + jnp.sum(jnp.abs(bw), axis=0, keepdims=True)) + 1e-6
        fw = fw / den
        bw = jnp.where(row == 0, 0.0, bw / den)
        kc_ref[o] = _dot3(cm, fw + bw) * (1.0 / L)
        ks_ref[o] = _dot3(sm, fw - bw) * (1.0 / L)


def _dft_tables(L):
    m = jnp.arange(L, dtype=jnp.int32)
    phase = ((2 * m + 1)[:, None] * m[None, :]) % (4 * L)
    ang = phase.astype(F32) * (2.0 * math.pi / (4 * L))
    return jnp.cos(ang), jnp.sin(ang)


def _filter_features(L):
    p = jnp.arange(L, dtype=F32)
    t = p / max(L - 1, 1)
    bands = jnp.linspace(1e-4, HY_BANDS - 1, HY_BANDS, dtype=F32)
    ang = (2.0 * math.pi / L) * p[:, None] * bands[None, :]
    z = jnp.concatenate([t[:, None], jnp.cos(ang), -jnp.sin(ang)], axis=-1)
    return jnp.pad(z, ((0, 0), (0, LANES - HY_EMB)))


def _hy_filter(L, cmat, smat, w1, b1, fr, w2, b2, w3, b3, log_decay):
    nf = HY_ORDER * 2 * HY_D
    args = (
        _filter_features(L),
        jnp.pad(w1, ((0, LANES - HY_EMB), (0, 0))),
        b1.reshape(1, -1), fr.reshape(1, -1), w2, b2.reshape(1, -1), w3, b3.reshape(1, -1),
        jnp.exp(log_decay.astype(F32)).reshape(1, nf),
        cmat, smat,
    )
    out_sds = jax.ShapeDtypeStruct((HY_ORDER, L, HY_D), F32)
    return pl.pallas_call(
        _hy_filter_kernel,
        grid=(1,),
        in_specs=[_full_spec(a.shape) for a in args],
        out_specs=[_full_spec(out_sds.shape)] * 2,
        out_shape=[out_sds, out_sds],
        compiler_params=_params("arbitrary", vmem=VMEM_LIMIT_BYTES),
        name=f"hy_filter_{L}",
    )(*args)


def _hy_mix_kernel(u0_ref, u1_ref, u2_ref, cw0_ref, cw1_ref, cw2_ref, cb0_ref, cb1_ref, cb2_ref,
                   kc_ref, ks_ref, hb_ref, f_ref, ft_ref, o_ref):
    S, L, _ = u0_ref.shape
    row = lax.broadcasted_iota(jnp.int32, (L, HY_CH), 0)

    def lanes(per_seq):
        return per_seq[0] if S == 1 else jnp.concatenate(per_seq, axis=1)

    def short_conv(u_ref, cw_ref, cb_ref):
        w = cw_ref[0]
        out = []
        for s in range(S):
            u = u_ref[s]
            prev = jnp.where(row == 0, 0.0, pltpu.roll(u, 1, 0))
            nxt = jnp.where(row == L - 1, 0.0, pltpu.roll(u, L - 1, 0))
            out.append((prev * w[0:1] + u * w[1:2]) + nxt * w[2:3] + cb_ref[0])
        return lanes(out)

    z = short_conv(u0_ref, cw0_ref, cb0_ref)
    gates = (short_conv(u1_ref, cw1_ref, cb1_ref), short_conv(u2_ref, cw2_ref, cb2_ref))
    for o in range(HY_ORDER):
        zz = _dot(f_ref[...], z.astype(BF16))
        cz, sz = zz[:L], zz[L:]
        kc, ks = lanes([kc_ref[o]] * S), lanes([ks_ref[o]] * S)
        w1 = cz * kc - sz * ks
        w2 = cz * ks + sz * kc
        ww = jnp.concatenate([w1, w2], axis=0).astype(BF16)
        conv = _dot(ft_ref[...], ww)
        z = gates[o] * (conv + z * lanes([hb_ref[0, o]] * S))
    for s in range(S):
        o_ref[s] = z[:, s * HY_CH:(s + 1) * HY_CH].astype(o_ref.dtype)


def _hy_mix(proj, row_block0, nseq, seqs_per_step, L, cw, cb, kc, ks, hb, fmat, ftmat):
    nch = HY_D // HY_CH
    nparts = HY_ORDER + 1
    S = seqs_per_step
    assert nseq % S == 0 and row_block0 % S == 0
    cw3 = cw.reshape(3, nparts * nch, HY_CH).transpose(1, 0, 2)
    cb3 = cb.reshape(nparts * nch, 1, HY_CH)
    hb3 = hb.reshape(HY_ORDER, nch, 1, HY_CH).transpose(1, 0, 2, 3)
    proj = proj.reshape(N_TOK // L, L, proj.shape[1])

    def u_spec(part):
        return pl.BlockSpec((S, L, HY_CH), lambda s, c: (row_block0 // S + s, 0, part * nch + c))

    def cw_spec(part):
        return pl.BlockSpec((1, 3, HY_CH), lambda s, c: (part * nch + c, 0, 0))

    def cb_spec(part):
        return pl.BlockSpec((1, 1, HY_CH), lambda s, c: (part * nch + c, 0, 0))

    return pl.pallas_call(
        _hy_mix_kernel,
        grid=(nseq // S, nch),
        in_specs=[
            u_spec(0), u_spec(1), u_spec(2),
            cw_spec(0), cw_spec(1), cw_spec(2),
            cb_spec(0), cb_spec(1), cb_spec(2),
            pl.BlockSpec((HY_ORDER, L, HY_CH), lambda s, c: (0, 0, c)),
            pl.BlockSpec((HY_ORDER, L, HY_CH), lambda s, c: (0, 0, c)),
            pl.BlockSpec((1, HY_ORDER, 1, HY_CH), lambda s, c: (c, 0, 0, 0)),
            pl.BlockSpec((2 * L, L), lambda s, c: (0, 0)),
            pl.BlockSpec((L, 2 * L), lambda s, c: (0, 0)),
        ],
        out_specs=pl.BlockSpec((S, L, HY_CH), lambda s, c: (s, 0, c)),
        out_shape=jax.ShapeDtypeStruct((nseq, L, HY_D), BF16),
        compiler_params=_params("arbitrary", "arbitrary", vmem=VMEM_LIMIT_BYTES),
        name=f"hy_mix_{L}",
    )(proj, proj, proj, cw3, cw3, cw3, cb3, cb3, cb3, kc, ks, hb3, fmat, ftmat).reshape(nseq * L, HY_D)


def _head_rms(xh, gain, dim):
    ms = jnp.sum(xh * xh, axis=-1, keepdims=True) * (1.0 / dim)
    return (xh * lax.rsqrt(ms + RMS_EPS)) * gain


def _context_rows_spec(width):
    return pl.BlockSpec((TM, width), lambda i, *_: (jnp.minimum(i, NT_PROMPT - 1), 0))


def _per_group(context_fn, latent_fn):
    is_context = pl.program_id(0) < NT_PROMPT
    pl.when(is_context)(context_fn)
    pl.when(jnp.logical_not(is_context))(latent_fn)


def _rope(xh, cos, sin, half):
    lane = _lane_iota(xh.shape)
    first = (lane % (2 * half)) < half
    rot = jnp.where(first, pltpu.roll(xh, LANES - half, 1), pltpu.roll(xh, half, 1))
    return xh * cos + rot * sin


def _mla_keys_values(lat_n, kr_blk, wkv_ref, kg_ref, rope):
    kv = _dot(lat_n.astype(BF16), wkv_ref[...])
    kr = pltpu.roll(kr_blk, MLA_NOPE, 1)
    nope = _lane_iota(kr.shape) < MLA_NOPE
    ks = []
    for h in range(MLA_HEADS):
        kh = jnp.where(nope, kv[:, h * HEAD_PAD:(h + 1) * HEAD_PAD], 0.0) + kr
        kh = _head_rms(kh, kg_ref[...], MLA_QK)
        if rope is not None:
            kh = _rope(kh, rope[0], rope[1], MLA_ROPE // 4)
        ks.append(kh)
    return ks, kv


def _mla_prep_kernel(qa_ref, kva_ref, gq_ref, wq_ref, qg_ref, gkv_ref, wkv_ref, kg_ref,
                     cos_ref, sin_ref, q_ref, k_ref, v_ref, lat_ref):
    def body(rope):
        qa = qa_ref[...]
        ms = jnp.mean(qa * qa, axis=-1, keepdims=True)
        qn = (qa * lax.rsqrt(ms + RMS_EPS)) * gq_ref[...]
        q = _dot(qn.astype(BF16), wq_ref[...])
        for h in range(MLA_HEADS):
            qh = _head_rms(q[:, h * HEAD_PAD:(h + 1) * HEAD_PAD], qg_ref[...], MLA_QK)
            if rope is not None:
                qh = _rope(qh, rope[0], rope[1], MLA_ROPE // 4)
            q_ref[:, h * HEAD_PAD:(h + 1) * HEAD_PAD] = qh.astype(BF16)

        kva = kva_ref[...]
        lat = kva[:, :MLA_KV_LORA]
        ms = jnp.mean(lat * lat, axis=-1, keepdims=True)
        lat_n = (lat * lax.rsqrt(ms + RMS_EPS)) * gkv_ref[...]
        kr_blk = kva[:, MLA_KV_LORA:]
        if rope is None:
            lat_ref[:, :MLA_KV_LORA] = lat_n
            lat_ref[:, MLA_KV_LORA:] = kr_blk[:, :MLA_ROPE]
        ks, v = _mla_keys_values(lat_n, kr_blk, wkv_ref, kg_ref, rope)
        for h in range(MLA_HEADS):
            k_ref[:, h * HEAD_PAD:(h + 1) * HEAD_PAD] = ks[h].astype(BF16)
        v_ref[...] = v.astype(BF16)

    _per_group(lambda: body(None), lambda: body((cos_ref[...], sin_ref[...])))


def _mla_prep(proj, gq, wq, qg, gkv, wkv, kg, cos_t, sin_t):
    nq = MLA_HEADS * HEAD_PAD
    nv = MLA_HEADS * (MLA_NOPE + MLA_V)
    qa_blk = (HY_ORDER + 1) * HY_D // KV_PAD
    return pl.pallas_call(
        _mla_prep_kernel,
        grid=(NT,),
        in_specs=[
            pl.BlockSpec((TM, KV_PAD), lambda i: (i, qa_blk)),
            pl.BlockSpec((TM, KV_PAD), lambda i: (i, qa_blk + 1)),
            _full_spec(gq.shape), _full_spec(wq.shape), _full_spec(qg.shape), _full_spec(gkv.shape),
            _full_spec(wkv.shape), _full_spec(kg.shape),
            _rope_spec(), _rope_spec(),
        ],
        out_specs=[_row_spec(nq), _row_spec(nq), _row_spec(nv), _context_rows_spec(MLA_KV_LORA + MLA_ROPE)],
        out_shape=[
            jax.ShapeDtypeStruct((N_TOK, nq), BF16),
            jax.ShapeDtypeStruct((N_TOK, nq), BF16),
            jax.ShapeDtypeStruct((N_TOK, nv), BF16),
            jax.ShapeDtypeStruct((N_PROMPT, MLA_KV_LORA + MLA_ROPE), F32),
        ],
        compiler_params=_params("arbitrary", vmem=VMEM_LIMIT_BYTES),
        name="mla_prep",
    )(proj, proj, gq, wq, qg, gkv, wkv, kg, cos_t, sin_t)


def _mla_ctx_kernel(lat_ref, wkv_ref, kg_ref, k_ref, v_ref):
    lat = lat_ref[...]
    ks, v = _mla_keys_values(lat[:, :MLA_KV_LORA], lat[:, MLA_KV_LORA:], wkv_ref, kg_ref, None)
    for h in range(MLA_HEADS):
        k_ref[:, h * HEAD_PAD:(h + 1) * HEAD_PAD] = ks[h].astype(BF16)
    v_ref[...] = v.astype(BF16)


def _mla_ctx(lat_pad, wkv, kg):
    n = lat_pad.shape[0]
    nq = MLA_HEADS * HEAD_PAD
    nv = MLA_HEADS * (MLA_NOPE + MLA_V)
    return pl.pallas_call(
        _mla_ctx_kernel,
        grid=(n // TM,),
        in_specs=[_row_spec(KV_PAD), _full_spec(wkv.shape), _full_spec(kg.shape)],
        out_specs=[_row_spec(nq), _row_spec(nv)],
        out_shape=[jax.ShapeDtypeStruct((n, nq), BF16), jax.ShapeDtypeStruct((n, nv), BF16)],
        compiler_params=_params("arbitrary"),
        name="mla_ctx",
    )(lat_pad, wkv, kg)


def _softmax_pv(q, keys, vals, scale):
    ss = [_dot_nt(q, k) for k in keys]
    if scale is not None:
        ss = [s * scale for s in ss]
    m = ss[0].max(axis=-1, keepdims=True)
    for s in ss[1:]:
        m = jnp.maximum(m, s.max(axis=-1, keepdims=True))
    ps = [jnp.exp(s - m) for s in ss]
    l = ps[0].sum(axis=-1, keepdims=True)
    for p in ps[1:]:
        l = l + p.sum(axis=-1, keepdims=True)
    o = _dot(ps[0].astype(BF16), vals[0])
    for p, v in zip(ps[1:], vals[1:]):
        o = o + _dot(p.astype(BF16), v)
    return o / l


def _mla_attn_kernel(*refs, has_ctx):
    if has_ctx:
        q_ref, k_ref, v_ref, kc_ref, vc_ref, o_ref = refs
    else:
        q_ref, k_ref, v_ref, o_ref = refs
    scale = MLA_QK ** -0.5
    lo = _lane_iota((q_ref.shape[0], LANES)) < MLA_V
    for j in range(MLA_HEADS // 2):
        outs = []
        for h in (2 * j, 2 * j + 1):
            sl = slice(h * HEAD_PAD, (h + 1) * HEAD_PAD)
            ks, vs = [k_ref[:, sl]], [v_ref[:, sl]]
            if has_ctx:
                ks.append(kc_ref[:, sl])
                vs.append(vc_ref[:, sl])
            outs.append(_softmax_pv(q_ref[:, sl], ks, vs, scale))
        pair = jnp.where(lo, pltpu.roll(outs[0], MLA_V, 1), outs[1])
        o_ref[:, j * LANES:(j + 1) * LANES] = pair.astype(o_ref.dtype)


def _gqa_attn_kernel(*refs, has_ctx):
    if has_ctx:
        q_ref, k_ref, v_ref, kc_ref, vc_ref, o_ref = refs
    else:
        q_ref, k_ref, v_ref, o_ref = refs
    scale = None
    lo = _lane_iota((q_ref.shape[0], LANES)) < GQA_HEAD_DIM
    pairs_per_kv = (GQA_HEADS // 2) // (GQA_KV_HEADS // 2)
    for p in range(GQA_HEADS // 2):
        kv = slice((p // pairs_per_kv) * LANES, (p // pairs_per_kv + 1) * LANES)
        ks, vs = [k_ref[:, kv]], [v_ref[:, kv]]
        if has_ctx:
            ks.append(kc_ref[:, kv])
            vs.append(vc_ref[:, kv])
        qp = q_ref[:, p * LANES:(p + 1) * LANES]
        zero = jnp.zeros_like(qp)
        o_lo = _softmax_pv(jnp.where(lo, qp, zero), ks, vs, scale)
        o_hi = _softmax_pv(jnp.where(lo, zero, qp), ks, vs, scale)
        o_ref[:, p * LANES:(p + 1) * LANES] = jnp.where(lo, o_lo, o_hi).astype(o_ref.dtype)


def _attention(body, q, k, v, kc, vc, wq, wk, wv, wo):
    outs = []
    outs.append(pl.pallas_call(
        functools.partial(body, has_ctx=False),
        grid=(BATCH,),
        in_specs=[pl.BlockSpec((SEQ, wq), lambda b: (b, 0)),
                  pl.BlockSpec((SEQ, wk), lambda b: (b, 0)),
                  pl.BlockSpec((SEQ, wv), lambda b: (b, 0))],
        out_specs=pl.BlockSpec((SEQ, wo), lambda b: (b, 0)),
        out_shape=jax.ShapeDtypeStruct((N_PROMPT, wo), BF16),
        compiler_params=_params("arbitrary", vmem=VMEM_LIMIT_BYTES),
        name="attn_prompt",
    )(q, k, v))
    qt = DEC_SEQ // TM
    q0 = N_PROMPT // TM
    s0 = N_PROMPT // DEC_SEQ
    outs.append(pl.pallas_call(
        functools.partial(body, has_ctx=True),
        grid=(DEC_BATCH, qt),
        in_specs=[pl.BlockSpec((TM, wq), lambda b, t: (q0 + b * qt + t, 0)),
                  pl.BlockSpec((DEC_SEQ, wk), lambda b, t: (s0 + b, 0)),
                  pl.BlockSpec((DEC_SEQ, wv), lambda b, t: (s0 + b, 0)),
                  pl.BlockSpec((PAST_LEN, wk), lambda b, t: (b, 0)),
                  pl.BlockSpec((PAST_LEN, wv), lambda b, t: (b, 0))],
        out_specs=pl.BlockSpec((TM, wo), lambda b, t: (b * qt + t, 0)),
        out_shape=jax.ShapeDtypeStruct((N_SAMPLE, wo), BF16),
        compiler_params=_params("arbitrary", "arbitrary", vmem=VMEM_LIMIT_BYTES),
        name="attn_sample",
    )(q, k, v, kc, vc))
    return outs


def _gqa_prep_kernel(x_ref, g_ref, sc_ref, sh_ref, w_ref, qg_ref, kg_ref, cos_ref, sin_ref,
                     q_ref, k_ref, v_ref, kp_ref, vp_ref):
    lo = _lane_iota((TM, LANES)) < GQA_HEAD_DIM
    nq = GQA_HEADS * GQA_HEAD_DIM
    nk = GQA_KV_HEADS * GQA_HEAD_DIM

    def pair_norm(xp, gain):
        sq = xp * xp
        ms_lo = jnp.sum(jnp.where(lo, sq, 0.0), axis=-1, keepdims=True)
        ms_hi = jnp.sum(jnp.where(lo, 0.0, sq), axis=-1, keepdims=True)
        ms = jnp.where(lo, ms_lo, ms_hi) * (1.0 / GQA_HEAD_DIM)
        return (xp * lax.rsqrt(ms + RMS_EPS)) * gain

    def body(rope):
        def rotate(xp):
            return xp if rope is None else _rope(xp, rope[0], rope[1], GQA_HEAD_DIM // 4)

        h = _norm_mod(x_ref[...], g_ref[...], sc_ref[0], sh_ref[0])
        qkv = _dot(h.astype(BF16), w_ref[...])
        for p in range(nq // LANES):
            sl = slice(p * LANES, (p + 1) * LANES)
            qn = pair_norm(qkv[:, sl], qg_ref[...])
            q_ref[:, sl] = (rotate(qn) * GQA_SCALE).astype(BF16)
        for p in range(nk // LANES):
            sl = slice(p * LANES, (p + 1) * LANES)
            kn = pair_norm(qkv[:, nq + p * LANES:nq + (p + 1) * LANES], kg_ref[...])
            if rope is None:
                kp_ref[:, sl] = kn
            k_ref[:, sl] = rotate(kn).astype(BF16)
        v = qkv[:, nq + nk:]
        if rope is None:
            vp_ref[...] = v
        v_ref[...] = v.astype(BF16)

    _per_group(lambda: body(None), lambda: body((cos_ref[...], sin_ref[...])))


def _gqa_prep(x, g, sc, sh, w, qg, kg, cos_t, sin_t):
    nq = GQA_HEADS * GQA_HEAD_DIM
    nk = GQA_KV_HEADS * GQA_HEAD_DIM
    return pl.pallas_call(
        _gqa_prep_kernel,
        grid=(NT,),
        in_specs=[_row_spec(D_MODEL), _gain_spec(g), _mod_spec(sc), _mod_spec(sh),
                  _full_spec(w.shape), _full_spec(qg.shape), _full_spec(kg.shape),
                  _rope_spec(), _rope_spec()],
        out_specs=[_row_spec(nq), _row_spec(nk), _row_spec(nk), _context_rows_spec(nk), _context_rows_spec(nk)],
        out_shape=[
            jax.ShapeDtypeStruct((N_TOK, nq), BF16),
            jax.ShapeDtypeStruct((N_TOK, nk), BF16),
            jax.ShapeDtypeStruct((N_TOK, nk), BF16),
            jax.ShapeDtypeStruct((N_PROMPT, nk), F32),
            jax.ShapeDtypeStruct((N_PROMPT, nk), F32),
        ],
        compiler_params=_params("arbitrary", vmem=VMEM_LIMIT_BYTES),
        name="gqa_prep",
    )(x, g[0], sc[0], sh[0], w, qg, kg, cos_t, sin_t)


def _outproj_kernel(*refs, n_in):
    ap_refs = refs[:n_in]
    as_refs = refs[n_in:2 * n_in]
    w_refs = refs[2 * n_in:3 * n_in]
    x_ref, g_ref, o_ref = refs[3 * n_in:3 * n_in + 3]
    w_scr = refs[3 * n_in + 3:]
    is_prompt = pl.program_id(0) < NT_PROMPT

    @pl.when(pl.program_id(0) == 0)
    def _():
        for w, s in zip(w_refs, w_scr):
            s[...] = w[...].astype(BF16)

    y = None
    for ap, asm, s in zip(ap_refs, as_refs, w_scr):
        a = jnp.where(is_prompt, ap[...], asm[...])
        d = _dot(a, s[...])
        y = d if y is None else y + d
    o_ref[...] = x_ref[...] + g_ref[0] * y


def _outproj(acts, ws, x, gate):
    n_in = len(acts)

    def w_spec(index, rows):
        return pl.BlockSpec((None, rows, D_MODEL), lambda i: index)

    def prompt_spec(width):
        return pl.BlockSpec((TM, width), lambda i: (jnp.minimum(i, NT_PROMPT - 1), 0))

    def sample_spec(width):
        return pl.BlockSpec((TM, width), lambda i: (jnp.maximum(i - NT_PROMPT, 0), 0))

    return pl.pallas_call(
        functools.partial(_outproj_kernel, n_in=n_in),
        grid=(NT,),
        in_specs=([prompt_spec(ap.shape[1]) for ap, _ in acts] + [sample_spec(asm.shape[1]) for _, asm in acts]
                  + [w_spec(index, rows) for _, index, rows in ws] + [_row_spec(D_MODEL), _mod_spec(gate)]),
        out_specs=_row_spec(D_MODEL),
        out_shape=jax.ShapeDtypeStruct((N_TOK, D_MODEL), F32),
        scratch_shapes=[pltpu.VMEM((rows, D_MODEL), BF16) for _, _, rows in ws],
        compiler_params=_params("arbitrary", vmem=VMEM_LIMIT_BYTES),
        name="outproj",
    )(*(ap for ap, _ in acts), *(asm for _, asm in acts), *(w for w, _, _ in ws), x, gate[0])


def _router_kernel(x_ref, g_ref, sc_ref, sh_ref, wh_ref, wl_ref, br_ref, tri_ref,
                   h_ref, idx_ref, gate_ref, pos_ref, cnt_ref, run_ref):
    @pl.when(pl.program_id(0) == 0)
    def _():
        run_ref[...] = jnp.zeros_like(run_ref)

    h = _norm_mod(x_ref[...], g_ref[...], sc_ref[0], sh_ref[0])
    h_ref[...] = h.reshape((TM,) + ROW_TILE)
    hh, hl = _split(h)
    logits = _dot(hh, wh_ref[...]) + (_dot(hh, wl_ref[...]) + _dot(hl, wh_ref[...])) + br_ref[...]
    lane = _lane_iota((TM, LANES)).astype(F32)
    neg = jnp.float32(-jnp.inf)
    lg = jnp.where(lane < N_EXPERTS, logits, neg)
    tops, sels, hots = [], [], []
    for _ in range(TOP_K):
        m = lg.max(axis=-1, keepdims=True)
        sel = jnp.where(lg == m, lane, float(LANES)).min(axis=-1, keepdims=True)
        hot = lane == sel
        lg = jnp.where(hot, neg, lg)
        tops.append(m)
        sels.append(sel)
        hots.append(hot)
    es = [jnp.exp(t - tops[0]) for t in tops]
    den = es[0] + es[1] + es[2] + es[3]
    member = jnp.zeros((TM, LANES), F32)
    for hot in hots:
        member = member + hot.astype(F32)
    ranks = _dot(tri_ref[...], member.astype(BF16)) + run_ref[...]
    lane4 = _lane_iota((TM, TOP_K))
    idx4 = jnp.zeros((TM, TOP_K), F32)
    gate4 = jnp.zeros((TM, TOP_K), F32)
    pos4 = jnp.zeros((TM, TOP_K), F32)
    for k in range(TOP_K):
        pk = jnp.sum(jnp.where(hots[k], ranks, 0.0), axis=-1, keepdims=True)
        idx4 = jnp.where(lane4 == k, sels[k], idx4)
        gate4 = jnp.where(lane4 == k, es[k] / den, gate4)
        pos4 = jnp.where(lane4 == k, pk, pos4)
    idx_ref[...] = idx4.astype(jnp.int32)
    gate_ref[...] = gate4
    pos_ref[...] = pos4.astype(jnp.int32)
    run_ref[...] = run_ref[...] + jnp.sum(member, axis=0, keepdims=True)
    cnt_ref[...] = run_ref[...]


def _router_weights(w_router, b_router):
    wpad = jnp.pad(w_router, ((0, 0), (0, 0), (0, LANES - N_EXPERTS)))
    wh = wpad.astype(BF16)
    wl = (wpad - wh.astype(F32)).astype(BF16)
    bpad = jnp.pad(b_router, ((0, 0), (0, LANES - N_EXPERTS))).reshape(DEPTH, 1, LANES)
    return wh, wl, bpad


def _router(layer, x, g, sc, sh, router_weights):
    wh, wl, bpad = router_weights
    r = np.arange(TM)
    tri = jnp.asarray(r[None, :] < r[:, None], dtype=BF16)
    narrow = pl.BlockSpec((TM, TOP_K), lambda i: (i, 0))

    def layer_spec(a):
        return pl.BlockSpec((None,) + a.shape[1:], lambda i: (layer, 0, 0))

    return pl.pallas_call(
        _router_kernel,
        grid=(NT,),
        in_specs=[_row_spec(D_MODEL), _gain_spec(g), _mod_spec(sc), _mod_spec(sh),
                  layer_spec(wh), layer_spec(wl), layer_spec(bpad), _full_spec(tri.shape)],
        out_specs=[pl.BlockSpec((TM,) + ROW_TILE, lambda i: (i, 0, 0)), narrow, narrow, narrow,
                   _full_spec((1, LANES))],
        out_shape=[
            jax.ShapeDtypeStruct((N_TOK,) + ROW_TILE, F32),
            jax.ShapeDtypeStruct((N_TOK, TOP_K), jnp.int32),
            jax.ShapeDtypeStruct((N_TOK, TOP_K), F32),
            jax.ShapeDtypeStruct((N_TOK, TOP_K), jnp.int32),
            jax.ShapeDtypeStruct((1, LANES), F32),
        ],
        scratch_shapes=[pltpu.VMEM((1, LANES), F32)],
        compiler_params=_params("arbitrary", vmem=VMEM_LIMIT_BYTES),
        name="router",
    )(x, g[0], sc[0], sh[0], wh, wl, bpad, tri)


ROW_UNROLL = 4
DMA_QUEUES = 2


def _start_all_rows(row_copy):
    def start_rows(j, carry):
        for u in range(ROW_UNROLL):
            for k in range(TOP_K):
                row_copy(j * ROW_UNROLL + u, k).start(priority=k % DMA_QUEUES)
        return carry

    lax.fori_loop(0, TM // ROW_UNROLL, start_rows, 0)


def _dispatch_kernel(dest_ref, clear_ref, h_ref, xs_ref, zero_ref, sem_ref):
    def row_copy(t, k):
        d = dest_ref[(pl.program_id(0) * TM + t) * TOP_K + k]
        return pltpu.make_async_copy(h_ref.at[pl.ds(t, 1)], xs_ref.at[pl.ds(d, 1)], sem_ref.at[0])

    def zero_part(j):
        row0 = pl.multiple_of(j * MOE_PART, MOE_PART)
        return pltpu.make_async_copy(zero_ref, xs_ref.at[pl.ds(row0, MOE_PART)], sem_ref.at[1])

    def for_parts(fn):
        def body(j, carry):
            @pl.when(clear_ref[j] > 0)
            def _():
                fn(zero_part(j))
            return carry
        lax.fori_loop(0, MOE_PARTS * N_BLOCKS, body, 0)

    @pl.when(pl.program_id(0) == 0)
    def _():
        zero_ref[...] = jnp.zeros_like(zero_ref)
        for_parts(lambda cp: cp.start())
        for_parts(lambda cp: cp.wait())

    _start_all_rows(row_copy)
    for _ in range(TOP_K):
        pltpu.make_async_copy(h_ref, xs_ref.at[pl.ds(0, TM)], sem_ref.at[0]).wait()


def _dispatch(dest_flat, clear, h):
    return pl.pallas_call(
        _dispatch_kernel,
        grid_spec=pltpu.PrefetchScalarGridSpec(
            num_scalar_prefetch=2,
            grid=(NT,),
            in_specs=[pl.BlockSpec((TM,) + ROW_TILE, lambda i, d, c: (i, 0, 0))],
            out_specs=pl.BlockSpec(memory_space=pl.ANY),
            scratch_shapes=[pltpu.VMEM((MOE_PART,) + ROW_TILE, F32), pltpu.SemaphoreType.DMA((2,))],
        ),
        out_shape=jax.ShapeDtypeStruct((CAP,) + ROW_TILE, F32),
        compiler_params=_params("arbitrary", vmem=VMEM_LIMIT_BYTES),
        name="moe_dispatch",
    )(dest_flat, clear, h)


def _expert_kernel(be_ref, nh_ref, last_ref, ebuf_ref, enext_ref, xs_ref, wgu_hbm, bgu_ref, wd_hbm, bd_ref, ys_ref,
                   wgu_s, wd_s, wgu_f, wd_f, sem_ref, *, layer):
    del last_ref
    b = pl.program_id(0)
    nh = nh_ref[b]

    def weight_copies(e, buf):
        return (pltpu.make_async_copy(wgu_hbm.at[layer, e], wgu_f.at[buf], sem_ref.at[buf, 0]),
                pltpu.make_async_copy(wd_hbm.at[layer, e], wd_f.at[buf], sem_ref.at[buf, 1]))

    @pl.when((nh > 0) & ((b == 0) | (be_ref[b] != be_ref[jnp.maximum(b - 1, 0)])))
    def _():
        buf = ebuf_ref[b]

        @pl.when(b == 0)
        def _():
            for cp in weight_copies(be_ref[b], buf):
                cp.start()

        for cp in weight_copies(be_ref[b], buf):
            cp.wait()

        @pl.when(enext_ref[b] >= 0)
        def _():
            for cp in weight_copies(enext_ref[b], 1 - buf):
                cp.start()

        wgu_s[...] = wgu_f[buf].astype(BF16)
        wd_s[...] = wd_f[buf].astype(BF16)

    def ffn(rows):
        x = xs_ref[0:rows].reshape(rows, D_MODEL)
        gu = _dot(x.astype(BF16), wgu_s[...]) + bgu_ref[...]
        g = jnp.minimum(gu[:, :D_EXPERT], SWIGLU_LIMIT)
        u = jnp.clip(gu[:, D_EXPERT:], -SWIGLU_LIMIT, SWIGLU_LIMIT)
        act = (u + 1.0) * (g * jax.nn.sigmoid(SWIGLU_ALPHA * g))
        y = _dot(act.astype(BF16), wd_s[...]) + bd_ref[...]
        ys_ref[0:rows] = y.reshape((rows,) + ROW_TILE)

    def partial_block(parts):
        rows = parts * MOE_PART
        ffn(rows)
        if rows < MOE_BLK:
            ys_ref[rows:] = jnp.zeros((MOE_BLK - rows,) + ROW_TILE, F32)

    for parts in range(1, MOE_PARTS + 1):
        pl.when(nh == parts)(functools.partial(partial_block, parts))

    @pl.when(nh == 0)
    def _():
        ys_ref[...] = jnp.zeros_like(ys_ref)


def _experts(layer, block_e, block_nh, last_used, block_buf, block_next, xs, w_gu, b_gu, w_down, b_down):
    def xs_map(b, be, nh, lu, eb, en):
        return (jnp.minimum(b, lu[0]), 0, 0)

    def e_map(b, be, nh, lu, eb, en):
        return (layer, be[b], 0, 0)

    hbm = pl.BlockSpec(memory_space=pl.ANY)
    return pl.pallas_call(
        functools.partial(_expert_kernel, layer=layer),
        grid_spec=pltpu.PrefetchScalarGridSpec(
            num_scalar_prefetch=5,
            grid=(N_BLOCKS,),
            in_specs=[
                pl.BlockSpec((MOE_BLK,) + ROW_TILE, xs_map),
                hbm,
                pl.BlockSpec((None, None, 1, 2 * D_EXPERT), e_map),
                hbm,
                pl.BlockSpec((None, None, 1, D_MODEL), e_map),
            ],
            out_specs=pl.BlockSpec((MOE_BLK,) + ROW_TILE, lambda b, be, nh, lu, eb, en: (b, 0, 0)),
            scratch_shapes=[
                pltpu.VMEM((D_MODEL, 2 * D_EXPERT), BF16), pltpu.VMEM((D_EXPERT, D_MODEL), BF16),
                pltpu.VMEM((2, D_MODEL, 2 * D_EXPERT), F32), pltpu.VMEM((2, D_EXPERT, D_MODEL), F32),
                pltpu.SemaphoreType.DMA((2, 2)),
            ],
        ),
        out_shape=jax.ShapeDtypeStruct((CAP,) + ROW_TILE, F32),
        compiler_params=_params("arbitrary", vmem=VMEM_LIMIT_BYTES),
        name="moe_experts",
    )(block_e, block_nh, last_used, block_buf, block_next, xs, w_gu, b_gu.reshape(DEPTH, N_EXPERTS, 1, -1),
      w_down, b_down.reshape(DEPTH, N_EXPERTS, 1, -1))


def _combine_kernel(dest_ref, x_ref, g_ref, gate_ref, ys_ref, *rest, per_group):
    out_refs, (buf_ref, sem_ref) = rest[:-2], rest[-2:]
    i = pl.program_id(0)
    cur = lax.rem(i, 2)

    def start_tile(tile, buf):
        def row_copy(t, k):
            d = dest_ref[(tile * TM + t) * TOP_K + k]
            return pltpu.make_async_copy(ys_ref.at[pl.ds(d, 1)], buf_ref.at[buf, k, pl.ds(t, 1)], sem_ref.at[buf])
        _start_all_rows(row_copy)

    @pl.when(i == 0)
    def _():
        start_tile(0, 0)

    @pl.when(i + 1 < NT)
    def _():
        start_tile(i + 1, 1 - cur)

    for k in range(TOP_K):
        pltpu.make_async_copy(ys_ref.at[pl.ds(0, TM)], buf_ref.at[cur, k], sem_ref.at[cur]).wait()
    gates = gate_ref[...]
    ff = gates[:, 0:1] * buf_ref[cur, 0].reshape(TM, D_MODEL)
    for k in range(1, TOP_K):
        ff = ff + gates[:, k:k + 1] * buf_ref[cur, k].reshape(TM, D_MODEL)
    out = x_ref[...] + g_ref[0] * ff
    if per_group:
        def store(ref):
            ref[...] = out
        _per_group(functools.partial(store, out_refs[0]), functools.partial(store, out_refs[1]))
    else:
        out_refs[0][...] = out


def _combine(dest_flat, x, gate_vec, gates, ys, per_group):
    if per_group:
        out_specs = [_context_rows_spec(D_MODEL),
                     pl.BlockSpec((TM, D_MODEL), lambda i, *_: (jnp.maximum(i - NT_PROMPT, 0), 0))]
        out_shape = [jax.ShapeDtypeStruct((N_PROMPT, D_MODEL), F32), jax.ShapeDtypeStruct((N_SAMPLE, D_MODEL), F32)]
    else:
        out_specs = pl.BlockSpec((TM, D_MODEL), lambda i, d: (i, 0))
        out_shape = jax.ShapeDtypeStruct((N_TOK, D_MODEL), F32)
    return pl.pallas_call(
        functools.partial(_combine_kernel, per_group=per_group),
        grid_spec=pltpu.PrefetchScalarGridSpec(
            num_scalar_prefetch=1,
            grid=(NT,),
            in_specs=[
                pl.BlockSpec((TM, D_MODEL), lambda i, d: (i, 0)),
                _mod_spec(gate_vec),
                pl.BlockSpec((TM, TOP_K), lambda i, d: (i, 0)),
                pl.BlockSpec(memory_space=pl.ANY),
            ],
            out_specs=out_specs,
            scratch_shapes=[pltpu.VMEM((2, TOP_K, TM) + ROW_TILE, F32), pltpu.SemaphoreType.DMA((2,))],
        ),
        out_shape=out_shape,
        compiler_params=_params("arbitrary", vmem=VMEM_LIMIT_BYTES),
        name="moe_combine",
    )(dest_flat, x, gate_vec[0], gates, ys)


def _moe(layer, x, g, sc, sh, gate_vec, router_weights, w_gu, b_gu, w_down, b_down, per_group=False):
    h, idx, gates, pos, counts = _router(layer, x, g, sc, sh, router_weights)
    cnt = counts[0, :N_EXPERTS].astype(jnp.int32)
    nparts = (cnt + MOE_PART - 1) // MOE_PART
    nblk = (nparts + MOE_PARTS - 1) // MOE_PARTS
    e_ids = jnp.arange(N_EXPERTS, dtype=jnp.int32)
    blk_end = jnp.sum(jnp.where(e_ids[None, :] <= e_ids[:, None], nblk[None, :], 0), axis=1)
    blk_start = blk_end - nblk
    last_used = blk_end[-1] - 1

    def per_expert(table, e):
        return jnp.sum(jnp.where(e[..., None] == e_ids, table, 0), axis=-1)

    def expert_of_block(blk):
        return jnp.sum((blk_end <= jnp.minimum(blk, last_used)[:, None]).astype(jnp.int32), axis=1)

    dest = per_expert(blk_start * MOE_BLK, idx) + pos
    b_ids = jnp.arange(N_BLOCKS, dtype=jnp.int32)
    block_e = expert_of_block(b_ids)
    block_nh = jnp.where(
        b_ids <= last_used,
        jnp.clip(per_expert(nparts, block_e) - MOE_PARTS * (b_ids - per_expert(blk_start, block_e)), 0, MOE_PARTS), 0)
    h_ids = jnp.arange(MOE_PARTS * N_BLOCKS, dtype=jnp.int32)
    h_e = expert_of_block(h_ids // MOE_PARTS)
    h_nparts = per_expert(nparts, h_e)
    h_local = h_ids - MOE_PARTS * per_expert(blk_start, h_e)
    holds_rows = (h_ids // MOE_PARTS <= last_used) & (h_local < h_nparts)
    clear = jnp.logical_not(holds_rows) | (h_local == h_nparts - 1)
    dest_flat = dest.reshape(-1).astype(jnp.int32)
    xs = _dispatch(dest_flat, clear.astype(jnp.int32), h)
    used = nblk > 0
    before = e_ids[None, :] < e_ids[:, None]
    expert_buf = jnp.sum(jnp.where(before & used[None, :], 1, 0), axis=1) % 2
    expert_next = jnp.min(jnp.where((e_ids[None, :] > e_ids[:, None]) & used[None, :], e_ids[None, :], N_EXPERTS),
                          axis=1)
    expert_next = jnp.where(expert_next == N_EXPERTS, -1, expert_next)
    ys = _experts(layer, block_e.astype(jnp.int32), block_nh.astype(jnp.int32),
                  last_used.reshape(1).astype(jnp.int32), per_expert(expert_buf, block_e).astype(jnp.int32),
                  per_expert(expert_next, block_e).astype(jnp.int32), xs, w_gu, b_gu, w_down, b_down)
    return _combine(dest_flat, x, gate_vec, gates, ys, per_group)


def _rope_tables(d_rot, lane0, period):
    n_rows = DEC_SEQ // GRID_W
    rows = np.repeat(np.arange(n_rows), GRID_W).astype(np.float32)
    cols = np.tile(np.arange(GRID_W), n_rows).astype(np.float32)
    half = d_rot // 2
    lane = np.arange(LANES)
    i = (lane - lane0) % period
    active = (lane >= lane0) & (i < d_rot)
    w = i % half
    f = w % (half // 2)
    pos = np.where((i // half)[None, :] == 0, rows[:, None], cols[:, None])
    sign = np.where(w < half // 2, -1.0, 1.0).astype(np.float32)
    inv = ROPE_THETA ** (-jnp.arange(0, half, 2, dtype=F32) / half)
    ang = jnp.asarray(pos) * inv[f][None, :]
    cos = jnp.where(active[None, :], jnp.cos(ang), 1.0)
    sin = jnp.where(active[None, :], jnp.sin(ang) * sign[None, :], 0.0)
    return cos, sin


def _rope_spec():
    def index(i):
        return (jnp.maximum(i - NT_PROMPT, 0) % TILES_PER_SAMPLE, 0)
    return pl.BlockSpec((TM, LANES), index)


def _pad_heads(w, n_heads, width):
    lead = w.shape[:-1]
    w = w.reshape(lead + (n_heads, width))
    w = jnp.pad(w, [(0, 0)] * len(lead) + [(0, 0), (0, HEAD_PAD - width)])
    return w.reshape(lead + (n_heads * HEAD_PAD,))


_Q_ORDER = (0, 4, 1, 5, 2, 6, 3, 7, 8, 12, 9, 13, 10, 14, 11, 15)


def _perm_q_heads(w, axis):
    shape = w.shape
    n = shape[axis]
    w = jnp.moveaxis(w, axis, 0).reshape((GQA_HEADS, n // GQA_HEADS) + tuple(s for a, s in enumerate(shape) if a != axis))
    w = w[jnp.array(_Q_ORDER)]
    w = w.reshape((n,) + w.shape[2:])
    return jnp.moveaxis(w, 0, axis)


def _constants():
    mla_cos, mla_sin = _rope_tables(MLA_ROPE, MLA_NOPE, LANES)
    gqa_cos, gqa_sin = _rope_tables(GQA_HEAD_DIM, 0, GQA_HEAD_DIM)
    dft = {}
    for L in (SEQ, DEC_SEQ):
        cm, sm = _dft_tables(L)
        fmat = jnp.concatenate([cm, sm], axis=0).astype(BF16)
        ftmat = fmat.T
        dft[L] = (cm, sm, fmat, ftmat)
    return dict(mla=(mla_cos, mla_sin), gqa=(gqa_cos, gqa_sin), dft=dft)


def _even_mixer(x, p, i, gmix, sc1, sh1, g1, consts):
    proj = _normlin(x, gmix, sc1, sh1, p["w_in_ab"], i, IN_AB_PAD)
    y_hy = []
    for L, blk0, nseq, per_step in ((SEQ, 0, BATCH, HY_PROMPT_SEQS), (DEC_SEQ, N_PROMPT // DEC_SEQ, DEC_BATCH, 1)):
        cm, sm, fmat, ftmat = consts["dft"][L]
        kc, ks = _hy_filter(L, cm, sm, p["hy_filter_w1"][i], p["hy_filter_b1"][i], p["hy_filter_freq"][i],
                            p["hy_filter_w2"][i], p["hy_filter_b2"][i], p["hy_filter_w3"][i],
                            p["hy_filter_b3"][i], p["hy_log_decay"][i])
        y_hy.append(_hy_mix(proj, blk0, nseq, per_step, L, p["hy_conv_w"][i], p["hy_conv_b"][i], kc, ks,
                            p["hy_bias"][i], fmat, ftmat))

    wq = _pad_heads(p["mla_wq_b"][i], MLA_HEADS, MLA_QK).astype(BF16)
    wkv = p["mla_wkv_b"][i].astype(BF16)
    qg = jnp.pad(p["mla_q_norm"][i], (0, HEAD_PAD - MLA_QK)).reshape(1, HEAD_PAD)
    kg = jnp.pad(p["mla_k_norm"][i], (0, HEAD_PAD - MLA_QK)).reshape(1, HEAD_PAD)
    mla_cos, mla_sin = consts["mla"]
    q, k, v, lat = _mla_prep(proj, p["mla_q_lora_norm"][i].reshape(1, -1), wq, qg,
                             p["mla_kv_lora_norm"][i].reshape(1, -1), wkv, kg, mla_cos, mla_sin)
    ctx = jnp.pad(p["cache_mla_latent"][:, i].reshape(DEC_BATCH * PAST_LEN, -1),
                  ((0, 0), (0, KV_PAD - MLA_KV_LORA - MLA_ROPE)))
    kc_ctx, vc_ctx = _mla_ctx(ctx, wkv, kg)
    nqk = MLA_HEADS * HEAD_PAD
    o = _attention(_mla_attn_kernel, q, k, v, kc_ctx, vc_ctx, nqk, nqk, nqk, MLA_HEADS * MLA_V)
    w_out = p["w_out_ab"]
    x = _outproj([y_hy, o], [(w_out, (i, 0, 0), HY_D), (w_out, (i, 1, 0), HY_D)], x, g1)
    return x, lat, dict(y_hy=jnp.concatenate(y_hy, axis=0), o=jnp.concatenate(o, axis=0))


def _odd_mixer(x, p, i, gmix, sc1, sh1, g1, consts):
    nq = GQA_HEADS * GQA_HEAD_DIM
    nk = GQA_KV_HEADS * GQA_HEAD_DIM
    w = p["w_qkv_c"][i]
    w_qkv = jnp.concatenate([_perm_q_heads(w[:, :nq], 1), w[:, nq:]], axis=1).astype(BF16)
    qg = jnp.tile(p["gqa_q_norm"][i], LANES // GQA_HEAD_DIM).reshape(1, LANES)
    kg = jnp.tile(p["gqa_k_norm"][i], LANES // GQA_HEAD_DIM).reshape(1, LANES)
    gqa_cos, gqa_sin = consts["gqa"]
    q, k, v, k_plain, v_plain = _gqa_prep(x, gmix, sc1, sh1, w_qkv, qg, kg, gqa_cos, gqa_sin)
    kc_ctx = p["cache_gqa_k"][:, i].reshape(DEC_BATCH * PAST_LEN, -1).astype(BF16)
    vc_ctx = p["cache_gqa_v"][:, i].reshape(DEC_BATCH * PAST_LEN, -1).astype(BF16)
    o = _attention(_gqa_attn_kernel, q, k, v, kc_ctx, vc_ctx, nq, nk, nk, nq)
    w_out = _perm_q_heads(p["w_out_c"][i], 0)[None]
    x = _outproj([o], [(w_out, (0, 0, 0), nq)], x, g1)
    return x, k_plain, v_plain


def kernel(x_prompt, x_sample, cache_mla_latent, cache_gqa_k, cache_gqa_v, c, c_ctx, w_ada, b_ada, norm_mix, norm_ffn, w_in_ab, hy_conv_w, hy_conv_b, hy_filter_w1, hy_filter_b1, hy_filter_freq, hy_filter_w2, hy_filter_b2, hy_filter_w3, hy_filter_b3, hy_log_decay, hy_bias, mla_q_lora_norm, mla_wq_b, mla_kv_lora_norm, mla_wkv_b, mla_q_norm, mla_k_norm, w_out_ab, w_qkv_c, gqa_q_norm, gqa_k_norm, w_out_c, moe_router_w, moe_router_b, moe_w_gate_up, moe_b_gate_up, moe_w_down, moe_b_down):
    p = dict(locals())
    x = jnp.concatenate([x_prompt.reshape(N_PROMPT, D_MODEL), x_sample.reshape(N_SAMPLE, D_MODEL)], axis=0)

    cond = jnp.concatenate([c_ctx[None, :], c, jnp.zeros((COND_ROWS - N_COND, D_MODEL), F32)], axis=0)
    mods = _modulation(cond, w_ada, b_ada)
    tile_cond = jnp.concatenate([jnp.zeros((NT_PROMPT,), jnp.int32),
                                 1 + jnp.arange(NT - NT_PROMPT, dtype=jnp.int32) // TILES_PER_SAMPLE])
    mods = mods[:, tile_cond].reshape(DEPTH, NT, 6, 1, D_MODEL).transpose(0, 2, 1, 3, 4)

    consts = _constants()
    router_weights = _router_weights(moe_router_w, moe_router_b)
    gains_mix = norm_mix.reshape(DEPTH, 1, D_MODEL)
    gains_ffn = norm_ffn.reshape(DEPTH, 1, D_MODEL)

    lat_out, k_out, v_out = [], [], []
    for l in range(DEPTH):
        sh1, sc1, g1, sh2, sc2, g2 = ((mods, l, j) for j in range(6))
        i = l // 2
        gmix = (gains_mix, l)
        if l % 2 == 0:
            x, lat, _ = _even_mixer(x, p, i, gmix, sc1, sh1, g1, consts)
            lat_out.append(lat.reshape(BATCH, SEQ, -1))
        else:
            x, k_plain, v_plain = _odd_mixer(x, p, i, gmix, sc1, sh1, g1, consts)
            k_out.append(k_plain.reshape(BATCH, SEQ, GQA_KV_HEADS, GQA_HEAD_DIM))
            v_out.append(v_plain.reshape(BATCH, SEQ, GQA_KV_HEADS, GQA_HEAD_DIM))
        x = _moe(l, x, (gains_ffn, l), sc2, sh2, g2, router_weights,
                 moe_w_gate_up, moe_b_gate_up, moe_w_down, moe_b_down, per_group=(l == DEPTH - 1))

    y_prompt = x[0].reshape(BATCH, SEQ, D_MODEL)
    y_sample = x[1].reshape(DEC_BATCH, DEC_SEQ, D_MODEL)
    return (y_prompt, y_sample, jnp.stack(lat_out, axis=1), jnp.stack(k_out, axis=1), jnp.stack(v_out, axis=1))
```

```python
import functools
import math

import jax
import jax.numpy as jnp
import numpy as np
from jax import lax
from jax.experimental import pallas as pl
from jax.experimental.pallas import tpu as pltpu

F32 = jnp.float32
BF16 = jnp.bfloat16

D_MODEL = 1024
BATCH = 32
SEQ = 256
DEPTH = 4
DEC_BATCH = 4
DEC_SEQ = 1024
PAST_LEN = 256
GRID_W = 64
N_EVEN = (DEPTH + 1) // 2
N_ODD = DEPTH // 2
HY_D = D_MODEL // 2
HY_ORDER = 2
HY_BANDS = 16
HY_EMB = 2 * HY_BANDS + 1
HY_FILTER_HIDDEN = 64
MLA_HEADS = 8
MLA_NOPE = 64
MLA_ROPE = 32
MLA_QK = MLA_NOPE + MLA_ROPE
MLA_V = HY_D // MLA_HEADS
MLA_Q_LORA = 3 * D_MODEL // 8
MLA_KV_LORA = D_MODEL // 4
GQA_HEADS = 16
GQA_KV_HEADS = 4
GQA_HEAD_DIM = D_MODEL // GQA_HEADS
N_EXPERTS = 32
TOP_K = 4
D_EXPERT = D_MODEL
SWIGLU_LIMIT = 7.0
SWIGLU_ALPHA = 1.702
ROPE_THETA = 10000.0
RMS_EPS = 1e-6
GQA_SCALE = GQA_HEAD_DIM ** -0.5
assert math.frexp(GQA_SCALE)[0] == 0.5
IN_AB = (HY_ORDER + 1) * HY_D + MLA_Q_LORA + MLA_KV_LORA + MLA_ROPE

N_PROMPT = BATCH * SEQ
N_SAMPLE = DEC_BATCH * DEC_SEQ
N_TOK = N_PROMPT + N_SAMPLE

LANES = 128
SUBLANES = 8
VMEM_LIMIT_BYTES = 56 * 1024 * 1024

TM = 512
NT = N_TOK // TM
NT_PROMPT = N_PROMPT // TM
TILES_PER_SAMPLE = DEC_SEQ // TM
N_COND = 1 + DEC_BATCH
COND_ROWS = 8
HEAD_PAD = LANES
IN_AB_PAD = 2304
KV_PAD = 384
MOE_BLK = 512
MOE_PARTS = 4
MOE_PART = MOE_BLK // MOE_PARTS
XS_RING = 3
N_SLOTS = N_TOK * TOP_K
N_BLOCKS = N_SLOTS // MOE_BLK + N_EXPERTS
CAP = N_BLOCKS * MOE_BLK
ROW_TILE = (SUBLANES, LANES)
assert SUBLANES * LANES == D_MODEL
HY_CH = 256
HY_PROMPT_SEQS = 8


def _dot(a, b):
    return jnp.dot(a, b, preferred_element_type=F32)


def _dot_nt(a, b):
    return lax.dot_general(a, b, (((1,), (1,)), ((), ())), preferred_element_type=F32)


def _split(x):
    hi = x.astype(BF16)
    lo = (x - hi.astype(F32)).astype(BF16)
    return hi, lo


def _dot3(a, b):
    ah, al = _split(a)
    bh, bl = _split(b)
    return _dot(ah, bh) + (_dot(ah, bl) + _dot(al, bh))


def _lane_iota(shape):
    return lax.broadcasted_iota(jnp.int32, shape, len(shape) - 1)


def _params(*sem, vmem=None):
    return pltpu.CompilerParams(dimension_semantics=sem, vmem_limit_bytes=vmem)


def _mod_kernel(c_ref, w_ref, b_ref, o_ref):
    c = c_ref[...]
    s = c * jax.nn.sigmoid(c)
    o_ref[0] = _dot(s.astype(BF16), w_ref[0].astype(BF16)) + b_ref[0]


def _modulation(cond, w_ada, b_ada):
    nblk = 6
    return pl.pallas_call(
        _mod_kernel,
        grid=(DEPTH, nblk),
        in_specs=[
            pl.BlockSpec((COND_ROWS, D_MODEL), lambda l, j: (0, 0)),
            pl.BlockSpec((1, D_MODEL, D_MODEL), lambda l, j: (l, 0, j)),
            pl.BlockSpec((1, 1, D_MODEL), lambda l, j: (l, 0, j)),
        ],
        out_specs=pl.BlockSpec((1, COND_ROWS, D_MODEL), lambda l, j: (l, 0, j)),
        out_shape=jax.ShapeDtypeStruct((DEPTH, COND_ROWS, nblk * D_MODEL), F32),
        compiler_params=_params("arbitrary", "arbitrary"),
        name="modulation",
    )(cond, w_ada, b_ada.reshape(DEPTH, 1, nblk * D_MODEL))


def _norm_mod(x, g, sc, sh):
    ms = jnp.mean(x * x, axis=-1, keepdims=True)
    y = x * lax.rsqrt(ms + RMS_EPS)
    return (y * g) * (1.0 + sc) + sh


def _row_spec(width):
    return pl.BlockSpec((TM, width), lambda i: (i, 0))


def _mod_spec(mod):
    _, layer, which = mod
    return pl.BlockSpec((None, None, 1, 1, D_MODEL), lambda i, *_: (layer, which, i, 0, 0))


def _gain_spec(gain):
    _, layer = gain
    return pl.BlockSpec((None, 1, D_MODEL), lambda i, *_: (layer, 0, 0))


def _full_spec(shape):
    nd = len(shape)
    return pl.BlockSpec(shape, lambda i: (0,) * nd)


def _normlin_kernel(x_ref, g_ref, sc_ref, sh_ref, w_ref, o_ref, w_scr):
    nin = w_ref.shape[1]

    @pl.when(pl.program_id(0) == 0)
    def _():
        w_scr[:, :nin] = w_ref[...].astype(BF16)
        w_scr[:, nin:] = jnp.zeros((D_MODEL, w_scr.shape[1] - nin), BF16)

    h = _norm_mod(x_ref[...], g_ref[...], sc_ref[0], sh_ref[0])
    o_ref[...] = _dot(h.astype(BF16), w_scr[...])


def _normlin(x, g, sc, sh, w_all, layer, nout):
    nin = w_all.shape[2]
    return pl.pallas_call(
        _normlin_kernel,
        grid=(NT,),
        in_specs=[_row_spec(D_MODEL), _gain_spec(g), _mod_spec(sc), _mod_spec(sh),
                  pl.BlockSpec((None, D_MODEL, nin), lambda i: (layer, 0, 0))],
        out_specs=_row_spec(nout),
        out_shape=jax.ShapeDtypeStruct((N_TOK, nout), F32),
        scratch_shapes=[pltpu.VMEM((D_MODEL, nout), BF16)],
        compiler_params=_params("arbitrary", vmem=VMEM_LIMIT_BYTES),
        name="normlin",
    )(x, g[0], sc[0], sh[0], w_all)


def _hy_filter_kernel(z_ref, w1_ref, b1_ref, fr_ref, w2_ref, b2_ref, w3_ref, b3_ref, ed_ref,
                      c_ref, s_ref, kc_ref, ks_ref):
    L = z_ref.shape[0]
    z = z_ref[...]
    fr = fr_ref[...]
    hdn = jnp.sin(fr * (_dot3(z, w1_ref[...]) + b1_ref[...]))
    hdn = jnp.sin(fr * (_dot3(hdn, w2_ref[...]) + b2_ref[...]))
    filt = _dot3(hdn, w3_ref[...]) + b3_ref[...]
    t = z[:, 0:1]
    filt = filt * jnp.exp(-t * ed_ref[...])
    row = lax.broadcasted_iota(jnp.int32, (L, HY_D), 0)
    cm = c_ref[...]
    sm = s_ref[...]
    for o in range(HY_ORDER):
        fw = filt[:, (2 * o) * HY_D:(2 * o + 1) * HY_D]
        bw = filt[:, (2 * o + 1) * HY_D:(2 * o + 2) * HY_D]
        den = (jnp.sum(jnp.abs(fw), axis=0, keepdims=True)
               + jnp.sum(jnp.abs(bw), axis=0, keepdims=True)) + 1e-6
        fw = fw / den
        bw = jnp.where(row == 0, 0.0, bw / den)
        kc_ref[o] = _dot3(cm, fw + bw) * (1.0 / L)
        ks_ref[o] = _dot3(sm, fw - bw) * (1.0 / L)


def _dft_tables(L):
    m = jnp.arange(L, dtype=jnp.int32)
    phase = ((2 * m + 1)[:, None] * m[None, :]) % (4 * L)
    ang = phase.astype(F32) * (2.0 * math.pi / (4 * L))
    return jnp.cos(ang), jnp.sin(ang)


def _filter_features(L):
    p = jnp.arange(L, dtype=F32)
    t = p / max(L - 1, 1)
    bands = jnp.linspace(1e-4, HY_BANDS - 1, HY_BANDS, dtype=F32)
    ang = (2.0 * math.pi / L) * p[:, None] * bands[None, :]
    z = jnp.concatenate([t[:, None], jnp.cos(ang), -jnp.sin(ang)], axis=-1)
    return jnp.pad(z, ((0, 0), (0, LANES - HY_EMB)))


def _hy_filter(L, cmat, smat, w1, b1, fr, w2, b2, w3, b3, log_decay):
    nf = HY_ORDER * 2 * HY_D
    args = (
        _filter_features(L),
        jnp.pad(w1, ((0, LANES - HY_EMB), (0, 0))),
        b1.reshape(1, -1), fr.reshape(1, -1), w2, b2.reshape(1, -1), w3, b3.reshape(1, -1),
        jnp.exp(log_decay.astype(F32)).reshape(1, nf),
        cmat, smat,
    )
    out_sds = jax.ShapeDtypeStruct((HY_ORDER, L, HY_D), F32)
    return pl.pallas_call(
        _hy_filter_kernel,
        grid=(1,),
        in_specs=[_full_spec(a.shape) for a in args],
        out_specs=[_full_spec(out_sds.shape)] * 2,
        out_shape=[out_sds, out_sds],
        compiler_params=_params("arbitrary", vmem=VMEM_LIMIT_BYTES),
        name=f"hy_filter_{L}",
    )(*args)


def _hy_mix_kernel(u0_ref, u1_ref, u2_ref, cw0_ref, cw1_ref, cw2_ref, cb0_ref, cb1_ref, cb2_ref,
                   kc_ref, ks_ref, hb_ref, f_ref, ft_ref, o_ref):
    S, L, _ = u0_ref.shape
    row = lax.broadcasted_iota(jnp.int32, (L, HY_CH), 0)

    def lanes(per_seq):
        return per_seq[0] if S == 1 else jnp.concatenate(per_seq, axis=1)

    def short_conv(u_ref, cw_ref, cb_ref):
        w = cw_ref[0]
        out = []
        for s in range(S):
            u = u_ref[s]
            prev = jnp.where(row == 0, 0.0, pltpu.roll(u, 1, 0))
            nxt = jnp.where(row == L - 1, 0.0, pltpu.roll(u, L - 1, 0))
            out.append((prev * w[0:1] + u * w[1:2]) + nxt * w[2:3] + cb_ref[0])
        return lanes(out)

    z = short_conv(u0_ref, cw0_ref, cb0_ref)
    gates = (short_conv(u1_ref, cw1_ref, cb1_ref), short_conv(u2_ref, cw2_ref, cb2_ref))
    for o in range(HY_ORDER):
        zz = _dot(f_ref[...], z.astype(BF16))
        cz, sz = zz[:L], zz[L:]
        kc, ks = lanes([kc_ref[o]] * S), lanes([ks_ref[o]] * S)
        w1 = cz * kc - sz * ks
        w2 = cz * ks + sz * kc
        ww = jnp.concatenate([w1, w2], axis=0).astype(BF16)
        conv = _dot(ft_ref[...], ww)
        z = gates[o] * (conv + z * lanes([hb_ref[0, o]] * S))
    for s in range(S):
        o_ref[s] = z[:, s * HY_CH:(s + 1) * HY_CH].astype(o_ref.dtype)


def _hy_mix(proj, row_block0, nseq, seqs_per_step, L, cw, cb, kc, ks, hb, fmat, ftmat):
    nch = HY_D // HY_CH
    nparts = HY_ORDER + 1
    S = seqs_per_step
    assert nseq % S == 0 and row_block0 % S == 0
    cw3 = cw.reshape(3, nparts * nch, HY_CH).transpose(1, 0, 2)
    cb3 = cb.reshape(nparts * nch, 1, HY_CH)
    hb3 = hb.reshape(HY_ORDER, nch, 1, HY_CH).transpose(1, 0, 2, 3)
    proj = proj.reshape(N_TOK // L, L, proj.shape[1])

    def u_spec(part):
        return pl.BlockSpec((S, L, HY_CH), lambda s, c: (row_block0 // S + s, 0, part * nch + c))

    def cw_spec(part):
        return pl.BlockSpec((1, 3, HY_CH), lambda s, c: (part * nch + c, 0, 0))

    def cb_spec(part):
        return pl.BlockSpec((1, 1, HY_CH), lambda s, c: (part * nch + c, 0, 0))

    return pl.pallas_call(
        _hy_mix_kernel,
        grid=(nseq // S, nch),
        in_specs=[
            u_spec(0), u_spec(1), u_spec(2),
            cw_spec(0), cw_spec(1), cw_spec(2),
            cb_spec(0), cb_spec(1), cb_spec(2),
            pl.BlockSpec((HY_ORDER, L, HY_CH), lambda s, c: (0, 0, c)),
            pl.BlockSpec((HY_ORDER, L, HY_CH), lambda s, c: (0, 0, c)),
            pl.BlockSpec((1, HY_ORDER, 1, HY_CH), lambda s, c: (c, 0, 0, 0)),
            pl.BlockSpec((2 * L, L), lambda s, c: (0, 0)),
            pl.BlockSpec((L, 2 * L), lambda s, c: (0, 0)),
        ],
        out_specs=pl.BlockSpec((S, L, HY_CH), lambda s, c: (s, 0, c)),
        out_shape=jax.ShapeDtypeStruct((nseq, L, HY_D), BF16),
        compiler_params=_params("arbitrary", "arbitrary", vmem=VMEM_LIMIT_BYTES),
        name=f"hy_mix_{L}",
    )(proj, proj, proj, cw3, cw3, cw3, cb3, cb3, cb3, kc, ks, hb3, fmat, ftmat).reshape(nseq * L, HY_D)


def _head_rms(xh, gain, dim):
    ms = jnp.sum(xh * xh, axis=-1, keepdims=True) * (1.0 / dim)
    return (xh * lax.rsqrt(ms + RMS_EPS)) * gain


def _context_rows_spec(width):
    return pl.BlockSpec((TM, width), lambda i, *_: (jnp.minimum(i, NT_PROMPT - 1), 0))


def _per_group(context_fn, latent_fn):
    is_context = pl.program_id(0) < NT_PROMPT
    pl.when(is_context)(context_fn)
    pl.when(jnp.logical_not(is_context))(latent_fn)


def _rope(xh, cos, sin, half):
    lane = _lane_iota(xh.shape)
    first = (lane % (2 * half)) < half
    rot = jnp.where(first, pltpu.roll(xh, LANES - half, 1), pltpu.roll(xh, half, 1))
    return xh * cos + rot * sin


def _mla_keys_values(lat_n, kr_blk, wkv_ref, kg_ref, rope):
    kv = _dot(lat_n.astype(BF16), wkv_ref[...])
    kr = pltpu.roll(kr_blk, MLA_NOPE, 1)
    nope = _lane_iota(kr.shape) < MLA_NOPE
    ks = []
    for h in range(MLA_HEADS):
        kh = jnp.where(nope, kv[:, h * HEAD_PAD:(h + 1) * HEAD_PAD], 0.0) + kr
        kh = _head_rms(kh, kg_ref[...], MLA_QK)
        if rope is not None:
            kh = _rope(kh, rope[0], rope[1], MLA_ROPE // 4)
        ks.append(kh)
    return ks, kv


def _mla_prep_kernel(qa_ref, kva_ref, gq_ref, wq_ref, qg_ref, gkv_ref, wkv_ref, kg_ref,
                     cos_ref, sin_ref, q_ref, k_ref, v_ref, lat_ref):
    def body(rope):
        qa = qa_ref[...]
        ms = jnp.mean(qa * qa, axis=-1, keepdims=True)
        qn = (qa * lax.rsqrt(ms + RMS_EPS)) * gq_ref[...]
        q = _dot(qn.astype(BF16), wq_ref[...])
        for h in range(MLA_HEADS):
            qh = _head_rms(q[:, h * HEAD_PAD:(h + 1) * HEAD_PAD], qg_ref[...], MLA_QK)
            if rope is not None:
                qh = _rope(qh, rope[0], rope[1], MLA_ROPE // 4)
            q_ref[:, h * HEAD_PAD:(h + 1) * HEAD_PAD] = qh.astype(BF16)

        kva = kva_ref[...]
        lat = kva[:, :MLA_KV_LORA]
        ms = jnp.mean(lat * lat, axis=-1, keepdims=True)
        lat_n = (lat * lax.rsqrt(ms + RMS_EPS)) * gkv_ref[...]
        kr_blk = kva[:, MLA_KV_LORA:]
        if rope is None:
            lat_ref[:, :MLA_KV_LORA] = lat_n
            lat_ref[:, MLA_KV_LORA:] = kr_blk[:, :MLA_ROPE]
        ks, v = _mla_keys_values(lat_n, kr_blk, wkv_ref, kg_ref, rope)
        for h in range(MLA_HEADS):
            k_ref[:, h * HEAD_PAD:(h + 1) * HEAD_PAD] = ks[h].astype(BF16)
        v_ref[...] = v.astype(BF16)

    _per_group(lambda: body(None), lambda: body((cos_ref[...], sin_ref[...])))


def _mla_prep(proj, gq, wq, qg, gkv, wkv, kg, cos_t, sin_t):
    nq = MLA_HEADS * HEAD_PAD
    nv = MLA_HEADS * (MLA_NOPE + MLA_V)
    qa_blk = (HY_ORDER + 1) * HY_D // KV_PAD
    return pl.pallas_call(
        _mla_prep_kernel,
        grid=(NT,),
        in_specs=[
            pl.BlockSpec((TM, KV_PAD), lambda i: (i, qa_blk)),
            pl.BlockSpec((TM, KV_PAD), lambda i: (i, qa_blk + 1)),
            _full_spec(gq.shape), _full_spec(wq.shape), _full_spec(qg.shape), _full_spec(gkv.shape),
            _full_spec(wkv.shape), _full_spec(kg.shape),
            _rope_spec(), _rope_spec(),
        ],
        out_specs=[_row_spec(nq), _row_spec(nq), _row_spec(nv), _context_rows_spec(MLA_KV_LORA + MLA_ROPE)],
        out_shape=[
            jax.ShapeDtypeStruct((N_TOK, nq), BF16),
            jax.ShapeDtypeStruct((N_TOK, nq), BF16),
            jax.ShapeDtypeStruct((N_TOK, nv), BF16),
            jax.ShapeDtypeStruct((N_PROMPT, MLA_KV_LORA + MLA_ROPE), F32),
        ],
        compiler_params=_params("arbitrary", vmem=VMEM_LIMIT_BYTES),
        name="mla_prep",
    )(proj, proj, gq, wq, qg, gkv, wkv, kg, cos_t, sin_t)


def _mla_ctx_kernel(lat_ref, wkv_ref, kg_ref, k_ref, v_ref):
    lat = lat_ref[...]
    ks, v = _mla_keys_values(lat[:, :MLA_KV_LORA], lat[:, MLA_KV_LORA:], wkv_ref, kg_ref, None)
    for h in range(MLA_HEADS):
        k_ref[:, h * HEAD_PAD:(h + 1) * HEAD_PAD] = ks[h].astype(BF16)
    v_ref[...] = v.astype(BF16)


def _mla_ctx(lat_pad, wkv, kg):
    n = lat_pad.shape[0]
    nq = MLA_HEADS * HEAD_PAD
    nv = MLA_HEADS * (MLA_NOPE + MLA_V)
    return pl.pallas_call(
        _mla_ctx_kernel,
        grid=(n // TM,),
        in_specs=[_row_spec(KV_PAD), _full_spec(wkv.shape), _full_spec(kg.shape)],
        out_specs=[_row_spec(nq), _row_spec(nv)],
        out_shape=[jax.ShapeDtypeStruct((n, nq), BF16), jax.ShapeDtypeStruct((n, nv), BF16)],
        compiler_params=_params("arbitrary"),
        name="mla_ctx",
    )(lat_pad, wkv, kg)


def _softmax_pv(q, keys, vals, scale):
    ss = [_dot_nt(q, k) for k in keys]
    if scale is not None:
        ss = [s * scale for s in ss]
    m = ss[0].max(axis=-1, keepdims=True)
    for s in ss[1:]:
        m = jnp.maximum(m, s.max(axis=-1, keepdims=True))
    ps = [jnp.exp(s - m) for s in ss]
    l = ps[0].sum(axis=-1, keepdims=True)
    for p in ps[1:]:
        l = l + p.sum(axis=-1, keepdims=True)
    o = _dot(ps[0].astype(BF16), vals[0])
    for p, v in zip(ps[1:], vals[1:]):
        o = o + _dot(p.astype(BF16), v)
    return o / l


def _mla_attn_kernel(*refs, has_ctx):
    if has_ctx:
        q_ref, k_ref, v_ref, kc_ref, vc_ref, o_ref = refs
    else:
        q_ref, k_ref, v_ref, o_ref = refs
    scale = MLA_QK ** -0.5
    lo = _lane_iota((q_ref.shape[0], LANES)) < MLA_V
    for j in range(MLA_HEADS // 2):
        outs = []
        for h in (2 * j, 2 * j + 1):
            sl = slice(h * HEAD_PAD, (h + 1) * HEAD_PAD)
            ks, vs = [k_ref[:, sl]], [v_ref[:, sl]]
            if has_ctx:
                ks.append(kc_ref[:, sl])
                vs.append(vc_ref[:, sl])
            outs.append(_softmax_pv(q_ref[:, sl], ks, vs, scale))
        pair = jnp.where(lo, pltpu.roll(outs[0], MLA_V, 1), outs[1])
        o_ref[:, j * LANES:(j + 1) * LANES] = pair.astype(o_ref.dtype)


def _gqa_attn_kernel(*refs, has_ctx):
    if has_ctx:
        q_ref, k_ref, v_ref, kc_ref, vc_ref, o_ref = refs
    else:
        q_ref, k_ref, v_ref, o_ref = refs
    scale = None
    lo = _lane_iota((q_ref.shape[0], LANES)) < GQA_HEAD_DIM
    pairs_per_kv = (GQA_HEADS // 2) // (GQA_KV_HEADS // 2)
    for p in range(GQA_HEADS // 2):
        kv = slice((p // pairs_per_kv) * LANES, (p // pairs_per_kv + 1) * LANES)
        ks, vs = [k_ref[:, kv]], [v_ref[:, kv]]
        if has_ctx:
            ks.append(kc_ref[:, kv])
            vs.append(vc_ref[:, kv])
        qp = q_ref[:, p * LANES:(p + 1) * LANES]
        zero = jnp.zeros_like(qp)
        o_lo = _softmax_pv(jnp.where(lo, qp, zero), ks, vs, scale)
        o_hi = _softmax_pv(jnp.where(lo, zero, qp), ks, vs, scale)
        o_ref[:, p * LANES:(p + 1) * LANES] = jnp.where(lo, o_lo, o_hi).astype(o_ref.dtype)


def _attention(body, q, k, v, kc, vc, wq, wk, wv, wo):
    outs = []
    outs.append(pl.pallas_call(
        functools.partial(body, has_ctx=False),
        grid=(BATCH,),
        in_specs=[pl.BlockSpec((SEQ, wq), lambda b: (b, 0)),
                  pl.BlockSpec((SEQ, wk), lambda b: (b, 0)),
                  pl.BlockSpec((SEQ, wv), lambda b: (b, 0))],
        out_specs=pl.BlockSpec((SEQ, wo), lambda b: (b, 0)),
        out_shape=jax.ShapeDtypeStruct((N_PROMPT, wo), BF16),
        compiler_params=_params("arbitrary", vmem=VMEM_LIMIT_BYTES),
        name="attn_prompt",
    )(q, k, v))
    qt = DEC_SEQ // TM
    q0 = N_PROMPT // TM
    s0 = N_PROMPT // DEC_SEQ
    outs.append(pl.pallas_call(
        functools.partial(body, has_ctx=True),
        grid=(DEC_BATCH, qt),
        in_specs=[pl.BlockSpec((TM, wq), lambda b, t: (q0 + b * qt + t, 0)),
                  pl.BlockSpec((DEC_SEQ, wk), lambda b, t: (s0 + b, 0)),
                  pl.BlockSpec((DEC_SEQ, wv), lambda b, t: (s0 + b, 0)),
                  pl.BlockSpec((PAST_LEN, wk), lambda b, t: (b, 0)),
                  pl.BlockSpec((PAST_LEN, wv), lambda b, t: (b, 0))],
        out_specs=pl.BlockSpec((TM, wo), lambda b, t: (b * qt + t, 0)),
        out_shape=jax.ShapeDtypeStruct((N_SAMPLE, wo), BF16),
        compiler_params=_params("arbitrary", "arbitrary", vmem=VMEM_LIMIT_BYTES),
        name="attn_sample",
    )(q, k, v, kc, vc))
    return outs


def _gqa_prep_kernel(x_ref, g_ref, sc_ref, sh_ref, w_ref, qg_ref, kg_ref, cos_ref, sin_ref,
                     q_ref, k_ref, v_ref, kp_ref, vp_ref):
    lo = _lane_iota((TM, LANES)) < GQA_HEAD_DIM
    nq = GQA_HEADS * GQA_HEAD_DIM
    nk = GQA_KV_HEADS * GQA_HEAD_DIM

    def pair_norm(xp, gain):
        sq = xp * xp
        ms_lo = jnp.sum(jnp.where(lo, sq, 0.0), axis=-1, keepdims=True)
        ms_hi = jnp.sum(jnp.where(lo, 0.0, sq), axis=-1, keepdims=True)
        ms = jnp.where(lo, ms_lo, ms_hi) * (1.0 / GQA_HEAD_DIM)
        return (xp * lax.rsqrt(ms + RMS_EPS)) * gain

    def body(rope):
        def rotate(xp):
            return xp if rope is None else _rope(xp, rope[0], rope[1], GQA_HEAD_DIM // 4)

        h = _norm_mod(x_ref[...], g_ref[...], sc_ref[0], sh_ref[0])
        qkv = _dot(h.astype(BF16), w_ref[...])
        for p in range(nq // LANES):
            sl = slice(p * LANES, (p + 1) * LANES)
            qn = pair_norm(qkv[:, sl], qg_ref[...])
            q_ref[:, sl] = (rotate(qn) * GQA_SCALE).astype(BF16)
        for p in range(nk // LANES):
            sl = slice(p * LANES, (p + 1) * LANES)
            kn = pair_norm(qkv[:, nq + p * LANES:nq + (p + 1) * LANES], kg_ref[...])
            if rope is None:
                kp_ref[:, sl] = kn
            k_ref[:, sl] = rotate(kn).astype(BF16)
        v = qkv[:, nq + nk:]
        if rope is None:
            vp_ref[...] = v
        v_ref[...] = v.astype(BF16)

    _per_group(lambda: body(None), lambda: body((cos_ref[...], sin_ref[...])))


def _gqa_prep(x, g, sc, sh, w, qg, kg, cos_t, sin_t):
    nq = GQA_HEADS * GQA_HEAD_DIM
    nk = GQA_KV_HEADS * GQA_HEAD_DIM
    return pl.pallas_call(
        _gqa_prep_kernel,
        grid=(NT,),
        in_specs=[_row_spec(D_MODEL), _gain_spec(g), _mod_spec(sc), _mod_spec(sh),
                  _full_spec(w.shape), _full_spec(qg.shape), _full_spec(kg.shape),
                  _rope_spec(), _rope_spec()],
        out_specs=[_row_spec(nq), _row_spec(nk), _row_spec(nk), _context_rows_spec(nk), _context_rows_spec(nk)],
        out_shape=[
            jax.ShapeDtypeStruct((N_TOK, nq), BF16),
            jax.ShapeDtypeStruct((N_TOK, nk), BF16),
            jax.ShapeDtypeStruct((N_TOK, nk), BF16),
            jax.ShapeDtypeStruct((N_PROMPT, nk), F32),
            jax.ShapeDtypeStruct((N_PROMPT, nk), F32),
        ],
        compiler_params=_params("arbitrary", vmem=VMEM_LIMIT_BYTES),
        name="gqa_prep",
    )(x, g[0], sc[0], sh[0], w, qg, kg, cos_t, sin_t)


def _outproj_kernel(*refs, n_in):
    ap_refs = refs[:n_in]
    as_refs = refs[n_in:2 * n_in]
    w_refs = refs[2 * n_in:3 * n_in]
    x_ref, g_ref, o_ref = refs[3 * n_in:3 * n_in + 3]
    w_scr = refs[3 * n_in + 3:]
    is_prompt = pl.program_id(0) < NT_PROMPT

    @pl.when(pl.program_id(0) == 0)
    def _():
        for w, s in zip(w_refs, w_scr):
            s[...] = w[...].astype(BF16)

    y = None
    for ap, asm, s in zip(ap_refs, as_refs, w_scr):
        a = jnp.where(is_prompt, ap[...], asm[...])
        d = _dot(a, s[...])
        y = d if y is None else y + d
    o_ref[...] = x_ref[...] + g_ref[0] * y


def _outproj(acts, ws, x, gate):
    n_in = len(acts)

    def w_spec(index, rows):
        return pl.BlockSpec((None, rows, D_MODEL), lambda i: index)

    def prompt_spec(width):
        return pl.BlockSpec((TM, width), lambda i: (jnp.minimum(i, NT_PROMPT - 1), 0))

    def sample_spec(width):
        return pl.BlockSpec((TM, width), lambda i: (jnp.maximum(i - NT_PROMPT, 0), 0))

    return pl.pallas_call(
        functools.partial(_outproj_kernel, n_in=n_in),
        grid=(NT,),
        in_specs=([prompt_spec(ap.shape[1]) for ap, _ in acts] + [sample_spec(asm.shape[1]) for _, asm in acts]
                  + [w_spec(index, rows) for _, index, rows in ws] + [_row_spec(D_MODEL), _mod_spec(gate)]),
        out_specs=_row_spec(D_MODEL),
        out_shape=jax.ShapeDtypeStruct((N_TOK, D_MODEL), F32),
        scratch_shapes=[pltpu.VMEM((rows, D_MODEL), BF16) for _, _, rows in ws],
        compiler_params=_params("arbitrary", vmem=VMEM_LIMIT_BYTES),
        name="outproj",
    )(*(ap for ap, _ in acts), *(asm for _, asm in acts), *(w for w, _, _ in ws), x, gate[0])


def _router_kernel(x_ref, g_ref, sc_ref, sh_ref, wh_ref, wl_ref, br_ref, tri_ref,
                   h_ref, idx_ref, gate_ref, pos_ref, cnt_ref, run_ref):
    @pl.when(pl.program_id(0) == 0)
    def _():
        run_ref[...] = jnp.zeros_like(run_ref)

    h = _norm_mod(x_ref[...], g_ref[...], sc_ref[0], sh_ref[0])
    h_ref[...] = h.reshape((TM,) + ROW_TILE)
    hh, hl = _split(h)
    logits = _dot(hh, wh_ref[...]) + (_dot(hh, wl_ref[...]) + _dot(hl, wh_ref[...])) + br_ref[...]
    lane = _lane_iota((TM, LANES)).astype(F32)
    neg = jnp.float32(-jnp.inf)
    lg = jnp.where(lane < N_EXPERTS, logits, neg)
    tops, sels, hots = [], [], []
    for _ in range(TOP_K):
        m = lg.max(axis=-1, keepdims=True)
        sel = jnp.where(lg == m, lane, float(LANES)).min(axis=-1, keepdims=True)
        hot = lane == sel
        lg = jnp.where(hot, neg, lg)
        tops.append(m)
        sels.append(sel)
        hots.append(hot)
    es = [jnp.exp(t - tops[0]) for t in tops]
    den = es[0] + es[1] + es[2] + es[3]
    member = jnp.zeros((TM, LANES), F32)
    for hot in hots:
        member = member + hot.astype(F32)
    ranks = _dot(tri_ref[...], member.astype(BF16)) + run_ref[...]
    lane4 = _lane_iota((TM, TOP_K))
    idx4 = jnp.zeros((TM, TOP_K), F32)
    gate4 = jnp.zeros((TM, TOP_K), F32)
    pos4 = jnp.zeros((TM, TOP_K), F32)
    for k in range(TOP_K):
        pk = jnp.sum(jnp.where(hots[k], ranks, 0.0), axis=-1, keepdims=True)
        idx4 = jnp.where(lane4 == k, sels[k], idx4)
        gate4 = jnp.where(lane4 == k, es[k] / den, gate4)
        pos4 = jnp.where(lane4 == k, pk, pos4)
    idx_ref[...] = idx4.astype(jnp.int32)
    gate_ref[...] = gate4
    pos_ref[...] = pos4.astype(jnp.int32)
    run_ref[...] = run_ref[...] + jnp.sum(member, axis=0, keepdims=True)
    cnt_ref[...] = run_ref[...]


def _router_weights(w_router, b_router):
    wpad = jnp.pad(w_router, ((0, 0), (0, 0), (0, LANES - N_EXPERTS)))
    wh = wpad.astype(BF16)
    wl = (wpad - wh.astype(F32)).astype(BF16)
    bpad = jnp.pad(b_router, ((0, 0), (0, LANES - N_EXPERTS))).reshape(DEPTH, 1, LANES)
    return wh, wl, bpad


def _router(layer, x, g, sc, sh, router_weights):
    wh, wl, bpad = router_weights
    r = np.arange(TM)
    tri = jnp.asarray(r[None, :] < r[:, None], dtype=BF16)
    narrow = pl.BlockSpec((TM, TOP_K), lambda i: (i, 0))

    def layer_spec(a):
        return pl.BlockSpec((None,) + a.shape[1:], lambda i: (layer, 0, 0))

    return pl.pallas_call(
        _router_kernel,
        grid=(NT,),
        in_specs=[_row_spec(D_MODEL), _gain_spec(g), _mod_spec(sc), _mod_spec(sh),
                  layer_spec(wh), layer_spec(wl), layer_spec(bpad), _full_spec(tri.shape)],
        out_specs=[pl.BlockSpec((TM,) + ROW_TILE, lambda i: (i, 0, 0)), narrow, narrow, narrow,
                   _full_spec((1, LANES))],
        out_shape=[
            jax.ShapeDtypeStruct((N_TOK,) + ROW_TILE, F32),
            jax.ShapeDtypeStruct((N_TOK, TOP_K), jnp.int32),
            jax.ShapeDtypeStruct((N_TOK, TOP_K), F32),
            jax.ShapeDtypeStruct((N_TOK, TOP_K), jnp.int32),
            jax.ShapeDtypeStruct((1, LANES), F32),
        ],
        scratch_shapes=[pltpu.VMEM((1, LANES), F32)],
        compiler_params=_params("arbitrary", vmem=VMEM_LIMIT_BYTES),
        name="router",
    )(x, g[0], sc[0], sh[0], wh, wl, bpad, tri)


ROW_UNROLL = 4
DMA_QUEUES = 2


def _start_all_rows(row_copy):
    def start_rows(j, carry):
        for u in range(ROW_UNROLL):
            for k in range(TOP_K):
                row_copy(j * ROW_UNROLL + u, k).start(priority=k % DMA_QUEUES)
        return carry

    lax.fori_loop(0, TM // ROW_UNROLL, start_rows, 0)


def _dispatch_kernel(dest_ref, clear_ref, h_ref, xs_ref, zero_ref, sem_ref):
    def row_copy(t, k):
        d = dest_ref[(pl.program_id(0) * TM + t) * TOP_K + k]
        return pltpu.make_async_copy(h_ref.at[pl.ds(t, 1)], xs_ref.at[pl.ds(d, 1)], sem_ref.at[0])

    def zero_part(j):
        row0 = pl.multiple_of(j * MOE_PART, MOE_PART)
        return pltpu.make_async_copy(zero_ref, xs_ref.at[pl.ds(row0, MOE_PART)], sem_ref.at[1])

    def for_parts(fn):
        def body(j, carry):
            @pl.when(clear_ref[j] > 0)
            def _():
                fn(zero_part(j))
            return carry
        lax.fori_loop(0, MOE_PARTS * N_BLOCKS, body, 0)

    @pl.when(pl.program_id(0) == 0)
    def _():
        zero_ref[...] = jnp.zeros_like(zero_ref)
        for_parts(lambda cp: cp.start())
        for_parts(lambda cp: cp.wait())

    _start_all_rows(row_copy)
    for _ in range(TOP_K):
        pltpu.make_async_copy(h_ref, xs_ref.at[pl.ds(0, TM)], sem_ref.at[0]).wait()


def _dispatch(dest_flat, clear, h):
    return pl.pallas_call(
        _dispatch_kernel,
        grid_spec=pltpu.PrefetchScalarGridSpec(
            num_scalar_prefetch=2,
            grid=(NT,),
            in_specs=[pl.BlockSpec((TM,) + ROW_TILE, lambda i, d, c: (i, 0, 0))],
            out_specs=pl.BlockSpec(memory_space=pl.ANY),
            scratch_shapes=[pltpu.VMEM((MOE_PART,) + ROW_TILE, F32), pltpu.SemaphoreType.DMA((2,))],
        ),
        out_shape=jax.ShapeDtypeStruct((CAP,) + ROW_TILE, F32),
        compiler_params=_params("arbitrary", vmem=VMEM_LIMIT_BYTES),
        name="moe_dispatch",
    )(dest_flat, clear, h)


def _expert_kernel(be_ref, nh_ref, last_ref, ebuf_ref, enext_ref, xs_hbm, wgu_hbm, bgu_ref, wd_hbm, bd_ref, ys_ref,
                   wgu_s, wd_s, wgu_f, wd_f, sem_ref, xs_ref, xsem_ref, *, layer):
    b = pl.program_id(0)
    nh = nh_ref[b]
    last_used = last_ref[0]
    slot = lax.rem(b, XS_RING)

    def block_copy(blk):
        s = lax.rem(blk, XS_RING)
        row0 = pl.multiple_of(blk * MOE_BLK, MOE_BLK)
        return pltpu.make_async_copy(xs_hbm.at[pl.ds(row0, MOE_BLK)], xs_ref.at[s], xsem_ref.at[s])

    @pl.when(b == 0)
    def _():
        for ahead in range(XS_RING - 1):
            pl.when(ahead <= last_used)(lambda ahead=ahead: block_copy(ahead).start())

    @pl.when(b + (XS_RING - 1) <= last_used)
    def _():
        block_copy(b + (XS_RING - 1)).start()

    @pl.when(b <= last_used)
    def _():
        block_copy(b).wait()

    def weight_copies(e, buf):
        return (pltpu.make_async_copy(wgu_hbm.at[layer, e], wgu_f.at[buf], sem_ref.at[buf, 0]),
                pltpu.make_async_copy(wd_hbm.at[layer, e], wd_f.at[buf], sem_ref.at[buf, 1]))

    @pl.when((nh > 0) & ((b == 0) | (be_ref[b] != be_ref[jnp.maximum(b - 1, 0)])))
    def _():
        buf = ebuf_ref[b]

        @pl.when(b == 0)
        def _():
            for cp in weight_copies(be_ref[b], buf):
                cp.start()

        for cp in weight_copies(be_ref[b], buf):
            cp.wait()

        @pl.when(enext_ref[b] >= 0)
        def _():
            for cp in weight_copies(enext_ref[b], 1 - buf):
                cp.start()

        wgu_s[...] = wgu_f[buf].astype(BF16)
        wd_s[...] = wd_f[buf].astype(BF16)

    def ffn(rows):
        x = xs_ref[slot, 0:rows].reshape(rows, D_MODEL)
        gu = _dot(x.astype(BF16), wgu_s[...]) + bgu_ref[...]
        g = jnp.minimum(gu[:, :D_EXPERT], SWIGLU_LIMIT)
        u = jnp.clip(gu[:, D_EXPERT:], -SWIGLU_LIMIT, SWIGLU_LIMIT)
        act = (u + 1.0) * (g * jax.nn.sigmoid(SWIGLU_ALPHA * g))
        y = _dot(act.astype(BF16), wd_s[...]) + bd_ref[...]
        ys_ref[0:rows] = y.reshape((rows,) + ROW_TILE)

    def partial_block(parts):
        rows = parts * MOE_PART
        ffn(rows)
        if rows < MOE_BLK:
            ys_ref[rows:] = jnp.zeros((MOE_BLK - rows,) + ROW_TILE, F32)

    for parts in range(1, MOE_PARTS + 1):
        pl.when(nh == parts)(functools.partial(partial_block, parts))

    @pl.when(nh == 0)
    def _():
        ys_ref[...] = jnp.zeros_like(ys_ref)


def _experts(layer, block_e, block_nh, last_used, block_buf, block_next, xs, w_gu, b_gu, w_down, b_down):
    def e_map(b, be, nh, lu, eb, en):
        return (layer, be[b], 0, 0)

    hbm = pl.BlockSpec(memory_space=pl.ANY)
    return pl.pallas_call(
        functools.partial(_expert_kernel, layer=layer),
        grid_spec=pltpu.PrefetchScalarGridSpec(
            num_scalar_prefetch=5,
            grid=(N_BLOCKS,),
            in_specs=[
                hbm,
                hbm,
                pl.BlockSpec((None, None, 1, 2 * D_EXPERT), e_map),
                hbm,
                pl.BlockSpec((None, None, 1, D_MODEL), e_map),
            ],
            out_specs=pl.BlockSpec((MOE_BLK,) + ROW_TILE, lambda b, be, nh, lu, eb, en: (b, 0, 0)),
            scratch_shapes=[
                pltpu.VMEM((D_MODEL, 2 * D_EXPERT), BF16), pltpu.VMEM((D_EXPERT, D_MODEL), BF16),
                pltpu.VMEM((2, D_MODEL, 2 * D_EXPERT), F32), pltpu.VMEM((2, D_EXPERT, D_MODEL), F32),
                pltpu.SemaphoreType.DMA((2, 2)),
                pltpu.VMEM((XS_RING, MOE_BLK) + ROW_TILE, F32), pltpu.SemaphoreType.DMA((XS_RING,)),
            ],
        ),
        out_shape=jax.ShapeDtypeStruct((CAP,) + ROW_TILE, F32),
        compiler_params=_params("arbitrary", vmem=VMEM_LIMIT_BYTES),
        name="moe_experts",
    )(block_e, block_nh, last_used, block_buf, block_next, xs, w_gu, b_gu.reshape(DEPTH, N_EXPERTS, 1, -1),
      w_down, b_down.reshape(DEPTH, N_EXPERTS, 1, -1))


def _combine_kernel(dest_ref, x_ref, g_ref, gate_ref, ys_ref, *rest, per_group):
    out_refs, (buf_ref, sem_ref) = rest[:-2], rest[-2:]
    i = pl.program_id(0)
    cur = lax.rem(i, 2)

    def start_tile(tile, buf):
        def row_copy(t, k):
            d = dest_ref[(tile * TM + t) * TOP_K + k]
            return pltpu.make_async_copy(ys_ref.at[pl.ds(d, 1)], buf_ref.at[buf, k, pl.ds(t, 1)], sem_ref.at[buf])
        _start_all_rows(row_copy)

    @pl.when(i == 0)
    def _():
        start_tile(0, 0)

    @pl.when(i + 1 < NT)
    def _():
        start_tile(i + 1, 1 - cur)

    for k in range(TOP_K):
        pltpu.make_async_copy(ys_ref.at[pl.ds(0, TM)], buf_ref.at[cur, k], sem_ref.at[cur]).wait()
    gates = gate_ref[...]
    ff = gates[:, 0:1] * buf_ref[cur, 0].reshape(TM, D_MODEL)
    for k in range(1, TOP_K):
        ff = ff + gates[:, k:k + 1] * buf_ref[cur, k].reshape(TM, D_MODEL)
    out = x_ref[...] + g_ref[0] * ff
    if per_group:
        def store(ref):
            ref[...] = out
        _per_group(functools.partial(store, out_refs[0]), functools.partial(store, out_refs[1]))
    else:
        out_refs[0][...] = out


def _combine(dest_flat, x, gate_vec, gates, ys, per_group):
    if per_group:
        out_specs = [_context_rows_spec(D_MODEL),
                     pl.BlockSpec((TM, D_MODEL), lambda i, *_: (jnp.maximum(i - NT_PROMPT, 0), 0))]
        out_shape = [jax.ShapeDtypeStruct((N_PROMPT, D_MODEL), F32), jax.ShapeDtypeStruct((N_SAMPLE, D_MODEL), F32)]
    else:
        out_specs = pl.BlockSpec((TM, D_MODEL), lambda i, d: (i, 0))
        out_shape = jax.ShapeDtypeStruct((N_TOK, D_MODEL), F32)
    return pl.pallas_call(
        functools.partial(_combine_kernel, per_group=per_group),
        grid_spec=pltpu.PrefetchScalarGridSpec(
            num_scalar_prefetch=1,
            grid=(NT,),
            in_specs=[
                pl.BlockSpec((TM, D_MODEL), lambda i, d: (i, 0)),
                _mod_spec(gate_vec),
                pl.BlockSpec((TM, TOP_K), lambda i, d: (i, 0)),
                pl.BlockSpec(memory_space=pl.ANY),
            ],
            out_specs=out_specs,
            scratch_shapes=[pltpu.VMEM((2, TOP_K, TM) + ROW_TILE, F32), pltpu.SemaphoreType.DMA((2,))],
        ),
        out_shape=out_shape,
        compiler_params=_params("arbitrary", vmem=VMEM_LIMIT_BYTES),
        name="moe_combine",
    )(dest_flat, x, gate_vec[0], gates, ys)


def _moe(layer, x, g, sc, sh, gate_vec, router_weights, w_gu, b_gu, w_down, b_down, per_group=False):
    h, idx, gates, pos, counts = _router(layer, x, g, sc, sh, router_weights)
    cnt = counts[0, :N_EXPERTS].astype(jnp.int32)
    nparts = (cnt + MOE_PART - 1) // MOE_PART
    nblk = (nparts + MOE_PARTS - 1) // MOE_PARTS
    e_ids = jnp.arange(N_EXPERTS, dtype=jnp.int32)
    blk_end = jnp.sum(jnp.where(e_ids[None, :] <= e_ids[:, None], nblk[None, :], 0), axis=1)
    blk_start = blk_end - nblk
    last_used = blk_end[-1] - 1

    def per_expert(table, e):
        return jnp.sum(jnp.where(e[..., None] == e_ids, table, 0), axis=-1)

    def expert_of_block(blk):
        return jnp.sum((blk_end <= jnp.minimum(blk, last_used)[:, None]).astype(jnp.int32), axis=1)

    dest = per_expert(blk_start * MOE_BLK, idx) + pos
    b_ids = jnp.arange(N_BLOCKS, dtype=jnp.int32)
    block_e = expert_of_block(b_ids)
    block_nh = jnp.where(
        b_ids <= last_used,
        jnp.clip(per_expert(nparts, block_e) - MOE_PARTS * (b_ids - per_expert(blk_start, block_e)), 0, MOE_PARTS), 0)
    h_ids = jnp.arange(MOE_PARTS * N_BLOCKS, dtype=jnp.int32)
    h_e = expert_of_block(h_ids // MOE_PARTS)
    h_nparts = per_expert(nparts, h_e)
    h_local = h_ids - MOE_PARTS * per_expert(blk_start, h_e)
    holds_rows = (h_ids // MOE_PARTS <= last_used) & (h_local < h_nparts)
    clear = jnp.logical_not(holds_rows) | (h_local == h_nparts - 1)
    dest_flat = dest.reshape(-1).astype(jnp.int32)
    xs = _dispatch(dest_flat, clear.astype(jnp.int32), h)
    used = nblk > 0
    before = e_ids[None, :] < e_ids[:, None]
    expert_buf = jnp.sum(jnp.where(before & used[None, :], 1, 0), axis=1) % 2
    expert_next = jnp.min(jnp.where((e_ids[None, :] > e_ids[:, None]) & used[None, :], e_ids[None, :], N_EXPERTS),
                          axis=1)
    expert_next = jnp.where(expert_next == N_EXPERTS, -1, expert_next)
    ys = _experts(layer, block_e.astype(jnp.int32), block_nh.astype(jnp.int32),
                  last_used.reshape(1).astype(jnp.int32), per_expert(expert_buf, block_e).astype(jnp.int32),
                  per_expert(expert_next, block_e).astype(jnp.int32), xs, w_gu, b_gu, w_down, b_down)
    return _combine(dest_flat, x, gate_vec, gates, ys, per_group)


def _rope_tables(d_rot, lane0, period):
    n_rows = DEC_SEQ // GRID_W
    rows = np.repeat(np.arange(n_rows), GRID_W).astype(np.float32)
    cols = np.tile(np.arange(GRID_W), n_rows).astype(np.float32)
    half = d_rot // 2
    lane = np.arange(LANES)
    i = (lane - lane0) % period
    active = (lane >= lane0) & (i < d_rot)
    w = i % half
    f = w % (half // 2)
    pos = np.where((i // half)[None, :] == 0, rows[:, None], cols[:, None])
    sign = np.where(w < half // 2, -1.0, 1.0).astype(np.float32)
    inv = ROPE_THETA ** (-jnp.arange(0, half, 2, dtype=F32) / half)
    ang = jnp.asarray(pos) * inv[f][None, :]
    cos = jnp.where(active[None, :], jnp.cos(ang), 1.0)
    sin = jnp.where(active[None, :], jnp.sin(ang) * sign[None, :], 0.0)
    return cos, sin


def _rope_spec():
    def index(i):
        return (jnp.maximum(i - NT_PROMPT, 0) % TILES_PER_SAMPLE, 0)
    return pl.BlockSpec((TM, LANES), index)


def _pad_heads(w, n_heads, width):
    lead = w.shape[:-1]
    w = w.reshape(lead + (n_heads, width))
    w = jnp.pad(w, [(0, 0)] * len(lead) + [(0, 0), (0, HEAD_PAD - width)])
    return w.reshape(lead + (n_heads * HEAD_PAD,))


_Q_ORDER = (0, 4, 1, 5, 2, 6, 3, 7, 8, 12, 9, 13, 10, 14, 11, 15)


def _perm_q_heads(w, axis):
    shape = w.shape
    n = shape[axis]
    w = jnp.moveaxis(w, axis, 0).reshape((GQA_HEADS, n // GQA_HEADS) + tuple(s for a, s in enumerate(shape) if a != axis))
    w = w[jnp.array(_Q_ORDER)]
    w = w.reshape((n,) + w.shape[2:])
    return jnp.moveaxis(w, 0, axis)


def _constants():
    mla_cos, mla_sin = _rope_tables(MLA_ROPE, MLA_NOPE, LANES)
    gqa_cos, gqa_sin = _rope_tables(GQA_HEAD_DIM, 0, GQA_HEAD_DIM)
    dft = {}
    for L in (SEQ, DEC_SEQ):
        cm, sm = _dft_tables(L)
        fmat = jnp.concatenate([cm, sm], axis=0).astype(BF16)
        ftmat = fmat.T
        dft[L] = (cm, sm, fmat, ftmat)
    return dict(mla=(mla_cos, mla_sin), gqa=(gqa_cos, gqa_sin), dft=dft)


def _even_mixer(x, p, i, gmix, sc1, sh1, g1, consts):
    proj = _normlin(x, gmix, sc1, sh1, p["w_in_ab"], i, IN_AB_PAD)
    y_hy = []
    for L, blk0, nseq, per_step in ((SEQ, 0, BATCH, HY_PROMPT_SEQS), (DEC_SEQ, N_PROMPT // DEC_SEQ, DEC_BATCH, 1)):
        cm, sm, fmat, ftmat = consts["dft"][L]
        kc, ks = _hy_filter(L, cm, sm, p["hy_filter_w1"][i], p["hy_filter_b1"][i], p["hy_filter_freq"][i],
                            p["hy_filter_w2"][i], p["hy_filter_b2"][i], p["hy_filter_w3"][i],
                            p["hy_filter_b3"][i], p["hy_log_decay"][i])
        y_hy.append(_hy_mix(proj, blk0, nseq, per_step, L, p["hy_conv_w"][i], p["hy_conv_b"][i], kc, ks,
                            p["hy_bias"][i], fmat, ftmat))

    wq = _pad_heads(p["mla_wq_b"][i], MLA_HEADS, MLA_QK).astype(BF16)
    wkv = p["mla_wkv_b"][i].astype(BF16)
    qg = jnp.pad(p["mla_q_norm"][i], (0, HEAD_PAD - MLA_QK)).reshape(1, HEAD_PAD)
    kg = jnp.pad(p["mla_k_norm"][i], (0, HEAD_PAD - MLA_QK)).reshape(1, HEAD_PAD)
    mla_cos, mla_sin = consts["mla"]
    q, k, v, lat = _mla_prep(proj, p["mla_q_lora_norm"][i].reshape(1, -1), wq, qg,
                             p["mla_kv_lora_norm"][i].reshape(1, -1), wkv, kg, mla_cos, mla_sin)
    ctx = jnp.pad(p["cache_mla_latent"][:, i].reshape(DEC_BATCH * PAST_LEN, -1),
                  ((0, 0), (0, KV_PAD - MLA_KV_LORA - MLA_ROPE)))
    kc_ctx, vc_ctx = _mla_ctx(ctx, wkv, kg)
    nqk = MLA_HEADS * HEAD_PAD
    o = _attention(_mla_attn_kernel, q, k, v, kc_ctx, vc_ctx, nqk, nqk, nqk, MLA_HEADS * MLA_V)
    w_out = p["w_out_ab"]
    x = _outproj([y_hy, o], [(w_out, (i, 0, 0), HY_D), (w_out, (i, 1, 0), HY_D)], x, g1)
    return x, lat, dict(y_hy=jnp.concatenate(y_hy, axis=0), o=jnp.concatenate(o, axis=0))


def _odd_mixer(x, p, i, gmix, sc1, sh1, g1, consts):
    nq = GQA_HEADS * GQA_HEAD_DIM
    nk = GQA_KV_HEADS * GQA_HEAD_DIM
    w = p["w_qkv_c"][i]
    w_qkv = jnp.concatenate([_perm_q_heads(w[:, :nq], 1), w[:, nq:]], axis=1).astype(BF16)
    qg = jnp.tile(p["gqa_q_norm"][i], LANES // GQA_HEAD_DIM).reshape(1, LANES)
    kg = jnp.tile(p["gqa_k_norm"][i], LANES // GQA_HEAD_DIM).reshape(1, LANES)
    gqa_cos, gqa_sin = consts["gqa"]
    q, k, v, k_plain, v_plain = _gqa_prep(x, gmix, sc1, sh1, w_qkv, qg, kg, gqa_cos, gqa_sin)
    kc_ctx = p["cache_gqa_k"][:, i].reshape(DEC_BATCH * PAST_LEN, -1).astype(BF16)
    vc_ctx = p["cache_gqa_v"][:, i].reshape(DEC_BATCH * PAST_LEN, -1).astype(BF16)
    o = _attention(_gqa_attn_kernel, q, k, v, kc_ctx, vc_ctx, nq, nk, nk, nq)
    w_out = _perm_q_heads(p["w_out_c"][i], 0)[None]
    x = _outproj([o], [(w_out, (0, 0, 0), nq)], x, g1)
    return x, k_plain, v_plain


def kernel(x_prompt, x_sample, cache_mla_latent, cache_gqa_k, cache_gqa_v, c, c_ctx, w_ada, b_ada, norm_mix, norm_ffn, w_in_ab, hy_conv_w, hy_conv_b, hy_filter_w1, hy_filter_b1, hy_filter_freq, hy_filter_w2, hy_filter_b2, hy_filter_w3, hy_filter_b3, hy_log_decay, hy_bias, mla_q_lora_norm, mla_wq_b, mla_kv_lora_norm, mla_wkv_b, mla_q_norm, mla_k_norm, w_out_ab, w_qkv_c, gqa_q_norm, gqa_k_norm, w_out_c, moe_router_w, moe_router_b, moe_w_gate_up, moe_b_gate_up, moe_w_down, moe_b_down):
    p = dict(locals())
    x = jnp.concatenate([x_prompt.reshape(N_PROMPT, D_MODEL), x_sample.reshape(N_SAMPLE, D_MODEL)], axis=0)

    cond = jnp.concatenate([c_ctx[None, :], c, jnp.zeros((COND_ROWS - N_COND, D_MODEL), F32)], axis=0)
    mods = _modulation(cond, w_ada, b_ada)
    tile_cond = jnp.concatenate([jnp.zeros((NT_PROMPT,), jnp.int32),
                                 1 + jnp.arange(NT - NT_PROMPT, dtype=jnp.int32) // TILES_PER_SAMPLE])
    mods = mods[:, tile_cond].reshape(DEPTH, NT, 6, 1, D_MODEL).transpose(0, 2, 1, 3, 4)

    consts = _constants()
    router_weights = _router_weights(moe_router_w, moe_router_b)
    gains_mix = norm_mix.reshape(DEPTH, 1, D_MODEL)
    gains_ffn = norm_ffn.reshape(DEPTH, 1, D_MODEL)

    lat_out, k_out, v_out = [], [], []
    for l in range(DEPTH):
        sh1, sc1, g1, sh2, sc2, g2 = ((mods, l, j) for j in range(6))
        i = l // 2
        gmix = (gains_mix, l)
        if l % 2 == 0:
            x, lat, _ = _even_mixer(x, p, i, gmix, sc1, sh1, g1, consts)
            lat_out.append(lat.reshape(BATCH, SEQ, -1))
        else:
            x, k_plain, v_plain = _odd_mixer(x, p, i, gmix, sc1, sh1, g1, consts)
            k_out.append(k_plain.reshape(BATCH, SEQ, GQA_KV_HEADS, GQA_HEAD_DIM))
            v_out.append(v_plain.reshape(BATCH, SEQ, GQA_KV_HEADS, GQA_HEAD_DIM))
        x = _moe(l, x, (gains_ffn, l), sc2, sh2, g2, router_weights,
                 moe_w_gate_up, moe_b_gate_up, moe_w_down, moe_b_down, per_group=(l == DEPTH - 1))

    y_prompt = x[0].reshape(BATCH, SEQ, D_MODEL)
    y_sample = x[1].reshape(DEC_BATCH, DEC_SEQ, D_MODEL)
    return (y_prompt, y_sample, jnp.stack(lat_out, axis=1), jnp.stack(k_out, axis=1), jnp.stack(v_out, axis=1))
```
